```python
import math
import jax, jax.numpy as jnp
from jax import lax
import numpy as np

D_MODEL = 1024
BATCH = 16
SEQ = 2048
DEPTH = 2

N_EVEN = (DEPTH + 1) // 2
N_ODD = DEPTH // 2

S5_WIDTH = D_MODEL // 2
S5_GROUP = 16
S5_GROUPS = S5_WIDTH // S5_GROUP
S5_STATE = 64
S5_DT_MIN = 1e-3
S5_DT_MAX = 1e-1
MOBA_WIDTH = D_MODEL // 2
MOBA_HEAD_DIM = 64
MOBA_HEADS = MOBA_WIDTH // MOBA_HEAD_DIM
MOBA_BLOCK = 256
MOBA_TOPK = 3
MOBA_Q_CHUNK = 16
REL_BUCKETS = 32
REL_MAX_DIST = 2048
GLA_HEADS = 4
GLA_KEY_DIM = D_MODEL // 2
GLA_VAL_DIM = D_MODEL
GLA_DK = GLA_KEY_DIM // GLA_HEADS
GLA_DV = GLA_VAL_DIM // GLA_HEADS
GLA_GATE_RANK = 16
GLA_GATE_TAU = 16.0
GLA_CHUNK = 64
N_EXPERTS = 32
TOP_K = 4
D_FF = D_MODEL
SWIGLU_LIMIT = 7.0
SWIGLU_ALPHA = 1.702
MOE_BLOCK = 256
DN_ALPHA = (2 * DEPTH) ** 0.25
DN_BETA = (8 * DEPTH) ** -0.25
LN_EPS = 1e-5
RMS_EPS = 1e-5

IN0 = S5_WIDTH + 3 * MOBA_WIDTH
IN1 = 2 * GLA_KEY_DIM + 2 * GLA_VAL_DIM + GLA_GATE_RANK

kernel_name = "hybrid_s5_moba_gla_moe_deepnorm"


def layer_norm(x, g, b):
    xf = x.astype(jnp.float32)
    mu = jnp.mean(xf, -1, keepdims=True)
    var = jnp.mean(jnp.square(xf - mu), -1, keepdims=True)
    return ((xf - mu) * lax.rsqrt(var + LN_EPS) * g + b).astype(x.dtype)


def _cplx_scan_op(e1, e2):
    a1r, a1i, b1r, b1i = e1
    a2r, a2i, b2r, b2i = e2
    ar = a2r * a1r - a2i * a1i
    ai = a2r * a1i + a2i * a1r
    br = a2r * b1r - a2i * b1i + b2r
    bi = a2r * b1i + a2i * b1r + b2i
    return ar, ai, br, bi


def s5_mixer(u, lam_re, lam_im, log_dt, b_re, b_im, c_re, c_im, d_skip, w_glu, b_glu):
    Bn, L, _ = u.shape
    f32 = jnp.float32
    G, P, H = S5_GROUPS, S5_STATE, S5_GROUP
    ug = u.astype(f32).reshape(Bn, L, G, H)
    dt = jnp.exp(log_dt.astype(f32))[:, None]
    lr, li = lam_re.astype(f32), lam_im.astype(f32)
    mag = jnp.exp(lr * dt)
    ab_re, ab_im = mag * jnp.cos(li * dt), mag * jnp.sin(li * dt)
    er, ei = ab_re - 1.0, ab_im
    den = lr * lr + li * li
    q_re = (er * lr + ei * li) / den
    q_im = (ei * lr - er * li) / den
    br_, bi_ = b_re.astype(f32), b_im.astype(f32)
    bb_re = q_re[..., None] * br_ - q_im[..., None] * bi_
    bb_im = q_re[..., None] * bi_ + q_im[..., None] * br_
    bu_re = jnp.einsum('blgh,gph->lbgp', ug, bb_re)
    bu_im = jnp.einsum('blgh,gph->lbgp', ug, bb_im)
    a_re = jnp.broadcast_to(ab_re[None, None], (L, 1, G, P))
    a_im = jnp.broadcast_to(ab_im[None, None], (L, 1, G, P))
    _, _, s_re, s_im = lax.associative_scan(_cplx_scan_op, (a_re, a_im, bu_re, bu_im), axis=0)
    y = (jnp.einsum('lbgp,ghp->blgh', s_re, c_re.astype(f32))
         - jnp.einsum('lbgp,ghp->blgh', s_im, c_im.astype(f32))
         + ug * d_skip.astype(f32))
    y = jax.nn.gelu(y.reshape(Bn, L, S5_WIDTH))
    y = y * jax.nn.sigmoid(y @ w_glu.astype(f32) + b_glu.astype(f32))
    return y.astype(u.dtype)


def rel_bucket(n):
    n = jnp.maximum(n, 0)
    max_exact = REL_BUCKETS // 2
    nf = jnp.maximum(n, 1).astype(jnp.float32)
    large = max_exact + (jnp.log(nf / max_exact) / math.log(REL_MAX_DIST / max_exact)
                         * (REL_BUCKETS - max_exact)).astype(jnp.int32)
    large = jnp.minimum(large, REL_BUCKETS - 1)
    return jnp.where(n < max_exact, n, large)


def moba_attention(q, k, v, rel_table):
    Bn, L, _ = q.shape
    H, dh, bs, qcn = MOBA_HEADS, MOBA_HEAD_DIM, MOBA_BLOCK, MOBA_Q_CHUNK
    Lp = -(-L // bs) * bs
    nb = Lp // bs
    topk = min(MOBA_TOPK, nb)
    f32 = jnp.float32

    def heads(t):
        t = t.astype(f32).reshape(Bn, L, H, dh).transpose(0, 2, 1, 3)
        return jnp.pad(t, ((0, 0), (0, 0), (0, Lp - L), (0, 0)))

    qh = heads(q) * dh ** -0.5
    kh, vh = heads(k), heads(v)
    kb = kh.reshape(Bn, H, nb, bs, dh)
    vb = vh.reshape(Bn, H, nb, bs, dh)
    kmean = jnp.mean(kb, axis=3)
    table_t = rel_table.astype(f32).T
    b_ix = jnp.arange(Bn)[:, None, None, None]
    h_ix = jnp.arange(H)[None, :, None, None]
    neg = jnp.finfo(f32).min

    def chunk(start):
        qc = lax.dynamic_slice_in_dim(qh, start, qcn, axis=2)
        qpos = start + jnp.arange(qcn)
        own = start // bs
        ko = lax.dynamic_slice_in_dim(kh, own * bs, bs, axis=2)
        vo = lax.dynamic_slice_in_dim(vh, own * bs, bs, axis=2)
        rel_o = qpos[:, None] - (own * bs + jnp.arange(bs))[None, :]
        s_own = jnp.einsum('bhqd,bhkd->bhqk', qc, ko) + table_t[:, rel_bucket(rel_o)]
        s_own = jnp.where(rel_o >= 0, s_own, neg)
        gate = jnp.einsum('bhqd,bhnd->bhqn', qc, kmean)
        gate = jnp.where(jnp.arange(nb) < own, gate, neg)
        _, idx = lax.top_k(gate, topk)
        valid = jnp.arange(topk) < own
        ks = kb[b_ix, h_ix, idx]
        vs = vb[b_ix, h_ix, idx]
        rel_s = qpos[:, None, None] - (idx[..., None] * bs + jnp.arange(bs))
        s_sel = (jnp.einsum('bhqd,bhqjkd->bhqjk', qc, ks)
                 + table_t[h_ix[..., None], rel_bucket(rel_s)])
        s_sel = jnp.where(valid[:, None], s_sel, neg).reshape(Bn, H, qcn, topk * bs)
        p = jax.nn.softmax(jnp.concatenate([s_own, s_sel], axis=-1), axis=-1)
        o = (jnp.einsum('bhqk,bhkd->bhqd', p[..., :bs], vo)
             + jnp.einsum('bhqjk,bhqjkd->bhqd', p[..., bs:].reshape(Bn, H, qcn, topk, bs), vs))
        return o

    starts = jnp.arange(0, Lp, qcn, dtype=jnp.int32)
    out = lax.map(chunk, starts)
    out = out.transpose(1, 2, 0, 3, 4).reshape(Bn, H, Lp, dh)[:, :, :L]
    return out.transpose(0, 2, 1, 3).reshape(Bn, L, H * dh).astype(q.dtype)


def gla_mixer(q, k, v, log_a, g, norm_w):
    Bn, L, _ = q.shape
    H, dk, dv, C = GLA_HEADS, GLA_DK, GLA_DV, GLA_CHUNK
    n = L // C
    f32 = jnp.float32

    def split(t, d):
        return t.astype(f32).reshape(Bn, n, C, H, d).transpose(0, 3, 1, 2, 4)

    qc = split(q, dk) * dk ** -0.5
    kc, vc, lg = split(k, dk), split(v, dv), split(log_a, dk)
    bcum = jnp.cumsum(lg, axis=3)
    btot = bcum[:, :, :, -1:]
    q_in = qc * jnp.exp(bcum)
    k_in = kc * jnp.exp(-bcum)
    k_st = kc * jnp.exp(btot - bcum)
    causal = jnp.tril(jnp.ones((C, C), dtype=bool))
    att = jnp.where(causal, jnp.einsum('bhnid,bhnjd->bhnij', q_in, k_in), 0.0)
    o_intra = jnp.einsum('bhnij,bhnjv->bhniv', att, vc)
    kv = jnp.einsum('bhnjd,bhnjv->bhndv', k_st, vc)
    decay = jnp.exp(btot[:, :, :, 0])

    def step(S, inp):
        dec, kv_c = inp
        return dec[..., None] * S + kv_c, S

    S0 = jnp.zeros((Bn, H, dk, dv), f32)
    _, S_prev = lax.scan(step, S0, (decay.transpose(2, 0, 1, 3), kv.transpose(2, 0, 1, 3, 4)))
    o = o_intra + jnp.einsum('bhnid,nbhdv->bhniv', q_in, S_prev)
    o = o.transpose(0, 2, 3, 1, 4).reshape(Bn, L, H, dv)
    o = o * lax.rsqrt(jnp.mean(o * o, -1, keepdims=True) + RMS_EPS) * norm_w.astype(f32)
    o = o.reshape(Bn, L, H * dv) * jax.nn.silu(g.astype(f32))
    return o.astype(q.dtype)


def moe_ffn(x, w_router, b_router, w_gate_up, b_gate_up, w_down, b_down):
    Bn, L, D = x.shape
    f32 = jnp.float32
    T = Bn * L
    A = T * TOP_K
    xt = x.reshape(T, D)
    logits = (xt @ w_router + b_router).astype(f32)
    top_v, top_i = lax.top_k(logits, TOP_K)
    top_w = jax.nn.softmax(top_v, axis=-1)
    e_flat = top_i.reshape(A)
    tok_flat = jnp.repeat(jnp.arange(T, dtype=jnp.int32), TOP_K)
    w_flat = top_w.reshape(A)
    order = jnp.argsort(e_flat)
    se, st, sw = e_flat[order], tok_flat[order], w_flat[order]
    counts = jnp.bincount(e_flat, length=N_EXPERTS)
    padded = (counts + MOE_BLOCK - 1) // MOE_BLOCK * MOE_BLOCK
    start_src = jnp.cumsum(counts) - counts
    start_dst = jnp.cumsum(padded) - padded
    dest = start_dst[se] + jnp.arange(A) - start_src[se]
    P = A + N_EXPERTS * MOE_BLOCK
    pad_tok = jnp.zeros((P,), jnp.int32).at[dest].set(st)
    pad_w = jnp.zeros((P,), f32).at[dest].set(sw)
    nblk = P // MOE_BLOCK
    blk_end = jnp.cumsum(padded // MOE_BLOCK)
    blk_expert = jnp.minimum(jnp.searchsorted(blk_end, jnp.arange(nblk), side='right'), N_EXPERTS - 1)
    xs = xt[pad_tok].reshape(nblk, MOE_BLOCK, D)

    def expert_block(args):
        xb, e = args
        h = (xb @ w_gate_up[e] + b_gate_up[e]).astype(f32)
        gate = jnp.minimum(h[:, 0::2], SWIGLU_LIMIT)
        up = jnp.clip(h[:, 1::2], -SWIGLU_LIMIT, SWIGLU_LIMIT)
        act = gate * jax.nn.sigmoid(SWIGLU_ALPHA * gate) * (up + 1.0)
        return (act.astype(xb.dtype) @ w_down[e] + b_down[e]).astype(f32)

    ys = lax.map(expert_block, (xs, blk_expert)).reshape(P, D)
    out = jnp.zeros((T, D), f32).at[pad_tok].add(ys * pad_w[:, None])
    return out.reshape(Bn, L, D).astype(x.dtype)


def setup_inputs(seed: int = 0) -> dict:
    key = jax.random.key(seed)
    ks = iter(jax.random.split(key, 40))
    f32 = jnp.float32

    def nrm(shape, scale):
        return jax.random.normal(next(ks), shape, f32) * scale

    NE, NO = N_EVEN, N_ODD
    G, P, H = S5_GROUPS, S5_STATE, S5_GROUP
    x = nrm((BATCH, SEQ, D_MODEL), 1.0)
    w_in0 = nrm((NE, D_MODEL, IN0), D_MODEL ** -0.5)
    s5_lam_re = -0.5 + nrm((NE, G, P), 0.01)
    s5_lam_im = jnp.pi * jnp.arange(P, dtype=f32) + nrm((NE, G, P), 0.01)
    s5_log_dt = jax.random.uniform(next(ks), (NE, G), f32, math.log(S5_DT_MIN), math.log(S5_DT_MAX))
    s5_b_re = nrm((NE, G, P, H), (2 * H) ** -0.5)
    s5_b_im = nrm((NE, G, P, H), (2 * H) ** -0.5)
    s5_c_re = nrm((NE, G, H, P), P ** -0.5)
    s5_c_im = nrm((NE, G, H, P), P ** -0.5)
    s5_d = nrm((NE, G, H), 1.0)
    s5_w_glu = nrm((NE, S5_WIDTH, S5_WIDTH), S5_WIDTH ** -0.5)
    s5_b_glu = nrm((NE, S5_WIDTH), 0.01)
    rel_bias = nrm((REL_BUCKETS, MOBA_HEADS), 0.5)
    w_out0 = nrm((NE, D_MODEL, D_MODEL), D_MODEL ** -0.5 * DN_BETA)
    w_in1 = nrm((NO, D_MODEL, IN1), D_MODEL ** -0.5)
    gla_w_gate2 = nrm((NO, GLA_GATE_RANK, GLA_KEY_DIM), GLA_GATE_RANK ** -0.5)
    gla_b_gate = nrm((NO, GLA_KEY_DIM), 0.1)
    gla_norm_w = 1.0 + nrm((NO, GLA_DV), 0.01)
    w_out1 = nrm((NO, GLA_VAL_DIM, D_MODEL), GLA_VAL_DIM ** -0.5 * DN_BETA)
    ln_mix_g = 1.0 + nrm((DEPTH, D_MODEL), 0.01)
    ln_mix_b = nrm((DEPTH, D_MODEL), 0.01)
    ln_ffn_g = 1.0 + nrm((DEPTH, D_MODEL), 0.01)
    ln_ffn_b = nrm((DEPTH, D_MODEL), 0.01)
    router_w = nrm((DEPTH, D_MODEL, N_EXPERTS), D_MODEL ** -0.5)
    router_b = nrm((DEPTH, N_EXPERTS), 0.01)
    exp_w_gate_up = nrm((DEPTH, N_EXPERTS, D_MODEL, 2 * D_FF), D_MODEL ** -0.5)
    exp_b_gate_up = nrm((DEPTH, N_EXPERTS, 2 * D_FF), 0.01)
    exp_w_down = nrm((DEPTH, N_EXPERTS, D_FF, D_MODEL), D_FF ** -0.5 * DN_BETA)
    exp_b_down = nrm((DEPTH, N_EXPERTS, D_MODEL), 0.01)
    return {"x": x, "w_in0": w_in0, "s5_lam_re": s5_lam_re, "s5_lam_im": s5_lam_im,
            "s5_log_dt": s5_log_dt, "s5_b_re": s5_b_re, "s5_b_im": s5_b_im,
            "s5_c_re": s5_c_re, "s5_c_im": s5_c_im, "s5_d": s5_d, "s5_w_glu": s5_w_glu,
            "s5_b_glu": s5_b_glu, "rel_bias": rel_bias, "w_out0": w_out0, "w_in1": w_in1,
            "gla_w_gate2": gla_w_gate2, "gla_b_gate": gla_b_gate, "gla_norm_w": gla_norm_w,
            "w_out1": w_out1, "ln_mix_g": ln_mix_g, "ln_mix_b": ln_mix_b, "ln_ffn_g": ln_ffn_g,
            "ln_ffn_b": ln_ffn_b, "router_w": router_w, "router_b": router_b,
            "exp_w_gate_up": exp_w_gate_up, "exp_b_gate_up": exp_b_gate_up,
            "exp_w_down": exp_w_down, "exp_b_down": exp_b_down}


def reference(x, w_in0, s5_lam_re, s5_lam_im, s5_log_dt, s5_b_re, s5_b_im, s5_c_re, s5_c_im,
              s5_d, s5_w_glu, s5_b_glu, rel_bias, w_out0, w_in1, gla_w_gate2, gla_b_gate,
              gla_norm_w, w_out1, ln_mix_g, ln_mix_b, ln_ffn_g, ln_ffn_b, router_w, router_b,
              exp_w_gate_up, exp_b_gate_up, exp_w_down, exp_b_down):
    f32 = jnp.float32
    for i in range(DEPTH):
        j = i // 2
        if i % 2 == 0:
            h = x @ w_in0[j]
            u, q, k, v = jnp.split(h, [S5_WIDTH, S5_WIDTH + MOBA_WIDTH, S5_WIDTH + 2 * MOBA_WIDTH], axis=-1)
            y_a = s5_mixer(u, s5_lam_re[j], s5_lam_im[j], s5_log_dt[j], s5_b_re[j], s5_b_im[j],
                           s5_c_re[j], s5_c_im[j], s5_d[j], s5_w_glu[j], s5_b_glu[j])
            y_b = moba_attention(q, k, v, rel_bias)
            mix = jnp.concatenate([y_a, y_b], axis=-1) @ w_out0[j]
        else:
            h = x @ w_in1[j]
            q, k, v, g, gr = jnp.split(h, [GLA_KEY_DIM, 2 * GLA_KEY_DIM, 2 * GLA_KEY_DIM + GLA_VAL_DIM,
                                           2 * GLA_KEY_DIM + 2 * GLA_VAL_DIM], axis=-1)
            log_a = jax.nn.log_sigmoid((gr @ gla_w_gate2[j] + gla_b_gate[j]).astype(f32)) / GLA_GATE_TAU
            mix = gla_mixer(q, k, v, log_a, g, gla_norm_w[j]) @ w_out1[j]
        x = layer_norm(DN_ALPHA * x + mix, ln_mix_g[i], ln_mix_b[i])
        ffn = moe_ffn(x, router_w[i], router_b[i], exp_w_gate_up[i], exp_b_gate_up[i],
                      exp_w_down[i], exp_b_down[i])
        x = layer_norm(DN_ALPHA * x + ffn, ln_ffn_g[i], ln_ffn_b[i])
    return x
```

```python
import functools
import math

import jax
import jax.numpy as jnp
import numpy as np
from jax import lax
from jax.experimental import pallas as pl
from jax.experimental.pallas import tpu as pltpu

F32 = jnp.float32
BF16 = jnp.bfloat16
HIGHEST = lax.Precision.HIGHEST

DEPTH = 2
S5_GROUP = 16
S5_STATE = 64
MOBA_HEAD_DIM = 64
MOBA_BLOCK = 256
MOBA_TOPK = 3
REL_BUCKETS = 32
REL_MAX_DIST = 2048
GLA_HEADS = 4
GLA_GATE_TAU = 16.0
GLA_CHUNK = 64
N_EXPERTS = 32
TOP_K = 4
SWIGLU_LIMIT = 7.0
SWIGLU_ALPHA = 1.702
MOE_BLOCK = 256
DN_ALPHA = (2 * DEPTH) ** 0.25
LN_EPS = 1e-5
RMS_EPS = 1e-5

V7X_VMEM_LIMIT_BYTES = 56 * 1024 * 1024
LANES = 128

S5_GROUPS_PER_CHUNK = LANES // S5_GROUP
S5_CHUNK_STATES = S5_GROUPS_PER_CHUNK * S5_STATE


def _params(sem):
    return pltpu.CompilerParams(dimension_semantics=sem, vmem_limit_bytes=V7X_VMEM_LIMIT_BYTES)


def _layer_norm(r, g, b):
    mu = jnp.mean(r, axis=-1, keepdims=True)
    c = r - mu
    var = jnp.mean(c * c, axis=-1, keepdims=True)
    return c * lax.rsqrt(var + LN_EPS) * g + b


def _inproj0_kernel(x_ref, w_ref, u_ref, qkv_ref, *, s5w):
    h = jnp.dot(x_ref[0].astype(BF16), w_ref[...], preferred_element_type=F32)
    u_ref[...] = h[:, :s5w]
    qkv_ref[0] = h[:, s5w:]


def _inproj0(x, w, s5w, tl=512):
    B, L, D = x.shape
    n = w.shape[1]
    return pl.pallas_call(
        functools.partial(_inproj0_kernel, s5w=s5w),
        grid=(B, L // tl),
        in_specs=[pl.BlockSpec((1, tl, D), lambda b, l: (b, l, 0)),
                  pl.BlockSpec((D, n), lambda b, l: (0, 0))],
        out_specs=[pl.BlockSpec((tl, s5w), lambda b, l: (l, b)),
                   pl.BlockSpec((1, tl, n - s5w), lambda b, l: (b, l, 0))],
        out_shape=[jax.ShapeDtypeStruct((L, B * s5w), F32),
                   jax.ShapeDtypeStruct((B, L, n - s5w), F32)],
        compiler_params=_params(("parallel", "parallel")),
        name="inproj0",
    )(x, w)


def _s5_discretize(lam_re, lam_im, log_dt, b_re, b_im):
    dt = jnp.exp(log_dt.astype(F32))[:, None]
    lr, li = lam_re.astype(F32), lam_im.astype(F32)
    mag = jnp.exp(lr * dt)
    ab_re, ab_im = mag * jnp.cos(li * dt), mag * jnp.sin(li * dt)
    er, ei = ab_re - 1.0, ab_im
    den = lr * lr + li * li
    q_re = (er * lr + ei * li) / den
    q_im = (ei * lr - er * li) / den
    br_, bi_ = b_re.astype(F32), b_im.astype(F32)
    bb_re = q_re[..., None] * br_ - q_im[..., None] * bi_
    bb_im = q_re[..., None] * bi_ + q_im[..., None] * br_
    return ab_re, ab_im, bb_re, bb_im


def _s5_kernel(u_ref, bm_ref, cm_ref, are_ref, aim_ref, d_ref, wg_ref, bg_ref, y_ref,
               s_scr, st_scr, z_scr, *, tl, nb, nchunk):
    ns = S5_CHUNK_STATES

    @pl.when(pl.program_id(0) == 0)
    def _():
        st_scr[...] = jnp.zeros_like(st_scr)

    u = u_ref[...].reshape(tl * nb, nchunk * LANES)
    for j in range(nchunk):
        uj = u[:, j * LANES:(j + 1) * LANES]
        s_scr[...] = jnp.dot(uj.astype(BF16), bm_ref[j], preferred_element_type=F32)
        ar = jnp.broadcast_to(are_ref[j], (nb, ns))
        ai = jnp.broadcast_to(aim_ref[j], (nb, ns))

        def step(t, carry, ar=ar, ai=ai):
            sre, sim = carry
            r0 = pl.multiple_of(t * nb, nb)
            nre = ar * sre - ai * sim + s_scr[pl.ds(r0, nb), 0:ns]
            nim = ar * sim + ai * sre + s_scr[pl.ds(r0, nb), ns:2 * ns]
            s_scr[pl.ds(r0, nb), 0:ns] = nre
            s_scr[pl.ds(r0, nb), ns:2 * ns] = nim
            return nre, nim

        sre, sim = lax.fori_loop(0, tl, step, (st_scr[j, :, 0:ns], st_scr[j, :, ns:2 * ns]), unroll=2)
        st_scr[j, :, 0:ns] = sre
        st_scr[j, :, ns:2 * ns] = sim
        yj = jnp.dot(s_scr[...].astype(BF16), cm_ref[j], preferred_element_type=F32)
        z_scr[:, j * LANES:(j + 1) * LANES] = yj + uj * d_ref[:, j * LANES:(j + 1) * LANES]
    z = jax.nn.gelu(z_scr[...])
    gate = jax.nn.sigmoid(jnp.dot(z.astype(BF16), wg_ref[...], preferred_element_type=F32) + bg_ref[...])
    y_ref[...] = (z * gate).reshape(tl, nb, nchunk * LANES)


def _s5_mixer(u_tm, lam_re, lam_im, log_dt, b_re, b_im, c_re, c_im, d_skip, w_glu, b_glu, tl=32):
    L, B, W = u_tm.shape
    G, P, H = lam_re.shape[0], S5_STATE, S5_GROUP
    gc = S5_GROUPS_PER_CHUNK
    nchunk = G // gc
    ab_re, ab_im, bb_re, bb_im = _s5_discretize(lam_re, lam_im, log_dt, b_re, b_im)
    eye = jnp.eye(gc, dtype=F32)

    def b_blocks(bb):
        return jnp.einsum('jgph,gk->jghkp', bb.reshape(nchunk, gc, P, H), eye).reshape(nchunk, gc * H, gc * P)

    def c_blocks(cc):
        return jnp.einsum('jghp,gk->jgpkh', cc.reshape(nchunk, gc, H, P), eye).reshape(nchunk, gc * P, gc * H)

    bm = jnp.concatenate([b_blocks(bb_re), b_blocks(bb_im)], axis=2).astype(BF16)
    cm = jnp.concatenate([c_blocks(c_re.astype(F32)), -c_blocks(c_im.astype(F32))], axis=1).astype(BF16)
    are = ab_re.reshape(nchunk, 1, gc * P)
    aim = ab_im.reshape(nchunk, 1, gc * P)
    m = tl * B
    const3 = lambda l: (0, 0, 0)
    const2 = lambda l: (0, 0)
    return pl.pallas_call(
        functools.partial(_s5_kernel, tl=tl, nb=B, nchunk=nchunk),
        grid=(L // tl,),
        in_specs=[pl.BlockSpec((tl, B, W), lambda l: (l, 0, 0)),
                  pl.BlockSpec(bm.shape, const3), pl.BlockSpec(cm.shape, const3),
                  pl.BlockSpec(are.shape, const3), pl.BlockSpec(aim.shape, const3),
                  pl.BlockSpec((1, W), const2), pl.BlockSpec((W, W), const2), pl.BlockSpec((1, W), const2)],
        out_specs=pl.BlockSpec((tl, B, W), lambda l: (l, 0, 0)),
        out_shape=jax.ShapeDtypeStruct((L, B, W), F32),
        scratch_shapes=[pltpu.VMEM((m, 2 * S5_CHUNK_STATES), F32),
                        pltpu.VMEM((nchunk, B, 2 * S5_CHUNK_STATES), F32),
                        pltpu.VMEM((m, W), F32)],
        compiler_params=_params(("arbitrary",)),
        name="s5_mixer",
    )(u_tm, bm, cm, are, aim, d_skip.reshape(1, W).astype(F32), w_glu.astype(BF16),
      b_glu.reshape(1, W).astype(F32))


def _rel_bucket(n):
    n = jnp.maximum(n, 0)
    max_exact = REL_BUCKETS // 2
    nf = jnp.maximum(n, 1).astype(F32)
    large = max_exact + (jnp.log(nf / max_exact) / math.log(REL_MAX_DIST / max_exact)
                         * (REL_BUCKETS - max_exact)).astype(jnp.int32)
    large = jnp.minimum(large, REL_BUCKETS - 1)
    return jnp.where(n < max_exact, n, large)


def _moba_kernel(q_ref, k_ref, v_ref, bias_ref, o_ref, *, nblk, bs, dh, topk):
    i = pl.program_id(2)
    hpb = LANES // dh
    neg = -1e30
    kmean = jnp.mean(k_ref[0].reshape(nblk, bs, LANES), axis=1)
    row = lax.broadcasted_iota(jnp.int32, (bs, bs), 0)
    col = lax.broadcasted_iota(jnp.int32, (bs, bs), 1)
    causal = row >= col
    blk_id = lax.broadcasted_iota(jnp.int32, (nblk, bs), 0)
    lane_id = lax.broadcasted_iota(jnp.int32, (bs, nblk), 1)
    outs = []
    for hh in range(hpb):
        lo, hi = hh * dh, (hh + 1) * dh
        qh = q_ref[0][:, lo:hi] * (dh ** -0.5)
        gate = lax.dot_general(kmean[:, lo:hi], qh, (((1,), (1,)), ((), ())),
                               precision=HIGHEST, preferred_element_type=F32)
        cnt = jnp.zeros((nblk, bs), jnp.int32)
        for m in range(nblk):
            gm = gate[m:m + 1, :]
            beats = (gm > gate) | ((gm == gate) & (m < blk_id))
            cnt = cnt + jnp.where(beats & (m < i), 1, 0)
        sel = jnp.where((blk_id < i) & (cnt < topk), 1.0, 0.0).astype(F32).T
        qb = qh.astype(BF16)

        def scores(r0, bias_tile):
            kj = k_ref[0, pl.ds(r0, bs), lo:hi].astype(BF16)
            s = lax.dot_general(qb, kj, (((1,), (1,)), ((), ())), preferred_element_type=F32)
            return s + bias_tile

        r_own = pl.multiple_of(i * bs, bs)
        s = jnp.where(causal, scores(r_own, bias_ref[hh, 0]), neg)
        m0 = jnp.max(s, axis=1, keepdims=True)
        p = jnp.exp(s - m0)
        l0 = jnp.sum(p, axis=1, keepdims=True)
        acc0 = jnp.dot(p.astype(BF16), v_ref[0, pl.ds(r_own, bs), lo:hi].astype(BF16),
                       preferred_element_type=F32)

        def body(j, carry, hh=hh, lo=lo, hi=hi, sel=sel, scores=scores):
            m_i, l_i, acc = carry
            r0 = pl.multiple_of(j * bs, bs)
            picked = jnp.sum(jnp.where(lane_id == j, sel, 0.0), axis=1, keepdims=True) > 0.5
            s = jnp.where(picked, scores(r0, bias_ref[hh, i - j]), neg)
            m_n = jnp.maximum(m_i, jnp.max(s, axis=1, keepdims=True))
            alpha = jnp.exp(m_i - m_n)
            p = jnp.exp(s - m_n)
            l_n = alpha * l_i + jnp.sum(p, axis=1, keepdims=True)
            pv = jnp.dot(p.astype(BF16), v_ref[0, pl.ds(r0, bs), lo:hi].astype(BF16),
                         preferred_element_type=F32)
            return m_n, l_n, alpha * acc + pv

        _, l_f, acc_f = lax.fori_loop(0, i, body, (m0, l0, acc0))
        outs.append(acc_f / l_f)
    o_ref[0] = jnp.concatenate(outs, axis=1)


def _moba(qkv, rel_bias):
    B, L, W3 = qkv.shape
    W = W3 // 3
    dh, bs = MOBA_HEAD_DIM, MOBA_BLOCK
    H = W // dh
    nblk = L // bs
    hpb = LANES // dh
    ncol = W // LANES
    dist = jnp.arange(L, dtype=jnp.int32)
    by_dist = rel_bias.astype(F32).T[:, _rel_bucket(dist)]
    a = np.arange(bs)
    idx = np.clip(np.arange(nblk)[:, None, None] * bs + a[None, :, None] - a[None, None, :], 0, L - 1)
    bias_tiles = by_dist[:, idx]
    return pl.pallas_call(
        functools.partial(_moba_kernel, nblk=nblk, bs=bs, dh=dh, topk=min(MOBA_TOPK, nblk)),
        grid=(H // hpb, B, nblk),
        in_specs=[pl.BlockSpec((1, bs, LANES), lambda h, b, i: (b, i, h)),
                  pl.BlockSpec((1, L, LANES), lambda h, b, i: (b, 0, ncol + h)),
                  pl.BlockSpec((1, L, LANES), lambda h, b, i: (b, 0, 2 * ncol + h)),
                  pl.BlockSpec((hpb, nblk, bs, bs), lambda h, b, i: (h, 0, 0, 0))],
        out_specs=pl.BlockSpec((1, bs, LANES), lambda h, b, i: (b, i, h)),
        out_shape=jax.ShapeDtypeStruct((B, L, W), F32),
        compiler_params=_params(("parallel", "parallel", "arbitrary")),
        name="moba_attention",
    )(qkv, qkv, qkv, bias_tiles)


def _mix_ln_kernel(*refs, n_in):
    x_ref = refs[0]
    y_refs = refs[1:1 + n_in]
    w_refs = refs[1 + n_in:1 + 2 * n_in]
    g_ref, b_ref, wr_ref, br_ref, x1_ref, lg_ref = refs[1 + 2 * n_in:]
    mix = None
    for y_ref, w_ref in zip(y_refs, w_refs):
        y = y_ref[...]
        y = y.reshape(y.shape[-2], y.shape[-1]).astype(BF16)
        t = jnp.dot(y, w_ref[...], preferred_element_type=F32)
        mix = t if mix is None else mix + t
    xn = _layer_norm(DN_ALPHA * x_ref[0] + mix, g_ref[...], b_ref[...])
    x1_ref[0] = xn
    lg_ref[0] = jnp.dot(xn, wr_ref[...], precision=HIGHEST, preferred_element_type=F32) + br_ref[...]


def _mix_ln(x, ys, ws, g, b, wr, br, tl=256):
    B, L, D = x.shape
    E = wr.shape[1]
    const = lambda bb, l: (0, 0)
    in_specs = [pl.BlockSpec((1, tl, D), lambda bb, l: (bb, l, 0))]
    in_specs += [spec for _, spec in ys]
    in_specs += [pl.BlockSpec(w.shape, const) for w in ws]
    in_specs += [pl.BlockSpec((1, D), const), pl.BlockSpec((1, D), const),
                 pl.BlockSpec((D, E), const), pl.BlockSpec((1, E), const)]
    return pl.pallas_call(
        functools.partial(_mix_ln_kernel, n_in=len(ys)),
        grid=(B, L // tl),
        in_specs=in_specs,
        out_specs=[pl.BlockSpec((1, tl, D), lambda bb, l: (bb, l, 0)),
                   pl.BlockSpec((1, tl, E), lambda bb, l: (bb, l, 0))],
        out_shape=[jax.ShapeDtypeStruct((B, L, D), F32), jax.ShapeDtypeStruct((B, L, E), F32)],
        compiler_params=_params(("parallel", "parallel")),
        name="outproj_ln_router",
    )(x, *[a for a, _ in ys], *ws, g.reshape(1, D), b.reshape(1, D), wr.astype(F32), br.reshape(1, E))


def _expert_kernel(be_ref, nu_ref, xs_ref, wg_ref, wu_ref, wd_ref, bg_ref, bu_ref, bd_ref, pw_ref, ys_ref):
    blk = pl.program_id(0)

    @pl.when(blk < nu_ref[0])
    def _():
        x = xs_ref[...]
        g = jnp.dot(x, wg_ref[0], preferred_element_type=F32) + bg_ref[0]
        u = jnp.dot(x, wu_ref[0], preferred_element_type=F32) + bu_ref[0]
        g = jnp.minimum(g, SWIGLU_LIMIT)
        u = jnp.clip(u, -SWIGLU_LIMIT, SWIGLU_LIMIT)
        act = g * jax.nn.sigmoid(SWIGLU_ALPHA * g) * (u + 1.0)
        y = jnp.dot(act.astype(BF16), wd_ref[0], preferred_element_type=F32) + bd_ref[0]
        ys_ref[...] = y * pw_ref[...]

    @pl.when(blk >= nu_ref[0])
    def _():
        ys_ref[...] = jnp.zeros_like(ys_ref)


def _expert_ffn(xs, pad_w, blk_expert, n_used, wg, wu, wd, bg, bu, bd, bm):
    P, D = xs.shape
    F = wg.shape[2]
    nblk = P // bm
    wmap = lambda i, be, nu: (be[i], 0, 0)
    rmap = lambda i, be, nu: (i, 0)
    grid_spec = pltpu.PrefetchScalarGridSpec(
        num_scalar_prefetch=2,
        grid=(nblk,),
        in_specs=[pl.BlockSpec((bm, D), rmap),
                  pl.BlockSpec((1, D, F), wmap), pl.BlockSpec((1, D, F), wmap), pl.BlockSpec((1, F, D), wmap),
                  pl.BlockSpec((1, 1, F), wmap), pl.BlockSpec((1, 1, F), wmap), pl.BlockSpec((1, 1, D), wmap),
                  pl.BlockSpec((bm, 1), rmap)],
        out_specs=pl.BlockSpec((bm, D), rmap),
    )
    return pl.pallas_call(
        _expert_kernel,
        grid_spec=grid_spec,
        out_shape=jax.ShapeDtypeStruct((P, D), F32),
        compiler_params=_params(("arbitrary",)),
        name="moe_experts",
    )(blk_expert, n_used, xs, wg, wu, wd, bg, bu, bd, pad_w)


def _moe(x1, logits, w_gate_up, b_gate_up, w_down, b_down):
    T, D = x1.shape
    E = logits.shape[1]
    A = T * TOP_K
    bm = MOE_BLOCK
    top_v, top_i = lax.top_k(logits, TOP_K)
    top_w = jax.nn.softmax(top_v, axis=-1)
    e_flat = top_i.reshape(A)
    tok_flat = jnp.repeat(jnp.arange(T, dtype=jnp.int32), TOP_K)
    w_flat = top_w.reshape(A)
    order = jnp.argsort(e_flat)
    se, st, sw = e_flat[order], tok_flat[order], w_flat[order]
    counts = jnp.bincount(e_flat, length=E)
    padded = (counts + bm - 1) // bm * bm
    start_src = jnp.cumsum(counts) - counts
    start_dst = jnp.cumsum(padded) - padded
    dest = start_dst[se] + jnp.arange(A) - start_src[se]
    P = A + E * bm
    pad_tok = jnp.zeros((P,), jnp.int32).at[dest].set(st)
    pad_w = jnp.zeros((P,), F32).at[dest].set(sw)
    nblk = P // bm
    blk_end = jnp.cumsum(padded // bm)
    blk_expert = jnp.minimum(jnp.searchsorted(blk_end, jnp.arange(nblk), side='right'), E - 1).astype(jnp.int32)
    n_used = blk_end[-1:].astype(jnp.int32)
    xs = x1.astype(BF16)[pad_tok]
    wg = w_gate_up[:, :, 0::2].astype(BF16)
    wu = w_gate_up[:, :, 1::2].astype(BF16)
    bg = b_gate_up[:, None, 0::2].astype(F32)
    bu = b_gate_up[:, None, 1::2].astype(F32)
    ys = _expert_ffn(xs, pad_w.reshape(P, 1), blk_expert, n_used, wg, wu, w_down.astype(BF16),
                     bg, bu, b_down[:, None, :].astype(F32), bm)
    return jnp.zeros((T, D), F32).at[pad_tok].add(ys)


def _res_ln_kernel(x_ref, f_ref, g_ref, b_ref, o_ref):
    o_ref[...] = _layer_norm(DN_ALPHA * x_ref[...] + f_ref[...], g_ref[...], b_ref[...])


def _res_ln(x, f, g, b, tm=512):
    T, D = x.shape
    const = lambda i: (0, 0)
    row = lambda i: (i, 0)
    return pl.pallas_call(
        _res_ln_kernel,
        grid=(T // tm,),
        in_specs=[pl.BlockSpec((tm, D), row), pl.BlockSpec((tm, D), row),
                  pl.BlockSpec((1, D), const), pl.BlockSpec((1, D), const)],
        out_specs=pl.BlockSpec((tm, D), row),
        out_shape=jax.ShapeDtypeStruct((T, D), F32),
        compiler_params=_params(("parallel",)),
        name="residual_ln",
    )(x, f, g.reshape(1, D), b.reshape(1, D))


def _res_ln_inproj1_kernel(x_ref, f_ref, g_ref, b_ref, wqk_ref, wvg_ref, wgr_ref, w2_ref, b2_ref,
                           xo_ref, qkl_ref, vg_ref, *, kd):
    xn = _layer_norm(DN_ALPHA * x_ref[...] + f_ref[...], g_ref[...], b_ref[...])
    xo_ref[...] = xn
    xb = xn.astype(BF16)
    qkl_ref[:, :2 * kd] = jnp.dot(xb, wqk_ref[...], preferred_element_type=F32)
    vg_ref[...] = jnp.dot(xb, wvg_ref[...], preferred_element_type=F32)
    gr = jnp.dot(xb, wgr_ref[...], preferred_element_type=F32)
    z = jnp.dot(gr, w2_ref[...], precision=HIGHEST, preferred_element_type=F32) + b2_ref[...]
    log_sig = jnp.minimum(z, 0.0) - jnp.log1p(jnp.exp(-jnp.abs(z)))
    qkl_ref[:, 2 * kd:] = log_sig / GLA_GATE_TAU


def _res_ln_inproj1(x, f, g, b, w_in1, w_gate2, b_gate, kd, vd, tm=512):
    T, D = x.shape
    rank = w_gate2.shape[0]
    wqk = w_in1[:, :2 * kd].astype(BF16)
    wvg = w_in1[:, 2 * kd:2 * kd + 2 * vd].astype(BF16)
    wgr = w_in1[:, 2 * kd + 2 * vd:].astype(BF16)
    const = lambda i: (0, 0)
    row = lambda i: (i, 0)
    return pl.pallas_call(
        functools.partial(_res_ln_inproj1_kernel, kd=kd),
        grid=(T // tm,),
        in_specs=[pl.BlockSpec((tm, D), row), pl.BlockSpec((tm, D), row),
                  pl.BlockSpec((1, D), const), pl.BlockSpec((1, D), const),
                  pl.BlockSpec(wqk.shape, const), pl.BlockSpec(wvg.shape, const), pl.BlockSpec(wgr.shape, const),
                  pl.BlockSpec((rank, kd), const), pl.BlockSpec((1, kd), const)],
        out_specs=[pl.BlockSpec((tm, D), row), pl.BlockSpec((tm, 3 * kd), row), pl.BlockSpec((tm, 2 * vd), row)],
        out_shape=[jax.ShapeDtypeStruct((T, D), F32), jax.ShapeDtypeStruct((T, 3 * kd), F32),
                   jax.ShapeDtypeStruct((T, 2 * vd), F32)],
        compiler_params=_params(("parallel",)),
        name="residual_ln_inproj1",
    )(x, f, g.reshape(1, D), b.reshape(1, D), wqk, wvg, wgr, w_gate2.astype(F32), b_gate.reshape(1, kd))


def _gla_kernel(q_ref, k_ref, la_ref, v_ref, g_ref, nw_ref, o_ref, s_scr, *, tl, dk, dv):
    C = GLA_CHUNK

    @pl.when(pl.program_id(2) == 0)
    def _():
        s_scr[...] = jnp.zeros_like(s_scr)

    ri = lax.broadcasted_iota(jnp.int32, (C, C), 0)
    ci = lax.broadcasted_iota(jnp.int32, (C, C), 1)
    lower = ri >= ci
    tri = jnp.where(lower, 1.0, 0.0).astype(F32)
    for c in range(tl // C):
        rows = slice(c * C, (c + 1) * C)
        lg = la_ref[0, rows, :]
        bcum = jnp.dot(tri, lg, precision=HIGHEST, preferred_element_type=F32)
        btot = bcum[C - 1:C, :]
        q_in = q_ref[0, rows, :] * (dk ** -0.5) * jnp.exp(bcum)
        kc = k_ref[0, rows, :]
        k_in = kc * jnp.exp(-bcum)
        k_st = kc * jnp.exp(btot - bcum)
        vb = v_ref[0, rows, :].astype(BF16)
        qb = q_in.astype(BF16)
        att = lax.dot_general(qb, k_in.astype(BF16), (((1,), (1,)), ((), ())), preferred_element_type=F32)
        att = jnp.where(lower, att, 0.0)
        s_prev = s_scr[...]
        o = (jnp.dot(att.astype(BF16), vb, preferred_element_type=F32)
             + jnp.dot(qb, s_prev.astype(BF16), preferred_element_type=F32))
        kv = jnp.dot(k_st.T.astype(BF16), vb, preferred_element_type=F32)
        decay = jnp.exp(jnp.broadcast_to(btot, (8, dk))).T[:, 0:1]
        s_scr[...] = decay * s_prev + kv
        o = o * lax.rsqrt(jnp.mean(o * o, axis=-1, keepdims=True) + RMS_EPS) * nw_ref[...]
        gg = g_ref[0, rows, :]
        o_ref[0, rows, :] = o * (gg * jax.nn.sigmoid(gg))


def _gla(qkl, vg, norm_w, B, L, tl=256):
    T, kd3 = qkl.shape
    kd = kd3 // 3
    vd = vg.shape[1] // 2
    H = GLA_HEADS
    dk, dv = kd // H, vd // H
    qkl3 = qkl.reshape(B, L, kd3)
    vg3 = vg.reshape(B, L, 2 * vd)
    return pl.pallas_call(
        functools.partial(_gla_kernel, tl=tl, dk=dk, dv=dv),
        grid=(B, H, L // tl),
        in_specs=[pl.BlockSpec((1, tl, dk), lambda b, h, l: (b, l, h)),
                  pl.BlockSpec((1, tl, dk), lambda b, h, l: (b, l, H + h)),
                  pl.BlockSpec((1, tl, dk), lambda b, h, l: (b, l, 2 * H + h)),
                  pl.BlockSpec((1, tl, dv), lambda b, h, l: (b, l, h)),
                  pl.BlockSpec((1, tl, dv), lambda b, h, l: (b, l, H + h)),
                  pl.BlockSpec((1, dv), lambda b, h, l: (0, 0))],
        out_specs=pl.BlockSpec((1, tl, dv), lambda b, h, l: (b, l, h)),
        out_shape=jax.ShapeDtypeStruct((B, L, vd), F32),
        scratch_shapes=[pltpu.VMEM((dk, dv), F32)],
        compiler_params=_params(("parallel", "parallel", "arbitrary")),
        name="gla_mixer",
    )(qkl3, qkl3, qkl3, vg3, vg3, norm_w.reshape(1, dv).astype(F32))


def kernel(x, w_in0, s5_lam_re, s5_lam_im, s5_log_dt, s5_b_re, s5_b_im, s5_c_re, s5_c_im, s5_d, s5_w_glu,
           s5_b_glu, rel_bias, w_out0, w_in1, gla_w_gate2, gla_b_gate, gla_norm_w, w_out1, ln_mix_g, ln_mix_b,
           ln_ffn_g, ln_ffn_b, router_w, router_b, exp_w_gate_up, exp_b_gate_up, exp_w_down, exp_b_down):
    B, L, D = x.shape
    T = B * L
    s5w = s5_w_glu.shape[-1]
    kd = gla_w_gate2.shape[-1]
    vd = gla_norm_w.shape[-1] * GLA_HEADS

    u_tm, qkv = _inproj0(x, w_in0[0].astype(BF16), s5w)
    y_a = _s5_mixer(u_tm.reshape(L, B, s5w), s5_lam_re[0], s5_lam_im[0], s5_log_dt[0], s5_b_re[0], s5_b_im[0],
                    s5_c_re[0], s5_c_im[0], s5_d[0].reshape(-1), s5_w_glu[0], s5_b_glu[0])
    y_b = _moba(qkv, rel_bias)
    tl = 256
    w0 = w_out0[0].astype(BF16)
    ys = [(y_a.reshape(L, B * s5w), pl.BlockSpec((tl, s5w), lambda bb, l: (l, bb))),
          (y_b, pl.BlockSpec((1, tl, D - s5w), lambda bb, l: (bb, l, 0)))]
    x1, lg = _mix_ln(x, ys, [w0[:s5w], w0[s5w:]], ln_mix_g[0], ln_mix_b[0], router_w[0], router_b[0], tl=tl)
    x1 = x1.reshape(T, D)
    ffn = _moe(x1, lg.reshape(T, -1), exp_w_gate_up[0], exp_b_gate_up[0], exp_w_down[0], exp_b_down[0])

    x2, qkl, vg = _res_ln_inproj1(x1, ffn, ln_ffn_g[0], ln_ffn_b[0], w_in1[0], gla_w_gate2[0], gla_b_gate[0], kd, vd)
    y_c = _gla(qkl, vg, gla_norm_w[0], B, L)
    ys = [(y_c, pl.BlockSpec((1, tl, vd), lambda bb, l: (bb, l, 0)))]
    x3, lg = _mix_ln(x2.reshape(B, L, D), ys, [w_out1[0].astype(BF16)], ln_mix_g[1], ln_mix_b[1],
                     router_w[1], router_b[1], tl=tl)
    x3 = x3.reshape(T, D)
    ffn = _moe(x3, lg.reshape(T, -1), exp_w_gate_up[1], exp_b_gate_up[1], exp_w_down[1], exp_b_down[1])
    out = _res_ln(x3, ffn, ln_ffn_g[1], ln_ffn_b[1])
    return out.reshape(B, L, D)
```

```python
import functools
import math

import jax
import jax.numpy as jnp
import numpy as np
from jax import lax
from jax.experimental import pallas as pl
from jax.experimental.pallas import tpu as pltpu

F32 = jnp.float32
BF16 = jnp.bfloat16
HIGHEST = lax.Precision.HIGHEST

DEPTH = 2
S5_GROUP = 16
S5_STATE = 64
MOBA_HEAD_DIM = 64
MOBA_BLOCK = 256
MOBA_TOPK = 3
REL_BUCKETS = 32
REL_MAX_DIST = 2048
GLA_HEADS = 4
GLA_GATE_TAU = 16.0
GLA_CHUNK = 64
N_EXPERTS = 32
TOP_K = 4
SWIGLU_LIMIT = 7.0
SWIGLU_ALPHA = 1.702
MOE_BLOCK = 256
DN_ALPHA = (2 * DEPTH) ** 0.25
LN_EPS = 1e-5
RMS_EPS = 1e-5

V7X_VMEM_LIMIT_BYTES = 56 * 1024 * 1024
LANES = 128

S5_GROUPS_PER_CHUNK = LANES // S5_GROUP
S5_CHUNK_STATES = S5_GROUPS_PER_CHUNK * S5_STATE


def _params(sem):
    return pltpu.CompilerParams(dimension_semantics=sem, vmem_limit_bytes=V7X_VMEM_LIMIT_BYTES)


def _layer_norm(r, g, b):
    mu = jnp.mean(r, axis=-1, keepdims=True)
    c = r - mu
    var = jnp.mean(c * c, axis=-1, keepdims=True)
    return c * lax.rsqrt(var + LN_EPS) * g + b


def _inproj0_kernel(x_ref, w_ref, u_ref, qkv_ref, *, s5w):
    h = jnp.dot(x_ref[0].astype(BF16), w_ref[...], preferred_element_type=F32)
    u_ref[...] = h[:, :s5w]
    qkv_ref[0] = h[:, s5w:]


def _inproj0(x, w, s5w, tl=512):
    B, L, D = x.shape
    n = w.shape[1]
    return pl.pallas_call(
        functools.partial(_inproj0_kernel, s5w=s5w),
        grid=(B, L // tl),
        in_specs=[pl.BlockSpec((1, tl, D), lambda b, l: (b, l, 0)),
                  pl.BlockSpec((D, n), lambda b, l: (0, 0))],
        out_specs=[pl.BlockSpec((tl, s5w), lambda b, l: (l, b)),
                   pl.BlockSpec((1, tl, n - s5w), lambda b, l: (b, l, 0))],
        out_shape=[jax.ShapeDtypeStruct((L, B * s5w), F32),
                   jax.ShapeDtypeStruct((B, L, n - s5w), F32)],
        compiler_params=_params(("parallel", "parallel")),
        name="inproj0",
    )(x, w)


def _s5_discretize(lam_re, lam_im, log_dt, b_re, b_im):
    dt = jnp.exp(log_dt.astype(F32))[:, None]
    lr, li = lam_re.astype(F32), lam_im.astype(F32)
    mag = jnp.exp(lr * dt)
    ab_re, ab_im = mag * jnp.cos(li * dt), mag * jnp.sin(li * dt)
    er, ei = ab_re - 1.0, ab_im
    den = lr * lr + li * li
    q_re = (er * lr + ei * li) / den
    q_im = (ei * lr - er * li) / den
    br_, bi_ = b_re.astype(F32), b_im.astype(F32)
    bb_re = q_re[..., None] * br_ - q_im[..., None] * bi_
    bb_im = q_re[..., None] * bi_ + q_im[..., None] * br_
    return ab_re, ab_im, bb_re, bb_im


def _s5_kernel(u_ref, bm_ref, cm_ref, are_ref, aim_ref, d_ref, wg_ref, bg_ref, y_ref,
               s_scr, st_scr, z_scr, *, tl, nb, nchunk):
    ns = S5_CHUNK_STATES

    @pl.when(pl.program_id(0) == 0)
    def _():
        st_scr[...] = jnp.zeros_like(st_scr)

    u = u_ref[...].reshape(tl * nb, nchunk * LANES)
    for j in range(nchunk):
        uj = u[:, j * LANES:(j + 1) * LANES]
        s_scr[...] = jnp.dot(uj.astype(BF16), bm_ref[j], preferred_element_type=F32)
        ar = jnp.broadcast_to(are_ref[j], (nb, ns))
        ai = jnp.broadcast_to(aim_ref[j], (nb, ns))

        def step(t, carry, ar=ar, ai=ai):
            sre, sim = carry
            r0 = pl.multiple_of(t * nb, nb)
            nre = ar * sre - ai * sim + s_scr[pl.ds(r0, nb), 0:ns]
            nim = ar * sim + ai * sre + s_scr[pl.ds(r0, nb), ns:2 * ns]
            s_scr[pl.ds(r0, nb), 0:ns] = nre
            s_scr[pl.ds(r0, nb), ns:2 * ns] = nim
            return nre, nim

        sre, sim = lax.fori_loop(0, tl, step, (st_scr[j, :, 0:ns], st_scr[j, :, ns:2 * ns]), unroll=2)
        st_scr[j, :, 0:ns] = sre
        st_scr[j, :, ns:2 * ns] = sim
        yj = jnp.dot(s_scr[...].astype(BF16), cm_ref[j], preferred_element_type=F32)
        z_scr[:, j * LANES:(j + 1) * LANES] = yj + uj * d_ref[:, j * LANES:(j + 1) * LANES]
    z = jax.nn.gelu(z_scr[...])
    gate = jax.nn.sigmoid(jnp.dot(z.astype(BF16), wg_ref[...], preferred_element_type=F32) + bg_ref[...])
    y_ref[...] = (z * gate).reshape(tl, nb, nchunk * LANES)


def _s5_mixer(u_tm, lam_re, lam_im, log_dt, b_re, b_im, c_re, c_im, d_skip, w_glu, b_glu, tl=32):
    L, B, W = u_tm.shape
    G, P, H = lam_re.shape[0], S5_STATE, S5_GROUP
    gc = S5_GROUPS_PER_CHUNK
    nchunk = G // gc
    ab_re, ab_im, bb_re, bb_im = _s5_discretize(lam_re, lam_im, log_dt, b_re, b_im)
    eye = jnp.eye(gc, dtype=F32)

    def b_blocks(bb):
        return jnp.einsum('jgph,gk->jghkp', bb.reshape(nchunk, gc, P, H), eye).reshape(nchunk, gc * H, gc * P)

    def c_blocks(cc):
        return jnp.einsum('jghp,gk->jgpkh', cc.reshape(nchunk, gc, H, P), eye).reshape(nchunk, gc * P, gc * H)

    bm = jnp.concatenate([b_blocks(bb_re), b_blocks(bb_im)], axis=2).astype(BF16)
    cm = jnp.concatenate([c_blocks(c_re.astype(F32)), -c_blocks(c_im.astype(F32))], axis=1).astype(BF16)
    are = ab_re.reshape(nchunk, 1, gc * P)
    aim = ab_im.reshape(nchunk, 1, gc * P)
    m = tl * B
    const3 = lambda l: (0, 0, 0)
    const2 = lambda l: (0, 0)
    return pl.pallas_call(
        functools.partial(_s5_kernel, tl=tl, nb=B, nchunk=nchunk),
        grid=(L // tl,),
        in_specs=[pl.BlockSpec((tl, B, W), lambda l: (l, 0, 0)),
                  pl.BlockSpec(bm.shape, const3), pl.BlockSpec(cm.shape, const3),
                  pl.BlockSpec(are.shape, const3), pl.BlockSpec(aim.shape, const3),
                  pl.BlockSpec((1, W), const2), pl.BlockSpec((W, W), const2), pl.BlockSpec((1, W), const2)],
        out_specs=pl.BlockSpec((tl, B, W), lambda l: (l, 0, 0)),
        out_shape=jax.ShapeDtypeStruct((L, B, W), F32),
        scratch_shapes=[pltpu.VMEM((m, 2 * S5_CHUNK_STATES), F32),
                        pltpu.VMEM((nchunk, B, 2 * S5_CHUNK_STATES), F32),
                        pltpu.VMEM((m, W), F32)],
        compiler_params=_params(("arbitrary",)),
        name="s5_mixer",
    )(u_tm, bm, cm, are, aim, d_skip.reshape(1, W).astype(F32), w_glu.astype(BF16),
      b_glu.reshape(1, W).astype(F32))


def _rel_bucket(n):
    n = jnp.maximum(n, 0)
    max_exact = REL_BUCKETS // 2
    nf = jnp.maximum(n, 1).astype(F32)
    large = max_exact + (jnp.log(nf / max_exact) / math.log(REL_MAX_DIST / max_exact)
                         * (REL_BUCKETS - max_exact)).astype(jnp.int32)
    large = jnp.minimum(large, REL_BUCKETS - 1)
    return jnp.where(n < max_exact, n, large)


def _moba_kernel(q_ref, k_ref, v_ref, bias_ref, o_ref, *, nblk, bs, dh, topk):
    i = pl.program_id(2)
    hpb = LANES // dh
    neg = -1e30
    kmean = jnp.mean(k_ref[0].reshape(nblk, bs, LANES), axis=1)
    row = lax.broadcasted_iota(jnp.int32, (bs, bs), 0)
    col = lax.broadcasted_iota(jnp.int32, (bs, bs), 1)
    causal = row >= col
    blk_id = lax.broadcasted_iota(jnp.int32, (nblk, bs), 0)
    lane_id = lax.broadcasted_iota(jnp.int32, (bs, nblk), 1)
    outs = []
    for hh in range(hpb):
        lo, hi = hh * dh, (hh + 1) * dh
        qh = q_ref[0][:, lo:hi] * (dh ** -0.5)
        gate = lax.dot_general(kmean[:, lo:hi], qh, (((1,), (1,)), ((), ())),
                               precision=HIGHEST, preferred_element_type=F32)
        cnt = jnp.zeros((nblk, bs), jnp.int32)
        for m in range(nblk):
            gm = gate[m:m + 1, :]
            beats = (gm > gate) | ((gm == gate) & (m < blk_id))
            cnt = cnt + jnp.where(beats & (m < i), 1, 0)
        sel = jnp.where((blk_id < i) & (cnt < topk), 1.0, 0.0).astype(F32).T
        qb = qh.astype(BF16)

        def scores(r0, bias_tile):
            kj = k_ref[0, pl.ds(r0, bs), lo:hi].astype(BF16)
            s = lax.dot_general(qb, kj, (((1,), (1,)), ((), ())), preferred_element_type=F32)
            return s + bias_tile

        r_own = pl.multiple_of(i * bs, bs)
        s = jnp.where(causal, scores(r_own, bias_ref[hh, 0]), neg)
        m0 = jnp.max(s, axis=1, keepdims=True)
        p = jnp.exp(s - m0)
        l0 = jnp.sum(p, axis=1, keepdims=True)
        acc0 = jnp.dot(p.astype(BF16), v_ref[0, pl.ds(r_own, bs), lo:hi].astype(BF16),
                       preferred_element_type=F32)

        def body(j, carry, hh=hh, lo=lo, hi=hi, sel=sel, scores=scores):
            m_i, l_i, acc = carry
            r0 = pl.multiple_of(j * bs, bs)
            picked = jnp.sum(jnp.where(lane_id == j, sel, 0.0), axis=1, keepdims=True) > 0.5
            s = jnp.where(picked, scores(r0, bias_ref[hh, i - j]), neg)
            m_n = jnp.maximum(m_i, jnp.max(s, axis=1, keepdims=True))
            alpha = jnp.exp(m_i - m_n)
            p = jnp.exp(s - m_n)
            l_n = alpha * l_i + jnp.sum(p, axis=1, keepdims=True)
            pv = jnp.dot(p.astype(BF16), v_ref[0, pl.ds(r0, bs), lo:hi].astype(BF16),
                         preferred_element_type=F32)
            return m_n, l_n, alpha * acc + pv

        _, l_f, acc_f = lax.fori_loop(0, i, body, (m0, l0, acc0))
        outs.append(acc_f / l_f)
    o_ref[0] = jnp.concatenate(outs, axis=1)


def _moba(qkv, rel_bias):
    B, L, W3 = qkv.shape
    W = W3 // 3
    dh, bs = MOBA_HEAD_DIM, MOBA_BLOCK
    H = W // dh
    nblk = L // bs
    hpb = LANES // dh
    ncol = W // LANES
    dist = jnp.arange(L, dtype=jnp.int32)
    by_dist = rel_bias.astype(F32).T[:, _rel_bucket(dist)]
    a = np.arange(bs)
    idx = np.clip(np.arange(nblk)[:, None, None] * bs + a[None, :, None] - a[None, None, :], 0, L - 1)
    bias_tiles = by_dist[:, idx]
    return pl.pallas_call(
        functools.partial(_moba_kernel, nblk=nblk, bs=bs, dh=dh, topk=min(MOBA_TOPK, nblk)),
        grid=(H // hpb, B, nblk),
        in_specs=[pl.BlockSpec((1, bs, LANES), lambda h, b, i: (b, i, h)),
                  pl.BlockSpec((1, L, LANES), lambda h, b, i: (b, 0, ncol + h)),
                  pl.BlockSpec((1, L, LANES), lambda h, b, i: (b, 0, 2 * ncol + h)),
                  pl.BlockSpec((hpb, nblk, bs, bs), lambda h, b, i: (h, 0, 0, 0))],
        out_specs=pl.BlockSpec((1, bs, LANES), lambda h, b, i: (b, i, h)),
        out_shape=jax.ShapeDtypeStruct((B, L, W), F32),
        compiler_params=_params(("parallel", "parallel", "arbitrary")),
        name="moba_attention",
    )(qkv, qkv, qkv, bias_tiles)


def _mix_ln_kernel(*refs, n_in):
    x_ref = refs[0]
    y_refs = refs[1:1 + n_in]
    w_refs = refs[1 + n_in:1 + 2 * n_in]
    g_ref, b_ref, wr_ref, br_ref, x1_ref, lg_ref = refs[1 + 2 * n_in:]
    mix = None
    for y_ref, w_ref in zip(y_refs, w_refs):
        y = y_ref[...]
        y = y.reshape(y.shape[-2], y.shape[-1]).astype(BF16)
        t = jnp.dot(y, w_ref[...], preferred_element_type=F32)
        mix = t if mix is None else mix + t
    xn = _layer_norm(DN_ALPHA * x_ref[0] + mix, g_ref[...], b_ref[...])
    x1_ref[0] = xn
    lg_ref[0] = jnp.dot(xn, wr_ref[...], precision=HIGHEST, preferred_element_type=F32) + br_ref[...]


def _mix_ln(x, ys, ws, g, b, wr, br, tl=256):
    B, L, D = x.shape
    E = wr.shape[1]
    const = lambda bb, l: (0, 0)
    in_specs = [pl.BlockSpec((1, tl, D), lambda bb, l: (bb, l, 0))]
    in_specs += [spec for _, spec in ys]
    in_specs += [pl.BlockSpec(w.shape, const) for w in ws]
    in_specs += [pl.BlockSpec((1, D), const), pl.BlockSpec((1, D), const),
                 pl.BlockSpec((D, E), const), pl.BlockSpec((1, E), const)]
    return pl.pallas_call(
        functools.partial(_mix_ln_kernel, n_in=len(ys)),
        grid=(B, L // tl),
        in_specs=in_specs,
        out_specs=[pl.BlockSpec((1, tl, D), lambda bb, l: (bb, l, 0)),
                   pl.BlockSpec((1, tl, E), lambda bb, l: (bb, l, 0))],
        out_shape=[jax.ShapeDtypeStruct((B, L, D), F32), jax.ShapeDtypeStruct((B, L, E), F32)],
        compiler_params=_params(("parallel", "parallel")),
        name="outproj_ln_router",
    )(x, *[a for a, _ in ys], *ws, g.reshape(1, D), b.reshape(1, D), wr.astype(F32), br.reshape(1, E))


def _deinterleave_kernel(w_ref, p_ref, o_ref, *, ff):
    w = w_ref[0].astype(BF16)
    for c in range(2 * ff // (2 * LANES)):
        t = jnp.dot(w[:, 2 * LANES * c:2 * LANES * (c + 1)], p_ref[...], preferred_element_type=F32)
        o_ref[0, :, LANES * c:LANES * (c + 1)] = t[:, :LANES].astype(BF16)
        o_ref[0, :, ff + LANES * c:ff + LANES * (c + 1)] = t[:, LANES:].astype(BF16)


def _deinterleave_gate_up(w_gate_up, tk=512):
    E, D, F2 = w_gate_up.shape
    src = np.arange(2 * LANES)
    dst = np.where(src % 2 == 0, src // 2, LANES + src // 2)
    perm = np.zeros((2 * LANES, 2 * LANES), np.float32)
    perm[src, dst] = 1.0
    return pl.pallas_call(
        functools.partial(_deinterleave_kernel, ff=F2 // 2),
        grid=(E, D // tk),
        in_specs=[pl.BlockSpec((1, tk, F2), lambda e, k: (e, k, 0)),
                  pl.BlockSpec((2 * LANES, 2 * LANES), lambda e, k: (0, 0))],
        out_specs=pl.BlockSpec((1, tk, F2), lambda e, k: (e, k, 0)),
        out_shape=jax.ShapeDtypeStruct((E, D, F2), BF16),
        compiler_params=_params(("parallel", "parallel")),
        name="deinterleave_gate_up",
    )(w_gate_up, jnp.asarray(perm, BF16))


def _expert_kernel(be_ref, nu_ref, xs_ref, wgu_ref, wd_ref, bgu_ref, bd_ref, pw_ref, ys_ref):
    blk = pl.program_id(0)
    ff = wd_ref.shape[1]

    @pl.when(blk < nu_ref[0])
    def _():
        x = xs_ref[...]
        h = jnp.dot(x, wgu_ref[0], preferred_element_type=F32) + bgu_ref[0]
        g = h[:, :ff]
        u = h[:, ff:]
        g = jnp.minimum(g, SWIGLU_LIMIT)
        u = jnp.clip(u, -SWIGLU_LIMIT, SWIGLU_LIMIT)
        act = g * jax.nn.sigmoid(SWIGLU_ALPHA * g) * (u + 1.0)
        y = jnp.dot(act.astype(BF16), wd_ref[0], preferred_element_type=F32) + bd_ref[0]
        ys_ref[...] = y * pw_ref[...]

    @pl.when(blk >= nu_ref[0])
    def _():
        ys_ref[...] = jnp.zeros_like(ys_ref)


def _expert_ffn(xs, pad_w, blk_expert, n_used, wgu, wd, bgu, bd, bm):
    P, D = xs.shape
    F = wd.shape[1]
    nblk = P // bm
    wmap = lambda i, be, nu: (be[i], 0, 0)
    rmap = lambda i, be, nu: (i, 0)
    grid_spec = pltpu.PrefetchScalarGridSpec(
        num_scalar_prefetch=2,
        grid=(nblk,),
        in_specs=[pl.BlockSpec((bm, D), rmap),
                  pl.BlockSpec((1, D, 2 * F), wmap), pl.BlockSpec((1, F, D), wmap),
                  pl.BlockSpec((1, 1, 2 * F), wmap), pl.BlockSpec((1, 1, D), wmap),
                  pl.BlockSpec((bm, 1), rmap)],
        out_specs=pl.BlockSpec((bm, D), rmap),
    )
    return pl.pallas_call(
        _expert_kernel,
        grid_spec=grid_spec,
        out_shape=jax.ShapeDtypeStruct((P, D), F32),
        compiler_params=_params(("arbitrary",)),
        name="moe_experts",
    )(blk_expert, n_used, xs, wgu, wd, bgu, bd, pad_w)


def _moe(x1, logits, w_gate_up, b_gate_up, w_down, b_down):
    T, D = x1.shape
    E = logits.shape[1]
    A = T * TOP_K
    bm = MOE_BLOCK
    top_v, top_i = lax.top_k(logits, TOP_K)
    top_w = jax.nn.softmax(top_v, axis=-1)
    e_flat = top_i.reshape(A)
    tok_flat = jnp.repeat(jnp.arange(T, dtype=jnp.int32), TOP_K)
    w_flat = top_w.reshape(A)
    order = jnp.argsort(e_flat)
    se, st, sw = e_flat[order], tok_flat[order], w_flat[order]
    counts = jnp.bincount(e_flat, length=E)
    padded = (counts + bm - 1) // bm * bm
    start_src = jnp.cumsum(counts) - counts
    start_dst = jnp.cumsum(padded) - padded
    dest = start_dst[se] + jnp.arange(A) - start_src[se]
    P = A + E * bm
    pad_tok = jnp.zeros((P,), jnp.int32).at[dest].set(st)
    pad_w = jnp.zeros((P,), F32).at[dest].set(sw)
    nblk = P // bm
    blk_end = jnp.cumsum(padded // bm)
    blk_expert = jnp.minimum(jnp.sum(blk_end[None, :] <= jnp.arange(nblk)[:, None], axis=1), E - 1).astype(jnp.int32)
    n_used = blk_end[-1:].astype(jnp.int32)
    xs = x1.astype(BF16)[pad_tok]
    wgu = _deinterleave_gate_up(w_gate_up)
    bgu = jnp.concatenate([b_gate_up[:, 0::2], b_gate_up[:, 1::2]], axis=-1)[:, None, :].astype(F32)
    ys = _expert_ffn(xs, pad_w.reshape(P, 1), blk_expert, n_used, wgu, w_down.astype(BF16),
                     bgu, b_down[:, None, :].astype(F32), bm)
    return jnp.zeros((T, D), F32).at[pad_tok].add(ys)


def _res_ln_kernel(x_ref, f_ref, g_ref, b_ref, o_ref):
    o_ref[...] = _layer_norm(DN_ALPHA * x_ref[...] + f_ref[...], g_ref[...], b_ref[...])


def _res_ln(x, f, g, b, tm=512):
    T, D = x.shape
    const = lambda i: (0, 0)
    row = lambda i: (i, 0)
    return pl.pallas_call(
        _res_ln_kernel,
        grid=(T // tm,),
        in_specs=[pl.BlockSpec((tm, D), row), pl.BlockSpec((tm, D), row),
                  pl.BlockSpec((1, D), const), pl.BlockSpec((1, D), const)],
        out_specs=pl.BlockSpec((tm, D), row),
        out_shape=jax.ShapeDtypeStruct((T, D), F32),
        compiler_params=_params(("parallel",)),
        name="residual_ln",
    )(x, f, g.reshape(1, D), b.reshape(1, D))


def _res_ln_inproj1_kernel(x_ref, f_ref, g_ref, b_ref, wqk_ref, wvg_ref, wgr_ref, w2_ref, b2_ref,
                           xo_ref, qkl_ref, vg_ref, *, kd):
    xn = _layer_norm(DN_ALPHA * x_ref[...] + f_ref[...], g_ref[...], b_ref[...])
    xo_ref[...] = xn
    xb = xn.astype(BF16)
    qkl_ref[:, :2 * kd] = jnp.dot(xb, wqk_ref[...], preferred_element_type=F32)
    vg_ref[...] = jnp.dot(xb, wvg_ref[...], preferred_element_type=F32)
    gr = jnp.dot(xb, wgr_ref[...], preferred_element_type=F32)
    z = jnp.dot(gr, w2_ref[...], precision=HIGHEST, preferred_element_type=F32) + b2_ref[...]
    log_sig = jnp.minimum(z, 0.0) - jnp.log1p(jnp.exp(-jnp.abs(z)))
    qkl_ref[:, 2 * kd:] = log_sig / GLA_GATE_TAU


def _res_ln_inproj1(x, f, g, b, w_in1, w_gate2, b_gate, kd, vd, tm=512):
    T, D = x.shape
    rank = w_gate2.shape[0]
    wqk = w_in1[:, :2 * kd].astype(BF16)
    wvg = w_in1[:, 2 * kd:2 * kd + 2 * vd].astype(BF16)
    wgr = w_in1[:, 2 * kd + 2 * vd:].astype(BF16)
    const = lambda i: (0, 0)
    row = lambda i: (i, 0)
    return pl.pallas_call(
        functools.partial(_res_ln_inproj1_kernel, kd=kd),
        grid=(T // tm,),
        in_specs=[pl.BlockSpec((tm, D), row), pl.BlockSpec((tm, D), row),
                  pl.BlockSpec((1, D), const), pl.BlockSpec((1, D), const),
                  pl.BlockSpec(wqk.shape, const), pl.BlockSpec(wvg.shape, const), pl.BlockSpec(wgr.shape, const),
                  pl.BlockSpec((rank, kd), const), pl.BlockSpec((1, kd), const)],
        out_specs=[pl.BlockSpec((tm, D), row), pl.BlockSpec((tm, 3 * kd), row), pl.BlockSpec((tm, 2 * vd), row)],
        out_shape=[jax.ShapeDtypeStruct((T, D), F32), jax.ShapeDtypeStruct((T, 3 * kd), F32),
                   jax.ShapeDtypeStruct((T, 2 * vd), F32)],
        compiler_params=_params(("parallel",)),
        name="residual_ln_inproj1",
    )(x, f, g.reshape(1, D), b.reshape(1, D), wqk, wvg, wgr, w_gate2.astype(F32), b_gate.reshape(1, kd))


def _gla_kernel(q_ref, k_ref, la_ref, v_ref, g_ref, nw_ref, o_ref, s_scr, *, tl, dk, dv):
    C = GLA_CHUNK

    @pl.when(pl.program_id(2) == 0)
    def _():
        s_scr[...] = jnp.zeros_like(s_scr)

    ri = lax.broadcasted_iota(jnp.int32, (C, C), 0)
    ci = lax.broadcasted_iota(jnp.int32, (C, C), 1)
    lower = ri >= ci
    tri = jnp.where(lower, 1.0, 0.0).astype(F32)
    for c in range(tl // C):
        rows = slice(c * C, (c + 1) * C)
        lg = la_ref[0, rows, :]
        bcum = jnp.dot(tri, lg, precision=HIGHEST, preferred_element_type=F32)
        btot = bcum[C - 1:C, :]
        q_in = q_ref[0, rows, :] * (dk ** -0.5) * jnp.exp(bcum)
        kc = k_ref[0, rows, :]
        k_in = kc * jnp.exp(-bcum)
        k_st = kc * jnp.exp(btot - bcum)
        vb = v_ref[0, rows, :].astype(BF16)
        qb = q_in.astype(BF16)
        att = lax.dot_general(qb, k_in.astype(BF16), (((1,), (1,)), ((), ())), preferred_element_type=F32)
        att = jnp.where(lower, att, 0.0)
        s_prev = s_scr[...]
        o = (jnp.dot(att.astype(BF16), vb, preferred_element_type=F32)
             + jnp.dot(qb, s_prev.astype(BF16), preferred_element_type=F32))
        kv = jnp.dot(k_st.T.astype(BF16), vb, preferred_element_type=F32)
        decay = jnp.exp(jnp.broadcast_to(btot, (8, dk))).T[:, 0:1]
        s_scr[...] = decay * s_prev + kv
        o = o * lax.rsqrt(jnp.mean(o * o, axis=-1, keepdims=True) + RMS_EPS) * nw_ref[...]
        gg = g_ref[0, rows, :]
        o_ref[0, rows, :] = o * (gg * jax.nn.sigmoid(gg))


def _gla(qkl, vg, norm_w, B, L, tl=256):
    T, kd3 = qkl.shape
    kd = kd3 // 3
    vd = vg.shape[1] // 2
    H = GLA_HEADS
    dk, dv = kd // H, vd // H
    qkl3 = qkl.reshape(B, L, kd3)
    vg3 = vg.reshape(B, L, 2 * vd)
    return pl.pallas_call(
        functools.partial(_gla_kernel, tl=tl, dk=dk, dv=dv),
        grid=(B, H, L // tl),
        in_specs=[pl.BlockSpec((1, tl, dk), lambda b, h, l: (b, l, h)),
                  pl.BlockSpec((1, tl, dk), lambda b, h, l: (b, l, H + h)),
                  pl.BlockSpec((1, tl, dk), lambda b, h, l: (b, l, 2 * H + h)),
                  pl.BlockSpec((1, tl, dv), lambda b, h, l: (b, l, h)),
                  pl.BlockSpec((1, tl, dv), lambda b, h, l: (b, l, H + h)),
                  pl.BlockSpec((1, dv), lambda b, h, l: (0, 0))],
        out_specs=pl.BlockSpec((1, tl, dv), lambda b, h, l: (b, l, h)),
        out_shape=jax.ShapeDtypeStruct((B, L, vd), F32),
        scratch_shapes=[pltpu.VMEM((dk, dv), F32)],
        compiler_params=_params(("parallel", "parallel", "arbitrary")),
        name="gla_mixer",
    )(qkl3, qkl3, qkl3, vg3, vg3, norm_w.reshape(1, dv).astype(F32))


def kernel(x, w_in0, s5_lam_re, s5_lam_im, s5_log_dt, s5_b_re, s5_b_im, s5_c_re, s5_c_im, s5_d, s5_w_glu,
           s5_b_glu, rel_bias, w_out0, w_in1, gla_w_gate2, gla_b_gate, gla_norm_w, w_out1, ln_mix_g, ln_mix_b,
           ln_ffn_g, ln_ffn_b, router_w, router_b, exp_w_gate_up, exp_b_gate_up, exp_w_down, exp_b_down):
    B, L, D = x.shape
    T = B * L
    s5w = s5_w_glu.shape[-1]
    kd = gla_w_gate2.shape[-1]
    vd = gla_norm_w.shape[-1] * GLA_HEADS

    u_tm, qkv = _inproj0(x, w_in0[0].astype(BF16), s5w)
    y_a = _s5_mixer(u_tm.reshape(L, B, s5w), s5_lam_re[0], s5_lam_im[0], s5_log_dt[0], s5_b_re[0], s5_b_im[0],
                    s5_c_re[0], s5_c_im[0], s5_d[0].reshape(-1), s5_w_glu[0], s5_b_glu[0])
    y_b = _moba(qkv, rel_bias)
    tl = 256
    w0 = w_out0[0].astype(BF16)
    ys = [(y_a.reshape(L, B * s5w), pl.BlockSpec((tl, s5w), lambda bb, l: (l, bb))),
          (y_b, pl.BlockSpec((1, tl, D - s5w), lambda bb, l: (bb, l, 0)))]
    x1, lg = _mix_ln(x, ys, [w0[:s5w], w0[s5w:]], ln_mix_g[0], ln_mix_b[0], router_w[0], router_b[0], tl=tl)
    x1 = x1.reshape(T, D)
    ffn = _moe(x1, lg.reshape(T, -1), exp_w_gate_up[0], exp_b_gate_up[0], exp_w_down[0], exp_b_down[0])

    x2, qkl, vg = _res_ln_inproj1(x1, ffn, ln_ffn_g[0], ln_ffn_b[0], w_in1[0], gla_w_gate2[0], gla_b_gate[0], kd, vd)
    y_c = _gla(qkl, vg, gla_norm_w[0], B, L)
    ys = [(y_c, pl.BlockSpec((1, tl, vd), lambda bb, l: (bb, l, 0)))]
    x3, lg = _mix_ln(x2.reshape(B, L, D), ys, [w_out1[0].astype(BF16)], ln_mix_g[1], ln_mix_b[1],
                     router_w[1], router_b[1], tl=tl)
    x3 = x3.reshape(T, D)
    ffn = _moe(x3, lg.reshape(T, -1), exp_w_gate_up[1], exp_b_gate_up[1], exp_w_down[1], exp_b_down[1])
    out = _res_ln(x3, ffn, ln_ffn_g[1], ln_ffn_b[1])
    return out.reshape(B, L, D)
```

```python
import functools
import math

import jax
import jax.numpy as jnp
import numpy as np
from jax import lax
from jax.experimental import pallas as pl
from jax.experimental.pallas import tpu as pltpu

F32 = jnp.float32
BF16 = jnp.bfloat16
HIGHEST = lax.Precision.HIGHEST

DEPTH = 2
S5_GROUP = 16
S5_STATE = 64
MOBA_HEAD_DIM = 64
MOBA_BLOCK = 256
MOBA_TOPK = 3
REL_BUCKETS = 32
REL_MAX_DIST = 2048
GLA_HEADS = 4
GLA_GATE_TAU = 16.0
GLA_CHUNK = 64
N_EXPERTS = 32
TOP_K = 4
SWIGLU_LIMIT = 7.0
SWIGLU_ALPHA = 1.702
MOE_BLOCK = 512
MOE_TILE = 256
MOE_CHUNK = 32
SUBLANES = 8
MOE_REMAINDERS = (16, 8)
DN_ALPHA = (2 * DEPTH) ** 0.25
LN_EPS = 1e-5
RMS_EPS = 1e-5

V7X_VMEM_LIMIT_BYTES = 56 * 1024 * 1024
LANES = 128

S5_GROUPS_PER_CHUNK = LANES // S5_GROUP
S5_CHUNK_STATES = S5_GROUPS_PER_CHUNK * S5_STATE


def _params(sem):
    return pltpu.CompilerParams(dimension_semantics=sem, vmem_limit_bytes=V7X_VMEM_LIMIT_BYTES)


def _layer_norm(r, g, b):
    mu = jnp.mean(r, axis=-1, keepdims=True)
    c = r - mu
    var = jnp.mean(c * c, axis=-1, keepdims=True)
    return c * lax.rsqrt(var + LN_EPS) * g + b


def _inproj0_kernel(x_ref, w_ref, u_ref, qkv_ref, *, s5w):
    h = jnp.dot(x_ref[0].astype(BF16), w_ref[...], preferred_element_type=F32)
    u_ref[...] = h[:, :s5w]
    qkv_ref[0] = h[:, s5w:]


def _inproj0(x, w, s5w, tl=512):
    B, L, D = x.shape
    n = w.shape[1]
    return pl.pallas_call(
        functools.partial(_inproj0_kernel, s5w=s5w),
        grid=(B, L // tl),
        in_specs=[pl.BlockSpec((1, tl, D), lambda b, l: (b, l, 0)),
                  pl.BlockSpec((D, n), lambda b, l: (0, 0))],
        out_specs=[pl.BlockSpec((tl, s5w), lambda b, l: (l, b)),
                   pl.BlockSpec((1, tl, n - s5w), lambda b, l: (b, l, 0))],
        out_shape=[jax.ShapeDtypeStruct((L, B * s5w), F32),
                   jax.ShapeDtypeStruct((B, L, n - s5w), F32)],
        compiler_params=_params(("parallel", "parallel")),
        name="inproj0",
    )(x, w)


def _s5_discretize(lam_re, lam_im, log_dt, b_re, b_im):
    dt = jnp.exp(log_dt.astype(F32))[:, None]
    lr, li = lam_re.astype(F32), lam_im.astype(F32)
    mag = jnp.exp(lr * dt)
    ab_re, ab_im = mag * jnp.cos(li * dt), mag * jnp.sin(li * dt)
    er, ei = ab_re - 1.0, ab_im
    den = lr * lr + li * li
    q_re = (er * lr + ei * li) / den
    q_im = (ei * lr - er * li) / den
    br_, bi_ = b_re.astype(F32), b_im.astype(F32)
    bb_re = q_re[..., None] * br_ - q_im[..., None] * bi_
    bb_im = q_re[..., None] * bi_ + q_im[..., None] * br_
    return ab_re, ab_im, bb_re, bb_im


def _s5_kernel(u_ref, bm_ref, cm_ref, are_ref, aim_ref, d_ref, wg_ref, bg_ref, y_ref,
               s_scr, st_scr, z_scr, *, tl, nb, nchunk):
    ns = S5_CHUNK_STATES

    @pl.when(pl.program_id(0) == 0)
    def _():
        st_scr[...] = jnp.zeros_like(st_scr)

    u = u_ref[...].reshape(tl * nb, nchunk * LANES)
    for j in range(nchunk):
        uj = u[:, j * LANES:(j + 1) * LANES]
        s_scr[...] = jnp.dot(uj.astype(BF16), bm_ref[j], preferred_element_type=F32)
        ar = jnp.broadcast_to(are_ref[j], (nb, ns))
        ai = jnp.broadcast_to(aim_ref[j], (nb, ns))

        def step(t, carry, ar=ar, ai=ai):
            sre, sim = carry
            r0 = pl.multiple_of(t * nb, nb)
            nre = ar * sre - ai * sim + s_scr[pl.ds(r0, nb), 0:ns]
            nim = ar * sim + ai * sre + s_scr[pl.ds(r0, nb), ns:2 * ns]
            s_scr[pl.ds(r0, nb), 0:ns] = nre
            s_scr[pl.ds(r0, nb), ns:2 * ns] = nim
            return nre, nim

        sre, sim = lax.fori_loop(0, tl, step, (st_scr[j, :, 0:ns], st_scr[j, :, ns:2 * ns]), unroll=2)
        st_scr[j, :, 0:ns] = sre
        st_scr[j, :, ns:2 * ns] = sim
        yj = jnp.dot(s_scr[...].astype(BF16), cm_ref[j], preferred_element_type=F32)
        z_scr[:, j * LANES:(j + 1) * LANES] = yj + uj * d_ref[:, j * LANES:(j + 1) * LANES]
    z = jax.nn.gelu(z_scr[...])
    gate = jax.nn.sigmoid(jnp.dot(z.astype(BF16), wg_ref[...], preferred_element_type=F32) + bg_ref[...])
    y_ref[...] = (z * gate).reshape(tl, nb, nchunk * LANES)


def _s5_mixer(u_tm, lam_re, lam_im, log_dt, b_re, b_im, c_re, c_im, d_skip, w_glu, b_glu, tl=32):
    L, B, W = u_tm.shape
    G, P, H = lam_re.shape[0], S5_STATE, S5_GROUP
    gc = S5_GROUPS_PER_CHUNK
    nchunk = G // gc
    ab_re, ab_im, bb_re, bb_im = _s5_discretize(lam_re, lam_im, log_dt, b_re, b_im)
    eye = jnp.eye(gc, dtype=F32)

    def b_blocks(bb):
        return jnp.einsum('jgph,gk->jghkp', bb.reshape(nchunk, gc, P, H), eye).reshape(nchunk, gc * H, gc * P)

    def c_blocks(cc):
        return jnp.einsum('jghp,gk->jgpkh', cc.reshape(nchunk, gc, H, P), eye).reshape(nchunk, gc * P, gc * H)

    bm = jnp.concatenate([b_blocks(bb_re), b_blocks(bb_im)], axis=2).astype(BF16)
    cm = jnp.concatenate([c_blocks(c_re.astype(F32)), -c_blocks(c_im.astype(F32))], axis=1).astype(BF16)
    are = ab_re.reshape(nchunk, 1, gc * P)
    aim = ab_im.reshape(nchunk, 1, gc * P)
    m = tl * B
    const3 = lambda l: (0, 0, 0)
    const2 = lambda l: (0, 0)
    return pl.pallas_call(
        functools.partial(_s5_kernel, tl=tl, nb=B, nchunk=nchunk),
        grid=(L // tl,),
        in_specs=[pl.BlockSpec((tl, B, W), lambda l: (l, 0, 0)),
                  pl.BlockSpec(bm.shape, const3), pl.BlockSpec(cm.shape, const3),
                  pl.BlockSpec(are.shape, const3), pl.BlockSpec(aim.shape, const3),
                  pl.BlockSpec((1, W), const2), pl.BlockSpec((W, W), const2), pl.BlockSpec((1, W), const2)],
        out_specs=pl.BlockSpec((tl, B, W), lambda l: (l, 0, 0)),
        out_shape=jax.ShapeDtypeStruct((L, B, W), F32),
        scratch_shapes=[pltpu.VMEM((m, 2 * S5_CHUNK_STATES), F32),
                        pltpu.VMEM((nchunk, B, 2 * S5_CHUNK_STATES), F32),
                        pltpu.VMEM((m, W), F32)],
        compiler_params=_params(("arbitrary",)),
        name="s5_mixer",
    )(u_tm, bm, cm, are, aim, d_skip.reshape(1, W).astype(F32), w_glu.astype(BF16),
      b_glu.reshape(1, W).astype(F32))


def _rel_bucket(n):
    n = jnp.maximum(n, 0)
    max_exact = REL_BUCKETS // 2
    nf = jnp.maximum(n, 1).astype(F32)
    large = max_exact + (jnp.log(nf / max_exact) / math.log(REL_MAX_DIST / max_exact)
                         * (REL_BUCKETS - max_exact)).astype(jnp.int32)
    large = jnp.minimum(large, REL_BUCKETS - 1)
    return jnp.where(n < max_exact, n, large)


def _moba_kernel(q_ref, k_ref, v_ref, bias_ref, o_ref, *, nblk, bs, dh, topk):
    i = pl.program_id(2)
    hpb = LANES // dh
    neg = -1e30
    kmean = jnp.mean(k_ref[0].reshape(nblk, bs, LANES), axis=1)
    row = lax.broadcasted_iota(jnp.int32, (bs, bs), 0)
    col = lax.broadcasted_iota(jnp.int32, (bs, bs), 1)
    causal = row >= col
    blk_id = lax.broadcasted_iota(jnp.int32, (nblk, bs), 0)
    lane_id = lax.broadcasted_iota(jnp.int32, (bs, nblk), 1)
    outs = []
    for hh in range(hpb):
        lo, hi = hh * dh, (hh + 1) * dh
        qh = q_ref[0][:, lo:hi] * (dh ** -0.5)
        gate = lax.dot_general(kmean[:, lo:hi], qh, (((1,), (1,)), ((), ())),
                               precision=HIGHEST, preferred_element_type=F32)
        cnt = jnp.zeros((nblk, bs), jnp.int32)
        for m in range(nblk):
            gm = gate[m:m + 1, :]
            beats = (gm > gate) | ((gm == gate) & (m < blk_id))
            cnt = cnt + jnp.where(beats & (m < i), 1, 0)
        sel = jnp.where((blk_id < i) & (cnt < topk), 1.0, 0.0).astype(F32).T
        qb = qh.astype(BF16)

        def scores(r0, bias_tile):
            kj = k_ref[0, pl.ds(r0, bs), lo:hi].astype(BF16)
            s = lax.dot_general(qb, kj, (((1,), (1,)), ((), ())), preferred_element_type=F32)
            return s + bias_tile

        r_own = pl.multiple_of(i * bs, bs)
        s = jnp.where(causal, scores(r_own, bias_ref[hh, 0]), neg)
        m0 = jnp.max(s, axis=1, keepdims=True)
        p = jnp.exp(s - m0)
        l0 = jnp.sum(p, axis=1, keepdims=True)
        acc0 = jnp.dot(p.astype(BF16), v_ref[0, pl.ds(r_own, bs), lo:hi].astype(BF16),
                       preferred_element_type=F32)

        def body(j, carry, hh=hh, lo=lo, hi=hi, sel=sel, scores=scores):
            m_i, l_i, acc = carry
            r0 = pl.multiple_of(j * bs, bs)
            picked = jnp.sum(jnp.where(lane_id == j, sel, 0.0), axis=1, keepdims=True) > 0.5
            s = jnp.where(picked, scores(r0, bias_ref[hh, i - j]), neg)
            m_n = jnp.maximum(m_i, jnp.max(s, axis=1, keepdims=True))
            alpha = jnp.exp(m_i - m_n)
            p = jnp.exp(s - m_n)
            l_n = alpha * l_i + jnp.sum(p, axis=1, keepdims=True)
            pv = jnp.dot(p.astype(BF16), v_ref[0, pl.ds(r0, bs), lo:hi].astype(BF16),
                         preferred_element_type=F32)
            return m_n, l_n, alpha * acc + pv

        _, l_f, acc_f = lax.fori_loop(0, i, body, (m0, l0, acc0))
        outs.append(acc_f / l_f)
    o_ref[0] = jnp.concatenate(outs, axis=1)


def _moba(qkv, rel_bias):
    B, L, W3 = qkv.shape
    W = W3 // 3
    dh, bs = MOBA_HEAD_DIM, MOBA_BLOCK
    H = W // dh
    nblk = L // bs
    hpb = LANES // dh
    ncol = W // LANES
    dist = jnp.arange(L, dtype=jnp.int32)
    by_dist = rel_bias.astype(F32).T[:, _rel_bucket(dist)]
    a = np.arange(bs)
    idx = np.clip(np.arange(nblk)[:, None, None] * bs + a[None, :, None] - a[None, None, :], 0, L - 1)
    bias_tiles = by_dist[:, idx]
    return pl.pallas_call(
        functools.partial(_moba_kernel, nblk=nblk, bs=bs, dh=dh, topk=min(MOBA_TOPK, nblk)),
        grid=(H // hpb, B, nblk),
        in_specs=[pl.BlockSpec((1, bs, LANES), lambda h, b, i: (b, i, h)),
                  pl.BlockSpec((1, L, LANES), lambda h, b, i: (b, 0, ncol + h)),
                  pl.BlockSpec((1, L, LANES), lambda h, b, i: (b, 0, 2 * ncol + h)),
                  pl.BlockSpec((hpb, nblk, bs, bs), lambda h, b, i: (h, 0, 0, 0))],
        out_specs=pl.BlockSpec((1, bs, LANES), lambda h, b, i: (b, i, h)),
        out_shape=jax.ShapeDtypeStruct((B, L, W), F32),
        compiler_params=_params(("parallel", "parallel", "arbitrary")),
        name="moba_attention",
    )(qkv, qkv, qkv, bias_tiles)


def _route_top_k(xn, wrt_ref, br_ref):
    lt = lax.dot_general(wrt_ref[...], xn, (((1,), (1,)), ((), ())), precision=HIGHEST,
                         preferred_element_type=F32) + br_ref[...]
    n_e = lt.shape[0]
    eid = lax.broadcasted_iota(jnp.int32, lt.shape, 0)
    cur = lt
    vals, idxs = [], []
    for _ in range(TOP_K):
        m = jnp.max(cur, axis=0, keepdims=True)
        idx = jnp.min(jnp.where(cur == m, eid, n_e), axis=0, keepdims=True)
        vals.append(m)
        idxs.append(idx)
        cur = jnp.where(eid == idx, -jnp.inf, cur)
    ex = [jnp.exp(v - vals[0]) for v in vals]
    den = ex[0]
    for t in ex[1:]:
        den = den + t
    hot = jnp.zeros(lt.shape, F32)
    for idx in idxs:
        hot = hot + jnp.where(eid == idx, 1.0, 0.0)
    return (jnp.concatenate(idxs, axis=0), jnp.concatenate([t / den for t in ex], axis=0),
            jnp.sum(hot, axis=1, keepdims=True))


def _mix_ln_kernel(*refs, n_in):
    x_ref = refs[0]
    y_refs = refs[1:1 + n_in]
    w_refs = refs[1 + n_in:1 + 2 * n_in]
    g_ref, b_ref, wrt_ref, br_ref, x1_ref, et_ref, wt_ref, cnt_ref = refs[1 + 2 * n_in:]
    mix = None
    for y_ref, w_ref in zip(y_refs, w_refs):
        y = y_ref[...]
        y = y.reshape(y.shape[-2], y.shape[-1]).astype(BF16)
        t = jnp.dot(y, w_ref[...], preferred_element_type=F32)
        mix = t if mix is None else mix + t
    xn = _layer_norm(DN_ALPHA * x_ref[0] + mix, g_ref[...], b_ref[...])
    x1_ref[0] = xn
    et, wt, cnt = _route_top_k(xn, wrt_ref, br_ref)
    et_ref[0] = et
    wt_ref[0] = wt
    cnt_ref[0] = cnt


def _mix_ln(x, ys, ws, g, b, wr, br):
    B, L, D = x.shape
    E = wr.shape[1]
    tl = MOE_TILE
    nl = L // tl
    const = lambda bb, l: (0, 0)
    tile = lambda bb, l: (bb * nl + l, 0, 0)
    in_specs = [pl.BlockSpec((1, tl, D), lambda bb, l: (bb, l, 0))]
    in_specs += [spec for _, spec in ys]
    in_specs += [pl.BlockSpec(w.shape, const) for w in ws]
    in_specs += [pl.BlockSpec((1, D), const), pl.BlockSpec((1, D), const),
                 pl.BlockSpec((E, D), const), pl.BlockSpec((E, 1), const)]
    return pl.pallas_call(
        functools.partial(_mix_ln_kernel, n_in=len(ys)),
        grid=(B, nl),
        in_specs=in_specs,
        out_specs=[pl.BlockSpec((1, tl, D), lambda bb, l: (bb, l, 0)),
                   pl.BlockSpec((1, TOP_K, tl), tile), pl.BlockSpec((1, TOP_K, tl), tile),
                   pl.BlockSpec((1, E, 1), tile)],
        out_shape=[jax.ShapeDtypeStruct((B, L, D), F32),
                   jax.ShapeDtypeStruct((B * nl, TOP_K, tl), jnp.int32),
                   jax.ShapeDtypeStruct((B * nl, TOP_K, tl), F32),
                   jax.ShapeDtypeStruct((B * nl, E, 1), F32)],
        compiler_params=_params(("parallel", "parallel")),
        name="outproj_ln_router",
    )(x, *[a for a, _ in ys], *ws, g.reshape(1, D), b.reshape(1, D), wr.astype(F32).T, br.reshape(E, 1))


def _deinterleave_kernel(w_ref, p_ref, o_ref, *, ff):
    w = w_ref[0].astype(BF16)
    for c in range(2 * ff // (2 * LANES)):
        t = jnp.dot(w[:, 2 * LANES * c:2 * LANES * (c + 1)], p_ref[...], preferred_element_type=F32)
        o_ref[0, :, LANES * c:LANES * (c + 1)] = t[:, :LANES].astype(BF16)
        o_ref[0, :, ff + LANES * c:ff + LANES * (c + 1)] = t[:, LANES:].astype(BF16)


def _deinterleave_gate_up(w_gate_up, tk=512):
    E, D, F2 = w_gate_up.shape
    src = np.arange(2 * LANES)
    dst = np.where(src % 2 == 0, src // 2, LANES + src // 2)
    perm = np.zeros((2 * LANES, 2 * LANES), np.float32)
    perm[src, dst] = 1.0
    return pl.pallas_call(
        functools.partial(_deinterleave_kernel, ff=F2 // 2),
        grid=(E, D // tk),
        in_specs=[pl.BlockSpec((1, tk, F2), lambda e, k: (e, k, 0)),
                  pl.BlockSpec((2 * LANES, 2 * LANES), lambda e, k: (0, 0))],
        out_specs=pl.BlockSpec((1, tk, F2), lambda e, k: (e, k, 0)),
        out_shape=jax.ShapeDtypeStruct((E, D, F2), BF16),
        compiler_params=_params(("parallel", "parallel")),
        name="deinterleave_gate_up",
    )(w_gate_up, jnp.asarray(perm, BF16))


def _expert_kernel(be_ref, nu_ref, xs_ref, wgu_ref, wd_ref, bgu_ref, bd_ref, ys_ref):
    blk = pl.program_id(0)
    ff = wd_ref.shape[1]
    d = wd_ref.shape[2]

    @pl.when(blk < nu_ref[0])
    def _():
        x = xs_ref[:, :d].astype(BF16)
        pw = xs_ref[:, d:d + 1]
        h = jnp.dot(x, wgu_ref[0], preferred_element_type=F32) + bgu_ref[0]
        g = h[:, :ff]
        u = h[:, ff:]
        g = jnp.minimum(g, SWIGLU_LIMIT)
        u = jnp.clip(u, -SWIGLU_LIMIT, SWIGLU_LIMIT)
        act = g * jax.nn.sigmoid(SWIGLU_ALPHA * g) * (u + 1.0)
        y = jnp.dot(act.astype(BF16), wd_ref[0], preferred_element_type=F32) + bd_ref[0]
        ys_ref[...] = y * pw

    @pl.when(blk >= nu_ref[0])
    def _():
        ys_ref[...] = jnp.zeros_like(ys_ref)


def _expert_ffn(xs, plan, wgu, wd, bgu, bd):
    P, DW = xs.shape
    F, D = wd.shape[1], wd.shape[2]
    bm = MOE_BLOCK
    wmap = lambda i, be, nu: (be[i], 0, 0)
    rmap = lambda i, be, nu: (i, 0)
    grid_spec = pltpu.PrefetchScalarGridSpec(
        num_scalar_prefetch=2,
        grid=(P // bm,),
        in_specs=[pl.BlockSpec((bm, DW), rmap),
                  pl.BlockSpec((1, D, 2 * F), wmap), pl.BlockSpec((1, F, D), wmap),
                  pl.BlockSpec((1, 1, 2 * F), wmap), pl.BlockSpec((1, 1, D), wmap)],
        out_specs=pl.BlockSpec((bm, D), rmap),
    )
    return pl.pallas_call(
        _expert_kernel,
        grid_spec=grid_spec,
        out_shape=jax.ShapeDtypeStruct((P, D), F32),
        compiler_params=_params(("arbitrary",)),
        name="moe_experts",
    )(plan["blk_expert"], plan["n_used"], xs, wgu, wd, bgu, bd)


def _moe_rows(T, nt, E):
    bound = T * TOP_K + (SUBLANES - 1) * E * nt + E * (MOE_CHUNK - SUBLANES + MOE_BLOCK)
    return -(-bound // MOE_BLOCK) * MOE_BLOCK


def _moe_plan(cnt, T):
    nt, E = cnt.shape[0], cnt.shape[1]
    c = cnt.reshape(nt, E).astype(jnp.int32)
    n8 = (c + SUBLANES - 1) // SUBLANES * SUBLANES
    tot = jnp.sum(n8, axis=0)
    seg = (tot + (MOE_CHUNK - SUBLANES) + MOE_BLOCK - 1) // MOE_BLOCK * MOE_BLOCK
    seg_start = jnp.cumsum(seg) - seg
    strip = seg_start[None, :] + jnp.cumsum(n8, axis=0) - n8
    off = jnp.cumsum(n8, axis=1) - n8
    nch = (n8 + MOE_CHUNK - 1) // MOE_CHUNK
    nblk = _moe_rows(T, nt, E) // MOE_BLOCK
    seg_blk = seg // MOE_BLOCK
    blk_end = jnp.cumsum(seg_blk)
    blk_ids = jnp.arange(nblk, dtype=jnp.int32)
    blk_expert = jnp.minimum(jnp.sum(blk_end[None, :] <= blk_ids[:, None], axis=1), E - 1).astype(jnp.int32)
    tail = jnp.stack([seg_start + tot, seg - tot], axis=1)
    exact = jnp.stack([jnp.sum(n8 // MOE_CHUNK, axis=1)]
                      + [jnp.sum((n8 // r) % 2, axis=1) for r in MOE_REMAINDERS], axis=1)
    return dict(strip=strip.reshape(-1).astype(jnp.int32), off=off.reshape(-1).astype(jnp.int32),
                nch=nch.reshape(-1).astype(jnp.int32), nct=jnp.sum(nch, axis=1).astype(jnp.int32),
                n8=n8.reshape(-1).astype(jnp.int32), exact=exact.reshape(-1).astype(jnp.int32),
                off_col=off.reshape(nt, E, 1).astype(jnp.int32), blk_expert=blk_expert,
                tail=tail.reshape(-1).astype(jnp.int32),
                n_used=blk_end[-1:].astype(jnp.int32))


def _stage_rows(tm):
    return TOP_K * tm + N_EXPERTS * SUBLANES


def _dispatch_kernel(strip_ref, off_ref, nch_ref, nct_ref, tail_ref, nu_ref, x_ref, et_ref, wt_ref, offc_ref,
                     xs_ref, lp_ref, stage, zero, sem, *, tm):
    i = pl.program_id(0)
    n_e = N_EXPERTS
    d = x_ref.shape[1]
    nrow = _stage_rows(tm)
    ch = MOE_CHUNK

    def wait_tile(tile):
        def one(c, carry):
            pltpu.make_async_copy(stage.at[pl.ds(0, ch)], xs_ref.at[pl.ds(0, ch)], sem.at[0]).wait()
            return carry
        lax.fori_loop(0, nct_ref[tile], one, 0)

    @pl.when(i == 0)
    def _():
        stage[nrow:, :] = jnp.zeros((ch, d + LANES), F32)

    et = et_ref[0]
    wt = wt_ref[0]
    eid = lax.broadcasted_iota(jnp.int32, (n_e, tm), 0)
    hots = [eid == et[k:k + 1, :] for k in range(TOP_K)]
    m_t = jnp.zeros((n_e, tm), F32)
    for h in hots:
        m_t = m_t + jnp.where(h, 1.0, 0.0)
    before = jnp.where(lax.broadcasted_iota(jnp.int32, (tm, tm), 0) < lax.broadcasted_iota(jnp.int32, (tm, tm), 1),
                       1.0, 0.0).astype(BF16)
    rank = jnp.dot(m_t.astype(BF16), before, preferred_element_type=F32)
    base = offc_ref[0].astype(F32) + rank
    lps = [jnp.sum(jnp.where(h, base, 0.0), axis=0, keepdims=True) for h in hots]
    rio = lax.broadcasted_iota(jnp.int32, (nrow, tm), 0)
    sel = jnp.zeros((nrow, tm), F32)
    wsel = jnp.zeros((nrow, tm), F32)
    for k in range(TOP_K):
        hit = rio == lps[k].astype(jnp.int32)
        sel = sel + jnp.where(hit, 1.0, 0.0)
        wsel = wsel + jnp.where(hit, wt[k:k + 1, :], 0.0)
    rows = jnp.dot(sel.astype(BF16), x_ref[...].astype(BF16), preferred_element_type=F32)
    wcol = jnp.sum(wsel, axis=1, keepdims=True)
    lp8 = jnp.concatenate(lps + [jnp.zeros((SUBLANES - TOP_K, tm), F32)], axis=0)
    lp_ref[...] = lp8.T.astype(jnp.int32)

    @pl.when(i > 0)
    def _():
        wait_tile(i - 1)

    stage[0:nrow, 0:d] = rows
    stage[0:nrow, d:d + LANES] = jnp.broadcast_to(wcol, (nrow, LANES))

    def per_expert(e, carry):
        idx = i * n_e + e

        def per_chunk(c, carry2):
            src0 = pl.multiple_of(off_ref[idx] + c * ch, SUBLANES)
            dst0 = pl.multiple_of(strip_ref[idx] + c * ch, SUBLANES)
            pltpu.make_async_copy(stage.at[pl.ds(src0, ch)], xs_ref.at[pl.ds(dst0, ch)], sem.at[0]).start()
            return carry2

        return lax.fori_loop(0, nch_ref[idx], per_chunk, carry)

    lax.fori_loop(0, n_e, per_expert, 0)

    def zero_fill(wait):
        def go(dst0, rows):
            cp = pltpu.make_async_copy(zero.at[pl.ds(0, rows)],
                                       xs_ref.at[pl.ds(pl.multiple_of(dst0, SUBLANES), rows)], sem.at[0])
            cp.wait() if wait else cp.start()

        def per_tail(e, carry):
            start, n = tail_ref[2 * e], tail_ref[2 * e + 1]
            nfull = n // ch

            def per_chunk(c, carry2):
                go(start + c * ch, ch)
                return carry2

            lax.fori_loop(0, nfull, per_chunk, 0)
            done = nfull * ch
            for r in MOE_REMAINDERS:
                has = (n // r) % 2

                @pl.when(has == 1)
                def _(done=done, r=r):
                    go(start + done, r)

                done = done + has * r
            return carry

        lax.fori_loop(0, n_e, per_tail, 0)

        def per_block(blk, carry):
            go(blk * MOE_BLOCK, MOE_BLOCK)
            return carry

        lax.fori_loop(nu_ref[0], xs_ref.shape[0] // MOE_BLOCK, per_block, 0)

    @pl.when(i == pl.num_programs(0) - 1)
    def _():
        wait_tile(i)
        zero[...] = jnp.zeros_like(zero)
        zero_fill(wait=False)
        zero_fill(wait=True)


def _dispatch(x1, et, wt, plan):
    T, D = x1.shape
    nt = et.shape[0]
    tm = T // nt
    E = N_EXPERTS
    P = _moe_rows(T, nt, E)
    tile3 = lambda i, *_: (i, 0, 0)
    grid_spec = pltpu.PrefetchScalarGridSpec(
        num_scalar_prefetch=6,
        grid=(nt,),
        in_specs=[pl.BlockSpec((tm, D), lambda i, *_: (i, 0)),
                  pl.BlockSpec((1, TOP_K, tm), tile3), pl.BlockSpec((1, TOP_K, tm), tile3),
                  pl.BlockSpec((1, E, 1), tile3)],
        out_specs=[pl.BlockSpec(memory_space=pl.ANY),
                   pl.BlockSpec((tm, SUBLANES), lambda i, *_: (i, 0))],
        scratch_shapes=[pltpu.VMEM((_stage_rows(tm) + MOE_CHUNK, D + LANES), F32),
                        pltpu.VMEM((MOE_BLOCK, D + LANES), F32),
                        pltpu.SemaphoreType.DMA((1,))],
    )
    return pl.pallas_call(
        functools.partial(_dispatch_kernel, tm=tm),
        grid_spec=grid_spec,
        out_shape=[jax.ShapeDtypeStruct((P, D + LANES), F32), jax.ShapeDtypeStruct((T, SUBLANES), jnp.int32)],
        compiler_params=_params(("arbitrary",)),
        name="moe_dispatch",
    )(plan["strip"], plan["off"], plan["nch"], plan["nct"], plan["tail"], plan["n_used"], x1, et, wt,
      plan["off_col"])


def _combine_kernel(strip_ref, off_ref, n8_ref, exact_ref, ys_ref, lp_ref, x_ref, g_ref, b_ref, *rest, tm, kd):
    if kd:
        wqk_ref, wvg_ref, wgr_ref, w2_ref, b2_ref, xo_ref, qkl_ref, vg_ref, land, sem = rest
    else:
        xo_ref, land, sem = rest
    i = pl.program_id(0)
    n_e = N_EXPERTS
    nrow = _stage_rows(tm)
    ch = MOE_CHUNK
    sizes = (ch,) + MOE_REMAINDERS

    def strip_copy(src0, dst0, rows):
        return pltpu.make_async_copy(ys_ref.at[pl.ds(pl.multiple_of(src0, SUBLANES), rows)],
                                     land.at[pl.ds(pl.multiple_of(dst0, SUBLANES), rows)], sem.at[0])

    @pl.when(i == 0)
    def _():
        land[...] = jnp.zeros_like(land)

    def per_expert(e, carry):
        idx = i * n_e + e
        n8, src, dst = n8_ref[idx], strip_ref[idx], off_ref[idx]
        nfull = n8 // ch

        def per_chunk(c, carry2):
            strip_copy(src + c * ch, dst + c * ch, ch).start()
            return carry2

        lax.fori_loop(0, nfull, per_chunk, 0)
        done = nfull * ch
        for r in MOE_REMAINDERS:
            has = (n8 // r) % 2

            @pl.when(has == 1)
            def _(done=done, r=r):
                strip_copy(src + done, dst + done, r).start()

            done = done + has * r
        return carry

    lax.fori_loop(0, n_e, per_expert, 0)

    lp = lp_ref[...]
    cio = lax.broadcasted_iota(jnp.int32, (tm, nrow), 1)
    selt = jnp.zeros((tm, nrow), F32)
    for k in range(TOP_K):
        selt = selt + jnp.where(cio == lp[:, k:k + 1], 1.0, 0.0)
    selt = selt.astype(BF16)

    for s, rows in enumerate(sizes):
        def one(c, carry, rows=rows):
            strip_copy(0, 0, rows).wait()
            return carry

        lax.fori_loop(0, exact_ref[i * len(sizes) + s], one, 0)

    v = land[0:nrow, :]
    hi = v.astype(BF16)
    lo = (v - hi.astype(F32)).astype(BF16)
    ffn = jnp.dot(selt, hi, preferred_element_type=F32) + jnp.dot(selt, lo, preferred_element_type=F32)
    xn = _layer_norm(DN_ALPHA * x_ref[...] + ffn, g_ref[...], b_ref[...])
    xo_ref[...] = xn
    if kd:
        xb = xn.astype(BF16)
        qkl_ref[:, :2 * kd] = jnp.dot(xb, wqk_ref[...], preferred_element_type=F32)
        vg_ref[...] = jnp.dot(xb, wvg_ref[...], preferred_element_type=F32)
        gr = jnp.dot(xb, wgr_ref[...], preferred_element_type=F32)
        z = jnp.dot(gr, w2_ref[...], precision=HIGHEST, preferred_element_type=F32) + b2_ref[...]
        log_sig = jnp.minimum(z, 0.0) - jnp.log1p(jnp.exp(-jnp.abs(z)))
        qkl_ref[:, 2 * kd:] = log_sig / GLA_GATE_TAU


def _combine_ln(ys, lp, x, plan, g, b, gla=None):
    T, D = x.shape
    nt = plan["nct"].shape[0]
    tm = T // nt
    const = lambda i, *_: (0, 0)
    row = lambda i, *_: (i, 0)
    in_specs = [pl.BlockSpec(memory_space=pl.ANY), pl.BlockSpec((tm, SUBLANES), row), pl.BlockSpec((tm, D), row),
                pl.BlockSpec((1, D), const), pl.BlockSpec((1, D), const)]
    out_specs = [pl.BlockSpec((tm, D), row)]
    out_shape = [jax.ShapeDtypeStruct((T, D), F32)]
    args = [ys, lp, x, g.reshape(1, D), b.reshape(1, D)]
    kd = 0
    if gla is not None:
        w_in1, w_gate2, b_gate, kd, vd = gla
        ws = [w_in1[:, :2 * kd].astype(BF16), w_in1[:, 2 * kd:2 * kd + 2 * vd].astype(BF16),
              w_in1[:, 2 * kd + 2 * vd:].astype(BF16), w_gate2.astype(F32), b_gate.reshape(1, kd)]
        in_specs += [pl.BlockSpec(w.shape, const) for w in ws]
        args += ws
        out_specs += [pl.BlockSpec((tm, 3 * kd), row), pl.BlockSpec((tm, 2 * vd), row)]
        out_shape += [jax.ShapeDtypeStruct((T, 3 * kd), F32), jax.ShapeDtypeStruct((T, 2 * vd), F32)]
    grid_spec = pltpu.PrefetchScalarGridSpec(
        num_scalar_prefetch=4,
        grid=(nt,),
        in_specs=in_specs,
        out_specs=out_specs,
        scratch_shapes=[pltpu.VMEM((_stage_rows(tm), D), F32), pltpu.SemaphoreType.DMA((1,))],
    )
    return pl.pallas_call(
        functools.partial(_combine_kernel, tm=tm, kd=kd),
        grid_spec=grid_spec,
        out_shape=out_shape,
        compiler_params=_params(("arbitrary",)),
        name="moe_combine_ln",
    )(plan["strip"], plan["off"], plan["n8"], plan["exact"], *args)


def _moe_ln(x1, et, wt, cnt, w_gate_up, b_gate_up, w_down, b_down, g, b, gla=None):
    T = x1.shape[0]
    plan = _moe_plan(cnt, T)
    xs, lp = _dispatch(x1, et, wt, plan)
    wgu = _deinterleave_gate_up(w_gate_up)
    bgu = jnp.concatenate([b_gate_up[:, 0::2], b_gate_up[:, 1::2]], axis=-1)[:, None, :].astype(F32)
    ys = _expert_ffn(xs, plan, wgu, w_down.astype(BF16), bgu, b_down[:, None, :].astype(F32))
    return _combine_ln(ys, lp, x1, plan, g, b, gla)


def _gla_kernel(q_ref, k_ref, la_ref, v_ref, g_ref, nw_ref, o_ref, s_scr, *, tl, dk, dv):
    C = GLA_CHUNK

    @pl.when(pl.program_id(2) == 0)
    def _():
        s_scr[...] = jnp.zeros_like(s_scr)

    ri = lax.broadcasted_iota(jnp.int32, (C, C), 0)
    ci = lax.broadcasted_iota(jnp.int32, (C, C), 1)
    lower = ri >= ci
    tri = jnp.where(lower, 1.0, 0.0).astype(F32)
    for c in range(tl // C):
        rows = slice(c * C, (c + 1) * C)
        lg = la_ref[0, rows, :]
        bcum = jnp.dot(tri, lg, precision=HIGHEST, preferred_element_type=F32)
        btot = bcum[C - 1:C, :]
        q_in = q_ref[0, rows, :] * (dk ** -0.5) * jnp.exp(bcum)
        kc = k_ref[0, rows, :]
        k_in = kc * jnp.exp(-bcum)
        k_st = kc * jnp.exp(btot - bcum)
        vb = v_ref[0, rows, :].astype(BF16)
        qb = q_in.astype(BF16)
        att = lax.dot_general(qb, k_in.astype(BF16), (((1,), (1,)), ((), ())), preferred_element_type=F32)
        att = jnp.where(lower, att, 0.0)
        s_prev = s_scr[...]
        o = (jnp.dot(att.astype(BF16), vb, preferred_element_type=F32)
             + jnp.dot(qb, s_prev.astype(BF16), preferred_element_type=F32))
        kv = jnp.dot(k_st.T.astype(BF16), vb, preferred_element_type=F32)
        decay = jnp.exp(jnp.broadcast_to(btot, (8, dk))).T[:, 0:1]
        s_scr[...] = decay * s_prev + kv
        o = o * lax.rsqrt(jnp.mean(o * o, axis=-1, keepdims=True) + RMS_EPS) * nw_ref[...]
        gg = g_ref[0, rows, :]
        o_ref[0, rows, :] = o * (gg * jax.nn.sigmoid(gg))


def _gla(qkl, vg, norm_w, B, L, tl=256):
    T, kd3 = qkl.shape
    kd = kd3 // 3
    vd = vg.shape[1] // 2
    H = GLA_HEADS
    dk, dv = kd // H, vd // H
    qkl3 = qkl.reshape(B, L, kd3)
    vg3 = vg.reshape(B, L, 2 * vd)
    return pl.pallas_call(
        functools.partial(_gla_kernel, tl=tl, dk=dk, dv=dv),
        grid=(B, H, L // tl),
        in_specs=[pl.BlockSpec((1, tl, dk), lambda b, h, l: (b, l, h)),
                  pl.BlockSpec((1, tl, dk), lambda b, h, l: (b, l, H + h)),
                  pl.BlockSpec((1, tl, dk), lambda b, h, l: (b, l, 2 * H + h)),
                  pl.BlockSpec((1, tl, dv), lambda b, h, l: (b, l, h)),
                  pl.BlockSpec((1, tl, dv), lambda b, h, l: (b, l, H + h)),
                  pl.BlockSpec((1, dv), lambda b, h, l: (0, 0))],
        out_specs=pl.BlockSpec((1, tl, dv), lambda b, h, l: (b, l, h)),
        out_shape=jax.ShapeDtypeStruct((B, L, vd), F32),
        scratch_shapes=[pltpu.VMEM((dk, dv), F32)],
        compiler_params=_params(("parallel", "parallel", "arbitrary")),
        name="gla_mixer",
    )(qkl3, qkl3, qkl3, vg3, vg3, norm_w.reshape(1, dv).astype(F32))


def kernel(x, w_in0, s5_lam_re, s5_lam_im, s5_log_dt, s5_b_re, s5_b_im, s5_c_re, s5_c_im, s5_d, s5_w_glu,
           s5_b_glu, rel_bias, w_out0, w_in1, gla_w_gate2, gla_b_gate, gla_norm_w, w_out1, ln_mix_g, ln_mix_b,
           ln_ffn_g, ln_ffn_b, router_w, router_b, exp_w_gate_up, exp_b_gate_up, exp_w_down, exp_b_down):
    B, L, D = x.shape
    T = B * L
    s5w = s5_w_glu.shape[-1]
    kd = gla_w_gate2.shape[-1]
    vd = gla_norm_w.shape[-1] * GLA_HEADS

    u_tm, qkv = _inproj0(x, w_in0[0].astype(BF16), s5w)
    y_a = _s5_mixer(u_tm.reshape(L, B, s5w), s5_lam_re[0], s5_lam_im[0], s5_log_dt[0], s5_b_re[0], s5_b_im[0],
                    s5_c_re[0], s5_c_im[0], s5_d[0].reshape(-1), s5_w_glu[0], s5_b_glu[0])
    y_b = _moba(qkv, rel_bias)
    tl = MOE_TILE
    w0 = w_out0[0].astype(BF16)
    ys = [(y_a.reshape(L, B * s5w), pl.BlockSpec((tl, s5w), lambda bb, l: (l, bb))),
          (y_b, pl.BlockSpec((1, tl, D - s5w), lambda bb, l: (bb, l, 0)))]
    x1, et, wt, cnt = _mix_ln(x, ys, [w0[:s5w], w0[s5w:]], ln_mix_g[0], ln_mix_b[0], router_w[0], router_b[0])
    x2, qkl, vg = _moe_ln(x1.reshape(T, D), et, wt, cnt, exp_w_gate_up[0], exp_b_gate_up[0], exp_w_down[0],
                          exp_b_down[0], ln_ffn_g[0], ln_ffn_b[0],
                          gla=(w_in1[0], gla_w_gate2[0], gla_b_gate[0], kd, vd))

    y_c = _gla(qkl, vg, gla_norm_w[0], B, L)
    ys = [(y_c, pl.BlockSpec((1, tl, vd), lambda bb, l: (bb, l, 0)))]
    x3, et, wt, cnt = _mix_ln(x2.reshape(B, L, D), ys, [w_out1[0].astype(BF16)], ln_mix_g[1], ln_mix_b[1],
                              router_w[1], router_b[1])
    (out,) = _moe_ln(x3.reshape(T, D), et, wt, cnt, exp_w_gate_up[1], exp_b_gate_up[1], exp_w_down[1],
                     exp_b_down[1], ln_ffn_g[1], ln_ffn_b[1])
    return out.reshape(B, L, D)
```

```python
import functools
import math

import jax
import jax.numpy as jnp
import numpy as np
from jax import lax
from jax.experimental import pallas as pl
from jax.experimental.pallas import tpu as pltpu

F32 = jnp.float32
BF16 = jnp.bfloat16
HIGHEST = lax.Precision.HIGHEST

DEPTH = 2
S5_GROUP = 16
S5_STATE = 64
MOBA_HEAD_DIM = 64
MOBA_BLOCK = 256
MOBA_TOPK = 3
REL_BUCKETS = 32
REL_MAX_DIST = 2048
GLA_HEADS = 4
GLA_GATE_TAU = 16.0
GLA_CHUNK = 64
N_EXPERTS = 32
TOP_K = 4
SWIGLU_LIMIT = 7.0
SWIGLU_ALPHA = 1.702
MOE_BLOCK = 512
MOE_TILE = 256
MOE_CHUNK = 32
SUBLANES = 8
MOE_REMAINDERS = (16, 8)
DN_ALPHA = (2 * DEPTH) ** 0.25
LN_EPS = 1e-5
RMS_EPS = 1e-5

V7X_VMEM_LIMIT_BYTES = 56 * 1024 * 1024
LANES = 128

S5_GROUPS_PER_CHUNK = LANES // S5_GROUP
S5_CHUNK_STATES = S5_GROUPS_PER_CHUNK * S5_STATE


def _params(sem):
    return pltpu.CompilerParams(dimension_semantics=sem, vmem_limit_bytes=V7X_VMEM_LIMIT_BYTES)


def _layer_norm(r, g, b):
    mu = jnp.mean(r, axis=-1, keepdims=True)
    c = r - mu
    var = jnp.mean(c * c, axis=-1, keepdims=True)
    return c * lax.rsqrt(var + LN_EPS) * g + b


def _inproj0_kernel(x_ref, wuk_ref, wqvt_ref, u_ref, k_ref, qt_ref, vt_ref, *, s5w):
    xb = x_ref[0].astype(BF16)
    h = jnp.dot(xb, wuk_ref[...], preferred_element_type=F32)
    u_ref[...] = h[:, :s5w]
    k_ref[0] = h[:, s5w:]
    ht = lax.dot_general(wqvt_ref[...], xb, (((1,), (1,)), ((), ())), preferred_element_type=F32)
    aw = ht.shape[0] // 2
    qt_ref[0] = ht[:aw]
    vt_ref[0] = ht[aw:].astype(BF16)


def _inproj0(x, w_in0, s5w, tl=512):
    B, L, D = x.shape
    aw = (w_in0.shape[1] - s5w) // 3
    wb = w_in0.astype(BF16)
    wuk = jnp.concatenate([wb[:, :s5w], wb[:, s5w + aw:s5w + 2 * aw]], axis=1)
    wqvt = jnp.concatenate([wb[:, s5w:s5w + aw], wb[:, s5w + 2 * aw:]], axis=1).T
    return pl.pallas_call(
        functools.partial(_inproj0_kernel, s5w=s5w),
        grid=(B, L // tl),
        in_specs=[pl.BlockSpec((1, tl, D), lambda b, l: (b, l, 0)),
                  pl.BlockSpec(wuk.shape, lambda b, l: (0, 0)),
                  pl.BlockSpec(wqvt.shape, lambda b, l: (0, 0))],
        out_specs=[pl.BlockSpec((tl, s5w), lambda b, l: (l, b)),
                   pl.BlockSpec((1, tl, aw), lambda b, l: (b, l, 0)),
                   pl.BlockSpec((1, aw, tl), lambda b, l: (b, 0, l)),
                   pl.BlockSpec((1, aw, tl), lambda b, l: (b, 0, l))],
        out_shape=[jax.ShapeDtypeStruct((L, B * s5w), F32),
                   jax.ShapeDtypeStruct((B, L, aw), F32),
                   jax.ShapeDtypeStruct((B, aw, L), F32),
                   jax.ShapeDtypeStruct((B, aw, L), BF16)],
        compiler_params=_params(("parallel", "parallel")),
        name="inproj0",
    )(x, wuk, wqvt)


def _s5_discretize(lam_re, lam_im, log_dt, b_re, b_im):
    dt = jnp.exp(log_dt.astype(F32))[:, None]
    lr, li = lam_re.astype(F32), lam_im.astype(F32)
    mag = jnp.exp(lr * dt)
    ab_re, ab_im = mag * jnp.cos(li * dt), mag * jnp.sin(li * dt)
    er, ei = ab_re - 1.0, ab_im
    den = lr * lr + li * li
    q_re = (er * lr + ei * li) / den
    q_im = (ei * lr - er * li) / den
    br_, bi_ = b_re.astype(F32), b_im.astype(F32)
    bb_re = q_re[..., None] * br_ - q_im[..., None] * bi_
    bb_im = q_re[..., None] * bi_ + q_im[..., None] * br_
    return ab_re, ab_im, bb_re, bb_im


def _s5_kernel(u_ref, bm_ref, cm_ref, are_ref, aim_ref, d_ref, wg_ref, bg_ref, y_ref,
               s_scr, st_scr, z_scr, *, tl, nb, nchunk):
    ns = S5_CHUNK_STATES

    @pl.when(pl.program_id(0) == 0)
    def _():
        st_scr[...] = jnp.zeros_like(st_scr)

    u = u_ref[...].reshape(tl * nb, nchunk * LANES)
    for j in range(nchunk):
        uj = u[:, j * LANES:(j + 1) * LANES]
        s_scr[...] = jnp.dot(uj.astype(BF16), bm_ref[j], preferred_element_type=F32)
        ar = jnp.broadcast_to(are_ref[j], (nb, ns))
        ai = jnp.broadcast_to(aim_ref[j], (nb, ns))

        def step(t, carry, ar=ar, ai=ai):
            sre, sim = carry
            r0 = pl.multiple_of(t * nb, nb)
            nre = ar * sre - ai * sim + s_scr[pl.ds(r0, nb), 0:ns]
            nim = ar * sim + ai * sre + s_scr[pl.ds(r0, nb), ns:2 * ns]
            s_scr[pl.ds(r0, nb), 0:ns] = nre
            s_scr[pl.ds(r0, nb), ns:2 * ns] = nim
            return nre, nim

        sre, sim = lax.fori_loop(0, tl, step, (st_scr[j, :, 0:ns], st_scr[j, :, ns:2 * ns]), unroll=2)
        st_scr[j, :, 0:ns] = sre
        st_scr[j, :, ns:2 * ns] = sim
        yj = jnp.dot(s_scr[...].astype(BF16), cm_ref[j], preferred_element_type=F32)
        z_scr[:, j * LANES:(j + 1) * LANES] = yj + uj * d_ref[:, j * LANES:(j + 1) * LANES]
    z = jax.nn.gelu(z_scr[...])
    gate = jax.nn.sigmoid(jnp.dot(z.astype(BF16), wg_ref[...], preferred_element_type=F32) + bg_ref[...])
    y_ref[...] = (z * gate).reshape(tl, nb, nchunk * LANES)


def _s5_mixer(u_tm, lam_re, lam_im, log_dt, b_re, b_im, c_re, c_im, d_skip, w_glu, b_glu, tl=32):
    L, B, W = u_tm.shape
    G, P, H = lam_re.shape[0], S5_STATE, S5_GROUP
    gc = S5_GROUPS_PER_CHUNK
    nchunk = G // gc
    ab_re, ab_im, bb_re, bb_im = _s5_discretize(lam_re, lam_im, log_dt, b_re, b_im)
    eye = jnp.eye(gc, dtype=F32)

    def b_blocks(bb):
        return jnp.einsum('jgph,gk->jghkp', bb.reshape(nchunk, gc, P, H), eye).reshape(nchunk, gc * H, gc * P)

    def c_blocks(cc):
        return jnp.einsum('jghp,gk->jgpkh', cc.reshape(nchunk, gc, H, P), eye).reshape(nchunk, gc * P, gc * H)

    bm = jnp.concatenate([b_blocks(bb_re), b_blocks(bb_im)], axis=2).astype(BF16)
    cm = jnp.concatenate([c_blocks(c_re.astype(F32)), -c_blocks(c_im.astype(F32))], axis=1).astype(BF16)
    are = ab_re.reshape(nchunk, 1, gc * P)
    aim = ab_im.reshape(nchunk, 1, gc * P)
    m = tl * B
    const3 = lambda l: (0, 0, 0)
    const2 = lambda l: (0, 0)
    return pl.pallas_call(
        functools.partial(_s5_kernel, tl=tl, nb=B, nchunk=nchunk),
        grid=(L // tl,),
        in_specs=[pl.BlockSpec((tl, B, W), lambda l: (l, 0, 0)),
                  pl.BlockSpec(bm.shape, const3), pl.BlockSpec(cm.shape, const3),
                  pl.BlockSpec(are.shape, const3), pl.BlockSpec(aim.shape, const3),
                  pl.BlockSpec((1, W), const2), pl.BlockSpec((W, W), const2), pl.BlockSpec((1, W), const2)],
        out_specs=pl.BlockSpec((tl, B, W), lambda l: (l, 0, 0)),
        out_shape=jax.ShapeDtypeStruct((L, B, W), F32),
        scratch_shapes=[pltpu.VMEM((m, 2 * S5_CHUNK_STATES), F32),
                        pltpu.VMEM((nchunk, B, 2 * S5_CHUNK_STATES), F32),
                        pltpu.VMEM((m, W), F32)],
        compiler_params=_params(("arbitrary",)),
        name="s5_mixer",
    )(u_tm, bm, cm, are, aim, d_skip.reshape(1, W).astype(F32), w_glu.astype(BF16),
      b_glu.reshape(1, W).astype(F32))


def _rel_bucket(n):
    n = jnp.maximum(n, 0)
    max_exact = REL_BUCKETS // 2
    nf = jnp.maximum(n, 1).astype(F32)
    large = max_exact + (jnp.log(nf / max_exact) / math.log(REL_MAX_DIST / max_exact)
                         * (REL_BUCKETS - max_exact)).astype(jnp.int32)
    large = jnp.minimum(large, REL_BUCKETS - 1)
    return jnp.where(n < max_exact, n, large)


MOBA_Q_TILE = 256
MOBA_ONES_ROWS = 16
MOBA_NEG = -1e30
LOG2E = math.log2(math.e)


def _moba_kernel(qt_ref, k_ref, vt_ref, wv_ref, hot_ref, o_ref, bias_scr, kk_scr, va_scr, km_scr,
                 *, nblk, bs, dh, topk):
    b, i = pl.program_id(1), pl.program_id(2)
    tq = MOBA_Q_TILE
    hpb = LANES // dh
    nsub = bs // tq
    own, sub = i // nsub, i % nsub

    @pl.when((b == 0) & (i == 0))
    def _():
        r_io = lax.broadcasted_iota(jnp.int32, (bs, 2 * bs), 0)
        c_io = lax.broadcasted_iota(jnp.int32, (bs, 2 * bs), 1)
        for hh in range(hpb):
            for dd in range(nblk):
                t = pltpu.roll(jnp.broadcast_to(wv_ref[hh, dd:dd + 1, :], (bs, 2 * bs)), bs + 1, 1,
                               stride=1, stride_axis=0)
                if dd == 0:
                    t = jnp.where(c_io >= r_io, t, MOBA_NEG)
                for s in range(nsub):
                    bias_scr[hh, dd, s] = t[:, s * tq:(s + 1) * tq]

    @pl.when(i == 0)
    def _():
        kf = k_ref[0]
        km_scr[...] = jnp.mean(kf.reshape(nblk, bs, LANES), axis=1)
        kk_scr[:, 0:LANES] = kf.astype(BF16)
        kk_scr[:, LANES:] = hot_ref[...]
        for n in range(nblk):
            va_scr[n, 0:LANES, :] = vt_ref[0, :, n * bs:(n + 1) * bs]
            va_scr[n, LANES:, :] = jnp.ones((MOBA_ONES_ROWS, bs), BF16)

    q2 = qt_ref[0] * (dh ** -0.5 * LOG2E)
    f_io = lax.broadcasted_iota(jnp.int32, (LANES, tq), 0)
    blk_io = lax.broadcasted_iota(jnp.int32, (nblk, tq), 0)
    qms, pens = [], []
    for hh in range(hpb):
        qm = jnp.where((f_io >= hh * dh) & (f_io < (hh + 1) * dh), q2, 0.0)
        gate = jnp.dot(km_scr[...], qm, precision=HIGHEST, preferred_element_type=F32)
        cnt = jnp.zeros((nblk, tq), jnp.int32)
        for m in range(nblk):
            gm = gate[m:m + 1, :]
            beats = (gm > gate) | ((gm == gate) & (m < blk_io))
            cnt = cnt + jnp.where(beats & (m < own), 1, 0)
        keep = (blk_io >= own) | (cnt < topk)
        pens.append(jnp.where(keep, 0.0, MOBA_NEG))
        qms.append(qm)
    qa = jnp.concatenate([jnp.concatenate(qms, axis=1), jnp.concatenate(pens, axis=1),
                          jnp.zeros((LANES - nblk, hpb * tq), F32)], axis=0).astype(BF16)

    def scores(j):
        r0 = pl.multiple_of(j * bs, bs)
        s = jnp.dot(kk_scr[pl.ds(r0, bs), :], qa, preferred_element_type=F32)
        return s + jnp.concatenate([bias_scr[hh, own - j, sub] for hh in range(hpb)], axis=1)

    def pair(ja, jb, pen_b, m_i, acc):
        sa = scores(ja)
        sb = scores(jb) + pen_b
        m_n = jnp.maximum(m_i, jnp.maximum(jnp.max(sa, axis=0, keepdims=True), jnp.max(sb, axis=0, keepdims=True)))
        alpha = jnp.exp2(m_i - m_n)
        pa = jnp.exp2(sa - m_n).astype(BF16)
        pb = jnp.exp2(sb - m_n).astype(BF16)
        return m_n, (alpha * acc + jnp.dot(va_scr[ja], pa, preferred_element_type=F32)
                     + jnp.dot(va_scr[jb], pb, preferred_element_type=F32))

    odd = own % 2
    m1, acc1 = pair(own, jnp.maximum(own - 1, 0), jnp.where(odd == 1, 0.0, MOBA_NEG),
                    jnp.full((1, hpb * tq), MOBA_NEG, F32), jnp.zeros((LANES + MOBA_ONES_ROWS, hpb * tq), F32))
    _, acc_f = lax.fori_loop(0, own // 2, lambda jj, c: pair(2 * jj, 2 * jj + 1, 0.0, *c), (m1, acc1))
    on = acc_f[0:LANES, :] / acc_f[LANES:LANES + 1, :]
    ot = jnp.concatenate([on[hh * dh:(hh + 1) * dh, hh * tq:(hh + 1) * tq] for hh in range(hpb)], axis=0)
    o_ref[0] = ot.T


def _moba(qt, k, vt, rel_bias):
    B, L, W = k.shape
    dh, bs, tq = MOBA_HEAD_DIM, MOBA_BLOCK, MOBA_Q_TILE
    H = W // dh
    nblk = L // bs
    hpb = LANES // dh
    dist = jnp.arange(L, dtype=jnp.int32)
    by_dist = rel_bias.astype(F32).T[:, _rel_bucket(dist)] * LOG2E
    idx = np.clip(np.arange(nblk)[:, None] * bs - (bs - 1) + np.arange(2 * bs)[None, :], 0, L - 1)
    vecs = by_dist[:, idx]
    hot = np.zeros((L, LANES), np.float32)
    hot[np.arange(L), np.arange(L) // bs] = 1.0
    return pl.pallas_call(
        functools.partial(_moba_kernel, nblk=nblk, bs=bs, dh=dh, topk=min(MOBA_TOPK, nblk)),
        grid=(H // hpb, B, L // tq),
        in_specs=[pl.BlockSpec((1, LANES, tq), lambda h, b, i: (b, h, i)),
                  pl.BlockSpec((1, L, LANES), lambda h, b, i: (b, 0, h)),
                  pl.BlockSpec((1, LANES, L), lambda h, b, i: (b, h, 0)),
                  pl.BlockSpec((hpb, nblk, 2 * bs), lambda h, b, i: (h, 0, 0)),
                  pl.BlockSpec((L, LANES), lambda h, b, i: (0, 0))],
        out_specs=pl.BlockSpec((1, tq, LANES), lambda h, b, i: (b, i, h)),
        out_shape=jax.ShapeDtypeStruct((B, L, W), F32),
        scratch_shapes=[pltpu.VMEM((hpb, nblk, bs // tq, bs, tq), F32),
                        pltpu.VMEM((L, 2 * LANES), BF16),
                        pltpu.VMEM((nblk, LANES + MOBA_ONES_ROWS, bs), BF16),
                        pltpu.VMEM((nblk, LANES), F32)],
        compiler_params=_params(("arbitrary", "arbitrary", "arbitrary")),
        name="moba_attention",
    )(qt, k, vt, vecs, jnp.asarray(hot, BF16))


def _route_top_k(xn, wrt_ref, br_ref):
    lt = lax.dot_general(wrt_ref[...], xn, (((1,), (1,)), ((), ())), precision=HIGHEST,
                         preferred_element_type=F32) + br_ref[...]
    n_e = lt.shape[0]
    eid = lax.broadcasted_iota(jnp.int32, lt.shape, 0)
    cur = lt
    vals, idxs = [], []
    for _ in range(TOP_K):
        m = jnp.max(cur, axis=0, keepdims=True)
        idx = jnp.min(jnp.where(cur == m, eid, n_e), axis=0, keepdims=True)
        vals.append(m)
        idxs.append(idx)
        cur = jnp.where(eid == idx, -jnp.inf, cur)
    ex = [jnp.exp(v - vals[0]) for v in vals]
    den = ex[0]
    for t in ex[1:]:
        den = den + t
    hot = jnp.zeros(lt.shape, F32)
    for idx in idxs:
        hot = hot + jnp.where(eid == idx, 1.0, 0.0)
    return (jnp.concatenate(idxs, axis=0), jnp.concatenate([t / den for t in ex], axis=0),
            jnp.sum(hot, axis=1, keepdims=True))


def _mix_ln_kernel(*refs, n_in):
    x_ref = refs[0]
    y_refs = refs[1:1 + n_in]
    w_refs = refs[1 + n_in:1 + 2 * n_in]
    g_ref, b_ref, wrt_ref, br_ref, x1_ref, et_ref, wt_ref, cnt_ref = refs[1 + 2 * n_in:]
    mix = None
    for y_ref, w_ref in zip(y_refs, w_refs):
        y = y_ref[...]
        y = y.reshape(y.shape[-2], y.shape[-1]).astype(BF16)
        t = jnp.dot(y, w_ref[...], preferred_element_type=F32)
        mix = t if mix is None else mix + t
    xn = _layer_norm(DN_ALPHA * x_ref[0] + mix, g_ref[...], b_ref[...])
    x1_ref[0] = xn
    et, wt, cnt = _route_top_k(xn, wrt_ref, br_ref)
    et_ref[0] = et
    wt_ref[0] = wt
    cnt_ref[0] = cnt


def _mix_ln(x, ys, ws, g, b, wr, br):
    B, L, D = x.shape
    E = wr.shape[1]
    tl = MOE_TILE
    nl = L // tl
    const = lambda bb, l: (0, 0)
    tile = lambda bb, l: (bb * nl + l, 0, 0)
    in_specs = [pl.BlockSpec((1, tl, D), lambda bb, l: (bb, l, 0))]
    in_specs += [spec for _, spec in ys]
    in_specs += [pl.BlockSpec(w.shape, const) for w in ws]
    in_specs += [pl.BlockSpec((1, D), const), pl.BlockSpec((1, D), const),
                 pl.BlockSpec((E, D), const), pl.BlockSpec((E, 1), const)]
    return pl.pallas_call(
        functools.partial(_mix_ln_kernel, n_in=len(ys)),
        grid=(B, nl),
        in_specs=in_specs,
        out_specs=[pl.BlockSpec((1, tl, D), lambda bb, l: (bb, l, 0)),
                   pl.BlockSpec((1, TOP_K, tl), tile), pl.BlockSpec((1, TOP_K, tl), tile),
                   pl.BlockSpec((1, E, 1), tile)],
        out_shape=[jax.ShapeDtypeStruct((B, L, D), F32),
                   jax.ShapeDtypeStruct((B * nl, TOP_K, tl), jnp.int32),
                   jax.ShapeDtypeStruct((B * nl, TOP_K, tl), F32),
                   jax.ShapeDtypeStruct((B * nl, E, 1), F32)],
        compiler_params=_params(("parallel", "parallel")),
        name="outproj_ln_router",
    )(x, *[a for a, _ in ys], *ws, g.reshape(1, D), b.reshape(1, D), wr.astype(F32).T, br.reshape(E, 1))


def _deinterleave_kernel(w_ref, p_ref, o_ref, *, ff):
    w = w_ref[0].astype(BF16)
    for c in range(2 * ff // (2 * LANES)):
        t = jnp.dot(w[:, 2 * LANES * c:2 * LANES * (c + 1)], p_ref[...], preferred_element_type=F32)
        o_ref[0, :, LANES * c:LANES * (c + 1)] = t[:, :LANES].astype(BF16)
        o_ref[0, :, ff + LANES * c:ff + LANES * (c + 1)] = t[:, LANES:].astype(BF16)


def _deinterleave_gate_up(w_gate_up, tk=512):
    E, D, F2 = w_gate_up.shape
    src = np.arange(2 * LANES)
    dst = np.where(src % 2 == 0, src // 2, LANES + src // 2)
    perm = np.zeros((2 * LANES, 2 * LANES), np.float32)
    perm[src, dst] = 1.0
    return pl.pallas_call(
        functools.partial(_deinterleave_kernel, ff=F2 // 2),
        grid=(E, D // tk),
        in_specs=[pl.BlockSpec((1, tk, F2), lambda e, k: (e, k, 0)),
                  pl.BlockSpec((2 * LANES, 2 * LANES), lambda e, k: (0, 0))],
        out_specs=pl.BlockSpec((1, tk, F2), lambda e, k: (e, k, 0)),
        out_shape=jax.ShapeDtypeStruct((E, D, F2), BF16),
        compiler_params=_params(("parallel", "parallel")),
        name="deinterleave_gate_up",
    )(w_gate_up, jnp.asarray(perm, BF16))


def _expert_kernel(be_ref, nu_ref, xs_ref, wgu_ref, wd_ref, bgu_ref, bd_ref, ys_ref):
    blk = pl.program_id(0)
    ff = wd_ref.shape[1]
    d = wd_ref.shape[2]

    @pl.when(blk < nu_ref[0])
    def _():
        x = xs_ref[:, :d].astype(BF16)
        pw = xs_ref[:, d:d + 1]
        h = jnp.dot(x, wgu_ref[0], preferred_element_type=F32) + bgu_ref[0]
        g = h[:, :ff]
        u = h[:, ff:]
        g = jnp.minimum(g, SWIGLU_LIMIT)
        u = jnp.clip(u, -SWIGLU_LIMIT, SWIGLU_LIMIT)
        act = g * jax.nn.sigmoid(SWIGLU_ALPHA * g) * (u + 1.0)
        y = jnp.dot(act.astype(BF16), wd_ref[0], preferred_element_type=F32) + bd_ref[0]
        ys_ref[...] = y * pw

    @pl.when(blk >= nu_ref[0])
    def _():
        ys_ref[...] = jnp.zeros_like(ys_ref)


def _expert_ffn(xs, plan, wgu, wd, bgu, bd):
    P, DW = xs.shape
    F, D = wd.shape[1], wd.shape[2]
    bm = MOE_BLOCK
    wmap = lambda i, be, nu: (be[i], 0, 0)
    rmap = lambda i, be, nu: (i, 0)
    grid_spec = pltpu.PrefetchScalarGridSpec(
        num_scalar_prefetch=2,
        grid=(P // bm,),
        in_specs=[pl.BlockSpec((bm, DW), rmap),
                  pl.BlockSpec((1, D, 2 * F), wmap), pl.BlockSpec((1, F, D), wmap),
                  pl.BlockSpec((1, 1, 2 * F), wmap), pl.BlockSpec((1, 1, D), wmap)],
        out_specs=pl.BlockSpec((bm, D), rmap),
    )
    return pl.pallas_call(
        _expert_kernel,
        grid_spec=grid_spec,
        out_shape=jax.ShapeDtypeStruct((P, D), F32),
        compiler_params=_params(("arbitrary",)),
        name="moe_experts",
    )(plan["blk_expert"], plan["n_used"], xs, wgu, wd, bgu, bd)


def _moe_rows(T, nt, E):
    bound = T * TOP_K + (SUBLANES - 1) * E * nt + E * (MOE_CHUNK - SUBLANES + MOE_BLOCK)
    return -(-bound // MOE_BLOCK) * MOE_BLOCK


def _moe_plan(cnt, T):
    nt, E = cnt.shape[0], cnt.shape[1]
    c = cnt.reshape(nt, E).astype(jnp.int32)
    n8 = (c + SUBLANES - 1) // SUBLANES * SUBLANES
    tot = jnp.sum(n8, axis=0)
    seg = (tot + (MOE_CHUNK - SUBLANES) + MOE_BLOCK - 1) // MOE_BLOCK * MOE_BLOCK
    seg_start = jnp.cumsum(seg) - seg
    strip = seg_start[None, :] + jnp.cumsum(n8, axis=0) - n8
    off = jnp.cumsum(n8, axis=1) - n8
    nch = (n8 + MOE_CHUNK - 1) // MOE_CHUNK
    nblk = _moe_rows(T, nt, E) // MOE_BLOCK
    seg_blk = seg // MOE_BLOCK
    blk_end = jnp.cumsum(seg_blk)
    blk_ids = jnp.arange(nblk, dtype=jnp.int32)
    blk_expert = jnp.minimum(jnp.sum(blk_end[None, :] <= blk_ids[:, None], axis=1), E - 1).astype(jnp.int32)
    tail = jnp.stack([seg_start + tot, seg - tot], axis=1)
    exact = jnp.stack([jnp.sum(n8 // MOE_CHUNK, axis=1)]
                      + [jnp.sum((n8 // r) % 2, axis=1) for r in MOE_REMAINDERS], axis=1)
    return dict(strip=strip.reshape(-1).astype(jnp.int32), off=off.reshape(-1).astype(jnp.int32),
                nch=nch.reshape(-1).astype(jnp.int32), nct=jnp.sum(nch, axis=1).astype(jnp.int32),
                n8=n8.reshape(-1).astype(jnp.int32), exact=exact.reshape(-1).astype(jnp.int32),
                off_col=off.reshape(nt, E, 1).astype(jnp.int32), blk_expert=blk_expert,
                tail=tail.reshape(-1).astype(jnp.int32),
                n_used=blk_end[-1:].astype(jnp.int32))


def _stage_rows(tm):
    return TOP_K * tm + N_EXPERTS * SUBLANES


def _dispatch_kernel(strip_ref, off_ref, nch_ref, nct_ref, tail_ref, nu_ref, x_ref, et_ref, wt_ref, offc_ref,
                     xs_ref, lp_ref, stage, zero, sem, *, tm):
    i = pl.program_id(0)
    n_e = N_EXPERTS
    d = x_ref.shape[1]
    nrow = _stage_rows(tm)
    ch = MOE_CHUNK

    def wait_tile(tile):
        def one(c, carry):
            pltpu.make_async_copy(stage.at[pl.ds(0, ch)], xs_ref.at[pl.ds(0, ch)], sem.at[0]).wait()
            return carry
        lax.fori_loop(0, nct_ref[tile], one, 0)

    @pl.when(i == 0)
    def _():
        stage[nrow:, :] = jnp.zeros((ch, d + LANES), F32)

    et = et_ref[0]
    wt = wt_ref[0]
    eid = lax.broadcasted_iota(jnp.int32, (n_e, tm), 0)
    hots = [eid == et[k:k + 1, :] for k in range(TOP_K)]
    m_t = jnp.zeros((n_e, tm), F32)
    for h in hots:
        m_t = m_t + jnp.where(h, 1.0, 0.0)
    before = jnp.where(lax.broadcasted_iota(jnp.int32, (tm, tm), 0) < lax.broadcasted_iota(jnp.int32, (tm, tm), 1),
                       1.0, 0.0).astype(BF16)
    rank = jnp.dot(m_t.astype(BF16), before, preferred_element_type=F32)
    base = offc_ref[0].astype(F32) + rank
    lps = [jnp.sum(jnp.where(h, base, 0.0), axis=0, keepdims=True) for h in hots]
    rio = lax.broadcasted_iota(jnp.int32, (nrow, tm), 0)
    sel = jnp.zeros((nrow, tm), F32)
    wsel = jnp.zeros((nrow, tm), F32)
    for k in range(TOP_K):
        hit = rio == lps[k].astype(jnp.int32)
        sel = sel + jnp.where(hit, 1.0, 0.0)
        wsel = wsel + jnp.where(hit, wt[k:k + 1, :], 0.0)
    rows = jnp.dot(sel.astype(BF16), x_ref[...].astype(BF16), preferred_element_type=F32)
    wcol = jnp.sum(wsel, axis=1, keepdims=True)
    lp8 = jnp.concatenate(lps + [jnp.zeros((SUBLANES - TOP_K, tm), F32)], axis=0)
    lp_ref[...] = lp8.T.astype(jnp.int32)

    @pl.when(i > 0)
    def _():
        wait_tile(i - 1)

    stage[0:nrow, 0:d] = rows
    stage[0:nrow, d:d + LANES] = jnp.broadcast_to(wcol, (nrow, LANES))

    def per_expert(e, carry):
        idx = i * n_e + e

        def per_chunk(c, carry2):
            src0 = pl.multiple_of(off_ref[idx] + c * ch, SUBLANES)
            dst0 = pl.multiple_of(strip_ref[idx] + c * ch, SUBLANES)
            pltpu.make_async_copy(stage.at[pl.ds(src0, ch)], xs_ref.at[pl.ds(dst0, ch)], sem.at[0]).start()
            return carry2

        return lax.fori_loop(0, nch_ref[idx], per_chunk, carry)

    lax.fori_loop(0, n_e, per_expert, 0)

    def zero_fill(wait):
        def go(dst0, rows):
            cp = pltpu.make_async_copy(zero.at[pl.ds(0, rows)],
                                       xs_ref.at[pl.ds(pl.multiple_of(dst0, SUBLANES), rows)], sem.at[0])
            cp.wait() if wait else cp.start()

        def per_tail(e, carry):
            start, n = tail_ref[2 * e], tail_ref[2 * e + 1]
            nfull = n // ch

            def per_chunk(c, carry2):
                go(start + c * ch, ch)
                return carry2

            lax.fori_loop(0, nfull, per_chunk, 0)
            done = nfull * ch
            for r in MOE_REMAINDERS:
                has = (n // r) % 2

                @pl.when(has == 1)
                def _(done=done, r=r):
                    go(start + done, r)

                done = done + has * r
            return carry

        lax.fori_loop(0, n_e, per_tail, 0)

        def per_block(blk, carry):
            go(blk * MOE_BLOCK, MOE_BLOCK)
            return carry

        lax.fori_loop(nu_ref[0], xs_ref.shape[0] // MOE_BLOCK, per_block, 0)

    @pl.when(i == pl.num_programs(0) - 1)
    def _():
        wait_tile(i)
        zero[...] = jnp.zeros_like(zero)
        zero_fill(wait=False)
        zero_fill(wait=True)


def _dispatch(x1, et, wt, plan):
    T, D = x1.shape
    nt = et.shape[0]
    tm = T // nt
    E = N_EXPERTS
    P = _moe_rows(T, nt, E)
    tile3 = lambda i, *_: (i, 0, 0)
    grid_spec = pltpu.PrefetchScalarGridSpec(
        num_scalar_prefetch=6,
        grid=(nt,),
        in_specs=[pl.BlockSpec((tm, D), lambda i, *_: (i, 0)),
                  pl.BlockSpec((1, TOP_K, tm), tile3), pl.BlockSpec((1, TOP_K, tm), tile3),
                  pl.BlockSpec((1, E, 1), tile3)],
        out_specs=[pl.BlockSpec(memory_space=pl.ANY),
                   pl.BlockSpec((tm, SUBLANES), lambda i, *_: (i, 0))],
        scratch_shapes=[pltpu.VMEM((_stage_rows(tm) + MOE_CHUNK, D + LANES), F32),
                        pltpu.VMEM((MOE_BLOCK, D + LANES), F32),
                        pltpu.SemaphoreType.DMA((1,))],
    )
    return pl.pallas_call(
        functools.partial(_dispatch_kernel, tm=tm),
        grid_spec=grid_spec,
        out_shape=[jax.ShapeDtypeStruct((P, D + LANES), F32), jax.ShapeDtypeStruct((T, SUBLANES), jnp.int32)],
        compiler_params=_params(("arbitrary",)),
        name="moe_dispatch",
    )(plan["strip"], plan["off"], plan["nch"], plan["nct"], plan["tail"], plan["n_used"], x1, et, wt,
      plan["off_col"])


def _combine_kernel(strip_ref, off_ref, n8_ref, exact_ref, ys_ref, lp_ref, x_ref, g_ref, b_ref, *rest, tm, kd):
    if kd:
        wqk_ref, wvg_ref, wgr_ref, w2_ref, b2_ref, xo_ref, qkl_ref, vg_ref, land, sem = rest
    else:
        xo_ref, land, sem = rest
    i = pl.program_id(0)
    n_e = N_EXPERTS
    nrow = _stage_rows(tm)
    ch = MOE_CHUNK
    sizes = (ch,) + MOE_REMAINDERS

    def strip_copy(src0, dst0, rows):
        return pltpu.make_async_copy(ys_ref.at[pl.ds(pl.multiple_of(src0, SUBLANES), rows)],
                                     land.at[pl.ds(pl.multiple_of(dst0, SUBLANES), rows)], sem.at[0])

    @pl.when(i == 0)
    def _():
        land[...] = jnp.zeros_like(land)

    def per_expert(e, carry):
        idx = i * n_e + e
        n8, src, dst = n8_ref[idx], strip_ref[idx], off_ref[idx]
        nfull = n8 // ch

        def per_chunk(c, carry2):
            strip_copy(src + c * ch, dst + c * ch, ch).start()
            return carry2

        lax.fori_loop(0, nfull, per_chunk, 0)
        done = nfull * ch
        for r in MOE_REMAINDERS:
            has = (n8 // r) % 2

            @pl.when(has == 1)
            def _(done=done, r=r):
                strip_copy(src + done, dst + done, r).start()

            done = done + has * r
        return carry

    lax.fori_loop(0, n_e, per_expert, 0)

    lp = lp_ref[...]
    cio = lax.broadcasted_iota(jnp.int32, (tm, nrow), 1)
    selt = jnp.zeros((tm, nrow), F32)
    for k in range(TOP_K):
        selt = selt + jnp.where(cio == lp[:, k:k + 1], 1.0, 0.0)
    selt = selt.astype(BF16)

    for s, rows in enumerate(sizes):
        def one(c, carry, rows=rows):
            strip_copy(0, 0, rows).wait()
            return carry

        lax.fori_loop(0, exact_ref[i * len(sizes) + s], one, 0)

    v = land[0:nrow, :]
    hi = v.astype(BF16)
    lo = (v - hi.astype(F32)).astype(BF16)
    ffn = jnp.dot(selt, hi, preferred_element_type=F32) + jnp.dot(selt, lo, preferred_element_type=F32)
    xn = _layer_norm(DN_ALPHA * x_ref[...] + ffn, g_ref[...], b_ref[...])
    xo_ref[...] = xn
    if kd:
        xb = xn.astype(BF16)
        qkl_ref[:, :2 * kd] = jnp.dot(xb, wqk_ref[...], preferred_element_type=F32)
        vg_ref[...] = jnp.dot(xb, wvg_ref[...], preferred_element_type=F32)
        gr = jnp.dot(xb, wgr_ref[...], preferred_element_type=F32)
        z = jnp.dot(gr, w2_ref[...], precision=HIGHEST, preferred_element_type=F32) + b2_ref[...]
        log_sig = jnp.minimum(z, 0.0) - jnp.log1p(jnp.exp(-jnp.abs(z)))
        qkl_ref[:, 2 * kd:] = log_sig / GLA_GATE_TAU


def _combine_ln(ys, lp, x, plan, g, b, gla=None):
    T, D = x.shape
    nt = plan["nct"].shape[0]
    tm = T // nt
    const = lambda i, *_: (0, 0)
    row = lambda i, *_: (i, 0)
    in_specs = [pl.BlockSpec(memory_space=pl.ANY), pl.BlockSpec((tm, SUBLANES), row), pl.BlockSpec((tm, D), row),
                pl.BlockSpec((1, D), const), pl.BlockSpec((1, D), const)]
    out_specs = [pl.BlockSpec((tm, D), row)]
    out_shape = [jax.ShapeDtypeStruct((T, D), F32)]
    args = [ys, lp, x, g.reshape(1, D), b.reshape(1, D)]
    kd = 0
    if gla is not None:
        w_in1, w_gate2, b_gate, kd, vd = gla
        ws = [w_in1[:, :2 * kd].astype(BF16), w_in1[:, 2 * kd:2 * kd + 2 * vd].astype(BF16),
              w_in1[:, 2 * kd + 2 * vd:].astype(BF16), w_gate2.astype(F32), b_gate.reshape(1, kd)]
        in_specs += [pl.BlockSpec(w.shape, const) for w in ws]
        args += ws
        out_specs += [pl.BlockSpec((tm, 3 * kd), row), pl.BlockSpec((tm, 2 * vd), row)]
        out_shape += [jax.ShapeDtypeStruct((T, 3 * kd), F32), jax.ShapeDtypeStruct((T, 2 * vd), F32)]
    grid_spec = pltpu.PrefetchScalarGridSpec(
        num_scalar_prefetch=4,
        grid=(nt,),
        in_specs=in_specs,
        out_specs=out_specs,
        scratch_shapes=[pltpu.VMEM((_stage_rows(tm), D), F32), pltpu.SemaphoreType.DMA((1,))],
    )
    return pl.pallas_call(
        functools.partial(_combine_kernel, tm=tm, kd=kd),
        grid_spec=grid_spec,
        out_shape=out_shape,
        compiler_params=_params(("arbitrary",)),
        name="moe_combine_ln",
    )(plan["strip"], plan["off"], plan["n8"], plan["exact"], *args)


def _moe_ln(x1, et, wt, cnt, w_gate_up, b_gate_up, w_down, b_down, g, b, gla=None):
    T = x1.shape[0]
    plan = _moe_plan(cnt, T)
    xs, lp = _dispatch(x1, et, wt, plan)
    wgu = _deinterleave_gate_up(w_gate_up)
    bgu = jnp.concatenate([b_gate_up[:, 0::2], b_gate_up[:, 1::2]], axis=-1)[:, None, :].astype(F32)
    ys = _expert_ffn(xs, plan, wgu, w_down.astype(BF16), bgu, b_down[:, None, :].astype(F32))
    return _combine_ln(ys, lp, x1, plan, g, b, gla)


def _gla_kernel(q_ref, k_ref, la_ref, v_ref, g_ref, nw_ref, o_ref, s_scr, *, tl, dk, dv):
    C = GLA_CHUNK

    @pl.when(pl.program_id(2) == 0)
    def _():
        s_scr[...] = jnp.zeros_like(s_scr)

    ri = lax.broadcasted_iota(jnp.int32, (C, C), 0)
    ci = lax.broadcasted_iota(jnp.int32, (C, C), 1)
    lower = ri >= ci
    tri = jnp.where(lower, 1.0, 0.0).astype(F32)
    for c in range(tl // C):
        rows = slice(c * C, (c + 1) * C)
        lg = la_ref[0, rows, :]
        bcum = jnp.dot(tri, lg, precision=HIGHEST, preferred_element_type=F32)
        btot = bcum[C - 1:C, :]
        q_in = q_ref[0, rows, :] * (dk ** -0.5) * jnp.exp(bcum)
        kc = k_ref[0, rows, :]
        k_in = kc * jnp.exp(-bcum)
        k_st = kc * jnp.exp(btot - bcum)
        vb = v_ref[0, rows, :].astype(BF16)
        qb = q_in.astype(BF16)
        att = lax.dot_general(qb, k_in.astype(BF16), (((1,), (1,)), ((), ())), preferred_element_type=F32)
        att = jnp.where(lower, att, 0.0)
        s_prev = s_scr[...]
        o = (jnp.dot(att.astype(BF16), vb, preferred_element_type=F32)
             + jnp.dot(qb, s_prev.astype(BF16), preferred_element_type=F32))
        kv = jnp.dot(k_st.T.astype(BF16), vb, preferred_element_type=F32)
        decay = jnp.exp(jnp.broadcast_to(btot, (8, dk))).T[:, 0:1]
        s_scr[...] = decay * s_prev + kv
        o = o * lax.rsqrt(jnp.mean(o * o, axis=-1, keepdims=True) + RMS_EPS) * nw_ref[...]
        gg = g_ref[0, rows, :]
        o_ref[0, rows, :] = o * (gg * jax.nn.sigmoid(gg))


def _gla(qkl, vg, norm_w, B, L, tl=256):
    T, kd3 = qkl.shape
    kd = kd3 // 3
    vd = vg.shape[1] // 2
    H = GLA_HEADS
    dk, dv = kd // H, vd // H
    qkl3 = qkl.reshape(B, L, kd3)
    vg3 = vg.reshape(B, L, 2 * vd)
    return pl.pallas_call(
        functools.partial(_gla_kernel, tl=tl, dk=dk, dv=dv),
        grid=(B, H, L // tl),
        in_specs=[pl.BlockSpec((1, tl, dk), lambda b, h, l: (b, l, h)),
                  pl.BlockSpec((1, tl, dk), lambda b, h, l: (b, l, H + h)),
                  pl.BlockSpec((1, tl, dk), lambda b, h, l: (b, l, 2 * H + h)),
                  pl.BlockSpec((1, tl, dv), lambda b, h, l: (b, l, h)),
                  pl.BlockSpec((1, tl, dv), lambda b, h, l: (b, l, H + h)),
                  pl.BlockSpec((1, dv), lambda b, h, l: (0, 0))],
        out_specs=pl.BlockSpec((1, tl, dv), lambda b, h, l: (b, l, h)),
        out_shape=jax.ShapeDtypeStruct((B, L, vd), F32),
        scratch_shapes=[pltpu.VMEM((dk, dv), F32)],
        compiler_params=_params(("parallel", "parallel", "arbitrary")),
        name="gla_mixer",
    )(qkl3, qkl3, qkl3, vg3, vg3, norm_w.reshape(1, dv).astype(F32))


def kernel(x, w_in0, s5_lam_re, s5_lam_im, s5_log_dt, s5_b_re, s5_b_im, s5_c_re, s5_c_im, s5_d, s5_w_glu,
           s5_b_glu, rel_bias, w_out0, w_in1, gla_w_gate2, gla_b_gate, gla_norm_w, w_out1, ln_mix_g, ln_mix_b,
           ln_ffn_g, ln_ffn_b, router_w, router_b, exp_w_gate_up, exp_b_gate_up, exp_w_down, exp_b_down):
    B, L, D = x.shape
    T = B * L
    s5w = s5_w_glu.shape[-1]
    kd = gla_w_gate2.shape[-1]
    vd = gla_norm_w.shape[-1] * GLA_HEADS

    u_tm, k_att, qt_att, vt_att = _inproj0(x, w_in0[0], s5w)
    y_a = _s5_mixer(u_tm.reshape(L, B, s5w), s5_lam_re[0], s5_lam_im[0], s5_log_dt[0], s5_b_re[0], s5_b_im[0],
                    s5_c_re[0], s5_c_im[0], s5_d[0].reshape(-1), s5_w_glu[0], s5_b_glu[0])
    y_b = _moba(qt_att, k_att, vt_att, rel_bias)
    tl = MOE_TILE
    w0 = w_out0[0].astype(BF16)
    ys = [(y_a.reshape(L, B * s5w), pl.BlockSpec((tl, s5w), lambda bb, l: (l, bb))),
          (y_b, pl.BlockSpec((1, tl, D - s5w), lambda bb, l: (bb, l, 0)))]
    x1, et, wt, cnt = _mix_ln(x, ys, [w0[:s5w], w0[s5w:]], ln_mix_g[0], ln_mix_b[0], router_w[0], router_b[0])
    x2, qkl, vg = _moe_ln(x1.reshape(T, D), et, wt, cnt, exp_w_gate_up[0], exp_b_gate_up[0], exp_w_down[0],
                          exp_b_down[0], ln_ffn_g[0], ln_ffn_b[0],
                          gla=(w_in1[0], gla_w_gate2[0], gla_b_gate[0], kd, vd))

    y_c = _gla(qkl, vg, gla_norm_w[0], B, L)
    ys = [(y_c, pl.BlockSpec((1, tl, vd), lambda bb, l: (bb, l, 0)))]
    x3, et, wt, cnt = _mix_ln(x2.reshape(B, L, D), ys, [w_out1[0].astype(BF16)], ln_mix_g[1], ln_mix_b[1],
                              router_w[1], router_b[1])
    (out,) = _moe_ln(x3.reshape(T, D), et, wt, cnt, exp_w_gate_up[1], exp_b_gate_up[1], exp_w_down[1],
                     exp_b_down[1], ln_ffn_g[1], ln_ffn_b[1])
    return out.reshape(B, L, D)
```

```python
import functools
import math

import jax
import jax.numpy as jnp
import numpy as np
from jax import lax
from jax.experimental import pallas as pl
from jax.experimental.pallas import tpu as pltpu

F32 = jnp.float32
BF16 = jnp.bfloat16
HIGHEST = lax.Precision.HIGHEST

DEPTH = 2
S5_GROUP = 16
S5_STATE = 64
MOBA_HEAD_DIM = 64
MOBA_BLOCK = 256
MOBA_TOPK = 3
REL_BUCKETS = 32
REL_MAX_DIST = 2048
GLA_HEADS = 4
GLA_GATE_TAU = 16.0
GLA_CHUNK = 64
N_EXPERTS = 32
TOP_K = 4
SWIGLU_LIMIT = 7.0
SWIGLU_ALPHA = 1.702
MOE_BLOCK = 512
MOE_TILE = 256
MOE_CHUNK = 32
SUBLANES = 8
MOE_REMAINDERS = (16, 8)
DN_ALPHA = (2 * DEPTH) ** 0.25
LN_EPS = 1e-5
RMS_EPS = 1e-5

V7X_VMEM_LIMIT_BYTES = 56 * 1024 * 1024
LANES = 128

S5_GROUPS_PER_CHUNK = LANES // S5_GROUP
S5_CHUNK_STATES = S5_GROUPS_PER_CHUNK * S5_STATE


def _params(sem):
    return pltpu.CompilerParams(dimension_semantics=sem, vmem_limit_bytes=V7X_VMEM_LIMIT_BYTES)


def _layer_norm(r, g, b):
    mu = jnp.mean(r, axis=-1, keepdims=True)
    c = r - mu
    var = jnp.mean(c * c, axis=-1, keepdims=True)
    return c * lax.rsqrt(var + LN_EPS) * g + b


def _inproj0_kernel(x_ref, wuk_ref, wqvt_ref, u_ref, k_ref, qt_ref, vt_ref, *, s5w):
    xb = x_ref[0].astype(BF16)
    h = jnp.dot(xb, wuk_ref[...], preferred_element_type=F32)
    u_ref[...] = h[:, :s5w]
    k_ref[0] = h[:, s5w:]
    ht = lax.dot_general(wqvt_ref[...], xb, (((1,), (1,)), ((), ())), preferred_element_type=F32)
    aw = ht.shape[0] // 2
    qt_ref[0] = ht[:aw]
    vt_ref[0] = ht[aw:].astype(BF16)


def _inproj0(x, w_in0, s5w, tl=512):
    B, L, D = x.shape
    aw = (w_in0.shape[1] - s5w) // 3
    wb = w_in0.astype(BF16)
    wuk = jnp.concatenate([wb[:, :s5w], wb[:, s5w + aw:s5w + 2 * aw]], axis=1)
    wqvt = jnp.concatenate([wb[:, s5w:s5w + aw], wb[:, s5w + 2 * aw:]], axis=1).T
    return pl.pallas_call(
        functools.partial(_inproj0_kernel, s5w=s5w),
        grid=(B, L // tl),
        in_specs=[pl.BlockSpec((1, tl, D), lambda b, l: (b, l, 0)),
                  pl.BlockSpec(wuk.shape, lambda b, l: (0, 0)),
                  pl.BlockSpec(wqvt.shape, lambda b, l: (0, 0))],
        out_specs=[pl.BlockSpec((tl, s5w), lambda b, l: (l, b)),
                   pl.BlockSpec((1, tl, aw), lambda b, l: (b, l, 0)),
                   pl.BlockSpec((1, aw, tl), lambda b, l: (b, 0, l)),
                   pl.BlockSpec((1, aw, tl), lambda b, l: (b, 0, l))],
        out_shape=[jax.ShapeDtypeStruct((L, B * s5w), F32),
                   jax.ShapeDtypeStruct((B, L, aw), F32),
                   jax.ShapeDtypeStruct((B, aw, L), F32),
                   jax.ShapeDtypeStruct((B, aw, L), BF16)],
        compiler_params=_params(("parallel", "parallel")),
        name="inproj0",
    )(x, wuk, wqvt)


def _s5_discretize(lam_re, lam_im, log_dt, b_re, b_im):
    dt = jnp.exp(log_dt.astype(F32))[:, None]
    lr, li = lam_re.astype(F32), lam_im.astype(F32)
    mag = jnp.exp(lr * dt)
    ab_re, ab_im = mag * jnp.cos(li * dt), mag * jnp.sin(li * dt)
    er, ei = ab_re - 1.0, ab_im
    den = lr * lr + li * li
    q_re = (er * lr + ei * li) / den
    q_im = (ei * lr - er * li) / den
    br_, bi_ = b_re.astype(F32), b_im.astype(F32)
    bb_re = q_re[..., None] * br_ - q_im[..., None] * bi_
    bb_im = q_re[..., None] * bi_ + q_im[..., None] * br_
    return ab_re, ab_im, bb_re, bb_im


def _s5_kernel(u_ref, bm_ref, cm_ref, are_ref, aim_ref, d_ref, wg_ref, bg_ref, y_ref,
               s_scr, st_scr, z_scr, *, tl, nb, nchunk):
    ns = S5_CHUNK_STATES

    @pl.when(pl.program_id(0) == 0)
    def _():
        st_scr[...] = jnp.zeros_like(st_scr)

    u = u_ref[...].reshape(tl * nb, nchunk * LANES)
    for j in range(nchunk):
        uj = u[:, j * LANES:(j + 1) * LANES]
        s_scr[...] = jnp.dot(uj.astype(BF16), bm_ref[j], preferred_element_type=F32)
        ar = jnp.broadcast_to(are_ref[j], (nb, ns))
        ai = jnp.broadcast_to(aim_ref[j], (nb, ns))

        def step(t, carry, ar=ar, ai=ai):
            sre, sim = carry
            r0 = pl.multiple_of(t * nb, nb)
            nre = ar * sre - ai * sim + s_scr[pl.ds(r0, nb), 0:ns]
            nim = ar * sim + ai * sre + s_scr[pl.ds(r0, nb), ns:2 * ns]
            s_scr[pl.ds(r0, nb), 0:ns] = nre
            s_scr[pl.ds(r0, nb), ns:2 * ns] = nim
            return nre, nim

        sre, sim = lax.fori_loop(0, tl, step, (st_scr[j, :, 0:ns], st_scr[j, :, ns:2 * ns]), unroll=2)
        st_scr[j, :, 0:ns] = sre
        st_scr[j, :, ns:2 * ns] = sim
        yj = jnp.dot(s_scr[...].astype(BF16), cm_ref[j], preferred_element_type=F32)
        z_scr[:, j * LANES:(j + 1) * LANES] = yj + uj * d_ref[:, j * LANES:(j + 1) * LANES]
    z = jax.nn.gelu(z_scr[...])
    gate = jax.nn.sigmoid(jnp.dot(z.astype(BF16), wg_ref[...], preferred_element_type=F32) + bg_ref[...])
    y_ref[...] = (z * gate).reshape(tl, nb, nchunk * LANES)


def _s5_mixer(u_tm, lam_re, lam_im, log_dt, b_re, b_im, c_re, c_im, d_skip, w_glu, b_glu, tl=32):
    L, B, W = u_tm.shape
    G, P, H = lam_re.shape[0], S5_STATE, S5_GROUP
    gc = S5_GROUPS_PER_CHUNK
    nchunk = G // gc
    ab_re, ab_im, bb_re, bb_im = _s5_discretize(lam_re, lam_im, log_dt, b_re, b_im)
    eye = jnp.eye(gc, dtype=F32)

    def b_blocks(bb):
        return jnp.einsum('jgph,gk->jghkp', bb.reshape(nchunk, gc, P, H), eye).reshape(nchunk, gc * H, gc * P)

    def c_blocks(cc):
        return jnp.einsum('jghp,gk->jgpkh', cc.reshape(nchunk, gc, H, P), eye).reshape(nchunk, gc * P, gc * H)

    bm = jnp.concatenate([b_blocks(bb_re), b_blocks(bb_im)], axis=2).astype(BF16)
    cm = jnp.concatenate([c_blocks(c_re.astype(F32)), -c_blocks(c_im.astype(F32))], axis=1).astype(BF16)
    are = ab_re.reshape(nchunk, 1, gc * P)
    aim = ab_im.reshape(nchunk, 1, gc * P)
    m = tl * B
    const3 = lambda l: (0, 0, 0)
    const2 = lambda l: (0, 0)
    return pl.pallas_call(
        functools.partial(_s5_kernel, tl=tl, nb=B, nchunk=nchunk),
        grid=(L // tl,),
        in_specs=[pl.BlockSpec((tl, B, W), lambda l: (l, 0, 0)),
                  pl.BlockSpec(bm.shape, const3), pl.BlockSpec(cm.shape, const3),
                  pl.BlockSpec(are.shape, const3), pl.BlockSpec(aim.shape, const3),
                  pl.BlockSpec((1, W), const2), pl.BlockSpec((W, W), const2), pl.BlockSpec((1, W), const2)],
        out_specs=pl.BlockSpec((tl, B, W), lambda l: (l, 0, 0)),
        out_shape=jax.ShapeDtypeStruct((L, B, W), F32),
        scratch_shapes=[pltpu.VMEM((m, 2 * S5_CHUNK_STATES), F32),
                        pltpu.VMEM((nchunk, B, 2 * S5_CHUNK_STATES), F32),
                        pltpu.VMEM((m, W), F32)],
        compiler_params=_params(("arbitrary",)),
        name="s5_mixer",
    )(u_tm, bm, cm, are, aim, d_skip.reshape(1, W).astype(F32), w_glu.astype(BF16),
      b_glu.reshape(1, W).astype(F32))


def _rel_bucket(n):
    n = jnp.maximum(n, 0)
    max_exact = REL_BUCKETS // 2
    nf = jnp.maximum(n, 1).astype(F32)
    large = max_exact + (jnp.log(nf / max_exact) / math.log(REL_MAX_DIST / max_exact)
                         * (REL_BUCKETS - max_exact)).astype(jnp.int32)
    large = jnp.minimum(large, REL_BUCKETS - 1)
    return jnp.where(n < max_exact, n, large)


MOBA_Q_TILE = 256
MOBA_ONES_ROWS = 16
MOBA_NEG = -1e30
LOG2E = math.log2(math.e)


def _moba_kernel(qt_ref, k_ref, vt_ref, wv_ref, hot_ref, o_ref, bias_scr, kk_scr, va_scr, km_scr,
                 *, nblk, bs, dh, topk):
    b, i = pl.program_id(1), pl.program_id(2)
    tq = MOBA_Q_TILE
    hpb = LANES // dh
    nsub = bs // tq
    own, sub = i // nsub, i % nsub

    @pl.when((b == 0) & (i == 0))
    def _():
        r_io = lax.broadcasted_iota(jnp.int32, (bs, 2 * bs), 0)
        c_io = lax.broadcasted_iota(jnp.int32, (bs, 2 * bs), 1)
        for hh in range(hpb):
            for dd in range(nblk):
                t = pltpu.roll(jnp.broadcast_to(wv_ref[hh, dd:dd + 1, :], (bs, 2 * bs)), bs + 1, 1,
                               stride=1, stride_axis=0)
                if dd == 0:
                    t = jnp.where(c_io >= r_io, t, MOBA_NEG)
                for s in range(nsub):
                    bias_scr[hh, dd, s] = t[:, s * tq:(s + 1) * tq]

    @pl.when(i == 0)
    def _():
        kf = k_ref[0]
        km_scr[...] = jnp.mean(kf.reshape(nblk, bs, LANES), axis=1)
        kk_scr[:, 0:LANES] = kf.astype(BF16)
        kk_scr[:, LANES:] = hot_ref[...]
        for n in range(nblk):
            va_scr[n, 0:LANES, :] = vt_ref[0, :, n * bs:(n + 1) * bs]
            va_scr[n, LANES:, :] = jnp.ones((MOBA_ONES_ROWS, bs), BF16)

    q2 = qt_ref[0] * (dh ** -0.5 * LOG2E)
    f_io = lax.broadcasted_iota(jnp.int32, (LANES, tq), 0)
    blk_io = lax.broadcasted_iota(jnp.int32, (nblk, tq), 0)
    qms, pens = [], []
    for hh in range(hpb):
        qm = jnp.where((f_io >= hh * dh) & (f_io < (hh + 1) * dh), q2, 0.0)
        gate = jnp.dot(km_scr[...], qm, precision=HIGHEST, preferred_element_type=F32)
        cnt = jnp.zeros((nblk, tq), jnp.int32)
        for m in range(nblk):
            gm = gate[m:m + 1, :]
            beats = (gm > gate) | ((gm == gate) & (m < blk_io))
            cnt = cnt + jnp.where(beats & (m < own), 1, 0)
        keep = (blk_io >= own) | (cnt < topk)
        pens.append(jnp.where(keep, 0.0, MOBA_NEG))
        qms.append(qm)
    qa = jnp.concatenate([jnp.concatenate(qms, axis=1), jnp.concatenate(pens, axis=1),
                          jnp.zeros((LANES - nblk, hpb * tq), F32)], axis=0).astype(BF16)

    def scores(j):
        r0 = pl.multiple_of(j * bs, bs)
        s = jnp.dot(kk_scr[pl.ds(r0, bs), :], qa, preferred_element_type=F32)
        return s + jnp.concatenate([bias_scr[hh, own - j, sub] for hh in range(hpb)], axis=1)

    def pair(ja, jb, pen_b, m_i, acc):
        sa = scores(ja)
        sb = scores(jb) + pen_b
        m_n = jnp.maximum(m_i, jnp.maximum(jnp.max(sa, axis=0, keepdims=True), jnp.max(sb, axis=0, keepdims=True)))
        alpha = jnp.exp2(m_i - m_n)
        pa = jnp.exp2(sa - m_n).astype(BF16)
        pb = jnp.exp2(sb - m_n).astype(BF16)
        return m_n, (alpha * acc + jnp.dot(va_scr[ja], pa, preferred_element_type=F32)
                     + jnp.dot(va_scr[jb], pb, preferred_element_type=F32))

    odd = own % 2
    m1, acc1 = pair(own, jnp.maximum(own - 1, 0), jnp.where(odd == 1, 0.0, MOBA_NEG),
                    jnp.full((1, hpb * tq), MOBA_NEG, F32), jnp.zeros((LANES + MOBA_ONES_ROWS, hpb * tq), F32))
    _, acc_f = lax.fori_loop(0, own // 2, lambda jj, c: pair(2 * jj, 2 * jj + 1, 0.0, *c), (m1, acc1))
    on = acc_f[0:LANES, :] / acc_f[LANES:LANES + 1, :]
    ot = jnp.concatenate([on[hh * dh:(hh + 1) * dh, hh * tq:(hh + 1) * tq] for hh in range(hpb)], axis=0)
    o_ref[0] = ot.T


def _moba(qt, k, vt, rel_bias):
    B, L, W = k.shape
    dh, bs, tq = MOBA_HEAD_DIM, MOBA_BLOCK, MOBA_Q_TILE
    H = W // dh
    nblk = L // bs
    hpb = LANES // dh
    dist = jnp.arange(L, dtype=jnp.int32)
    by_dist = rel_bias.astype(F32).T[:, _rel_bucket(dist)] * LOG2E
    idx = np.clip(np.arange(nblk)[:, None] * bs - (bs - 1) + np.arange(2 * bs)[None, :], 0, L - 1)
    vecs = by_dist[:, idx]
    hot = np.zeros((L, LANES), np.float32)
    hot[np.arange(L), np.arange(L) // bs] = 1.0
    return pl.pallas_call(
        functools.partial(_moba_kernel, nblk=nblk, bs=bs, dh=dh, topk=min(MOBA_TOPK, nblk)),
        grid=(H // hpb, B, L // tq),
        in_specs=[pl.BlockSpec((1, LANES, tq), lambda h, b, i: (b, h, i)),
                  pl.BlockSpec((1, L, LANES), lambda h, b, i: (b, 0, h)),
                  pl.BlockSpec((1, LANES, L), lambda h, b, i: (b, h, 0)),
                  pl.BlockSpec((hpb, nblk, 2 * bs), lambda h, b, i: (h, 0, 0)),
                  pl.BlockSpec((L, LANES), lambda h, b, i: (0, 0))],
        out_specs=pl.BlockSpec((1, tq, LANES), lambda h, b, i: (b, i, h)),
        out_shape=jax.ShapeDtypeStruct((B, L, W), F32),
        scratch_shapes=[pltpu.VMEM((hpb, nblk, bs // tq, bs, tq), F32),
                        pltpu.VMEM((L, 2 * LANES), BF16),
                        pltpu.VMEM((nblk, LANES + MOBA_ONES_ROWS, bs), BF16),
                        pltpu.VMEM((nblk, LANES), F32)],
        compiler_params=_params(("arbitrary", "arbitrary", "arbitrary")),
        name="moba_attention",
    )(qt, k, vt, vecs, jnp.asarray(hot, BF16))


def _route_top_k(xn, wrt_ref, br_ref):
    lt = lax.dot_general(wrt_ref[...], xn, (((1,), (1,)), ((), ())), precision=HIGHEST,
                         preferred_element_type=F32) + br_ref[...]
    n_e = lt.shape[0]
    eid = lax.broadcasted_iota(jnp.int32, lt.shape, 0)
    cur = lt
    vals, idxs = [], []
    for _ in range(TOP_K):
        m = jnp.max(cur, axis=0, keepdims=True)
        idx = jnp.min(jnp.where(cur == m, eid, n_e), axis=0, keepdims=True)
        vals.append(m)
        idxs.append(idx)
        cur = jnp.where(eid == idx, -jnp.inf, cur)
    ex = [jnp.exp(v - vals[0]) for v in vals]
    den = ex[0]
    for t in ex[1:]:
        den = den + t
    hot = jnp.zeros(lt.shape, F32)
    for idx in idxs:
        hot = hot + jnp.where(eid == idx, 1.0, 0.0)
    return (jnp.concatenate(idxs, axis=0), jnp.concatenate([t / den for t in ex], axis=0),
            jnp.sum(hot, axis=1, keepdims=True))


def _mix_ln_kernel(*refs, n_in):
    x_ref = refs[0]
    y_refs = refs[1:1 + n_in]
    w_refs = refs[1 + n_in:1 + 2 * n_in]
    g_ref, b_ref, wrt_ref, br_ref, x1_ref, et_ref, wt_ref, cnt_ref = refs[1 + 2 * n_in:]
    mix = None
    for y_ref, w_ref in zip(y_refs, w_refs):
        y = y_ref[...]
        y = y.reshape(y.shape[-2], y.shape[-1]).astype(BF16)
        t = jnp.dot(y, w_ref[...], preferred_element_type=F32)
        mix = t if mix is None else mix + t
    xn = _layer_norm(DN_ALPHA * x_ref[0] + mix, g_ref[...], b_ref[...])
    x1_ref[0] = xn
    et, wt, cnt = _route_top_k(xn, wrt_ref, br_ref)
    et_ref[0] = et
    wt_ref[0] = wt
    cnt_ref[0] = cnt


def _mix_ln(x, ys, ws, g, b, wr, br):
    B, L, D = x.shape
    E = wr.shape[1]
    tl = MOE_TILE
    nl = L // tl
    const = lambda bb, l: (0, 0)
    tile = lambda bb, l: (bb * nl + l, 0, 0)
    in_specs = [pl.BlockSpec((1, tl, D), lambda bb, l: (bb, l, 0))]
    in_specs += [spec for _, spec in ys]
    in_specs += [pl.BlockSpec(w.shape, const) for w in ws]
    in_specs += [pl.BlockSpec((1, D), const), pl.BlockSpec((1, D), const),
                 pl.BlockSpec((E, D), const), pl.BlockSpec((E, 1), const)]
    return pl.pallas_call(
        functools.partial(_mix_ln_kernel, n_in=len(ys)),
        grid=(B, nl),
        in_specs=in_specs,
        out_specs=[pl.BlockSpec((1, tl, D), lambda bb, l: (bb, l, 0)),
                   pl.BlockSpec((1, TOP_K, tl), tile), pl.BlockSpec((1, TOP_K, tl), tile),
                   pl.BlockSpec((1, E, 1), tile)],
        out_shape=[jax.ShapeDtypeStruct((B, L, D), F32),
                   jax.ShapeDtypeStruct((B * nl, TOP_K, tl), jnp.int32),
                   jax.ShapeDtypeStruct((B * nl, TOP_K, tl), F32),
                   jax.ShapeDtypeStruct((B * nl, E, 1), F32)],
        compiler_params=_params(("parallel", "parallel")),
        name="outproj_ln_router",
    )(x, *[a for a, _ in ys], *ws, g.reshape(1, D), b.reshape(1, D), wr.astype(F32).T, br.reshape(E, 1))


def _deinterleave_kernel(w_ref, p_ref, o_ref, *, ff):
    w = w_ref[0, 0].astype(BF16)
    for c in range(2 * ff // (2 * LANES)):
        t = jnp.dot(w[:, 2 * LANES * c:2 * LANES * (c + 1)], p_ref[...], preferred_element_type=F32)
        o_ref[0, :, LANES * c:LANES * (c + 1)] = t[:, :LANES].astype(BF16)
        o_ref[0, :, ff + LANES * c:ff + LANES * (c + 1)] = t[:, LANES:].astype(BF16)


def _deinterleave_gate_up(w_gate_up, layer, tk=512):
    _, E, D, F2 = w_gate_up.shape
    src = np.arange(2 * LANES)
    dst = np.where(src % 2 == 0, src // 2, LANES + src // 2)
    perm = np.zeros((2 * LANES, 2 * LANES), np.float32)
    perm[src, dst] = 1.0
    return pl.pallas_call(
        functools.partial(_deinterleave_kernel, ff=F2 // 2),
        grid=(E, D // tk),
        in_specs=[pl.BlockSpec((1, 1, tk, F2), lambda e, k: (layer, e, k, 0)),
                  pl.BlockSpec((2 * LANES, 2 * LANES), lambda e, k: (0, 0))],
        out_specs=pl.BlockSpec((1, tk, F2), lambda e, k: (e, k, 0)),
        out_shape=jax.ShapeDtypeStruct((E, D, F2), BF16),
        compiler_params=_params(("parallel", "parallel")),
        name="deinterleave_gate_up",
    )(w_gate_up, jnp.asarray(perm, BF16))


def _expert_kernel(be_ref, nu_ref, xs_ref, wgu_ref, wd_ref, bgu_ref, bd_ref, ys_ref):
    blk = pl.program_id(0)
    ff = wd_ref.shape[2]
    d = wd_ref.shape[3]

    @pl.when(blk < nu_ref[0])
    def _():
        x = xs_ref[:, :d].astype(BF16)
        pw = xs_ref[:, d:d + 1]
        h = jnp.dot(x, wgu_ref[0], preferred_element_type=F32) + bgu_ref[0]
        g = h[:, :ff]
        u = h[:, ff:]
        g = jnp.minimum(g, SWIGLU_LIMIT)
        u = jnp.clip(u, -SWIGLU_LIMIT, SWIGLU_LIMIT)
        act = g * jax.nn.sigmoid(SWIGLU_ALPHA * g) * (u + 1.0)
        y = jnp.dot(act.astype(BF16), wd_ref[0, 0].astype(BF16), preferred_element_type=F32) + bd_ref[0]
        ys_ref[...] = y * pw

    @pl.when(blk >= nu_ref[0])
    def _():
        ys_ref[...] = jnp.zeros_like(ys_ref)


def _expert_ffn(xs, plan, wgu, w_down, layer, bgu, bd):
    P, DW = xs.shape
    F, D = w_down.shape[2], w_down.shape[3]
    bm = MOE_BLOCK
    wmap = lambda i, be, nu: (be[i], 0, 0)
    rmap = lambda i, be, nu: (i, 0)
    grid_spec = pltpu.PrefetchScalarGridSpec(
        num_scalar_prefetch=2,
        grid=(P // bm,),
        in_specs=[pl.BlockSpec((bm, DW), rmap),
                  pl.BlockSpec((1, D, 2 * F), wmap),
                  pl.BlockSpec((1, 1, F, D), lambda i, be, nu: (layer, be[i], 0, 0)),
                  pl.BlockSpec((1, 1, 2 * F), wmap), pl.BlockSpec((1, 1, D), wmap)],
        out_specs=pl.BlockSpec((bm, D), rmap),
    )
    return pl.pallas_call(
        _expert_kernel,
        grid_spec=grid_spec,
        out_shape=jax.ShapeDtypeStruct((P, D), F32),
        compiler_params=_params(("arbitrary",)),
        name="moe_experts",
    )(plan["blk_expert"], plan["n_used"], xs, wgu, w_down, bgu, bd)


def _moe_rows(T, nt, E):
    bound = T * TOP_K + (SUBLANES - 1) * E * nt + E * (MOE_CHUNK - SUBLANES + MOE_BLOCK)
    return -(-bound // MOE_BLOCK) * MOE_BLOCK


def _moe_plan(cnt, T):
    nt, E = cnt.shape[0], cnt.shape[1]
    c = cnt.reshape(nt, E).astype(jnp.int32)
    n8 = (c + SUBLANES - 1) // SUBLANES * SUBLANES
    tot = jnp.sum(n8, axis=0)
    seg = (tot + (MOE_CHUNK - SUBLANES) + MOE_BLOCK - 1) // MOE_BLOCK * MOE_BLOCK
    seg_start = jnp.cumsum(seg) - seg
    strip = seg_start[None, :] + jnp.cumsum(n8, axis=0) - n8
    off = jnp.cumsum(n8, axis=1) - n8
    nch = (n8 + MOE_CHUNK - 1) // MOE_CHUNK
    nblk = _moe_rows(T, nt, E) // MOE_BLOCK
    seg_blk = seg // MOE_BLOCK
    blk_end = jnp.cumsum(seg_blk)
    blk_ids = jnp.arange(nblk, dtype=jnp.int32)
    blk_expert = jnp.minimum(jnp.sum(blk_end[None, :] <= blk_ids[:, None], axis=1), E - 1).astype(jnp.int32)
    tail = jnp.stack([seg_start + tot, seg - tot], axis=1)
    exact = jnp.stack([jnp.sum(n8 // MOE_CHUNK, axis=1)]
                      + [jnp.sum((n8 // r) % 2, axis=1) for r in MOE_REMAINDERS], axis=1)
    return dict(strip=strip.reshape(-1).astype(jnp.int32), off=off.reshape(-1).astype(jnp.int32),
                nch=nch.reshape(-1).astype(jnp.int32), nct=jnp.sum(nch, axis=1).astype(jnp.int32),
                n8=n8.reshape(-1).astype(jnp.int32), exact=exact.reshape(-1).astype(jnp.int32),
                off_col=off.reshape(nt, E, 1).astype(jnp.int32), blk_expert=blk_expert,
                tail=tail.reshape(-1).astype(jnp.int32),
                n_used=blk_end[-1:].astype(jnp.int32))


def _stage_rows(tm):
    return TOP_K * tm + N_EXPERTS * SUBLANES


def _dispatch_kernel(strip_ref, off_ref, nch_ref, nct_ref, tail_ref, nu_ref, x_ref, et_ref, wt_ref, offc_ref,
                     xs_ref, lp_ref, stage, zero, sem, *, tm):
    i = pl.program_id(0)
    n_e = N_EXPERTS
    d = x_ref.shape[1]
    nrow = _stage_rows(tm)
    ch = MOE_CHUNK

    def wait_tile(tile):
        def one(c, carry):
            pltpu.make_async_copy(stage.at[pl.ds(0, ch)], xs_ref.at[pl.ds(0, ch)], sem.at[0]).wait()
            return carry
        lax.fori_loop(0, nct_ref[tile], one, 0)

    @pl.when(i == 0)
    def _():
        stage[nrow:, :] = jnp.zeros((ch, d + LANES), F32)

    et = et_ref[0]
    wt = wt_ref[0]
    eid = lax.broadcasted_iota(jnp.int32, (n_e, tm), 0)
    hots = [eid == et[k:k + 1, :] for k in range(TOP_K)]
    m_t = jnp.zeros((n_e, tm), F32)
    for h in hots:
        m_t = m_t + jnp.where(h, 1.0, 0.0)
    before = jnp.where(lax.broadcasted_iota(jnp.int32, (tm, tm), 0) < lax.broadcasted_iota(jnp.int32, (tm, tm), 1),
                       1.0, 0.0).astype(BF16)
    rank = jnp.dot(m_t.astype(BF16), before, preferred_element_type=F32)
    base = offc_ref[0].astype(F32) + rank
    lps = [jnp.sum(jnp.where(h, base, 0.0), axis=0, keepdims=True) for h in hots]
    rio = lax.broadcasted_iota(jnp.int32, (nrow, tm), 0)
    sel = jnp.zeros((nrow, tm), F32)
    wsel = jnp.zeros((nrow, tm), F32)
    for k in range(TOP_K):
        hit = rio == lps[k].astype(jnp.int32)
        sel = sel + jnp.where(hit, 1.0, 0.0)
        wsel = wsel + jnp.where(hit, wt[k:k + 1, :], 0.0)
    rows = jnp.dot(sel.astype(BF16), x_ref[...].astype(BF16), preferred_element_type=F32)
    wcol = jnp.sum(wsel, axis=1, keepdims=True)
    lp8 = jnp.concatenate(lps + [jnp.zeros((SUBLANES - TOP_K, tm), F32)], axis=0)
    lp_ref[...] = lp8.T.astype(jnp.int32)

    @pl.when(i > 0)
    def _():
        wait_tile(i - 1)

    stage[0:nrow, 0:d] = rows
    stage[0:nrow, d:d + LANES] = jnp.broadcast_to(wcol, (nrow, LANES))

    def per_expert(e, carry):
        idx = i * n_e + e

        def per_chunk(c, carry2):
            src0 = pl.multiple_of(off_ref[idx] + c * ch, SUBLANES)
            dst0 = pl.multiple_of(strip_ref[idx] + c * ch, SUBLANES)
            pltpu.make_async_copy(stage.at[pl.ds(src0, ch)], xs_ref.at[pl.ds(dst0, ch)], sem.at[0]).start()
            return carry2

        return lax.fori_loop(0, nch_ref[idx], per_chunk, carry)

    lax.fori_loop(0, n_e, per_expert, 0)

    def zero_fill(wait):
        def go(dst0, rows):
            cp = pltpu.make_async_copy(zero.at[pl.ds(0, rows)],
                                       xs_ref.at[pl.ds(pl.multiple_of(dst0, SUBLANES), rows)], sem.at[0])
            cp.wait() if wait else cp.start()

        def per_tail(e, carry):
            start, n = tail_ref[2 * e], tail_ref[2 * e + 1]
            nfull = n // ch

            def per_chunk(c, carry2):
                go(start + c * ch, ch)
                return carry2

            lax.fori_loop(0, nfull, per_chunk, 0)
            done = nfull * ch
            for r in MOE_REMAINDERS:
                has = (n // r) % 2

                @pl.when(has == 1)
                def _(done=done, r=r):
                    go(start + done, r)

                done = done + has * r
            return carry

        lax.fori_loop(0, n_e, per_tail, 0)

        def per_block(blk, carry):
            go(blk * MOE_BLOCK, MOE_BLOCK)
            return carry

        lax.fori_loop(nu_ref[0], xs_ref.shape[0] // MOE_BLOCK, per_block, 0)

    @pl.when(i == pl.num_programs(0) - 1)
    def _():
        wait_tile(i)
        zero[...] = jnp.zeros_like(zero)
        zero_fill(wait=False)
        zero_fill(wait=True)


def _dispatch(x1, et, wt, plan):
    T, D = x1.shape
    nt = et.shape[0]
    tm = T // nt
    E = N_EXPERTS
    P = _moe_rows(T, nt, E)
    tile3 = lambda i, *_: (i, 0, 0)
    grid_spec = pltpu.PrefetchScalarGridSpec(
        num_scalar_prefetch=6,
        grid=(nt,),
        in_specs=[pl.BlockSpec((tm, D), lambda i, *_: (i, 0)),
                  pl.BlockSpec((1, TOP_K, tm), tile3), pl.BlockSpec((1, TOP_K, tm), tile3),
                  pl.BlockSpec((1, E, 1), tile3)],
        out_specs=[pl.BlockSpec(memory_space=pl.ANY),
                   pl.BlockSpec((tm, SUBLANES), lambda i, *_: (i, 0))],
        scratch_shapes=[pltpu.VMEM((_stage_rows(tm) + MOE_CHUNK, D + LANES), F32),
                        pltpu.VMEM((MOE_BLOCK, D + LANES), F32),
                        pltpu.SemaphoreType.DMA((1,))],
    )
    return pl.pallas_call(
        functools.partial(_dispatch_kernel, tm=tm),
        grid_spec=grid_spec,
        out_shape=[jax.ShapeDtypeStruct((P, D + LANES), F32), jax.ShapeDtypeStruct((T, SUBLANES), jnp.int32)],
        compiler_params=_params(("arbitrary",)),
        name="moe_dispatch",
    )(plan["strip"], plan["off"], plan["nch"], plan["nct"], plan["tail"], plan["n_used"], x1, et, wt,
      plan["off_col"])


def _combine_kernel(strip_ref, off_ref, n8_ref, exact_ref, ys_ref, lp_ref, x_ref, g_ref, b_ref, *rest, tm, kd):
    if kd:
        wqk_ref, wvg_ref, wgr_ref, w2_ref, b2_ref, xo_ref, qkl_ref, vg_ref, land, sem = rest
    else:
        xo_ref, land, sem = rest
    i = pl.program_id(0)
    n_e = N_EXPERTS
    nrow = _stage_rows(tm)
    ch = MOE_CHUNK
    sizes = (ch,) + MOE_REMAINDERS
    slot = i % 2

    def strip_copy(src0, dst0, rows, sl):
        return pltpu.make_async_copy(ys_ref.at[pl.ds(pl.multiple_of(src0, SUBLANES), rows)],
                                     land.at[sl, pl.ds(pl.multiple_of(dst0, SUBLANES), rows)], sem.at[sl])

    def fetch(tile, sl):
        def per_expert(e, carry):
            idx = tile * n_e + e
            n8, src, dst = n8_ref[idx], strip_ref[idx], off_ref[idx]
            nfull = n8 // ch

            def per_chunk(c, carry2):
                strip_copy(src + c * ch, dst + c * ch, ch, sl).start()
                return carry2

            lax.fori_loop(0, nfull, per_chunk, 0)
            done = nfull * ch
            for r in MOE_REMAINDERS:
                has = (n8 // r) % 2

                @pl.when(has == 1)
                def _(done=done, r=r):
                    strip_copy(src + done, dst + done, r, sl).start()

                done = done + has * r
            return carry

        lax.fori_loop(0, n_e, per_expert, 0)

    @pl.when(i == 0)
    def _():
        land[...] = jnp.zeros_like(land)
        fetch(0, 0)

    @pl.when(i + 1 < pl.num_programs(0))
    def _():
        fetch(i + 1, 1 - slot)

    lp = lp_ref[...]
    cio = lax.broadcasted_iota(jnp.int32, (tm, nrow), 1)
    selt = jnp.zeros((tm, nrow), F32)
    for k in range(TOP_K):
        selt = selt + jnp.where(cio == lp[:, k:k + 1], 1.0, 0.0)
    selt = selt.astype(BF16)

    for s, rows in enumerate(sizes):
        def one(c, carry, rows=rows):
            strip_copy(0, 0, rows, slot).wait()
            return carry

        lax.fori_loop(0, exact_ref[i * len(sizes) + s], one, 0)

    v = land[slot, 0:nrow, :]
    hi = v.astype(BF16)
    lo = (v - hi.astype(F32)).astype(BF16)
    ffn = jnp.dot(selt, hi, preferred_element_type=F32) + jnp.dot(selt, lo, preferred_element_type=F32)
    xn = _layer_norm(DN_ALPHA * x_ref[...] + ffn, g_ref[...], b_ref[...])
    xo_ref[...] = xn
    if kd:
        xb = xn.astype(BF16)
        qkl_ref[:, :2 * kd] = jnp.dot(xb, wqk_ref[...], preferred_element_type=F32)
        vg_ref[...] = jnp.dot(xb, wvg_ref[...], preferred_element_type=F32)
        gr = jnp.dot(xb, wgr_ref[...], preferred_element_type=F32)
        z = jnp.dot(gr, w2_ref[...], precision=HIGHEST, preferred_element_type=F32) + b2_ref[...]
        log_sig = jnp.minimum(z, 0.0) - jnp.log1p(jnp.exp(-jnp.abs(z)))
        qkl_ref[:, 2 * kd:] = log_sig / GLA_GATE_TAU


def _combine_ln(ys, lp, x, plan, g, b, gla=None):
    T, D = x.shape
    nt = plan["nct"].shape[0]
    tm = T // nt
    const = lambda i, *_: (0, 0)
    row = lambda i, *_: (i, 0)
    in_specs = [pl.BlockSpec(memory_space=pl.ANY), pl.BlockSpec((tm, SUBLANES), row), pl.BlockSpec((tm, D), row),
                pl.BlockSpec((1, D), const), pl.BlockSpec((1, D), const)]
    out_specs = [pl.BlockSpec((tm, D), row)]
    out_shape = [jax.ShapeDtypeStruct((T, D), F32)]
    args = [ys, lp, x, g.reshape(1, D), b.reshape(1, D)]
    kd = 0
    if gla is not None:
        w_in1, w_gate2, b_gate, kd, vd = gla
        ws = [w_in1[:, :2 * kd].astype(BF16), w_in1[:, 2 * kd:2 * kd + 2 * vd].astype(BF16),
              w_in1[:, 2 * kd + 2 * vd:].astype(BF16), w_gate2.astype(F32), b_gate.reshape(1, kd)]
        in_specs += [pl.BlockSpec(w.shape, const) for w in ws]
        args += ws
        out_specs += [pl.BlockSpec((tm, 3 * kd), row), pl.BlockSpec((tm, 2 * vd), row)]
        out_shape += [jax.ShapeDtypeStruct((T, 3 * kd), F32), jax.ShapeDtypeStruct((T, 2 * vd), F32)]
    grid_spec = pltpu.PrefetchScalarGridSpec(
        num_scalar_prefetch=4,
        grid=(nt,),
        in_specs=in_specs,
        out_specs=out_specs,
        scratch_shapes=[pltpu.VMEM((2, _stage_rows(tm), D), F32), pltpu.SemaphoreType.DMA((2,))],
    )
    return pl.pallas_call(
        functools.partial(_combine_kernel, tm=tm, kd=kd),
        grid_spec=grid_spec,
        out_shape=out_shape,
        compiler_params=_params(("arbitrary",)),
        name="moe_combine_ln",
    )(plan["strip"], plan["off"], plan["n8"], plan["exact"], *args)


def _moe_ln(x1, et, wt, cnt, layer, w_gate_up, b_gate_up, w_down, b_down, g, b, gla=None):
    T = x1.shape[0]
    plan = _moe_plan(cnt, T)
    xs, lp = _dispatch(x1, et, wt, plan)
    wgu = _deinterleave_gate_up(w_gate_up, layer)
    bgu = jnp.concatenate([b_gate_up[layer, :, 0::2], b_gate_up[layer, :, 1::2]], axis=-1)[:, None, :].astype(F32)
    ys = _expert_ffn(xs, plan, wgu, w_down, layer, bgu, b_down[layer, :, None, :].astype(F32))
    return _combine_ln(ys, lp, x1, plan, g, b, gla)


def _gla_kernel(q_ref, k_ref, la_ref, v_ref, g_ref, nw_ref, o_ref, s_scr, *, tl, dk, dv):
    C = GLA_CHUNK

    @pl.when(pl.program_id(2) == 0)
    def _():
        s_scr[...] = jnp.zeros_like(s_scr)

    ri = lax.broadcasted_iota(jnp.int32, (C, C), 0)
    ci = lax.broadcasted_iota(jnp.int32, (C, C), 1)
    lower = ri >= ci
    tri = jnp.where(lower, 1.0, 0.0).astype(F32)
    for c in range(tl // C):
        rows = slice(c * C, (c + 1) * C)
        lg = la_ref[0, rows, :]
        bcum = jnp.dot(tri, lg, precision=HIGHEST, preferred_element_type=F32)
        btot = bcum[C - 1:C, :]
        q_in = q_ref[0, rows, :] * (dk ** -0.5) * jnp.exp(bcum)
        kc = k_ref[0, rows, :]
        k_in = kc * jnp.exp(-bcum)
        k_st = kc * jnp.exp(btot - bcum)
        vb = v_ref[0, rows, :].astype(BF16)
        qb = q_in.astype(BF16)
        att = lax.dot_general(qb, k_in.astype(BF16), (((1,), (1,)), ((), ())), preferred_element_type=F32)
        att = jnp.where(lower, att, 0.0)
        s_prev = s_scr[...]
        o = (jnp.dot(att.astype(BF16), vb, preferred_element_type=F32)
             + jnp.dot(qb, s_prev.astype(BF16), preferred_element_type=F32))
        kv = jnp.dot(k_st.T.astype(BF16), vb, preferred_element_type=F32)
        decay = jnp.exp(jnp.broadcast_to(btot, (8, dk))).T[:, 0:1]
        s_scr[...] = decay * s_prev + kv
        o = o * lax.rsqrt(jnp.mean(o * o, axis=-1, keepdims=True) + RMS_EPS) * nw_ref[...]
        gg = g_ref[0, rows, :]
        o_ref[0, rows, :] = o * (gg * jax.nn.sigmoid(gg))


def _gla(qkl, vg, norm_w, B, L, tl=256):
    T, kd3 = qkl.shape
    kd = kd3 // 3
    vd = vg.shape[1] // 2
    H = GLA_HEADS
    dk, dv = kd // H, vd // H
    qkl3 = qkl.reshape(B, L, kd3)
    vg3 = vg.reshape(B, L, 2 * vd)
    return pl.pallas_call(
        functools.partial(_gla_kernel, tl=tl, dk=dk, dv=dv),
        grid=(B, H, L // tl),
        in_specs=[pl.BlockSpec((1, tl, dk), lambda b, h, l: (b, l, h)),
                  pl.BlockSpec((1, tl, dk), lambda b, h, l: (b, l, H + h)),
                  pl.BlockSpec((1, tl, dk), lambda b, h, l: (b, l, 2 * H + h)),
                  pl.BlockSpec((1, tl, dv), lambda b, h, l: (b, l, h)),
                  pl.BlockSpec((1, tl, dv), lambda b, h, l: (b, l, H + h)),
                  pl.BlockSpec((1, dv), lambda b, h, l: (0, 0))],
        out_specs=pl.BlockSpec((1, tl, dv), lambda b, h, l: (b, l, h)),
        out_shape=jax.ShapeDtypeStruct((B, L, vd), F32),
        scratch_shapes=[pltpu.VMEM((dk, dv), F32)],
        compiler_params=_params(("parallel", "parallel", "arbitrary")),
        name="gla_mixer",
    )(qkl3, qkl3, qkl3, vg3, vg3, norm_w.reshape(1, dv).astype(F32))


def kernel(x, w_in0, s5_lam_re, s5_lam_im, s5_log_dt, s5_b_re, s5_b_im, s5_c_re, s5_c_im, s5_d, s5_w_glu,
           s5_b_glu, rel_bias, w_out0, w_in1, gla_w_gate2, gla_b_gate, gla_norm_w, w_out1, ln_mix_g, ln_mix_b,
           ln_ffn_g, ln_ffn_b, router_w, router_b, exp_w_gate_up, exp_b_gate_up, exp_w_down, exp_b_down):
    B, L, D = x.shape
    T = B * L
    s5w = s5_w_glu.shape[-1]
    kd = gla_w_gate2.shape[-1]
    vd = gla_norm_w.shape[-1] * GLA_HEADS

    u_tm, k_att, qt_att, vt_att = _inproj0(x, w_in0[0], s5w)
    y_a = _s5_mixer(u_tm.reshape(L, B, s5w), s5_lam_re[0], s5_lam_im[0], s5_log_dt[0], s5_b_re[0], s5_b_im[0],
                    s5_c_re[0], s5_c_im[0], s5_d[0].reshape(-1), s5_w_glu[0], s5_b_glu[0])
    y_b = _moba(qt_att, k_att, vt_att, rel_bias)
    tl = MOE_TILE
    w0 = w_out0[0].astype(BF16)
    ys = [(y_a.reshape(L, B * s5w), pl.BlockSpec((tl, s5w), lambda bb, l: (l, bb))),
          (y_b, pl.BlockSpec((1, tl, D - s5w), lambda bb, l: (bb, l, 0)))]
    x1, et, wt, cnt = _mix_ln(x, ys, [w0[:s5w], w0[s5w:]], ln_mix_g[0], ln_mix_b[0], router_w[0], router_b[0])
    x2, qkl, vg = _moe_ln(x1.reshape(T, D), et, wt, cnt, 0, exp_w_gate_up, exp_b_gate_up, exp_w_down,
                          exp_b_down, ln_ffn_g[0], ln_ffn_b[0],
                          gla=(w_in1[0], gla_w_gate2[0], gla_b_gate[0], kd, vd))

    y_c = _gla(qkl, vg, gla_norm_w[0], B, L)
    ys = [(y_c, pl.BlockSpec((1, tl, vd), lambda bb, l: (bb, l, 0)))]
    x3, et, wt, cnt = _mix_ln(x2.reshape(B, L, D), ys, [w_out1[0].astype(BF16)], ln_mix_g[1], ln_mix_b[1],
                              router_w[1], router_b[1])
    (out,) = _moe_ln(x3.reshape(T, D), et, wt, cnt, 1, exp_w_gate_up, exp_b_gate_up, exp_w_down,
                     exp_b_down, ln_ffn_g[1], ln_ffn_b[1])
    return out.reshape(B, L, D)
```

```python
import functools
import math

import jax
import jax.numpy as jnp
import numpy as np
from jax import lax
from jax.experimental import pallas as pl
from jax.experimental.pallas import tpu as pltpu

F32 = jnp.float32
BF16 = jnp.bfloat16
HIGHEST = lax.Precision.HIGHEST

DEPTH = 2
S5_GROUP = 16
S5_STATE = 64
MOBA_HEAD_DIM = 64
MOBA_BLOCK = 256
MOBA_TOPK = 3
REL_BUCKETS = 32
REL_MAX_DIST = 2048
GLA_HEADS = 4
GLA_GATE_TAU = 16.0
GLA_CHUNK = 64
N_EXPERTS = 32
TOP_K = 4
SWIGLU_LIMIT = 7.0
SWIGLU_ALPHA = 1.702
MOE_BLOCK = 512
MOE_TILE = 256
MIX_TILE = 512
MOE_CHUNK = 32
SUBLANES = 8
MOE_REMAINDERS = (16, 8)
DN_ALPHA = (2 * DEPTH) ** 0.25
LN_EPS = 1e-5
RMS_EPS = 1e-5

V7X_VMEM_LIMIT_BYTES = 56 * 1024 * 1024
LANES = 128

S5_GROUPS_PER_CHUNK = LANES // S5_GROUP
S5_CHUNK_STATES = S5_GROUPS_PER_CHUNK * S5_STATE


def _params(sem):
    return pltpu.CompilerParams(dimension_semantics=sem, vmem_limit_bytes=V7X_VMEM_LIMIT_BYTES)


def _layer_norm(r, g, b):
    mu = jnp.mean(r, axis=-1, keepdims=True)
    c = r - mu
    var = jnp.mean(c * c, axis=-1, keepdims=True)
    return c * lax.rsqrt(var + LN_EPS) * g + b


def _inproj0_kernel(x_ref, wuk_ref, wqvt_ref, u_ref, k_ref, qt_ref, vt_ref, *, s5w):
    xb = x_ref[0].astype(BF16)
    h = jnp.dot(xb, wuk_ref[...], preferred_element_type=F32)
    u_ref[...] = h[:, :s5w]
    k_ref[0] = h[:, s5w:]
    ht = lax.dot_general(wqvt_ref[...], xb, (((1,), (1,)), ((), ())), preferred_element_type=F32)
    aw = ht.shape[0] // 2
    qt_ref[0] = ht[:aw]
    vt_ref[0] = ht[aw:].astype(BF16)


def _inproj0(x, w_in0, s5w, tl=512):
    B, L, D = x.shape
    aw = (w_in0.shape[1] - s5w) // 3
    wb = w_in0.astype(BF16)
    wuk = jnp.concatenate([wb[:, :s5w], wb[:, s5w + aw:s5w + 2 * aw]], axis=1)
    wqvt = jnp.concatenate([wb[:, s5w:s5w + aw], wb[:, s5w + 2 * aw:]], axis=1).T
    return pl.pallas_call(
        functools.partial(_inproj0_kernel, s5w=s5w),
        grid=(B, L // tl),
        in_specs=[pl.BlockSpec((1, tl, D), lambda b, l: (b, l, 0)),
                  pl.BlockSpec(wuk.shape, lambda b, l: (0, 0)),
                  pl.BlockSpec(wqvt.shape, lambda b, l: (0, 0))],
        out_specs=[pl.BlockSpec((tl, s5w), lambda b, l: (l, b)),
                   pl.BlockSpec((1, tl, aw), lambda b, l: (b, l, 0)),
                   pl.BlockSpec((1, aw, tl), lambda b, l: (b, 0, l)),
                   pl.BlockSpec((1, aw, tl), lambda b, l: (b, 0, l))],
        out_shape=[jax.ShapeDtypeStruct((L, B * s5w), F32),
                   jax.ShapeDtypeStruct((B, L, aw), F32),
                   jax.ShapeDtypeStruct((B, aw, L), F32),
                   jax.ShapeDtypeStruct((B, aw, L), BF16)],
        compiler_params=_params(("parallel", "parallel")),
        name="inproj0",
    )(x, wuk, wqvt)


def _s5_discretize(lam_re, lam_im, log_dt, b_re, b_im):
    dt = jnp.exp(log_dt.astype(F32))[:, None]
    lr, li = lam_re.astype(F32), lam_im.astype(F32)
    mag = jnp.exp(lr * dt)
    ab_re, ab_im = mag * jnp.cos(li * dt), mag * jnp.sin(li * dt)
    er, ei = ab_re - 1.0, ab_im
    den = lr * lr + li * li
    q_re = (er * lr + ei * li) / den
    q_im = (ei * lr - er * li) / den
    br_, bi_ = b_re.astype(F32), b_im.astype(F32)
    bb_re = q_re[..., None] * br_ - q_im[..., None] * bi_
    bb_im = q_re[..., None] * bi_ + q_im[..., None] * br_
    return ab_re, ab_im, bb_re, bb_im


def _s5_kernel(u_ref, bm_ref, cm_ref, are_ref, aim_ref, d_ref, wg_ref, bg_ref, y_ref,
               s_scr, st_scr, z_scr, *, tl, nb, nchunk):
    ns = S5_CHUNK_STATES

    @pl.when(pl.program_id(0) == 0)
    def _():
        st_scr[...] = jnp.zeros_like(st_scr)

    u = u_ref[...].reshape(tl * nb, nchunk * LANES)
    for j in range(nchunk):
        uj = u[:, j * LANES:(j + 1) * LANES]
        s_scr[...] = jnp.dot(uj.astype(BF16), bm_ref[j], preferred_element_type=F32)
        ar = jnp.broadcast_to(are_ref[j], (nb, ns))
        ai = jnp.broadcast_to(aim_ref[j], (nb, ns))

        def step(t, carry, ar=ar, ai=ai):
            sre, sim = carry
            r0 = pl.multiple_of(t * nb, nb)
            nre = ar * sre - ai * sim + s_scr[pl.ds(r0, nb), 0:ns]
            nim = ar * sim + ai * sre + s_scr[pl.ds(r0, nb), ns:2 * ns]
            s_scr[pl.ds(r0, nb), 0:ns] = nre
            s_scr[pl.ds(r0, nb), ns:2 * ns] = nim
            return nre, nim

        sre, sim = lax.fori_loop(0, tl, step, (st_scr[j, :, 0:ns], st_scr[j, :, ns:2 * ns]), unroll=2)
        st_scr[j, :, 0:ns] = sre
        st_scr[j, :, ns:2 * ns] = sim
        yj = jnp.dot(s_scr[...].astype(BF16), cm_ref[j], preferred_element_type=F32)
        z_scr[:, j * LANES:(j + 1) * LANES] = yj + uj * d_ref[:, j * LANES:(j + 1) * LANES]
    z = jax.nn.gelu(z_scr[...])
    gate = jax.nn.sigmoid(jnp.dot(z.astype(BF16), wg_ref[...], preferred_element_type=F32) + bg_ref[...])
    y_ref[...] = (z * gate).reshape(tl, nb, nchunk * LANES)


def _s5_mixer(u_tm, lam_re, lam_im, log_dt, b_re, b_im, c_re, c_im, d_skip, w_glu, b_glu, tl=32):
    L, B, W = u_tm.shape
    G, P, H = lam_re.shape[0], S5_STATE, S5_GROUP
    gc = S5_GROUPS_PER_CHUNK
    nchunk = G // gc
    ab_re, ab_im, bb_re, bb_im = _s5_discretize(lam_re, lam_im, log_dt, b_re, b_im)
    eye = jnp.eye(gc, dtype=F32)

    def b_blocks(bb):
        return jnp.einsum('jgph,gk->jghkp', bb.reshape(nchunk, gc, P, H), eye).reshape(nchunk, gc * H, gc * P)

    def c_blocks(cc):
        return jnp.einsum('jghp,gk->jgpkh', cc.reshape(nchunk, gc, H, P), eye).reshape(nchunk, gc * P, gc * H)

    bm = jnp.concatenate([b_blocks(bb_re), b_blocks(bb_im)], axis=2).astype(BF16)
    cm = jnp.concatenate([c_blocks(c_re.astype(F32)), -c_blocks(c_im.astype(F32))], axis=1).astype(BF16)
    are = ab_re.reshape(nchunk, 1, gc * P)
    aim = ab_im.reshape(nchunk, 1, gc * P)
    m = tl * B
    const3 = lambda l: (0, 0, 0)
    const2 = lambda l: (0, 0)
    return pl.pallas_call(
        functools.partial(_s5_kernel, tl=tl, nb=B, nchunk=nchunk),
        grid=(L // tl,),
        in_specs=[pl.BlockSpec((tl, B, W), lambda l: (l, 0, 0)),
                  pl.BlockSpec(bm.shape, const3), pl.BlockSpec(cm.shape, const3),
                  pl.BlockSpec(are.shape, const3), pl.BlockSpec(aim.shape, const3),
                  pl.BlockSpec((1, W), const2), pl.BlockSpec((W, W), const2), pl.BlockSpec((1, W), const2)],
        out_specs=pl.BlockSpec((tl, B, W), lambda l: (l, 0, 0)),
        out_shape=jax.ShapeDtypeStruct((L, B, W), F32),
        scratch_shapes=[pltpu.VMEM((m, 2 * S5_CHUNK_STATES), F32),
                        pltpu.VMEM((nchunk, B, 2 * S5_CHUNK_STATES), F32),
                        pltpu.VMEM((m, W), F32)],
        compiler_params=_params(("arbitrary",)),
        name="s5_mixer",
    )(u_tm, bm, cm, are, aim, d_skip.reshape(1, W).astype(F32), w_glu.astype(BF16),
      b_glu.reshape(1, W).astype(F32))


def _rel_bucket(n):
    n = jnp.maximum(n, 0)
    max_exact = REL_BUCKETS // 2
    nf = jnp.maximum(n, 1).astype(F32)
    large = max_exact + (jnp.log(nf / max_exact) / math.log(REL_MAX_DIST / max_exact)
                         * (REL_BUCKETS - max_exact)).astype(jnp.int32)
    large = jnp.minimum(large, REL_BUCKETS - 1)
    return jnp.where(n < max_exact, n, large)


MOBA_ONES_ROWS = 16
MOBA_NEG = -1e30
LOG2E = math.log2(math.e)


def _moba_kernel(qt_ref, k_ref, vt_ref, wv_ref, hot_ref, o_ref, bias_scr, kk_scr, va_scr, km_scr, qa_scr,
                 *, nblk, bs, dh, topk):
    b, own = pl.program_id(1), pl.program_id(2)
    i = own
    tq = bs
    hpb = LANES // dh

    @pl.when((b == 0) & (i == 0))
    def _():
        r_io = lax.broadcasted_iota(jnp.int32, (bs, 2 * bs), 0)
        c_io = lax.broadcasted_iota(jnp.int32, (bs, 2 * bs), 1)
        for hh in range(hpb):
            for dd in range(nblk):
                t = pltpu.roll(jnp.broadcast_to(wv_ref[hh, dd:dd + 1, :], (bs, 2 * bs)), bs + 1, 1,
                               stride=1, stride_axis=0)
                if dd == 0:
                    t = jnp.where(c_io >= r_io, t, MOBA_NEG)
                bias_scr[hh, dd] = t[:, 0:bs]

    @pl.when(i == 0)
    def _():
        kf = k_ref[0]
        km_scr[...] = jnp.mean(kf.reshape(nblk, bs, LANES), axis=1)
        kk_scr[:, 0:LANES] = kf.astype(BF16)
        kk_scr[:, LANES:] = hot_ref[...]
        for n in range(nblk):
            va_scr[n, 0:LANES, :] = vt_ref[0, :, n * bs:(n + 1) * bs]
            va_scr[n, LANES:, :] = jnp.ones((MOBA_ONES_ROWS, bs), BF16)
        seq = nblk * bs
        q2 = qt_ref[0] * (dh ** -0.5 * LOG2E)
        f_io = lax.broadcasted_iota(jnp.int32, (LANES, seq), 0)
        blk_io = lax.broadcasted_iota(jnp.int32, (nblk, seq), 0)
        own_q = lax.broadcasted_iota(jnp.int32, (nblk, seq), 1) // bs
        qms, pens = [], []
        for hh in range(hpb):
            qm = jnp.where((f_io >= hh * dh) & (f_io < (hh + 1) * dh), q2, 0.0)
            gate = jnp.dot(km_scr[...], qm, precision=HIGHEST, preferred_element_type=F32)
            cnt = jnp.zeros((nblk, seq), jnp.int32)
            for m in range(nblk):
                gm = gate[m:m + 1, :]
                beats = (gm > gate) | ((gm == gate) & (m < blk_io))
                cnt = cnt + jnp.where(beats & (m < own_q), 1, 0)
            keep = (blk_io >= own_q) | (cnt < topk)
            pens.append(jnp.where(keep, 0.0, MOBA_NEG))
            qms.append(qm.astype(BF16))
        for n in range(nblk):
            cols = slice(n * bs, (n + 1) * bs)
            qa_scr[n, 0:LANES, :] = jnp.concatenate([qm[:, cols] for qm in qms], axis=1)
            pen = jnp.concatenate([p[:, cols] for p in pens], axis=1)
            qa_scr[n, LANES:, :] = jnp.concatenate(
                [pen, jnp.zeros((LANES - nblk, hpb * bs), F32)], axis=0).astype(BF16)

    qa = qa_scr[own]

    def scores(j):
        r0 = pl.multiple_of(j * bs, bs)
        s = jnp.dot(kk_scr[pl.ds(r0, bs), :], qa, preferred_element_type=F32)
        return s + jnp.concatenate([bias_scr[hh, own - j] for hh in range(hpb)], axis=1)

    def pair(ja, jb, pen_b, m_i, acc):
        sa = scores(ja)
        sb = scores(jb) + pen_b
        m_n = jnp.maximum(m_i, jnp.maximum(jnp.max(sa, axis=0, keepdims=True), jnp.max(sb, axis=0, keepdims=True)))
        alpha = jnp.exp2(m_i - m_n)
        pa = jnp.exp2(sa - m_n).astype(BF16)
        pb = jnp.exp2(sb - m_n).astype(BF16)
        return m_n, (alpha * acc + jnp.dot(va_scr[ja], pa, preferred_element_type=F32)
                     + jnp.dot(va_scr[jb], pb, preferred_element_type=F32))

    odd = own % 2
    m1, acc1 = pair(own, jnp.maximum(own - 1, 0), jnp.where(odd == 1, 0.0, MOBA_NEG),
                    jnp.full((1, hpb * tq), MOBA_NEG, F32), jnp.zeros((LANES + MOBA_ONES_ROWS, hpb * tq), F32))
    _, acc_f = lax.fori_loop(0, own // 2, lambda jj, c: pair(2 * jj, 2 * jj + 1, 0.0, *c), (m1, acc1))
    on = acc_f[0:LANES, :] / acc_f[LANES:LANES + 1, :]
    ot = jnp.concatenate([on[hh * dh:(hh + 1) * dh, hh * tq:(hh + 1) * tq] for hh in range(hpb)], axis=0)
    o_ref[0] = ot.T


def _moba(qt, k, vt, rel_bias):
    B, L, W = k.shape
    dh, bs = MOBA_HEAD_DIM, MOBA_BLOCK
    H = W // dh
    nblk = L // bs
    hpb = LANES // dh
    dist = jnp.arange(L, dtype=jnp.int32)
    by_dist = rel_bias.astype(F32).T[:, _rel_bucket(dist)] * LOG2E
    idx = np.clip(np.arange(nblk)[:, None] * bs - (bs - 1) + np.arange(2 * bs)[None, :], 0, L - 1)
    vecs = by_dist[:, idx]
    hot = np.zeros((L, LANES), np.float32)
    hot[np.arange(L), np.arange(L) // bs] = 1.0
    return pl.pallas_call(
        functools.partial(_moba_kernel, nblk=nblk, bs=bs, dh=dh, topk=min(MOBA_TOPK, nblk)),
        grid=(H // hpb, B, nblk),
        in_specs=[pl.BlockSpec((1, LANES, L), lambda h, b, i: (b, h, 0)),
                  pl.BlockSpec((1, L, LANES), lambda h, b, i: (b, 0, h)),
                  pl.BlockSpec((1, LANES, L), lambda h, b, i: (b, h, 0)),
                  pl.BlockSpec((hpb, nblk, 2 * bs), lambda h, b, i: (h, 0, 0)),
                  pl.BlockSpec((L, LANES), lambda h, b, i: (0, 0))],
        out_specs=pl.BlockSpec((1, bs, LANES), lambda h, b, i: (b, i, h)),
        out_shape=jax.ShapeDtypeStruct((B, L, W), F32),
        scratch_shapes=[pltpu.VMEM((hpb, nblk, bs, bs), F32),
                        pltpu.VMEM((L, 2 * LANES), BF16),
                        pltpu.VMEM((nblk, LANES + MOBA_ONES_ROWS, bs), BF16),
                        pltpu.VMEM((nblk, LANES), F32),
                        pltpu.VMEM((nblk, 2 * LANES, hpb * bs), BF16)],
        compiler_params=_params(("arbitrary", "arbitrary", "arbitrary")),
        name="moba_attention",
    )(qt, k, vt, vecs, jnp.asarray(hot, BF16))


def _route_top_k(xn, wrt_ref, br_ref):
    lt = lax.dot_general(wrt_ref[...], xn, (((1,), (1,)), ((), ())), precision=HIGHEST,
                         preferred_element_type=F32) + br_ref[...]
    n_e = lt.shape[0]
    eid = lax.broadcasted_iota(jnp.int32, lt.shape, 0)
    cur = lt
    vals, idxs = [], []
    for _ in range(TOP_K):
        m = jnp.max(cur, axis=0, keepdims=True)
        idx = jnp.min(jnp.where(cur == m, eid, n_e), axis=0, keepdims=True)
        vals.append(m)
        idxs.append(idx)
        cur = jnp.where(eid == idx, -jnp.inf, cur)
    ex = [jnp.exp(v - vals[0]) for v in vals]
    den = ex[0]
    for t in ex[1:]:
        den = den + t
    hot = jnp.zeros(lt.shape, F32)
    for idx in idxs:
        hot = hot + jnp.where(eid == idx, 1.0, 0.0)
    return (jnp.concatenate(idxs, axis=0), jnp.concatenate([t / den for t in ex], axis=0),
            jnp.sum(hot, axis=1, keepdims=True))


def _mix_ln_kernel(*refs, n_in):
    x_ref = refs[0]
    y_refs = refs[1:1 + n_in]
    w_refs = refs[1 + n_in:1 + 2 * n_in]
    g_ref, b_ref, wrt_ref, br_ref, x1_ref, et_ref, wt_ref, cnt_ref = refs[1 + 2 * n_in:]
    mix = None
    for y_ref, w_ref in zip(y_refs, w_refs):
        y = y_ref[...]
        y = y.reshape(y.shape[-2], y.shape[-1]).astype(BF16)
        t = jnp.dot(y, w_ref[...], preferred_element_type=F32)
        mix = t if mix is None else mix + t
    xn = _layer_norm(DN_ALPHA * x_ref[0] + mix, g_ref[...], b_ref[...])
    x1_ref[0] = xn
    for s in range(et_ref.shape[0]):
        et, wt, cnt = _route_top_k(xn[s * MOE_TILE:(s + 1) * MOE_TILE], wrt_ref, br_ref)
        et_ref[s] = et
        wt_ref[s] = wt
        cnt_ref[s] = cnt


def _mix_ln(x, ys, ws, g, b, wr, br):
    B, L, D = x.shape
    E = wr.shape[1]
    tl = MIX_TILE
    nl = L // tl
    sub = tl // MOE_TILE
    nt = B * L // MOE_TILE
    const = lambda bb, l: (0, 0)
    tile = lambda bb, l: (bb * nl + l, 0, 0)
    in_specs = [pl.BlockSpec((1, tl, D), lambda bb, l: (bb, l, 0))]
    in_specs += [spec for _, spec in ys]
    in_specs += [pl.BlockSpec(w.shape, const) for w in ws]
    in_specs += [pl.BlockSpec((1, D), const), pl.BlockSpec((1, D), const),
                 pl.BlockSpec((E, D), const), pl.BlockSpec((E, 1), const)]
    return pl.pallas_call(
        functools.partial(_mix_ln_kernel, n_in=len(ys)),
        grid=(B, nl),
        in_specs=in_specs,
        out_specs=[pl.BlockSpec((1, tl, D), lambda bb, l: (bb, l, 0)),
                   pl.BlockSpec((sub, TOP_K, MOE_TILE), tile), pl.BlockSpec((sub, TOP_K, MOE_TILE), tile),
                   pl.BlockSpec((sub, E, 1), tile)],
        out_shape=[jax.ShapeDtypeStruct((B, L, D), F32),
                   jax.ShapeDtypeStruct((nt, TOP_K, MOE_TILE), jnp.int32),
                   jax.ShapeDtypeStruct((nt, TOP_K, MOE_TILE), F32),
                   jax.ShapeDtypeStruct((nt, E, 1), F32)],
        compiler_params=_params(("parallel", "parallel")),
        name="outproj_ln_router",
    )(x, *[a for a, _ in ys], *ws, g.reshape(1, D), b.reshape(1, D), wr.astype(F32).T, br.reshape(E, 1))


def _deinterleave_kernel(w_ref, p_ref, o_ref, *, ff):
    w = w_ref[0, 0].astype(BF16)
    for c in range(2 * ff // (2 * LANES)):
        t = jnp.dot(w[:, 2 * LANES * c:2 * LANES * (c + 1)], p_ref[...], preferred_element_type=F32)
        o_ref[0, :, LANES * c:LANES * (c + 1)] = t[:, :LANES].astype(BF16)
        o_ref[0, :, ff + LANES * c:ff + LANES * (c + 1)] = t[:, LANES:].astype(BF16)


def _deinterleave_gate_up(w_gate_up, layer, tk=512):
    _, E, D, F2 = w_gate_up.shape
    src = np.arange(2 * LANES)
    dst = np.where(src % 2 == 0, src // 2, LANES + src // 2)
    perm = np.zeros((2 * LANES, 2 * LANES), np.float32)
    perm[src, dst] = 1.0
    return pl.pallas_call(
        functools.partial(_deinterleave_kernel, ff=F2 // 2),
        grid=(E, D // tk),
        in_specs=[pl.BlockSpec((1, 1, tk, F2), lambda e, k: (layer, e, k, 0)),
                  pl.BlockSpec((2 * LANES, 2 * LANES), lambda e, k: (0, 0))],
        out_specs=pl.BlockSpec((1, tk, F2), lambda e, k: (e, k, 0)),
        out_shape=jax.ShapeDtypeStruct((E, D, F2), BF16),
        compiler_params=_params(("parallel", "parallel")),
        name="deinterleave_gate_up",
    )(w_gate_up, jnp.asarray(perm, BF16))


def _expert_kernel(be_ref, nu_ref, xs_ref, wgu_ref, wd_ref, bgu_ref, bd_ref, ys_ref):
    blk = pl.program_id(0)
    ff = wd_ref.shape[2]
    d = wd_ref.shape[3]

    @pl.when(blk < nu_ref[0])
    def _():
        x = xs_ref[:, :d].astype(BF16)
        pw = xs_ref[:, d:d + 1]
        h = jnp.dot(x, wgu_ref[0], preferred_element_type=F32) + bgu_ref[0]
        g = h[:, :ff]
        u = h[:, ff:]
        g = jnp.minimum(g, SWIGLU_LIMIT)
        u = jnp.clip(u, -SWIGLU_LIMIT, SWIGLU_LIMIT)
        act = g * jax.nn.sigmoid(SWIGLU_ALPHA * g) * (u + 1.0)
        y = jnp.dot(act.astype(BF16), wd_ref[0, 0].astype(BF16), preferred_element_type=F32) + bd_ref[0]
        ys_ref[...] = y * pw

    @pl.when(blk >= nu_ref[0])
    def _():
        ys_ref[...] = jnp.zeros_like(ys_ref)


def _expert_ffn(xs, plan, wgu, w_down, layer, bgu, bd):
    P, DW = xs.shape
    F, D = w_down.shape[2], w_down.shape[3]
    bm = MOE_BLOCK
    wmap = lambda i, be, nu: (be[i], 0, 0)
    rmap = lambda i, be, nu: (i, 0)
    grid_spec = pltpu.PrefetchScalarGridSpec(
        num_scalar_prefetch=2,
        grid=(P // bm,),
        in_specs=[pl.BlockSpec((bm, DW), rmap),
                  pl.BlockSpec((1, D, 2 * F), wmap),
                  pl.BlockSpec((1, 1, F, D), lambda i, be, nu: (layer, be[i], 0, 0)),
                  pl.BlockSpec((1, 1, 2 * F), wmap), pl.BlockSpec((1, 1, D), wmap)],
        out_specs=pl.BlockSpec((bm, D), rmap),
    )
    return pl.pallas_call(
        _expert_kernel,
        grid_spec=grid_spec,
        out_shape=jax.ShapeDtypeStruct((P, D), F32),
        compiler_params=_params(("arbitrary",)),
        name="moe_experts",
    )(plan["blk_expert"], plan["n_used"], xs, wgu, w_down, bgu, bd)


def _moe_rows(T, nt, E):
    bound = T * TOP_K + (SUBLANES - 1) * E * nt + E * (MOE_CHUNK - SUBLANES + MOE_BLOCK)
    return -(-bound // MOE_BLOCK) * MOE_BLOCK


def _moe_plan(cnt, T):
    nt, E = cnt.shape[0], cnt.shape[1]
    c = cnt.reshape(nt, E).astype(jnp.int32)
    n8 = (c + SUBLANES - 1) // SUBLANES * SUBLANES
    tot = jnp.sum(n8, axis=0)
    seg = (tot + (MOE_CHUNK - SUBLANES) + MOE_BLOCK - 1) // MOE_BLOCK * MOE_BLOCK
    seg_start = jnp.cumsum(seg) - seg
    strip = seg_start[None, :] + jnp.cumsum(n8, axis=0) - n8
    off = jnp.cumsum(n8, axis=1) - n8
    nch = (n8 + MOE_CHUNK - 1) // MOE_CHUNK
    nblk = _moe_rows(T, nt, E) // MOE_BLOCK
    seg_blk = seg // MOE_BLOCK
    blk_end = jnp.cumsum(seg_blk)
    blk_ids = jnp.arange(nblk, dtype=jnp.int32)
    blk_expert = jnp.minimum(jnp.sum(blk_end[None, :] <= blk_ids[:, None], axis=1), E - 1).astype(jnp.int32)
    tail = jnp.stack([seg_start + tot, seg - tot], axis=1)
    exact = jnp.stack([jnp.sum(n8 // MOE_CHUNK, axis=1)]
                      + [jnp.sum((n8 // r) % 2, axis=1) for r in MOE_REMAINDERS], axis=1)
    return dict(strip=strip.reshape(-1).astype(jnp.int32), off=off.reshape(-1).astype(jnp.int32),
                nch=nch.reshape(-1).astype(jnp.int32), nct=jnp.sum(nch, axis=1).astype(jnp.int32),
                n8=n8.reshape(-1).astype(jnp.int32), exact=exact.reshape(-1).astype(jnp.int32),
                off_col=off.reshape(nt, E, 1).astype(jnp.int32), blk_expert=blk_expert,
                tail=tail.reshape(-1).astype(jnp.int32),
                n_used=blk_end[-1:].astype(jnp.int32))


def _stage_rows(tm):
    return TOP_K * tm + N_EXPERTS * SUBLANES


def _dispatch_kernel(strip_ref, off_ref, nch_ref, nct_ref, tail_ref, nu_ref, x_ref, et_ref, wt_ref, offc_ref,
                     xs_ref, lp_ref, stage, zero, sem, *, tm):
    i = pl.program_id(0)
    n_e = N_EXPERTS
    d = x_ref.shape[1]
    nrow = _stage_rows(tm)
    ch = MOE_CHUNK

    def wait_tile(tile):
        def one(c, carry):
            pltpu.make_async_copy(stage.at[pl.ds(0, ch)], xs_ref.at[pl.ds(0, ch)], sem.at[0]).wait()
            return carry
        lax.fori_loop(0, nct_ref[tile], one, 0)

    @pl.when(i == 0)
    def _():
        stage[nrow:, :] = jnp.zeros((ch, d + LANES), F32)

    et = et_ref[0]
    wt = wt_ref[0]
    eid = lax.broadcasted_iota(jnp.int32, (n_e, tm), 0)
    hots = [eid == et[k:k + 1, :] for k in range(TOP_K)]
    m_t = jnp.zeros((n_e, tm), F32)
    for h in hots:
        m_t = m_t + jnp.where(h, 1.0, 0.0)
    before = jnp.where(lax.broadcasted_iota(jnp.int32, (tm, tm), 0) < lax.broadcasted_iota(jnp.int32, (tm, tm), 1),
                       1.0, 0.0).astype(BF16)
    rank = jnp.dot(m_t.astype(BF16), before, preferred_element_type=F32)
    base = offc_ref[0].astype(F32) + rank
    lps = [jnp.sum(jnp.where(h, base, 0.0), axis=0, keepdims=True) for h in hots]
    rio = lax.broadcasted_iota(jnp.int32, (nrow, tm), 0)
    sel = jnp.zeros((nrow, tm), F32)
    wsel = jnp.zeros((nrow, tm), F32)
    for k in range(TOP_K):
        hit = rio == lps[k].astype(jnp.int32)
        sel = sel + jnp.where(hit, 1.0, 0.0)
        wsel = wsel + jnp.where(hit, wt[k:k + 1, :], 0.0)
    rows = jnp.dot(sel.astype(BF16), x_ref[...].astype(BF16), preferred_element_type=F32)
    wcol = jnp.sum(wsel, axis=1, keepdims=True)
    lp8 = jnp.concatenate(lps + [jnp.zeros((SUBLANES - TOP_K, tm), F32)], axis=0)
    lp_ref[...] = lp8.T.astype(jnp.int32)

    @pl.when(i > 0)
    def _():
        wait_tile(i - 1)

    stage[0:nrow, 0:d] = rows
    stage[0:nrow, d:d + LANES] = jnp.broadcast_to(wcol, (nrow, LANES))

    def per_expert(e, carry):
        idx = i * n_e + e

        def per_chunk(c, carry2):
            src0 = pl.multiple_of(off_ref[idx] + c * ch, SUBLANES)
            dst0 = pl.multiple_of(strip_ref[idx] + c * ch, SUBLANES)
            pltpu.make_async_copy(stage.at[pl.ds(src0, ch)], xs_ref.at[pl.ds(dst0, ch)], sem.at[0]).start()
            return carry2

        return lax.fori_loop(0, nch_ref[idx], per_chunk, carry)

    lax.fori_loop(0, n_e, per_expert, 0)

    def zero_fill(wait):
        def go(dst0, rows):
            cp = pltpu.make_async_copy(zero.at[pl.ds(0, rows)],
                                       xs_ref.at[pl.ds(pl.multiple_of(dst0, SUBLANES), rows)], sem.at[0])
            cp.wait() if wait else cp.start()

        def per_tail(e, carry):
            start, n = tail_ref[2 * e], tail_ref[2 * e + 1]
            nfull = n // ch

            def per_chunk(c, carry2):
                go(start + c * ch, ch)
                return carry2

            lax.fori_loop(0, nfull, per_chunk, 0)
            done = nfull * ch
            for r in MOE_REMAINDERS:
                has = (n // r) % 2

                @pl.when(has == 1)
                def _(done=done, r=r):
                    go(start + done, r)

                done = done + has * r
            return carry

        lax.fori_loop(0, n_e, per_tail, 0)

        def per_block(blk, carry):
            go(blk * MOE_BLOCK, MOE_BLOCK)
            return carry

        lax.fori_loop(nu_ref[0], xs_ref.shape[0] // MOE_BLOCK, per_block, 0)

    @pl.when(i == pl.num_programs(0) - 1)
    def _():
        wait_tile(i)
        zero[...] = jnp.zeros_like(zero)
        zero_fill(wait=False)
        zero_fill(wait=True)


def _dispatch(x1, et, wt, plan):
    T, D = x1.shape
    nt = et.shape[0]
    tm = T // nt
    E = N_EXPERTS
    P = _moe_rows(T, nt, E)
    tile3 = lambda i, *_: (i, 0, 0)
    grid_spec = pltpu.PrefetchScalarGridSpec(
        num_scalar_prefetch=6,
        grid=(nt,),
        in_specs=[pl.BlockSpec((tm, D), lambda i, *_: (i, 0)),
                  pl.BlockSpec((1, TOP_K, tm), tile3), pl.BlockSpec((1, TOP_K, tm), tile3),
                  pl.BlockSpec((1, E, 1), tile3)],
        out_specs=[pl.BlockSpec(memory_space=pl.ANY),
                   pl.BlockSpec((tm, SUBLANES), lambda i, *_: (i, 0))],
        scratch_shapes=[pltpu.VMEM((_stage_rows(tm) + MOE_CHUNK, D + LANES), F32),
                        pltpu.VMEM((MOE_BLOCK, D + LANES), F32),
                        pltpu.SemaphoreType.DMA((1,))],
    )
    return pl.pallas_call(
        functools.partial(_dispatch_kernel, tm=tm),
        grid_spec=grid_spec,
        out_shape=[jax.ShapeDtypeStruct((P, D + LANES), F32), jax.ShapeDtypeStruct((T, SUBLANES), jnp.int32)],
        compiler_params=_params(("arbitrary",)),
        name="moe_dispatch",
    )(plan["strip"], plan["off"], plan["nch"], plan["nct"], plan["tail"], plan["n_used"], x1, et, wt,
      plan["off_col"])


def _combine_kernel(strip_ref, off_ref, n8_ref, exact_ref, ys_ref, lp_ref, x_ref, g_ref, b_ref, *rest, tm, kd):
    if kd:
        wqk_ref, wvg_ref, wgr_ref, w2_ref, b2_ref, xo_ref, qkl_ref, vg_ref, land, sem = rest
    else:
        xo_ref, land, sem = rest
    i = pl.program_id(0)
    n_e = N_EXPERTS
    nrow = _stage_rows(tm)
    ch = MOE_CHUNK
    sizes = (ch,) + MOE_REMAINDERS
    slot = i % 2

    def strip_copy(src0, dst0, rows, sl):
        return pltpu.make_async_copy(ys_ref.at[pl.ds(pl.multiple_of(src0, SUBLANES), rows)],
                                     land.at[sl, pl.ds(pl.multiple_of(dst0, SUBLANES), rows)], sem.at[sl])

    def fetch(tile, sl):
        def per_expert(e, carry):
            idx = tile * n_e + e
            n8, src, dst = n8_ref[idx], strip_ref[idx], off_ref[idx]
            nfull = n8 // ch

            def per_chunk(c, carry2):
                strip_copy(src + c * ch, dst + c * ch, ch, sl).start()
                return carry2

            lax.fori_loop(0, nfull, per_chunk, 0)
            done = nfull * ch
            for r in MOE_REMAINDERS:
                has = (n8 // r) % 2

                @pl.when(has == 1)
                def _(done=done, r=r):
                    strip_copy(src + done, dst + done, r, sl).start()

                done = done + has * r
            return carry

        lax.fori_loop(0, n_e, per_expert, 0)

    @pl.when(i == 0)
    def _():
        land[...] = jnp.zeros_like(land)
        fetch(0, 0)

    @pl.when(i + 1 < pl.num_programs(0))
    def _():
        fetch(i + 1, 1 - slot)

    lp = lp_ref[...]
    cio = lax.broadcasted_iota(jnp.int32, (tm, nrow), 1)
    selt = jnp.zeros((tm, nrow), F32)
    for k in range(TOP_K):
        selt = selt + jnp.where(cio == lp[:, k:k + 1], 1.0, 0.0)
    selt = selt.astype(BF16)

    for s, rows in enumerate(sizes):
        def one(c, carry, rows=rows):
            strip_copy(0, 0, rows, slot).wait()
            return carry

        lax.fori_loop(0, exact_ref[i * len(sizes) + s], one, 0)

    v = land[slot, 0:nrow, :]
    hi = v.astype(BF16)
    lo = (v - hi.astype(F32)).astype(BF16)
    ffn = jnp.dot(selt, hi, preferred_element_type=F32) + jnp.dot(selt, lo, preferred_element_type=F32)
    xn = _layer_norm(DN_ALPHA * x_ref[...] + ffn, g_ref[...], b_ref[...])
    xo_ref[...] = xn
    if kd:
        xb = xn.astype(BF16)
        qkl_ref[:, :2 * kd] = jnp.dot(xb, wqk_ref[...], preferred_element_type=F32)
        vg_ref[...] = jnp.dot(xb, wvg_ref[...], preferred_element_type=F32)
        gr = jnp.dot(xb, wgr_ref[...], preferred_element_type=F32)
        z = jnp.dot(gr, w2_ref[...], precision=HIGHEST, preferred_element_type=F32) + b2_ref[...]
        log_sig = jnp.minimum(z, 0.0) - jnp.log1p(jnp.exp(-jnp.abs(z)))
        qkl_ref[:, 2 * kd:] = log_sig / GLA_GATE_TAU


def _combine_ln(ys, lp, x, plan, g, b, gla=None):
    T, D = x.shape
    nt = plan["nct"].shape[0]
    tm = T // nt
    const = lambda i, *_: (0, 0)
    row = lambda i, *_: (i, 0)
    in_specs = [pl.BlockSpec(memory_space=pl.ANY), pl.BlockSpec((tm, SUBLANES), row), pl.BlockSpec((tm, D), row),
                pl.BlockSpec((1, D), const), pl.BlockSpec((1, D), const)]
    out_specs = [pl.BlockSpec((tm, D), row)]
    out_shape = [jax.ShapeDtypeStruct((T, D), F32)]
    args = [ys, lp, x, g.reshape(1, D), b.reshape(1, D)]
    kd = 0
    if gla is not None:
        w_in1, w_gate2, b_gate, kd, vd = gla
        ws = [w_in1[:, :2 * kd].astype(BF16), w_in1[:, 2 * kd:2 * kd + 2 * vd].astype(BF16),
              w_in1[:, 2 * kd + 2 * vd:].astype(BF16), w_gate2.astype(F32), b_gate.reshape(1, kd)]
        in_specs += [pl.BlockSpec(w.shape, const) for w in ws]
        args += ws
        out_specs += [pl.BlockSpec((tm, 3 * kd), row), pl.BlockSpec((tm, 2 * vd), row)]
        out_shape += [jax.ShapeDtypeStruct((T, 3 * kd), F32), jax.ShapeDtypeStruct((T, 2 * vd), F32)]
    grid_spec = pltpu.PrefetchScalarGridSpec(
        num_scalar_prefetch=4,
        grid=(nt,),
        in_specs=in_specs,
        out_specs=out_specs,
        scratch_shapes=[pltpu.VMEM((2, _stage_rows(tm), D), F32), pltpu.SemaphoreType.DMA((2,))],
    )
    return pl.pallas_call(
        functools.partial(_combine_kernel, tm=tm, kd=kd),
        grid_spec=grid_spec,
        out_shape=out_shape,
        compiler_params=_params(("arbitrary",)),
        name="moe_combine_ln",
    )(plan["strip"], plan["off"], plan["n8"], plan["exact"], *args)


def _moe_ln(x1, et, wt, cnt, layer, w_gate_up, b_gate_up, w_down, b_down, g, b, gla=None):
    T = x1.shape[0]
    plan = _moe_plan(cnt, T)
    xs, lp = _dispatch(x1, et, wt, plan)
    wgu = _deinterleave_gate_up(w_gate_up, layer)
    bgu = jnp.concatenate([b_gate_up[layer, :, 0::2], b_gate_up[layer, :, 1::2]], axis=-1)[:, None, :].astype(F32)
    ys = _expert_ffn(xs, plan, wgu, w_down, layer, bgu, b_down[layer, :, None, :].astype(F32))
    return _combine_ln(ys, lp, x1, plan, g, b, gla)


def _gla_kernel(q_ref, k_ref, la_ref, v_ref, g_ref, nw_ref, o_ref, s_scr, *, tl, nh, dk, dv):
    C = GLA_CHUNK

    @pl.when(pl.program_id(1) == 0)
    def _():
        s_scr[...] = jnp.zeros_like(s_scr)

    ri = lax.broadcasted_iota(jnp.int32, (C, C), 0)
    ci = lax.broadcasted_iota(jnp.int32, (C, C), 1)
    lower = ri >= ci
    tri = jnp.where(lower, 1.0, 0.0).astype(F32)
    states = [s_scr[h] for h in range(nh)]
    for c in range(tl // C):
        rows = slice(c * C, (c + 1) * C)
        bcum = jnp.dot(tri, la_ref[0, rows, :], precision=HIGHEST, preferred_element_type=F32)
        btot = bcum[C - 1:C, :]
        qb = (q_ref[0, rows, :] * (dk ** -0.5) * jnp.exp(bcum)).astype(BF16)
        kc = k_ref[0, rows, :]
        k_in = (kc * jnp.exp(-bcum)).astype(BF16)
        k_st = kc * jnp.exp(btot - bcum)
        dec = jnp.exp(btot)
        for h in range(nh):
            ks, vs = slice(h * dk, (h + 1) * dk), slice(h * dv, (h + 1) * dv)
            vb = v_ref[0, rows, vs].astype(BF16)
            att = lax.dot_general(qb[:, ks], k_in[:, ks], (((1,), (1,)), ((), ())), preferred_element_type=F32)
            att = jnp.where(lower, att, 0.0)
            o = (jnp.dot(att.astype(BF16), vb, preferred_element_type=F32)
                 + jnp.dot(qb[:, ks], states[h].astype(BF16), preferred_element_type=F32))
            kv = jnp.dot(k_st[:, ks].T.astype(BF16), vb, preferred_element_type=F32)
            decay = jnp.broadcast_to(dec[:, ks], (SUBLANES, dk)).T[:, 0:1]
            states[h] = decay * states[h] + kv
            o = o * lax.rsqrt(jnp.mean(o * o, axis=-1, keepdims=True) + RMS_EPS) * nw_ref[...]
            gg = g_ref[0, rows, vs]
            o_ref[0, rows, vs] = o * (gg * jax.nn.sigmoid(gg))
    for h in range(nh):
        s_scr[h] = states[h]


def _gla(qkl, vg, norm_w, B, L, tl=256):
    T, kd3 = qkl.shape
    kd = kd3 // 3
    vd = vg.shape[1] // 2
    H = GLA_HEADS
    dk, dv = kd // H, vd // H
    qkl3 = qkl.reshape(B, L, kd3)
    vg3 = vg.reshape(B, L, 2 * vd)
    return pl.pallas_call(
        functools.partial(_gla_kernel, tl=tl, nh=H, dk=dk, dv=dv),
        grid=(B, L // tl),
        in_specs=[pl.BlockSpec((1, tl, kd), lambda b, l: (b, l, 0)),
                  pl.BlockSpec((1, tl, kd), lambda b, l: (b, l, 1)),
                  pl.BlockSpec((1, tl, kd), lambda b, l: (b, l, 2)),
                  pl.BlockSpec((1, tl, vd), lambda b, l: (b, l, 0)),
                  pl.BlockSpec((1, tl, vd), lambda b, l: (b, l, 1)),
                  pl.BlockSpec((1, dv), lambda b, l: (0, 0))],
        out_specs=pl.BlockSpec((1, tl, vd), lambda b, l: (b, l, 0)),
        out_shape=jax.ShapeDtypeStruct((B, L, vd), F32),
        scratch_shapes=[pltpu.VMEM((H, dk, dv), F32)],
        compiler_params=_params(("parallel", "arbitrary")),
        name="gla_mixer",
    )(qkl3, qkl3, qkl3, vg3, vg3, norm_w.reshape(1, dv).astype(F32))


def kernel(x, w_in0, s5_lam_re, s5_lam_im, s5_log_dt, s5_b_re, s5_b_im, s5_c_re, s5_c_im, s5_d, s5_w_glu,
           s5_b_glu, rel_bias, w_out0, w_in1, gla_w_gate2, gla_b_gate, gla_norm_w, w_out1, ln_mix_g, ln_mix_b,
           ln_ffn_g, ln_ffn_b, router_w, router_b, exp_w_gate_up, exp_b_gate_up, exp_w_down, exp_b_down):
    B, L, D = x.shape
    T = B * L
    s5w = s5_w_glu.shape[-1]
    kd = gla_w_gate2.shape[-1]
    vd = gla_norm_w.shape[-1] * GLA_HEADS

    u_tm, k_att, qt_att, vt_att = _inproj0(x, w_in0[0], s5w)
    y_a = _s5_mixer(u_tm.reshape(L, B, s5w), s5_lam_re[0], s5_lam_im[0], s5_log_dt[0], s5_b_re[0], s5_b_im[0],
                    s5_c_re[0], s5_c_im[0], s5_d[0].reshape(-1), s5_w_glu[0], s5_b_glu[0])
    y_b = _moba(qt_att, k_att, vt_att, rel_bias)
    tl = MIX_TILE
    w0 = w_out0[0].astype(BF16)
    ys = [(y_a.reshape(L, B * s5w), pl.BlockSpec((tl, s5w), lambda bb, l: (l, bb))),
          (y_b, pl.BlockSpec((1, tl, D - s5w), lambda bb, l: (bb, l, 0)))]
    x1, et, wt, cnt = _mix_ln(x, ys, [w0[:s5w], w0[s5w:]], ln_mix_g[0], ln_mix_b[0], router_w[0], router_b[0])
    x2, qkl, vg = _moe_ln(x1.reshape(T, D), et, wt, cnt, 0, exp_w_gate_up, exp_b_gate_up, exp_w_down,
                          exp_b_down, ln_ffn_g[0], ln_ffn_b[0],
                          gla=(w_in1[0], gla_w_gate2[0], gla_b_gate[0], kd, vd))

    y_c = _gla(qkl, vg, gla_norm_w[0], B, L)
    ys = [(y_c, pl.BlockSpec((1, tl, vd), lambda bb, l: (bb, l, 0)))]
    x3, et, wt, cnt = _mix_ln(x2.reshape(B, L, D), ys, [w_out1[0].astype(BF16)], ln_mix_g[1], ln_mix_b[1],
                              router_w[1], router_b[1])
    (out,) = _moe_ln(x3.reshape(T, D), et, wt, cnt, 1, exp_w_gate_up, exp_b_gate_up, exp_w_down,
                     exp_b_down, ln_ffn_g[1], ln_ffn_b[1])
    return out.reshape(B, L, D)
```

```python
import functools
import math

import jax
import jax.numpy as jnp
import numpy as np
from jax import lax
from jax.experimental import pallas as pl
from jax.experimental.pallas import tpu as pltpu

F32 = jnp.float32
BF16 = jnp.bfloat16
HIGHEST = lax.Precision.HIGHEST

DEPTH = 2
S5_GROUP = 16
S5_STATE = 64
MOBA_HEAD_DIM = 64
MOBA_BLOCK = 256
MOBA_TOPK = 3
REL_BUCKETS = 32
REL_MAX_DIST = 2048
GLA_HEADS = 4
GLA_GATE_TAU = 16.0
GLA_CHUNK = 64
N_EXPERTS = 32
TOP_K = 4
SWIGLU_LIMIT = 7.0
SWIGLU_ALPHA = 1.702
MOE_BLOCK = 512
MOE_TILE = 256
MIX_TILE = 512
MOE_CHUNK = 32
SUBLANES = 8
MOE_REMAINDERS = (16, 8)
DN_ALPHA = (2 * DEPTH) ** 0.25
LN_EPS = 1e-5
RMS_EPS = 1e-5

V7X_VMEM_LIMIT_BYTES = 56 * 1024 * 1024
LANES = 128

S5_GROUPS_PER_CHUNK = LANES // S5_GROUP
S5_CHUNK_STATES = S5_GROUPS_PER_CHUNK * S5_STATE


def _params(sem):
    return pltpu.CompilerParams(dimension_semantics=sem, vmem_limit_bytes=V7X_VMEM_LIMIT_BYTES)


def _layer_norm(r, g, b):
    mu = jnp.mean(r, axis=-1, keepdims=True)
    c = r - mu
    var = jnp.mean(c * c, axis=-1, keepdims=True)
    return c * lax.rsqrt(var + LN_EPS) * g + b


def _inproj0_kernel(x_ref, wuk_ref, wqvt_ref, u_ref, k_ref, qt_ref, vt_ref, *, s5w):
    xb = x_ref[0].astype(BF16)
    h = jnp.dot(xb, wuk_ref[...], preferred_element_type=F32)
    u_ref[...] = h[:, :s5w]
    k_ref[0] = h[:, s5w:]
    ht = lax.dot_general(wqvt_ref[...], xb, (((1,), (1,)), ((), ())), preferred_element_type=F32)
    aw = ht.shape[0] // 2
    qt_ref[0] = ht[:aw]
    vt_ref[0] = ht[aw:].astype(BF16)


def _inproj0(x, w_in0, s5w, tl=512):
    B, L, D = x.shape
    aw = (w_in0.shape[1] - s5w) // 3
    wb = w_in0.astype(BF16)
    wuk = jnp.concatenate([wb[:, :s5w], wb[:, s5w + aw:s5w + 2 * aw]], axis=1)
    wqvt = jnp.concatenate([wb[:, s5w:s5w + aw], wb[:, s5w + 2 * aw:]], axis=1).T
    return pl.pallas_call(
        functools.partial(_inproj0_kernel, s5w=s5w),
        grid=(B, L // tl),
        in_specs=[pl.BlockSpec((1, tl, D), lambda b, l: (b, l, 0)),
                  pl.BlockSpec(wuk.shape, lambda b, l: (0, 0)),
                  pl.BlockSpec(wqvt.shape, lambda b, l: (0, 0))],
        out_specs=[pl.BlockSpec((tl, s5w), lambda b, l: (l, b)),
                   pl.BlockSpec((1, tl, aw), lambda b, l: (b, l, 0)),
                   pl.BlockSpec((1, aw, tl), lambda b, l: (b, 0, l)),
                   pl.BlockSpec((1, aw, tl), lambda b, l: (b, 0, l))],
        out_shape=[jax.ShapeDtypeStruct((L, B * s5w), F32),
                   jax.ShapeDtypeStruct((B, L, aw), F32),
                   jax.ShapeDtypeStruct((B, aw, L), F32),
                   jax.ShapeDtypeStruct((B, aw, L), BF16)],
        compiler_params=_params(("parallel", "parallel")),
        name="inproj0",
    )(x, wuk, wqvt)


def _s5_discretize(lam_re, lam_im, log_dt, b_re, b_im):
    dt = jnp.exp(log_dt.astype(F32))[:, None]
    lr, li = lam_re.astype(F32), lam_im.astype(F32)
    mag = jnp.exp(lr * dt)
    ab_re, ab_im = mag * jnp.cos(li * dt), mag * jnp.sin(li * dt)
    er, ei = ab_re - 1.0, ab_im
    den = lr * lr + li * li
    q_re = (er * lr + ei * li) / den
    q_im = (ei * lr - er * li) / den
    br_, bi_ = b_re.astype(F32), b_im.astype(F32)
    bb_re = q_re[..., None] * br_ - q_im[..., None] * bi_
    bb_im = q_re[..., None] * bi_ + q_im[..., None] * br_
    return ab_re, ab_im, bb_re, bb_im


def _s5_kernel(u_ref, bm_ref, cm_ref, are_ref, aim_ref, d_ref, wg_ref, bg_ref, y_ref,
               s_scr, st_scr, z_scr, *, tl, nb, nchunk):
    ns = S5_CHUNK_STATES

    @pl.when(pl.program_id(0) == 0)
    def _():
        st_scr[...] = jnp.zeros_like(st_scr)

    u = u_ref[...].reshape(tl * nb, nchunk * LANES)
    for j in range(nchunk):
        uj = u[:, j * LANES:(j + 1) * LANES]
        s_scr[...] = jnp.dot(uj.astype(BF16), bm_ref[j], preferred_element_type=F32)
        ar = jnp.broadcast_to(are_ref[j], (nb, ns))
        ai = jnp.broadcast_to(aim_ref[j], (nb, ns))

        def step(t, carry, ar=ar, ai=ai):
            sre, sim = carry
            r0 = pl.multiple_of(t * nb, nb)
            nre = ar * sre - ai * sim + s_scr[pl.ds(r0, nb), 0:ns]
            nim = ar * sim + ai * sre + s_scr[pl.ds(r0, nb), ns:2 * ns]
            s_scr[pl.ds(r0, nb), 0:ns] = nre
            s_scr[pl.ds(r0, nb), ns:2 * ns] = nim
            return nre, nim

        sre, sim = lax.fori_loop(0, tl, step, (st_scr[j, :, 0:ns], st_scr[j, :, ns:2 * ns]), unroll=2)
        st_scr[j, :, 0:ns] = sre
        st_scr[j, :, ns:2 * ns] = sim
        yj = jnp.dot(s_scr[...].astype(BF16), cm_ref[j], preferred_element_type=F32)
        z_scr[:, j * LANES:(j + 1) * LANES] = yj + uj * d_ref[:, j * LANES:(j + 1) * LANES]
    z = jax.nn.gelu(z_scr[...])
    gate = jax.nn.sigmoid(jnp.dot(z.astype(BF16), wg_ref[...], preferred_element_type=F32) + bg_ref[...])
    y_ref[...] = (z * gate).reshape(tl, nb, nchunk * LANES)


def _s5_mixer(u_tm, lam_re, lam_im, log_dt, b_re, b_im, c_re, c_im, d_skip, w_glu, b_glu, tl=32):
    L, B, W = u_tm.shape
    G, P, H = lam_re.shape[0], S5_STATE, S5_GROUP
    gc = S5_GROUPS_PER_CHUNK
    nchunk = G // gc
    ab_re, ab_im, bb_re, bb_im = _s5_discretize(lam_re, lam_im, log_dt, b_re, b_im)
    eye = jnp.eye(gc, dtype=F32)

    def b_blocks(bb):
        return jnp.einsum('jgph,gk->jghkp', bb.reshape(nchunk, gc, P, H), eye).reshape(nchunk, gc * H, gc * P)

    def c_blocks(cc):
        return jnp.einsum('jghp,gk->jgpkh', cc.reshape(nchunk, gc, H, P), eye).reshape(nchunk, gc * P, gc * H)

    bm = jnp.concatenate([b_blocks(bb_re), b_blocks(bb_im)], axis=2).astype(BF16)
    cm = jnp.concatenate([c_blocks(c_re.astype(F32)), -c_blocks(c_im.astype(F32))], axis=1).astype(BF16)
    are = ab_re.reshape(nchunk, 1, gc * P)
    aim = ab_im.reshape(nchunk, 1, gc * P)
    m = tl * B
    const3 = lambda l: (0, 0, 0)
    const2 = lambda l: (0, 0)
    return pl.pallas_call(
        functools.partial(_s5_kernel, tl=tl, nb=B, nchunk=nchunk),
        grid=(L // tl,),
        in_specs=[pl.BlockSpec((tl, B, W), lambda l: (l, 0, 0)),
                  pl.BlockSpec(bm.shape, const3), pl.BlockSpec(cm.shape, const3),
                  pl.BlockSpec(are.shape, const3), pl.BlockSpec(aim.shape, const3),
                  pl.BlockSpec((1, W), const2), pl.BlockSpec((W, W), const2), pl.BlockSpec((1, W), const2)],
        out_specs=pl.BlockSpec((tl, B, W), lambda l: (l, 0, 0)),
        out_shape=jax.ShapeDtypeStruct((L, B, W), F32),
        scratch_shapes=[pltpu.VMEM((m, 2 * S5_CHUNK_STATES), F32),
                        pltpu.VMEM((nchunk, B, 2 * S5_CHUNK_STATES), F32),
                        pltpu.VMEM((m, W), F32)],
        compiler_params=_params(("arbitrary",)),
        name="s5_mixer",
    )(u_tm, bm, cm, are, aim, d_skip.reshape(1, W).astype(F32), w_glu.astype(BF16),
      b_glu.reshape(1, W).astype(F32))


def _rel_bucket(n):
    n = jnp.maximum(n, 0)
    max_exact = REL_BUCKETS // 2
    nf = jnp.maximum(n, 1).astype(F32)
    large = max_exact + (jnp.log(nf / max_exact) / math.log(REL_MAX_DIST / max_exact)
                         * (REL_BUCKETS - max_exact)).astype(jnp.int32)
    large = jnp.minimum(large, REL_BUCKETS - 1)
    return jnp.where(n < max_exact, n, large)


MOBA_ONES_ROWS = 16
MOBA_NEG = -1e30
LOG2E = math.log2(math.e)


def _moba_kernel(qt_ref, k_ref, vt_ref, wv_ref, hot_ref, o_ref, bias_scr, kk_scr, va_scr, km_scr, qa_scr,
                 *, nblk, bs, dh, topk):
    b, own = pl.program_id(1), pl.program_id(2)
    i = own
    tq = bs
    hpb = LANES // dh

    @pl.when((b == 0) & (i == 0))
    def _():
        r_io = lax.broadcasted_iota(jnp.int32, (bs, 2 * bs), 0)
        c_io = lax.broadcasted_iota(jnp.int32, (bs, 2 * bs), 1)
        for hh in range(hpb):
            for dd in range(nblk):
                t = pltpu.roll(jnp.broadcast_to(wv_ref[hh, dd:dd + 1, :], (bs, 2 * bs)), bs + 1, 1,
                               stride=1, stride_axis=0)
                if dd == 0:
                    t = jnp.where(c_io >= r_io, t, MOBA_NEG)
                bias_scr[hh, dd] = t[:, 0:bs]

    @pl.when(i == 0)
    def _():
        kf = k_ref[0]
        km_scr[...] = jnp.mean(kf.reshape(nblk, bs, LANES), axis=1)
        kk_scr[:, 0:LANES] = kf.astype(BF16)
        kk_scr[:, LANES:] = hot_ref[...]
        for n in range(nblk):
            va_scr[n, 0:LANES, :] = vt_ref[0, :, n * bs:(n + 1) * bs]
            va_scr[n, LANES:, :] = jnp.ones((MOBA_ONES_ROWS, bs), BF16)
        seq = nblk * bs
        q2 = qt_ref[0] * (dh ** -0.5 * LOG2E)
        f_io = lax.broadcasted_iota(jnp.int32, (LANES, seq), 0)
        blk_io = lax.broadcasted_iota(jnp.int32, (nblk, seq), 0)
        own_q = lax.broadcasted_iota(jnp.int32, (nblk, seq), 1) // bs
        qms, pens = [], []
        for hh in range(hpb):
            qm = jnp.where((f_io >= hh * dh) & (f_io < (hh + 1) * dh), q2, 0.0)
            gate = jnp.dot(km_scr[...], qm, precision=HIGHEST, preferred_element_type=F32)
            cnt = jnp.zeros((nblk, seq), jnp.int32)
            for m in range(nblk):
                gm = gate[m:m + 1, :]
                beats = (gm > gate) | ((gm == gate) & (m < blk_io))
                cnt = cnt + jnp.where(beats & (m < own_q), 1, 0)
            keep = (blk_io >= own_q) | (cnt < topk)
            pens.append(jnp.where(keep, 0.0, MOBA_NEG))
            qms.append(qm.astype(BF16))
        for n in range(nblk):
            cols = slice(n * bs, (n + 1) * bs)
            qa_scr[n, 0:LANES, :] = jnp.concatenate([qm[:, cols] for qm in qms], axis=1)
            pen = jnp.concatenate([p[:, cols] for p in pens], axis=1)
            qa_scr[n, LANES:, :] = jnp.concatenate(
                [pen, jnp.zeros((LANES - nblk, hpb * bs), F32)], axis=0).astype(BF16)

    qa = qa_scr[own]

    def scores(j):
        r0 = pl.multiple_of(j * bs, bs)
        s = jnp.dot(kk_scr[pl.ds(r0, bs), :], qa, preferred_element_type=F32)
        return s + jnp.concatenate([bias_scr[hh, own - j] for hh in range(hpb)], axis=1)

    def pair(ja, jb, pen_b, m_i, acc):
        sa = scores(ja)
        sb = scores(jb) + pen_b
        m_n = jnp.maximum(m_i, jnp.maximum(jnp.max(sa, axis=0, keepdims=True), jnp.max(sb, axis=0, keepdims=True)))
        alpha = jnp.exp2(m_i - m_n)
        pa = jnp.exp2(sa - m_n).astype(BF16)
        pb = jnp.exp2(sb - m_n).astype(BF16)
        return m_n, (alpha * acc + jnp.dot(va_scr[ja], pa, preferred_element_type=F32)
                     + jnp.dot(va_scr[jb], pb, preferred_element_type=F32))

    odd = own % 2
    m1, acc1 = pair(own, jnp.maximum(own - 1, 0), jnp.where(odd == 1, 0.0, MOBA_NEG),
                    jnp.full((1, hpb * tq), MOBA_NEG, F32), jnp.zeros((LANES + MOBA_ONES_ROWS, hpb * tq), F32))
    _, acc_f = lax.fori_loop(0, own // 2, lambda jj, c: pair(2 * jj, 2 * jj + 1, 0.0, *c), (m1, acc1))
    on = acc_f[0:LANES, :] / acc_f[LANES:LANES + 1, :]
    ot = jnp.concatenate([on[hh * dh:(hh + 1) * dh, hh * tq:(hh + 1) * tq] for hh in range(hpb)], axis=0)
    o_ref[0] = ot.T


def _moba(qt, k, vt, rel_bias):
    B, L, W = k.shape
    dh, bs = MOBA_HEAD_DIM, MOBA_BLOCK
    H = W // dh
    nblk = L // bs
    hpb = LANES // dh
    dist = jnp.arange(L, dtype=jnp.int32)
    by_dist = rel_bias.astype(F32).T[:, _rel_bucket(dist)] * LOG2E
    idx = np.clip(np.arange(nblk)[:, None] * bs - (bs - 1) + np.arange(2 * bs)[None, :], 0, L - 1)
    vecs = by_dist[:, idx]
    hot = np.zeros((L, LANES), np.float32)
    hot[np.arange(L), np.arange(L) // bs] = 1.0
    return pl.pallas_call(
        functools.partial(_moba_kernel, nblk=nblk, bs=bs, dh=dh, topk=min(MOBA_TOPK, nblk)),
        grid=(H // hpb, B, nblk),
        in_specs=[pl.BlockSpec((1, LANES, L), lambda h, b, i: (b, h, 0)),
                  pl.BlockSpec((1, L, LANES), lambda h, b, i: (b, 0, h)),
                  pl.BlockSpec((1, LANES, L), lambda h, b, i: (b, h, 0)),
                  pl.BlockSpec((hpb, nblk, 2 * bs), lambda h, b, i: (h, 0, 0)),
                  pl.BlockSpec((L, LANES), lambda h, b, i: (0, 0))],
        out_specs=pl.BlockSpec((1, bs, LANES), lambda h, b, i: (b, i, h)),
        out_shape=jax.ShapeDtypeStruct((B, L, W), F32),
        scratch_shapes=[pltpu.VMEM((hpb, nblk, bs, bs), F32),
                        pltpu.VMEM((L, 2 * LANES), BF16),
                        pltpu.VMEM((nblk, LANES + MOBA_ONES_ROWS, bs), BF16),
                        pltpu.VMEM((nblk, LANES), F32),
                        pltpu.VMEM((nblk, 2 * LANES, hpb * bs), BF16)],
        compiler_params=_params(("arbitrary", "arbitrary", "arbitrary")),
        name="moba_attention",
    )(qt, k, vt, vecs, jnp.asarray(hot, BF16))


def _route_top_k(xn, wrt_ref, br_ref):
    lt = lax.dot_general(wrt_ref[...], xn, (((1,), (1,)), ((), ())), precision=HIGHEST,
                         preferred_element_type=F32) + br_ref[...]
    n_e = lt.shape[0]
    eid = lax.broadcasted_iota(jnp.int32, lt.shape, 0)
    cur = lt
    vals, idxs = [], []
    for _ in range(TOP_K):
        m = jnp.max(cur, axis=0, keepdims=True)
        idx = jnp.min(jnp.where(cur == m, eid, n_e), axis=0, keepdims=True)
        vals.append(m)
        idxs.append(idx)
        cur = jnp.where(eid == idx, -jnp.inf, cur)
    ex = [jnp.exp(v - vals[0]) for v in vals]
    den = ex[0]
    for t in ex[1:]:
        den = den + t
    hot = jnp.zeros(lt.shape, F32)
    for idx in idxs:
        hot = hot + jnp.where(eid == idx, 1.0, 0.0)
    return (jnp.concatenate(idxs, axis=0), jnp.concatenate([t / den for t in ex], axis=0),
            jnp.sum(hot, axis=1, keepdims=True))


def _mix_ln_kernel(*refs, n_in):
    x_ref = refs[0]
    y_refs = refs[1:1 + n_in]
    w_refs = refs[1 + n_in:1 + 2 * n_in]
    g_ref, b_ref, wrt_ref, br_ref, x1_ref, et_ref, wt_ref, cnt_ref = refs[1 + 2 * n_in:]
    mix = None
    for y_ref, w_ref in zip(y_refs, w_refs):
        y = y_ref[...]
        y = y.reshape(y.shape[-2], y.shape[-1]).astype(BF16)
        t = jnp.dot(y, w_ref[...], preferred_element_type=F32)
        mix = t if mix is None else mix + t
    xn = _layer_norm(DN_ALPHA * x_ref[0] + mix, g_ref[...], b_ref[...])
    x1_ref[0] = xn
    for s in range(et_ref.shape[0]):
        et, wt, cnt = _route_top_k(xn[s * MOE_TILE:(s + 1) * MOE_TILE], wrt_ref, br_ref)
        et_ref[s] = et
        wt_ref[s] = wt
        cnt_ref[s] = cnt


def _mix_ln(x, ys, ws, g, b, wr, br):
    B, L, D = x.shape
    E = wr.shape[1]
    tl = MIX_TILE
    nl = L // tl
    sub = tl // MOE_TILE
    nt = B * L // MOE_TILE
    const = lambda bb, l: (0, 0)
    tile = lambda bb, l: (bb * nl + l, 0, 0)
    in_specs = [pl.BlockSpec((1, tl, D), lambda bb, l: (bb, l, 0))]
    in_specs += [spec for _, spec in ys]
    in_specs += [pl.BlockSpec(w.shape, const) for w in ws]
    in_specs += [pl.BlockSpec((1, D), const), pl.BlockSpec((1, D), const),
                 pl.BlockSpec((E, D), const), pl.BlockSpec((E, 1), const)]
    return pl.pallas_call(
        functools.partial(_mix_ln_kernel, n_in=len(ys)),
        grid=(B, nl),
        in_specs=in_specs,
        out_specs=[pl.BlockSpec((1, tl, D), lambda bb, l: (bb, l, 0)),
                   pl.BlockSpec((sub, TOP_K, MOE_TILE), tile), pl.BlockSpec((sub, TOP_K, MOE_TILE), tile),
                   pl.BlockSpec((sub, E, 1), tile)],
        out_shape=[jax.ShapeDtypeStruct((B, L, D), F32),
                   jax.ShapeDtypeStruct((nt, TOP_K, MOE_TILE), jnp.int32),
                   jax.ShapeDtypeStruct((nt, TOP_K, MOE_TILE), F32),
                   jax.ShapeDtypeStruct((nt, E, 1), F32)],
        compiler_params=_params(("parallel", "parallel")),
        name="outproj_ln_router",
    )(x, *[a for a, _ in ys], *ws, g.reshape(1, D), b.reshape(1, D), wr.astype(F32).T, br.reshape(E, 1))


def _deinterleave_kernel(w_ref, p_ref, o_ref, *, ff):
    w = w_ref[0, 0].astype(BF16)
    for c in range(2 * ff // (2 * LANES)):
        t = jnp.dot(w[:, 2 * LANES * c:2 * LANES * (c + 1)], p_ref[...], preferred_element_type=F32)
        o_ref[0, :, LANES * c:LANES * (c + 1)] = t[:, :LANES].astype(BF16)
        o_ref[0, :, ff + LANES * c:ff + LANES * (c + 1)] = t[:, LANES:].astype(BF16)


def _deinterleave_gate_up(w_gate_up, layer, tk=512):
    _, E, D, F2 = w_gate_up.shape
    src = np.arange(2 * LANES)
    dst = np.where(src % 2 == 0, src // 2, LANES + src // 2)
    perm = np.zeros((2 * LANES, 2 * LANES), np.float32)
    perm[src, dst] = 1.0
    return pl.pallas_call(
        functools.partial(_deinterleave_kernel, ff=F2 // 2),
        grid=(E, D // tk),
        in_specs=[pl.BlockSpec((1, 1, tk, F2), lambda e, k: (layer, e, k, 0)),
                  pl.BlockSpec((2 * LANES, 2 * LANES), lambda e, k: (0, 0))],
        out_specs=pl.BlockSpec((1, tk, F2), lambda e, k: (e, k, 0)),
        out_shape=jax.ShapeDtypeStruct((E, D, F2), BF16),
        compiler_params=_params(("parallel", "parallel")),
        name="deinterleave_gate_up",
    )(w_gate_up, jnp.asarray(perm, BF16))


def _expert_kernel(be_ref, nu_ref, xs_ref, wgu_ref, wd_ref, bgu_ref, bd_ref, ys_ref):
    blk = pl.program_id(0)
    ff = wd_ref.shape[2]
    d = wd_ref.shape[3]

    @pl.when(blk < nu_ref[0])
    def _():
        x = xs_ref[:, :d].astype(BF16)
        pw = xs_ref[:, d:d + 1]
        h = jnp.dot(x, wgu_ref[0], preferred_element_type=F32) + bgu_ref[0]
        g = h[:, :ff]
        u = h[:, ff:]
        g = jnp.minimum(g, SWIGLU_LIMIT)
        u = jnp.clip(u, -SWIGLU_LIMIT, SWIGLU_LIMIT)
        act = g * jax.nn.sigmoid(SWIGLU_ALPHA * g) * (u + 1.0)
        y = jnp.dot(act.astype(BF16), wd_ref[0, 0].astype(BF16), preferred_element_type=F32) + bd_ref[0]
        ys_ref[...] = y * pw

    @pl.when(blk >= nu_ref[0])
    def _():
        ys_ref[...] = jnp.zeros_like(ys_ref)


def _expert_ffn(xs, plan, wgu, w_down, layer, bgu, bd):
    P, DW = xs.shape
    F, D = w_down.shape[2], w_down.shape[3]
    bm = MOE_BLOCK
    wmap = lambda i, be, nu: (be[i], 0, 0)
    rmap = lambda i, be, nu: (i, 0)
    grid_spec = pltpu.PrefetchScalarGridSpec(
        num_scalar_prefetch=2,
        grid=(P // bm,),
        in_specs=[pl.BlockSpec((bm, DW), rmap),
                  pl.BlockSpec((1, D, 2 * F), wmap),
                  pl.BlockSpec((1, 1, F, D), lambda i, be, nu: (layer, be[i], 0, 0)),
                  pl.BlockSpec((1, 1, 2 * F), wmap), pl.BlockSpec((1, 1, D), wmap)],
        out_specs=pl.BlockSpec((bm, D), rmap),
    )
    return pl.pallas_call(
        _expert_kernel,
        grid_spec=grid_spec,
        out_shape=jax.ShapeDtypeStruct((P, D), F32),
        compiler_params=_params(("arbitrary",)),
        name="moe_experts",
    )(plan["blk_expert"], plan["n_used"], xs, wgu, w_down, bgu, bd)


def _moe_rows(T, nt, E):
    bound = T * TOP_K + (SUBLANES - 1) * E * nt + E * MOE_BLOCK
    return -(-bound // MOE_BLOCK) * MOE_BLOCK


def _moe_plan(cnt, T):
    nt, E = cnt.shape[0], cnt.shape[1]
    c = cnt.reshape(nt, E).astype(jnp.int32)
    n8 = (c + SUBLANES - 1) // SUBLANES * SUBLANES
    tot = jnp.sum(n8, axis=0)
    seg = (tot + MOE_BLOCK - 1) // MOE_BLOCK * MOE_BLOCK
    seg_start = jnp.cumsum(seg) - seg
    strip = seg_start[None, :] + jnp.cumsum(n8, axis=0) - n8
    off = jnp.cumsum(n8, axis=1) - n8
    nblk = _moe_rows(T, nt, E) // MOE_BLOCK
    seg_blk = seg // MOE_BLOCK
    blk_end = jnp.cumsum(seg_blk)
    blk_ids = jnp.arange(nblk, dtype=jnp.int32)
    blk_expert = jnp.minimum(jnp.sum(blk_end[None, :] <= blk_ids[:, None], axis=1), E - 1).astype(jnp.int32)
    tail = jnp.stack([seg_start + tot, seg - tot], axis=1)
    exact = jnp.stack([jnp.sum(n8 // MOE_CHUNK, axis=1)]
                      + [jnp.sum((n8 // r) % 2, axis=1) for r in MOE_REMAINDERS], axis=1)
    return dict(strip=strip.reshape(-1).astype(jnp.int32), off=off.reshape(-1).astype(jnp.int32),
                n8=n8.reshape(-1).astype(jnp.int32), exact=exact.reshape(-1).astype(jnp.int32),
                off_col=off.reshape(nt, E, 1).astype(jnp.int32), blk_expert=blk_expert,
                tail=tail.reshape(-1).astype(jnp.int32),
                n_used=blk_end[-1:].astype(jnp.int32))


def _stage_rows(tm):
    return TOP_K * tm + N_EXPERTS * SUBLANES


MOE_COPY_SIZES = (MOE_CHUNK,) + MOE_REMAINDERS


def _for_each_piece(n, fn):
    nfull = n // MOE_CHUNK

    def per_chunk(c, carry):
        fn(c * MOE_CHUNK, MOE_CHUNK)
        return carry

    lax.fori_loop(0, nfull, per_chunk, 0)
    done = nfull * MOE_CHUNK
    for r in MOE_REMAINDERS:
        has = (n // r) % 2

        @pl.when(has == 1)
        def _(done=done, r=r):
            fn(done, r)

        done = done + has * r


def _dispatch_kernel(strip_ref, off_ref, n8_ref, exact_ref, tail_ref, nu_ref, x_ref, et_ref, wt_ref, offc_ref,
                     xs_ref, lp_ref, stage, sel_scr, zero, sem, *, tm):
    i = pl.program_id(0)
    last = pl.num_programs(0) - 1
    n_e = N_EXPERTS
    d = x_ref.shape[1]
    nrow = _stage_rows(tm)
    slot = i % 2

    def strip_copy(src0, dst0, rows, sl):
        return pltpu.make_async_copy(stage.at[sl, pl.ds(pl.multiple_of(src0, SUBLANES), rows)],
                                     xs_ref.at[pl.ds(pl.multiple_of(dst0, SUBLANES), rows)], sem.at[sl])

    def wait_tile(tile, sl):
        for s, rows in enumerate(MOE_COPY_SIZES):
            def one(c, carry, rows=rows):
                strip_copy(0, 0, rows, sl).wait()
                return carry

            lax.fori_loop(0, exact_ref[tile * len(MOE_COPY_SIZES) + s], one, 0)

    et = et_ref[0]
    wt = wt_ref[0]
    eid = lax.broadcasted_iota(jnp.int32, (n_e, tm), 0)
    hots = [eid == et[k:k + 1, :] for k in range(TOP_K)]
    m_t = jnp.zeros((n_e, tm), F32)
    for h in hots:
        m_t = m_t + jnp.where(h, 1.0, 0.0)
    before = jnp.where(lax.broadcasted_iota(jnp.int32, (tm, tm), 0) < lax.broadcasted_iota(jnp.int32, (tm, tm), 1),
                       1.0, 0.0).astype(BF16)
    rank = jnp.dot(m_t.astype(BF16), before, preferred_element_type=F32)
    base = offc_ref[0].astype(F32) + rank
    lps = [jnp.sum(jnp.where(h, base, 0.0), axis=0, keepdims=True) for h in hots]
    lpi = [lp.astype(jnp.int32) for lp in lps]
    lp_ref[0] = jnp.concatenate(lpi, axis=0)

    wcols = []
    for rc in range(nrow // LANES):
        rio = lax.broadcasted_iota(jnp.int32, (LANES, tm), 0) + rc * LANES
        sel = jnp.zeros((LANES, tm), F32)
        wsel = jnp.zeros((LANES, tm), F32)
        for k in reversed(range(TOP_K)):
            hit = rio == lpi[k]
            sel = jnp.where(hit, 1.0, sel)
            wsel = jnp.where(hit, wt[k:k + 1, :], wsel)
        sel_scr[rc * LANES:(rc + 1) * LANES, :] = sel.astype(BF16)
        wcols.append(jnp.sum(wsel, axis=1, keepdims=True))

    @pl.when(i >= 2)
    def _():
        wait_tile(i - 2, slot)

    stage[slot, :, 0:d] = jnp.dot(sel_scr[...], x_ref[...].astype(BF16), preferred_element_type=F32)
    for rc, wcol in enumerate(wcols):
        stage[slot, rc * LANES:(rc + 1) * LANES, d:d + LANES] = jnp.broadcast_to(wcol, (LANES, LANES))

    def per_expert(e, carry):
        idx = i * n_e + e
        src, dst = off_ref[idx], strip_ref[idx]
        _for_each_piece(n8_ref[idx], lambda o, rows: strip_copy(src + o, dst + o, rows, slot).start())
        return carry

    lax.fori_loop(0, n_e, per_expert, 0)

    def zero_fill(wait):
        def go(dst0, rows):
            cp = pltpu.make_async_copy(zero.at[pl.ds(0, rows)],
                                       xs_ref.at[pl.ds(pl.multiple_of(dst0, SUBLANES), rows)], sem.at[2])
            cp.wait() if wait else cp.start()

        def per_tail(e, carry):
            start = tail_ref[2 * e]
            _for_each_piece(tail_ref[2 * e + 1], lambda o, rows: go(start + o, rows))
            return carry

        lax.fori_loop(0, n_e, per_tail, 0)

        def per_block(blk, carry):
            go(blk * MOE_BLOCK, MOE_BLOCK)
            return carry

        lax.fori_loop(nu_ref[0], xs_ref.shape[0] // MOE_BLOCK, per_block, 0)

    @pl.when(i == last)
    def _():
        @pl.when(i >= 1)
        def _():
            wait_tile(i - 1, 1 - slot)

        wait_tile(i, slot)
        zero[...] = jnp.zeros_like(zero)
        zero_fill(wait=False)
        zero_fill(wait=True)


def _dispatch(x1, et, wt, plan):
    T, D = x1.shape
    nt = et.shape[0]
    tm = T // nt
    E = N_EXPERTS
    P = _moe_rows(T, nt, E)
    nrow = _stage_rows(tm)
    tile3 = lambda i, *_: (i, 0, 0)
    grid_spec = pltpu.PrefetchScalarGridSpec(
        num_scalar_prefetch=6,
        grid=(nt,),
        in_specs=[pl.BlockSpec((tm, D), lambda i, *_: (i, 0)),
                  pl.BlockSpec((1, TOP_K, tm), tile3), pl.BlockSpec((1, TOP_K, tm), tile3),
                  pl.BlockSpec((1, E, 1), tile3)],
        out_specs=[pl.BlockSpec(memory_space=pl.ANY), pl.BlockSpec((1, TOP_K, tm), tile3)],
        scratch_shapes=[pltpu.VMEM((2, nrow, D + LANES), F32),
                        pltpu.VMEM((nrow, tm), BF16),
                        pltpu.VMEM((MOE_BLOCK, D + LANES), F32),
                        pltpu.SemaphoreType.DMA((3,))],
    )
    return pl.pallas_call(
        functools.partial(_dispatch_kernel, tm=tm),
        grid_spec=grid_spec,
        out_shape=[jax.ShapeDtypeStruct((P, D + LANES), F32), jax.ShapeDtypeStruct((nt, TOP_K, tm), jnp.int32)],
        compiler_params=_params(("arbitrary",)),
        name="moe_dispatch",
    )(plan["strip"], plan["off"], plan["n8"], plan["exact"], plan["tail"], plan["n_used"], x1, et, wt,
      plan["off_col"])


def _combine_kernel(strip_ref, off_ref, n8_ref, exact_ref, ys_ref, lp_ref, x_ref, g_ref, b_ref, *rest, tm, kd):
    if kd:
        wqk_ref, wvg_ref, wgr_ref, w2_ref, b2_ref, xo_ref, qkl_ref, vg_ref, land, sel_scr, sem = rest
    else:
        xo_ref, land, sel_scr, sem = rest
    i = pl.program_id(0)
    n_e = N_EXPERTS
    nrow = _stage_rows(tm)
    slot = i % 2

    def strip_copy(src0, dst0, rows, sl):
        return pltpu.make_async_copy(ys_ref.at[pl.ds(pl.multiple_of(src0, SUBLANES), rows)],
                                     land.at[sl, pl.ds(pl.multiple_of(dst0, SUBLANES), rows)], sem.at[sl])

    def fetch(tile, sl):
        def per_expert(e, carry):
            idx = tile * n_e + e
            src, dst = strip_ref[idx], off_ref[idx]
            _for_each_piece(n8_ref[idx], lambda o, rows: strip_copy(src + o, dst + o, rows, sl).start())
            return carry

        lax.fori_loop(0, n_e, per_expert, 0)

    @pl.when(i == 0)
    def _():
        land[...] = jnp.zeros_like(land)
        fetch(0, 0)

    @pl.when(i + 1 < pl.num_programs(0))
    def _():
        fetch(i + 1, 1 - slot)

    lp = lp_ref[0]
    for rc in range(nrow // LANES):
        rio = lax.broadcasted_iota(jnp.int32, (LANES, tm), 0) + rc * LANES
        piece = jnp.zeros((LANES, tm), F32)
        for k in range(TOP_K):
            piece = jnp.where(rio == lp[k:k + 1, :], 1.0, piece)
        sel_scr[rc * LANES:(rc + 1) * LANES, :] = piece.astype(BF16)

    for s, rows in enumerate(MOE_COPY_SIZES):
        def one(c, carry, rows=rows):
            strip_copy(0, 0, rows, slot).wait()
            return carry

        lax.fori_loop(0, exact_ref[i * len(MOE_COPY_SIZES) + s], one, 0)

    v = land[slot, 0:nrow, :]
    hi = v.astype(BF16)
    lo = (v - hi.astype(F32)).astype(BF16)
    sel = sel_scr[...]
    rows_in = (((0,), (0,)), ((), ()))
    ffn = (lax.dot_general(sel, hi, rows_in, preferred_element_type=F32)
           + lax.dot_general(sel, lo, rows_in, preferred_element_type=F32))
    xn = _layer_norm(DN_ALPHA * x_ref[...] + ffn, g_ref[...], b_ref[...])
    xo_ref[...] = xn
    if kd:
        xb = xn.astype(BF16)
        qkl_ref[:, :2 * kd] = jnp.dot(xb, wqk_ref[...], preferred_element_type=F32)
        vg_ref[...] = jnp.dot(xb, wvg_ref[...], preferred_element_type=F32)
        gr = jnp.dot(xb, wgr_ref[...], preferred_element_type=F32)
        z = jnp.dot(gr, w2_ref[...], precision=HIGHEST, preferred_element_type=F32) + b2_ref[...]
        log_sig = jnp.minimum(z, 0.0) - jnp.log1p(jnp.exp(-jnp.abs(z)))
        qkl_ref[:, 2 * kd:] = log_sig / GLA_GATE_TAU


def _combine_ln(ys, lp, x, plan, g, b, gla=None):
    T, D = x.shape
    tm = MOE_TILE
    nt = T // tm
    const = lambda i, *_: (0, 0)
    row = lambda i, *_: (i, 0)
    in_specs = [pl.BlockSpec(memory_space=pl.ANY), pl.BlockSpec((1, TOP_K, tm), lambda i, *_: (i, 0, 0)),
                pl.BlockSpec((tm, D), row),
                pl.BlockSpec((1, D), const), pl.BlockSpec((1, D), const)]
    out_specs = [pl.BlockSpec((tm, D), row)]
    out_shape = [jax.ShapeDtypeStruct((T, D), F32)]
    args = [ys, lp, x, g.reshape(1, D), b.reshape(1, D)]
    kd = 0
    if gla is not None:
        w_in1, w_gate2, b_gate, kd, vd = gla
        ws = [w_in1[:, :2 * kd].astype(BF16), w_in1[:, 2 * kd:2 * kd + 2 * vd].astype(BF16),
              w_in1[:, 2 * kd + 2 * vd:].astype(BF16), w_gate2.astype(F32), b_gate.reshape(1, kd)]
        in_specs += [pl.BlockSpec(w.shape, const) for w in ws]
        args += ws
        out_specs += [pl.BlockSpec((tm, 3 * kd), row), pl.BlockSpec((tm, 2 * vd), row)]
        out_shape += [jax.ShapeDtypeStruct((T, 3 * kd), F32), jax.ShapeDtypeStruct((T, 2 * vd), F32)]
    grid_spec = pltpu.PrefetchScalarGridSpec(
        num_scalar_prefetch=4,
        grid=(nt,),
        in_specs=in_specs,
        out_specs=out_specs,
        scratch_shapes=[pltpu.VMEM((2, _stage_rows(tm), D), F32), pltpu.VMEM((_stage_rows(tm), tm), BF16),
                        pltpu.SemaphoreType.DMA((2,))],
    )
    return pl.pallas_call(
        functools.partial(_combine_kernel, tm=tm, kd=kd),
        grid_spec=grid_spec,
        out_shape=out_shape,
        compiler_params=_params(("arbitrary",)),
        name="moe_combine_ln",
    )(plan["strip"], plan["off"], plan["n8"], plan["exact"], *args)


def _moe_ln(x1, et, wt, cnt, layer, w_gate_up, b_gate_up, w_down, b_down, g, b, gla=None):
    T = x1.shape[0]
    plan = _moe_plan(cnt, T)
    xs, lp = _dispatch(x1, et, wt, plan)
    wgu = _deinterleave_gate_up(w_gate_up, layer)
    bgu = jnp.concatenate([b_gate_up[layer, :, 0::2], b_gate_up[layer, :, 1::2]], axis=-1)[:, None, :].astype(F32)
    ys = _expert_ffn(xs, plan, wgu, w_down, layer, bgu, b_down[layer, :, None, :].astype(F32))
    return _combine_ln(ys, lp, x1, plan, g, b, gla)


def _gla_kernel(q_ref, k_ref, la_ref, v_ref, g_ref, nw_ref, o_ref, s_scr, *, tl, nh, dk, dv):
    C = GLA_CHUNK

    @pl.when(pl.program_id(1) == 0)
    def _():
        s_scr[...] = jnp.zeros_like(s_scr)

    ri = lax.broadcasted_iota(jnp.int32, (C, C), 0)
    ci = lax.broadcasted_iota(jnp.int32, (C, C), 1)
    lower = ri >= ci
    tri = jnp.where(lower, 1.0, 0.0).astype(F32)
    states = [s_scr[h] for h in range(nh)]
    for c in range(tl // C):
        rows = slice(c * C, (c + 1) * C)
        bcum = jnp.dot(tri, la_ref[0, rows, :], precision=HIGHEST, preferred_element_type=F32)
        btot = bcum[C - 1:C, :]
        qb = (q_ref[0, rows, :] * (dk ** -0.5) * jnp.exp(bcum)).astype(BF16)
        kc = k_ref[0, rows, :]
        k_in = (kc * jnp.exp(-bcum)).astype(BF16)
        k_st = kc * jnp.exp(btot - bcum)
        dec = jnp.exp(btot)
        for h in range(nh):
            ks, vs = slice(h * dk, (h + 1) * dk), slice(h * dv, (h + 1) * dv)
            vb = v_ref[0, rows, vs].astype(BF16)
            att = lax.dot_general(qb[:, ks], k_in[:, ks], (((1,), (1,)), ((), ())), preferred_element_type=F32)
            att = jnp.where(lower, att, 0.0)
            o = (jnp.dot(att.astype(BF16), vb, preferred_element_type=F32)
                 + jnp.dot(qb[:, ks], states[h].astype(BF16), preferred_element_type=F32))
            kv = jnp.dot(k_st[:, ks].T.astype(BF16), vb, preferred_element_type=F32)
            decay = jnp.broadcast_to(dec[:, ks], (SUBLANES, dk)).T[:, 0:1]
            states[h] = decay * states[h] + kv
            o = o * lax.rsqrt(jnp.mean(o * o, axis=-1, keepdims=True) + RMS_EPS) * nw_ref[...]
            gg = g_ref[0, rows, vs]
            o_ref[0, rows, vs] = o * (gg * jax.nn.sigmoid(gg))
    for h in range(nh):
        s_scr[h] = states[h]


def _gla(qkl, vg, norm_w, B, L, tl=256):
    T, kd3 = qkl.shape
    kd = kd3 // 3
    vd = vg.shape[1] // 2
    H = GLA_HEADS
    dk, dv = kd // H, vd // H
    qkl3 = qkl.reshape(B, L, kd3)
    vg3 = vg.reshape(B, L, 2 * vd)
    return pl.pallas_call(
        functools.partial(_gla_kernel, tl=tl, nh=H, dk=dk, dv=dv),
        grid=(B, L // tl),
        in_specs=[pl.BlockSpec((1, tl, kd), lambda b, l: (b, l, 0)),
                  pl.BlockSpec((1, tl, kd), lambda b, l: (b, l, 1)),
                  pl.BlockSpec((1, tl, kd), lambda b, l: (b, l, 2)),
                  pl.BlockSpec((1, tl, vd), lambda b, l: (b, l, 0)),
                  pl.BlockSpec((1, tl, vd), lambda b, l: (b, l, 1)),
                  pl.BlockSpec((1, dv), lambda b, l: (0, 0))],
        out_specs=pl.BlockSpec((1, tl, vd), lambda b, l: (b, l, 0)),
        out_shape=jax.ShapeDtypeStruct((B, L, vd), F32),
        scratch_shapes=[pltpu.VMEM((H, dk, dv), F32)],
        compiler_params=_params(("parallel", "arbitrary")),
        name="gla_mixer",
    )(qkl3, qkl3, qkl3, vg3, vg3, norm_w.reshape(1, dv).astype(F32))


def kernel(x, w_in0, s5_lam_re, s5_lam_im, s5_log_dt, s5_b_re, s5_b_im, s5_c_re, s5_c_im, s5_d, s5_w_glu,
           s5_b_glu, rel_bias, w_out0, w_in1, gla_w_gate2, gla_b_gate, gla_norm_w, w_out1, ln_mix_g, ln_mix_b,
           ln_ffn_g, ln_ffn_b, router_w, router_b, exp_w_gate_up, exp_b_gate_up, exp_w_down, exp_b_down):
    B, L, D = x.shape
    T = B * L
    s5w = s5_w_glu.shape[-1]
    kd = gla_w_gate2.shape[-1]
    vd = gla_norm_w.shape[-1] * GLA_HEADS

    u_tm, k_att, qt_att, vt_att = _inproj0(x, w_in0[0], s5w)
    y_a = _s5_mixer(u_tm.reshape(L, B, s5w), s5_lam_re[0], s5_lam_im[0], s5_log_dt[0], s5_b_re[0], s5_b_im[0],
                    s5_c_re[0], s5_c_im[0], s5_d[0].reshape(-1), s5_w_glu[0], s5_b_glu[0])
    y_b = _moba(qt_att, k_att, vt_att, rel_bias)
    tl = MIX_TILE
    w0 = w_out0[0].astype(BF16)
    ys = [(y_a.reshape(L, B * s5w), pl.BlockSpec((tl, s5w), lambda bb, l: (l, bb))),
          (y_b, pl.BlockSpec((1, tl, D - s5w), lambda bb, l: (bb, l, 0)))]
    x1, et, wt, cnt = _mix_ln(x, ys, [w0[:s5w], w0[s5w:]], ln_mix_g[0], ln_mix_b[0], router_w[0], router_b[0])
    x2, qkl, vg = _moe_ln(x1.reshape(T, D), et, wt, cnt, 0, exp_w_gate_up, exp_b_gate_up, exp_w_down,
                          exp_b_down, ln_ffn_g[0], ln_ffn_b[0],
                          gla=(w_in1[0], gla_w_gate2[0], gla_b_gate[0], kd, vd))

    y_c = _gla(qkl, vg, gla_norm_w[0], B, L)
    ys = [(y_c, pl.BlockSpec((1, tl, vd), lambda bb, l: (bb, l, 0)))]
    x3, et, wt, cnt = _mix_ln(x2.reshape(B, L, D), ys, [w_out1[0].astype(BF16)], ln_mix_g[1], ln_mix_b[1],
                              router_w[1], router_b[1])
    (out,) = _moe_ln(x3.reshape(T, D), et, wt, cnt, 1, exp_w_gate_up, exp_b_gate_up, exp_w_down,
                     exp_b_down, ln_ffn_g[1], ln_ffn_b[1])
    return out.reshape(B, L, D)
```

```python
import functools
import math

import jax
import jax.numpy as jnp
import numpy as np
from jax import lax
from jax.experimental import pallas as pl
from jax.experimental.pallas import tpu as pltpu

F32 = jnp.float32
BF16 = jnp.bfloat16
HIGHEST = lax.Precision.HIGHEST

DEPTH = 2
S5_GROUP = 16
S5_STATE = 64
MOBA_HEAD_DIM = 64
MOBA_BLOCK = 256
MOBA_TOPK = 3
REL_BUCKETS = 32
REL_MAX_DIST = 2048
GLA_HEADS = 4
GLA_GATE_TAU = 16.0
GLA_CHUNK = 64
N_EXPERTS = 32
TOP_K = 4
SWIGLU_LIMIT = 7.0
SWIGLU_ALPHA = 1.702
MOE_BLOCK = 512
MOE_TILE = 256
MIX_TILE = 512
MOE_CHUNK = 32
SUBLANES = 8
MOE_REMAINDERS = (16, 8)
DN_ALPHA = (2 * DEPTH) ** 0.25
LN_EPS = 1e-5
RMS_EPS = 1e-5

V7X_VMEM_LIMIT_BYTES = 56 * 1024 * 1024
LANES = 128

S5_GROUPS_PER_CHUNK = LANES // S5_GROUP
S5_CHUNK_STATES = S5_GROUPS_PER_CHUNK * S5_STATE


def _params(sem):
    return pltpu.CompilerParams(dimension_semantics=sem, vmem_limit_bytes=V7X_VMEM_LIMIT_BYTES)


def _layer_norm(r, g, b):
    mu = jnp.mean(r, axis=-1, keepdims=True)
    c = r - mu
    var = jnp.mean(c * c, axis=-1, keepdims=True)
    return c * lax.rsqrt(var + LN_EPS) * g + b


def _inproj0_kernel(x_ref, wuk_ref, wqvt_ref, u_ref, k_ref, qt_ref, vt_ref, *, s5w):
    xb = x_ref[0].astype(BF16)
    h = jnp.dot(xb, wuk_ref[...], preferred_element_type=F32)
    u_ref[...] = h[:, :s5w]
    k_ref[0] = h[:, s5w:]
    ht = lax.dot_general(wqvt_ref[...], xb, (((1,), (1,)), ((), ())), preferred_element_type=F32)
    aw = ht.shape[0] // 2
    qt_ref[0] = ht[:aw]
    vt_ref[0] = ht[aw:].astype(BF16)


def _inproj0(x, w_in0, s5w, tl=512):
    B, L, D = x.shape
    aw = (w_in0.shape[1] - s5w) // 3
    wb = w_in0.astype(BF16)
    wuk = jnp.concatenate([wb[:, :s5w], wb[:, s5w + aw:s5w + 2 * aw]], axis=1)
    wqvt = jnp.concatenate([wb[:, s5w:s5w + aw], wb[:, s5w + 2 * aw:]], axis=1).T
    return pl.pallas_call(
        functools.partial(_inproj0_kernel, s5w=s5w),
        grid=(B, L // tl),
        in_specs=[pl.BlockSpec((1, tl, D), lambda b, l: (b, l, 0)),
                  pl.BlockSpec(wuk.shape, lambda b, l: (0, 0)),
                  pl.BlockSpec(wqvt.shape, lambda b, l: (0, 0))],
        out_specs=[pl.BlockSpec((tl, s5w), lambda b, l: (l, b)),
                   pl.BlockSpec((1, tl, aw), lambda b, l: (b, l, 0)),
                   pl.BlockSpec((1, aw, tl), lambda b, l: (b, 0, l)),
                   pl.BlockSpec((1, aw, tl), lambda b, l: (b, 0, l))],
        out_shape=[jax.ShapeDtypeStruct((L, B * s5w), F32),
                   jax.ShapeDtypeStruct((B, L, aw), F32),
                   jax.ShapeDtypeStruct((B, aw, L), F32),
                   jax.ShapeDtypeStruct((B, aw, L), BF16)],
        compiler_params=_params(("parallel", "parallel")),
        name="inproj0",
    )(x, wuk, wqvt)


def _s5_discretize(lam_re, lam_im, log_dt, b_re, b_im):
    dt = jnp.exp(log_dt.astype(F32))[:, None]
    lr, li = lam_re.astype(F32), lam_im.astype(F32)
    mag = jnp.exp(lr * dt)
    ab_re, ab_im = mag * jnp.cos(li * dt), mag * jnp.sin(li * dt)
    er, ei = ab_re - 1.0, ab_im
    den = lr * lr + li * li
    q_re = (er * lr + ei * li) / den
    q_im = (ei * lr - er * li) / den
    br_, bi_ = b_re.astype(F32), b_im.astype(F32)
    bb_re = q_re[..., None] * br_ - q_im[..., None] * bi_
    bb_im = q_re[..., None] * bi_ + q_im[..., None] * br_
    return ab_re, ab_im, bb_re, bb_im


def _s5_kernel(u_ref, bm_ref, cm_ref, are_ref, aim_ref, d_ref, wg_ref, bg_ref, y_ref,
               s_scr, st_scr, z_scr, *, tl, nb, nchunk):
    ns = S5_CHUNK_STATES

    @pl.when(pl.program_id(0) == 0)
    def _():
        st_scr[...] = jnp.zeros_like(st_scr)

    u = u_ref[...].reshape(tl * nb, nchunk * LANES)
    for j in range(nchunk):
        uj = u[:, j * LANES:(j + 1) * LANES]
        s_scr[...] = jnp.dot(uj.astype(BF16), bm_ref[j], preferred_element_type=F32)
        ar = jnp.broadcast_to(are_ref[j], (nb, ns))
        ai = jnp.broadcast_to(aim_ref[j], (nb, ns))

        def step(t, carry, ar=ar, ai=ai):
            sre, sim = carry
            r0 = pl.multiple_of(t * nb, nb)
            nre = ar * sre - ai * sim + s_scr[pl.ds(r0, nb), 0:ns]
            nim = ar * sim + ai * sre + s_scr[pl.ds(r0, nb), ns:2 * ns]
            s_scr[pl.ds(r0, nb), 0:ns] = nre
            s_scr[pl.ds(r0, nb), ns:2 * ns] = nim
            return nre, nim

        sre, sim = lax.fori_loop(0, tl, step, (st_scr[j, :, 0:ns], st_scr[j, :, ns:2 * ns]), unroll=2)
        st_scr[j, :, 0:ns] = sre
        st_scr[j, :, ns:2 * ns] = sim
        yj = jnp.dot(s_scr[...].astype(BF16), cm_ref[j], preferred_element_type=F32)
        z_scr[:, j * LANES:(j + 1) * LANES] = yj + uj * d_ref[:, j * LANES:(j + 1) * LANES]
    z = jax.nn.gelu(z_scr[...])
    gate = jax.nn.sigmoid(jnp.dot(z.astype(BF16), wg_ref[...], preferred_element_type=F32) + bg_ref[...])
    y_ref[...] = (z * gate).reshape(tl, nb, nchunk * LANES)


def _s5_mixer(u_tm, lam_re, lam_im, log_dt, b_re, b_im, c_re, c_im, d_skip, w_glu, b_glu, tl=32):
    L, B, W = u_tm.shape
    G, P, H = lam_re.shape[0], S5_STATE, S5_GROUP
    gc = S5_GROUPS_PER_CHUNK
    nchunk = G // gc
    ab_re, ab_im, bb_re, bb_im = _s5_discretize(lam_re, lam_im, log_dt, b_re, b_im)
    eye = jnp.eye(gc, dtype=F32)

    def b_blocks(bb):
        return jnp.einsum('jgph,gk->jghkp', bb.reshape(nchunk, gc, P, H), eye).reshape(nchunk, gc * H, gc * P)

    def c_blocks(cc):
        return jnp.einsum('jghp,gk->jgpkh', cc.reshape(nchunk, gc, H, P), eye).reshape(nchunk, gc * P, gc * H)

    bm = jnp.concatenate([b_blocks(bb_re), b_blocks(bb_im)], axis=2).astype(BF16)
    cm = jnp.concatenate([c_blocks(c_re.astype(F32)), -c_blocks(c_im.astype(F32))], axis=1).astype(BF16)
    are = ab_re.reshape(nchunk, 1, gc * P)
    aim = ab_im.reshape(nchunk, 1, gc * P)
    m = tl * B
    const3 = lambda l: (0, 0, 0)
    const2 = lambda l: (0, 0)
    return pl.pallas_call(
        functools.partial(_s5_kernel, tl=tl, nb=B, nchunk=nchunk),
        grid=(L // tl,),
        in_specs=[pl.BlockSpec((tl, B, W), lambda l: (l, 0, 0)),
                  pl.BlockSpec(bm.shape, const3), pl.BlockSpec(cm.shape, const3),
                  pl.BlockSpec(are.shape, const3), pl.BlockSpec(aim.shape, const3),
                  pl.BlockSpec((1, W), const2), pl.BlockSpec((W, W), const2), pl.BlockSpec((1, W), const2)],
        out_specs=pl.BlockSpec((tl, B, W), lambda l: (l, 0, 0)),
        out_shape=jax.ShapeDtypeStruct((L, B, W), F32),
        scratch_shapes=[pltpu.VMEM((m, 2 * S5_CHUNK_STATES), F32),
                        pltpu.VMEM((nchunk, B, 2 * S5_CHUNK_STATES), F32),
                        pltpu.VMEM((m, W), F32)],
        compiler_params=_params(("arbitrary",)),
        name="s5_mixer",
    )(u_tm, bm, cm, are, aim, d_skip.reshape(1, W).astype(F32), w_glu.astype(BF16),
      b_glu.reshape(1, W).astype(F32))


def _rel_bucket(n):
    n = jnp.maximum(n, 0)
    max_exact = REL_BUCKETS // 2
    nf = jnp.maximum(n, 1).astype(F32)
    large = max_exact + (jnp.log(nf / max_exact) / math.log(REL_MAX_DIST / max_exact)
                         * (REL_BUCKETS - max_exact)).astype(jnp.int32)
    large = jnp.minimum(large, REL_BUCKETS - 1)
    return jnp.where(n < max_exact, n, large)


MOBA_ONES_ROWS = 16
MOBA_NEG = -1e30
LOG2E = math.log2(math.e)


def _moba_kernel(qt_ref, k_ref, vt_ref, wv_ref, hot_ref, o_ref, bias_scr, kk_scr, va_scr, km_scr, qa_scr,
                 *, nblk, bs, dh, topk):
    b, own = pl.program_id(1), pl.program_id(2)
    i = own
    tq = bs
    hpb = LANES // dh

    @pl.when((b == 0) & (i == 0))
    def _():
        r_io = lax.broadcasted_iota(jnp.int32, (bs, 2 * bs), 0)
        c_io = lax.broadcasted_iota(jnp.int32, (bs, 2 * bs), 1)
        for hh in range(hpb):
            for dd in range(nblk):
                t = pltpu.roll(jnp.broadcast_to(wv_ref[hh, dd:dd + 1, :], (bs, 2 * bs)), bs + 1, 1,
                               stride=1, stride_axis=0)
                if dd == 0:
                    t = jnp.where(c_io >= r_io, t, MOBA_NEG)
                bias_scr[hh, dd] = t[:, 0:bs]

    @pl.when(i == 0)
    def _():
        kf = k_ref[0]
        km_scr[...] = jnp.mean(kf.reshape(nblk, bs, LANES), axis=1)
        kk_scr[:, 0:LANES] = kf.astype(BF16)
        kk_scr[:, LANES:] = hot_ref[...]
        for n in range(nblk):
            va_scr[n, 0:LANES, :] = vt_ref[0, :, n * bs:(n + 1) * bs]
            va_scr[n, LANES:, :] = jnp.ones((MOBA_ONES_ROWS, bs), BF16)
        seq = nblk * bs
        q2 = qt_ref[0] * (dh ** -0.5 * LOG2E)
        f_io = lax.broadcasted_iota(jnp.int32, (LANES, seq), 0)
        blk_io = lax.broadcasted_iota(jnp.int32, (nblk, seq), 0)
        own_q = lax.broadcasted_iota(jnp.int32, (nblk, seq), 1) // bs
        qms, pens = [], []
        for hh in range(hpb):
            qm = jnp.where((f_io >= hh * dh) & (f_io < (hh + 1) * dh), q2, 0.0)
            gate = jnp.dot(km_scr[...], qm, precision=HIGHEST, preferred_element_type=F32)
            cnt = jnp.zeros((nblk, seq), jnp.int32)
            for m in range(nblk):
                gm = gate[m:m + 1, :]
                beats = (gm > gate) | ((gm == gate) & (m < blk_io))
                cnt = cnt + jnp.where(beats & (m < own_q), 1, 0)
            keep = (blk_io >= own_q) | (cnt < topk)
            pens.append(jnp.where(keep, 0.0, MOBA_NEG))
            qms.append(qm.astype(BF16))
        for n in range(nblk):
            cols = slice(n * bs, (n + 1) * bs)
            qa_scr[n, 0:LANES, :] = jnp.concatenate([qm[:, cols] for qm in qms], axis=1)
            pen = jnp.concatenate([p[:, cols] for p in pens], axis=1)
            qa_scr[n, LANES:, :] = jnp.concatenate(
                [pen, jnp.zeros((LANES - nblk, hpb * bs), F32)], axis=0).astype(BF16)

    qa = qa_scr[own]

    def scores(j):
        r0 = pl.multiple_of(j * bs, bs)
        s = jnp.dot(kk_scr[pl.ds(r0, bs), :], qa, preferred_element_type=F32)
        return s + jnp.concatenate([bias_scr[hh, own - j] for hh in range(hpb)], axis=1)

    def pair(ja, jb, pen_b, m_i, acc):
        sa = scores(ja)
        sb = scores(jb) + pen_b
        m_n = jnp.maximum(m_i, jnp.maximum(jnp.max(sa, axis=0, keepdims=True), jnp.max(sb, axis=0, keepdims=True)))
        alpha = jnp.exp2(m_i - m_n)
        pa = jnp.exp2(sa - m_n).astype(BF16)
        pb = jnp.exp2(sb - m_n).astype(BF16)
        return m_n, (alpha * acc + jnp.dot(va_scr[ja], pa, preferred_element_type=F32)
                     + jnp.dot(va_scr[jb], pb, preferred_element_type=F32))

    odd = own % 2
    m1, acc1 = pair(own, jnp.maximum(own - 1, 0), jnp.where(odd == 1, 0.0, MOBA_NEG),
                    jnp.full((1, hpb * tq), MOBA_NEG, F32), jnp.zeros((LANES + MOBA_ONES_ROWS, hpb * tq), F32))
    _, acc_f = lax.fori_loop(0, own // 2, lambda jj, c: pair(2 * jj, 2 * jj + 1, 0.0, *c), (m1, acc1))
    on = acc_f[0:LANES, :] / acc_f[LANES:LANES + 1, :]
    ot = jnp.concatenate([on[hh * dh:(hh + 1) * dh, hh * tq:(hh + 1) * tq] for hh in range(hpb)], axis=0)
    o_ref[0] = ot.T


def _moba(qt, k, vt, rel_bias):
    B, L, W = k.shape
    dh, bs = MOBA_HEAD_DIM, MOBA_BLOCK
    H = W // dh
    nblk = L // bs
    hpb = LANES // dh
    dist = jnp.arange(L, dtype=jnp.int32)
    by_dist = rel_bias.astype(F32).T[:, _rel_bucket(dist)] * LOG2E
    idx = np.clip(np.arange(nblk)[:, None] * bs - (bs - 1) + np.arange(2 * bs)[None, :], 0, L - 1)
    vecs = by_dist[:, idx]
    hot = np.zeros((L, LANES), np.float32)
    hot[np.arange(L), np.arange(L) // bs] = 1.0
    return pl.pallas_call(
        functools.partial(_moba_kernel, nblk=nblk, bs=bs, dh=dh, topk=min(MOBA_TOPK, nblk)),
        grid=(H // hpb, B, nblk),
        in_specs=[pl.BlockSpec((1, LANES, L), lambda h, b, i: (b, h, 0)),
                  pl.BlockSpec((1, L, LANES), lambda h, b, i: (b, 0, h)),
                  pl.BlockSpec((1, LANES, L), lambda h, b, i: (b, h, 0)),
                  pl.BlockSpec((hpb, nblk, 2 * bs), lambda h, b, i: (h, 0, 0)),
                  pl.BlockSpec((L, LANES), lambda h, b, i: (0, 0))],
        out_specs=pl.BlockSpec((1, bs, LANES), lambda h, b, i: (b, i, h)),
        out_shape=jax.ShapeDtypeStruct((B, L, W), F32),
        scratch_shapes=[pltpu.VMEM((hpb, nblk, bs, bs), F32),
                        pltpu.VMEM((L, 2 * LANES), BF16),
                        pltpu.VMEM((nblk, LANES + MOBA_ONES_ROWS, bs), BF16),
                        pltpu.VMEM((nblk, LANES), F32),
                        pltpu.VMEM((nblk, 2 * LANES, hpb * bs), BF16)],
        compiler_params=_params(("arbitrary", "arbitrary", "arbitrary")),
        name="moba_attention",
    )(qt, k, vt, vecs, jnp.asarray(hot, BF16))


def _route_top_k(xn, wrt_ref, br_ref):
    lt = lax.dot_general(wrt_ref[...], xn, (((1,), (1,)), ((), ())), precision=HIGHEST,
                         preferred_element_type=F32) + br_ref[...]
    n_e = lt.shape[0]
    eid = lax.broadcasted_iota(jnp.int32, lt.shape, 0)
    cur = lt
    vals, idxs = [], []
    for _ in range(TOP_K):
        m = jnp.max(cur, axis=0, keepdims=True)
        idx = jnp.min(jnp.where(cur == m, eid, n_e), axis=0, keepdims=True)
        vals.append(m)
        idxs.append(idx)
        cur = jnp.where(eid == idx, -jnp.inf, cur)
    ex = [jnp.exp(v - vals[0]) for v in vals]
    den = ex[0]
    for t in ex[1:]:
        den = den + t
    hot = jnp.zeros(lt.shape, F32)
    for idx in idxs:
        hot = hot + jnp.where(eid == idx, 1.0, 0.0)
    return (jnp.concatenate(idxs, axis=0), jnp.concatenate([t / den for t in ex], axis=0),
            jnp.sum(hot, axis=1, keepdims=True))


def _mix_ln_kernel(*refs, n_in):
    x_ref = refs[0]
    y_refs = refs[1:1 + n_in]
    w_refs = refs[1 + n_in:1 + 2 * n_in]
    g_ref, b_ref, wrt_ref, br_ref, x1_ref, et_ref, wt_ref, cnt_ref = refs[1 + 2 * n_in:]
    mix = None
    for y_ref, w_ref in zip(y_refs, w_refs):
        y = y_ref[...]
        y = y.reshape(y.shape[-2], y.shape[-1]).astype(BF16)
        t = jnp.dot(y, w_ref[...], preferred_element_type=F32)
        mix = t if mix is None else mix + t
    xn = _layer_norm(DN_ALPHA * x_ref[0] + mix, g_ref[...], b_ref[...])
    x1_ref[0] = xn
    for s in range(et_ref.shape[0]):
        et, wt, cnt = _route_top_k(xn[s * MOE_TILE:(s + 1) * MOE_TILE], wrt_ref, br_ref)
        et_ref[s] = et
        wt_ref[s] = wt
        cnt_ref[s] = cnt


def _mix_ln(x, ys, ws, g, b, wr, br):
    B, L, D = x.shape
    E = wr.shape[1]
    tl = MIX_TILE
    nl = L // tl
    sub = tl // MOE_TILE
    nt = B * L // MOE_TILE
    const = lambda bb, l: (0, 0)
    tile = lambda bb, l: (bb * nl + l, 0, 0)
    in_specs = [pl.BlockSpec((1, tl, D), lambda bb, l: (bb, l, 0))]
    in_specs += [spec for _, spec in ys]
    in_specs += [pl.BlockSpec(w.shape, const) for w in ws]
    in_specs += [pl.BlockSpec((1, D), const), pl.BlockSpec((1, D), const),
                 pl.BlockSpec((E, D), const), pl.BlockSpec((E, 1), const)]
    return pl.pallas_call(
        functools.partial(_mix_ln_kernel, n_in=len(ys)),
        grid=(B, nl),
        in_specs=in_specs,
        out_specs=[pl.BlockSpec((1, tl, D), lambda bb, l: (bb, l, 0)),
                   pl.BlockSpec((sub, TOP_K, MOE_TILE), tile), pl.BlockSpec((sub, TOP_K, MOE_TILE), tile),
                   pl.BlockSpec((sub, E, 1), tile)],
        out_shape=[jax.ShapeDtypeStruct((B, L, D), F32),
                   jax.ShapeDtypeStruct((nt, TOP_K, MOE_TILE), jnp.int32),
                   jax.ShapeDtypeStruct((nt, TOP_K, MOE_TILE), F32),
                   jax.ShapeDtypeStruct((nt, E, 1), F32)],
        compiler_params=_params(("parallel", "parallel")),
        name="outproj_ln_router",
    )(x, *[a for a, _ in ys], *ws, g.reshape(1, D), b.reshape(1, D), wr.astype(F32).T, br.reshape(E, 1))


def _deinterleave_kernel(w_ref, p_ref, o_ref, *, ff):
    w = w_ref[0, 0].astype(BF16)
    for c in range(2 * ff // (2 * LANES)):
        t = jnp.dot(w[:, 2 * LANES * c:2 * LANES * (c + 1)], p_ref[...], preferred_element_type=F32)
        o_ref[0, :, LANES * c:LANES * (c + 1)] = t[:, :LANES].astype(BF16)
        o_ref[0, :, ff + LANES * c:ff + LANES * (c + 1)] = t[:, LANES:].astype(BF16)


def _deinterleave_gate_up(w_gate_up, layer, tk=512):
    _, E, D, F2 = w_gate_up.shape
    src = np.arange(2 * LANES)
    dst = np.where(src % 2 == 0, src // 2, LANES + src // 2)
    perm = np.zeros((2 * LANES, 2 * LANES), np.float32)
    perm[src, dst] = 1.0
    return pl.pallas_call(
        functools.partial(_deinterleave_kernel, ff=F2 // 2),
        grid=(E, D // tk),
        in_specs=[pl.BlockSpec((1, 1, tk, F2), lambda e, k: (layer, e, k, 0)),
                  pl.BlockSpec((2 * LANES, 2 * LANES), lambda e, k: (0, 0))],
        out_specs=pl.BlockSpec((1, tk, F2), lambda e, k: (e, k, 0)),
        out_shape=jax.ShapeDtypeStruct((E, D, F2), BF16),
        compiler_params=_params(("parallel", "parallel")),
        name="deinterleave_gate_up",
    )(w_gate_up, jnp.asarray(perm, BF16))


def _expert_kernel(be_ref, nu_ref, xs_ref, wgu_ref, wd_ref, bgu_ref, bd_ref, ys_ref):
    blk = pl.program_id(0)
    ff = wd_ref.shape[2]
    d = wd_ref.shape[3]

    @pl.when(blk < nu_ref[0])
    def _():
        x = xs_ref[:, :d].astype(BF16)
        pw = xs_ref[:, d:d + 1]
        h = jnp.dot(x, wgu_ref[0], preferred_element_type=F32) + bgu_ref[0]
        g = h[:, :ff]
        u = h[:, ff:]
        g = jnp.minimum(g, SWIGLU_LIMIT)
        u = jnp.clip(u, -SWIGLU_LIMIT, SWIGLU_LIMIT)
        act = g * jax.nn.sigmoid(SWIGLU_ALPHA * g) * (u + 1.0)
        y = jnp.dot(act.astype(BF16), wd_ref[0, 0].astype(BF16), preferred_element_type=F32) + bd_ref[0]
        ys_ref[...] = y * pw

    @pl.when(blk >= nu_ref[0])
    def _():
        ys_ref[...] = jnp.zeros_like(ys_ref)


def _expert_ffn(xs, plan, wgu, w_down, layer, bgu, bd):
    P, DW = xs.shape
    F, D = w_down.shape[2], w_down.shape[3]
    bm = MOE_BLOCK
    wmap = lambda i, be, nu: (be[i], 0, 0)
    rmap = lambda i, be, nu: (i, 0)
    grid_spec = pltpu.PrefetchScalarGridSpec(
        num_scalar_prefetch=2,
        grid=(P // bm,),
        in_specs=[pl.BlockSpec((bm, DW), rmap),
                  pl.BlockSpec((1, D, 2 * F), wmap),
                  pl.BlockSpec((1, 1, F, D), lambda i, be, nu: (layer, be[i], 0, 0)),
                  pl.BlockSpec((1, 1, 2 * F), wmap), pl.BlockSpec((1, 1, D), wmap)],
        out_specs=pl.BlockSpec((bm, D), rmap),
    )
    return pl.pallas_call(
        _expert_kernel,
        grid_spec=grid_spec,
        out_shape=jax.ShapeDtypeStruct((P, D), F32),
        compiler_params=_params(("arbitrary",)),
        name="moe_experts",
    )(plan["blk_expert"], plan["n_used"], xs, wgu, w_down, bgu, bd)


def _moe_rows(T, nt, E):
    bound = T * TOP_K + (SUBLANES - 1) * E * nt + E * MOE_BLOCK
    return -(-bound // MOE_BLOCK) * MOE_BLOCK


def _moe_plan(cnt, T):
    nt, E = cnt.shape[0], cnt.shape[1]
    c = cnt.reshape(nt, E).astype(jnp.int32)
    n8 = (c + SUBLANES - 1) // SUBLANES * SUBLANES
    tot = jnp.sum(n8, axis=0)
    seg = (tot + MOE_BLOCK - 1) // MOE_BLOCK * MOE_BLOCK
    seg_start = jnp.cumsum(seg) - seg
    strip = seg_start[None, :] + jnp.cumsum(n8, axis=0) - n8
    off = jnp.cumsum(n8, axis=1) - n8
    nblk = _moe_rows(T, nt, E) // MOE_BLOCK
    seg_blk = seg // MOE_BLOCK
    blk_end = jnp.cumsum(seg_blk)
    blk_ids = jnp.arange(nblk, dtype=jnp.int32)
    blk_expert = jnp.minimum(jnp.sum(blk_end[None, :] <= blk_ids[:, None], axis=1), E - 1).astype(jnp.int32)
    tail = jnp.stack([seg_start + tot, seg - tot], axis=1)
    exact = jnp.stack([jnp.sum(n8 // MOE_CHUNK, axis=1)]
                      + [jnp.sum((n8 // r) % 2, axis=1) for r in MOE_REMAINDERS], axis=1)
    return dict(strip=strip.reshape(-1).astype(jnp.int32), off=off.reshape(-1).astype(jnp.int32),
                n8=n8.reshape(-1).astype(jnp.int32), exact=exact.reshape(-1).astype(jnp.int32),
                off_col=off.reshape(nt, E, 1).astype(jnp.int32), blk_expert=blk_expert,
                tail=tail.reshape(-1).astype(jnp.int32),
                n_used=blk_end[-1:].astype(jnp.int32))


def _stage_rows(tm):
    return TOP_K * tm + N_EXPERTS * SUBLANES


MOE_COPY_SIZES = (MOE_CHUNK,) + MOE_REMAINDERS


def _for_each_piece(n, fn):
    nfull = n // MOE_CHUNK

    def per_chunk(c, carry):
        fn(c * MOE_CHUNK, MOE_CHUNK)
        return carry

    lax.fori_loop(0, nfull, per_chunk, 0)
    done = nfull * MOE_CHUNK
    for r in MOE_REMAINDERS:
        has = (n // r) % 2

        @pl.when(has == 1)
        def _(done=done, r=r):
            fn(done, r)

        done = done + has * r


def _dispatch_kernel(strip_ref, off_ref, n8_ref, exact_ref, tail_ref, nu_ref, x_ref, et_ref, wt_ref, offc_ref,
                     xs_ref, lp_ref, stage, sel_scr, zero, sem, *, tm):
    i = pl.program_id(0)
    last = pl.num_programs(0) - 1
    n_e = N_EXPERTS
    d = x_ref.shape[1]
    nrow = _stage_rows(tm)
    slot = i % 2

    def strip_copy(src0, dst0, rows, sl):
        return pltpu.make_async_copy(stage.at[sl, pl.ds(pl.multiple_of(src0, SUBLANES), rows)],
                                     xs_ref.at[pl.ds(pl.multiple_of(dst0, SUBLANES), rows)], sem.at[sl])

    def wait_tile(tile, sl):
        for s, rows in enumerate(MOE_COPY_SIZES):
            def one(c, carry, rows=rows):
                strip_copy(0, 0, rows, sl).wait()
                return carry

            lax.fori_loop(0, exact_ref[tile * len(MOE_COPY_SIZES) + s], one, 0)

    et = et_ref[0]
    wt = wt_ref[0]
    eid = lax.broadcasted_iota(jnp.int32, (n_e, tm), 0)
    hots = [eid == et[k:k + 1, :] for k in range(TOP_K)]
    m_t = jnp.zeros((n_e, tm), F32)
    for h in hots:
        m_t = m_t + jnp.where(h, 1.0, 0.0)
    before = jnp.where(lax.broadcasted_iota(jnp.int32, (tm, tm), 0) < lax.broadcasted_iota(jnp.int32, (tm, tm), 1),
                       1.0, 0.0).astype(BF16)
    rank = jnp.dot(m_t.astype(BF16), before, preferred_element_type=F32)
    base = offc_ref[0].astype(F32) + rank
    lps = [jnp.sum(jnp.where(h, base, 0.0), axis=0, keepdims=True) for h in hots]
    lpi = [lp.astype(jnp.int32) for lp in lps]
    lp_ref[0] = jnp.concatenate(lpi, axis=0)

    wcols = []
    for rc in range(nrow // LANES):
        rio = lax.broadcasted_iota(jnp.int32, (LANES, tm), 0) + rc * LANES
        sel = jnp.zeros((LANES, tm), F32)
        wsel = jnp.zeros((LANES, tm), F32)
        for k in reversed(range(TOP_K)):
            hit = rio == lpi[k]
            sel = jnp.where(hit, 1.0, sel)
            wsel = jnp.where(hit, wt[k:k + 1, :], wsel)
        sel_scr[rc * LANES:(rc + 1) * LANES, :] = sel.astype(BF16)
        wcols.append(jnp.sum(wsel, axis=1, keepdims=True))

    @pl.when(i >= 2)
    def _():
        wait_tile(i - 2, slot)

    stage[slot, :, 0:d] = jnp.dot(sel_scr[...], x_ref[...].astype(BF16), preferred_element_type=F32)
    for rc, wcol in enumerate(wcols):
        stage[slot, rc * LANES:(rc + 1) * LANES, d:d + LANES] = jnp.broadcast_to(wcol, (LANES, LANES))

    def per_expert(e, carry):
        idx = i * n_e + e
        src, dst = off_ref[idx], strip_ref[idx]
        _for_each_piece(n8_ref[idx], lambda o, rows: strip_copy(src + o, dst + o, rows, slot).start())
        return carry

    lax.fori_loop(0, n_e, per_expert, 0)

    def zero_fill(wait):
        def go(dst0, rows):
            cp = pltpu.make_async_copy(zero.at[pl.ds(0, rows)],
                                       xs_ref.at[pl.ds(pl.multiple_of(dst0, SUBLANES), rows)], sem.at[2])
            cp.wait() if wait else cp.start()

        def per_tail(e, carry):
            start = tail_ref[2 * e]
            _for_each_piece(tail_ref[2 * e + 1], lambda o, rows: go(start + o, rows))
            return carry

        lax.fori_loop(0, n_e, per_tail, 0)

        def per_block(blk, carry):
            go(blk * MOE_BLOCK, MOE_BLOCK)
            return carry

        lax.fori_loop(nu_ref[0], xs_ref.shape[0] // MOE_BLOCK, per_block, 0)

    @pl.when(i == last)
    def _():
        @pl.when(i >= 1)
        def _():
            wait_tile(i - 1, 1 - slot)

        wait_tile(i, slot)
        zero[...] = jnp.zeros_like(zero)
        zero_fill(wait=False)
        zero_fill(wait=True)


def _dispatch(x1, et, wt, plan):
    T, D = x1.shape
    nt = et.shape[0]
    tm = T // nt
    E = N_EXPERTS
    P = _moe_rows(T, nt, E)
    nrow = _stage_rows(tm)
    tile3 = lambda i, *_: (i, 0, 0)
    grid_spec = pltpu.PrefetchScalarGridSpec(
        num_scalar_prefetch=6,
        grid=(nt,),
        in_specs=[pl.BlockSpec((tm, D), lambda i, *_: (i, 0)),
                  pl.BlockSpec((1, TOP_K, tm), tile3), pl.BlockSpec((1, TOP_K, tm), tile3),
                  pl.BlockSpec((1, E, 1), tile3)],
        out_specs=[pl.BlockSpec(memory_space=pl.ANY), pl.BlockSpec((1, TOP_K, tm), tile3)],
        scratch_shapes=[pltpu.VMEM((2, nrow, D + LANES), F32),
                        pltpu.VMEM((nrow, tm), BF16),
                        pltpu.VMEM((MOE_BLOCK, D + LANES), F32),
                        pltpu.SemaphoreType.DMA((3,))],
    )
    return pl.pallas_call(
        functools.partial(_dispatch_kernel, tm=tm),
        grid_spec=grid_spec,
        out_shape=[jax.ShapeDtypeStruct((P, D + LANES), F32), jax.ShapeDtypeStruct((nt, TOP_K, tm), jnp.int32)],
        compiler_params=_params(("arbitrary",)),
        name="moe_dispatch",
    )(plan["strip"], plan["off"], plan["n8"], plan["exact"], plan["tail"], plan["n_used"], x1, et, wt,
      plan["off_col"])


def _combine_kernel(strip_ref, off_ref, n8_ref, exact_ref, ys_ref, lp_ref, x_ref, g_ref, b_ref, *rest, tm, kd):
    if kd:
        wqk_ref, wvg_ref, wgr_ref, w2_ref, b2_ref, xo_ref, qkl_ref, vg_ref, land, sel_scr, sem = rest
    else:
        xo_ref, land, sel_scr, sem = rest
    i = pl.program_id(0)
    n_e = N_EXPERTS
    nrow = _stage_rows(tm)
    slot = i % 2

    def strip_copy(src0, dst0, rows, sl):
        return pltpu.make_async_copy(ys_ref.at[pl.ds(pl.multiple_of(src0, SUBLANES), rows)],
                                     land.at[sl, pl.ds(pl.multiple_of(dst0, SUBLANES), rows)], sem.at[sl])

    def fetch(tile, sl):
        def per_expert(e, carry):
            idx = tile * n_e + e
            src, dst = strip_ref[idx], off_ref[idx]
            _for_each_piece(n8_ref[idx], lambda o, rows: strip_copy(src + o, dst + o, rows, sl).start())
            return carry

        lax.fori_loop(0, n_e, per_expert, 0)

    @pl.when(i == 0)
    def _():
        land[...] = jnp.zeros_like(land)
        fetch(0, 0)

    @pl.when(i + 1 < pl.num_programs(0))
    def _():
        fetch(i + 1, 1 - slot)

    lp = lp_ref[0]
    for rc in range(nrow // LANES):
        rio = lax.broadcasted_iota(jnp.int32, (LANES, tm), 0) + rc * LANES
        piece = jnp.zeros((LANES, tm), F32)
        for k in range(TOP_K):
            piece = jnp.where(rio == lp[k:k + 1, :], 1.0, piece)
        sel_scr[rc * LANES:(rc + 1) * LANES, :] = piece.astype(BF16)

    for s, rows in enumerate(MOE_COPY_SIZES):
        def one(c, carry, rows=rows):
            strip_copy(0, 0, rows, slot).wait()
            return carry

        lax.fori_loop(0, exact_ref[i * len(MOE_COPY_SIZES) + s], one, 0)

    ffn = lax.dot_general(sel_scr[...], land[slot, 0:nrow, :].astype(BF16), (((0,), (0,)), ((), ())),
                          preferred_element_type=F32)
    xn = _layer_norm(DN_ALPHA * x_ref[...] + ffn, g_ref[...], b_ref[...])
    xo_ref[...] = xn
    if kd:
        xb = xn.astype(BF16)
        qkl_ref[:, :2 * kd] = jnp.dot(xb, wqk_ref[...], preferred_element_type=F32)
        vg_ref[...] = jnp.dot(xb, wvg_ref[...], preferred_element_type=F32)
        gr = jnp.dot(xb, wgr_ref[...], preferred_element_type=F32)
        z = jnp.dot(gr, w2_ref[...], precision=HIGHEST, preferred_element_type=F32) + b2_ref[...]
        log_sig = jnp.minimum(z, 0.0) - jnp.log1p(jnp.exp(-jnp.abs(z)))
        qkl_ref[:, 2 * kd:] = log_sig / GLA_GATE_TAU


def _combine_ln(ys, lp, x, plan, g, b, gla=None):
    T, D = x.shape
    tm = MOE_TILE
    nt = T // tm
    const = lambda i, *_: (0, 0)
    row = lambda i, *_: (i, 0)
    in_specs = [pl.BlockSpec(memory_space=pl.ANY), pl.BlockSpec((1, TOP_K, tm), lambda i, *_: (i, 0, 0)),
                pl.BlockSpec((tm, D), row),
                pl.BlockSpec((1, D), const), pl.BlockSpec((1, D), const)]
    out_specs = [pl.BlockSpec((tm, D), row)]
    out_shape = [jax.ShapeDtypeStruct((T, D), F32)]
    args = [ys, lp, x, g.reshape(1, D), b.reshape(1, D)]
    kd = 0
    if gla is not None:
        w_in1, w_gate2, b_gate, kd, vd = gla
        ws = [w_in1[:, :2 * kd].astype(BF16), w_in1[:, 2 * kd:2 * kd + 2 * vd].astype(BF16),
              w_in1[:, 2 * kd + 2 * vd:].astype(BF16), w_gate2.astype(F32), b_gate.reshape(1, kd)]
        in_specs += [pl.BlockSpec(w.shape, const) for w in ws]
        args += ws
        out_specs += [pl.BlockSpec((tm, 3 * kd), row), pl.BlockSpec((tm, 2 * vd), row)]
        out_shape += [jax.ShapeDtypeStruct((T, 3 * kd), F32), jax.ShapeDtypeStruct((T, 2 * vd), F32)]
    grid_spec = pltpu.PrefetchScalarGridSpec(
        num_scalar_prefetch=4,
        grid=(nt,),
        in_specs=in_specs,
        out_specs=out_specs,
        scratch_shapes=[pltpu.VMEM((2, _stage_rows(tm), D), F32), pltpu.VMEM((_stage_rows(tm), tm), BF16),
                        pltpu.SemaphoreType.DMA((2,))],
    )
    return pl.pallas_call(
        functools.partial(_combine_kernel, tm=tm, kd=kd),
        grid_spec=grid_spec,
        out_shape=out_shape,
        compiler_params=_params(("arbitrary",)),
        name="moe_combine_ln",
    )(plan["strip"], plan["off"], plan["n8"], plan["exact"], *args)


def _moe_ln(x1, et, wt, cnt, layer, w_gate_up, b_gate_up, w_down, b_down, g, b, gla=None):
    T = x1.shape[0]
    plan = _moe_plan(cnt, T)
    xs, lp = _dispatch(x1, et, wt, plan)
    wgu = _deinterleave_gate_up(w_gate_up, layer)
    bgu = jnp.concatenate([b_gate_up[layer, :, 0::2], b_gate_up[layer, :, 1::2]], axis=-1)[:, None, :].astype(F32)
    ys = _expert_ffn(xs, plan, wgu, w_down, layer, bgu, b_down[layer, :, None, :].astype(F32))
    return _combine_ln(ys, lp, x1, plan, g, b, gla)


def _gla_kernel(q_ref, k_ref, la_ref, v_ref, g_ref, nw_ref, o_ref, s_scr, *, tl, nh, dk, dv):
    C = GLA_CHUNK

    @pl.when(pl.program_id(1) == 0)
    def _():
        s_scr[...] = jnp.zeros_like(s_scr)

    ri = lax.broadcasted_iota(jnp.int32, (C, C), 0)
    ci = lax.broadcasted_iota(jnp.int32, (C, C), 1)
    lower = ri >= ci
    tri = jnp.where(lower, 1.0, 0.0).astype(F32)
    states = [s_scr[h] for h in range(nh)]
    for c in range(tl // C):
        rows = slice(c * C, (c + 1) * C)
        bcum = jnp.dot(tri, la_ref[0, rows, :], precision=HIGHEST, preferred_element_type=F32)
        btot = bcum[C - 1:C, :]
        qb = (q_ref[0, rows, :] * (dk ** -0.5) * jnp.exp(bcum)).astype(BF16)
        kc = k_ref[0, rows, :]
        k_in = (kc * jnp.exp(-bcum)).astype(BF16)
        k_st = kc * jnp.exp(btot - bcum)
        dec = jnp.exp(btot)
        for h in range(nh):
            ks, vs = slice(h * dk, (h + 1) * dk), slice(h * dv, (h + 1) * dv)
            vb = v_ref[0, rows, vs].astype(BF16)
            att = lax.dot_general(qb[:, ks], k_in[:, ks], (((1,), (1,)), ((), ())), preferred_element_type=F32)
            att = jnp.where(lower, att, 0.0)
            o = (jnp.dot(att.astype(BF16), vb, preferred_element_type=F32)
                 + jnp.dot(qb[:, ks], states[h].astype(BF16), preferred_element_type=F32))
            kv = jnp.dot(k_st[:, ks].T.astype(BF16), vb, preferred_element_type=F32)
            decay = jnp.broadcast_to(dec[:, ks], (SUBLANES, dk)).T[:, 0:1]
            states[h] = decay * states[h] + kv
            o = o * lax.rsqrt(jnp.mean(o * o, axis=-1, keepdims=True) + RMS_EPS) * nw_ref[...]
            gg = g_ref[0, rows, vs]
            o_ref[0, rows, vs] = o * (gg * jax.nn.sigmoid(gg))
    for h in range(nh):
        s_scr[h] = states[h]


def _gla(qkl, vg, norm_w, B, L, tl=256):
    T, kd3 = qkl.shape
    kd = kd3 // 3
    vd = vg.shape[1] // 2
    H = GLA_HEADS
    dk, dv = kd // H, vd // H
    qkl3 = qkl.reshape(B, L, kd3)
    vg3 = vg.reshape(B, L, 2 * vd)
    return pl.pallas_call(
        functools.partial(_gla_kernel, tl=tl, nh=H, dk=dk, dv=dv),
        grid=(B, L // tl),
        in_specs=[pl.BlockSpec((1, tl, kd), lambda b, l: (b, l, 0)),
                  pl.BlockSpec((1, tl, kd), lambda b, l: (b, l, 1)),
                  pl.BlockSpec((1, tl, kd), lambda b, l: (b, l, 2)),
                  pl.BlockSpec((1, tl, vd), lambda b, l: (b, l, 0)),
                  pl.BlockSpec((1, tl, vd), lambda b, l: (b, l, 1)),
                  pl.BlockSpec((1, dv), lambda b, l: (0, 0))],
        out_specs=pl.BlockSpec((1, tl, vd), lambda b, l: (b, l, 0)),
        out_shape=jax.ShapeDtypeStruct((B, L, vd), F32),
        scratch_shapes=[pltpu.VMEM((H, dk, dv), F32)],
        compiler_params=_params(("parallel", "arbitrary")),
        name="gla_mixer",
    )(qkl3, qkl3, qkl3, vg3, vg3, norm_w.reshape(1, dv).astype(F32))


def kernel(x, w_in0, s5_lam_re, s5_lam_im, s5_log_dt, s5_b_re, s5_b_im, s5_c_re, s5_c_im, s5_d, s5_w_glu,
           s5_b_glu, rel_bias, w_out0, w_in1, gla_w_gate2, gla_b_gate, gla_norm_w, w_out1, ln_mix_g, ln_mix_b,
           ln_ffn_g, ln_ffn_b, router_w, router_b, exp_w_gate_up, exp_b_gate_up, exp_w_down, exp_b_down):
    B, L, D = x.shape
    T = B * L
    s5w = s5_w_glu.shape[-1]
    kd = gla_w_gate2.shape[-1]
    vd = gla_norm_w.shape[-1] * GLA_HEADS

    u_tm, k_att, qt_att, vt_att = _inproj0(x, w_in0[0], s5w)
    y_a = _s5_mixer(u_tm.reshape(L, B, s5w), s5_lam_re[0], s5_lam_im[0], s5_log_dt[0], s5_b_re[0], s5_b_im[0],
                    s5_c_re[0], s5_c_im[0], s5_d[0].reshape(-1), s5_w_glu[0], s5_b_glu[0])
    y_b = _moba(qt_att, k_att, vt_att, rel_bias)
    tl = MIX_TILE
    w0 = w_out0[0].astype(BF16)
    ys = [(y_a.reshape(L, B * s5w), pl.BlockSpec((tl, s5w), lambda bb, l: (l, bb))),
          (y_b, pl.BlockSpec((1, tl, D - s5w), lambda bb, l: (bb, l, 0)))]
    x1, et, wt, cnt = _mix_ln(x, ys, [w0[:s5w], w0[s5w:]], ln_mix_g[0], ln_mix_b[0], router_w[0], router_b[0])
    x2, qkl, vg = _moe_ln(x1.reshape(T, D), et, wt, cnt, 0, exp_w_gate_up, exp_b_gate_up, exp_w_down,
                          exp_b_down, ln_ffn_g[0], ln_ffn_b[0],
                          gla=(w_in1[0], gla_w_gate2[0], gla_b_gate[0], kd, vd))

    y_c = _gla(qkl, vg, gla_norm_w[0], B, L)
    ys = [(y_c, pl.BlockSpec((1, tl, vd), lambda bb, l: (bb, l, 0)))]
    x3, et, wt, cnt = _mix_ln(x2.reshape(B, L, D), ys, [w_out1[0].astype(BF16)], ln_mix_g[1], ln_mix_b[1],
                              router_w[1], router_b[1])
    (out,) = _moe_ln(x3.reshape(T, D), et, wt, cnt, 1, exp_w_gate_up, exp_b_gate_up, exp_w_down,
                     exp_b_down, ln_ffn_g[1], ln_ffn_b[1])
    return out.reshape(B, L, D)
```

```python
import functools
import math

import jax
import jax.numpy as jnp
import numpy as np
from jax import lax
from jax.experimental import pallas as pl
from jax.experimental.pallas import tpu as pltpu

F32 = jnp.float32
BF16 = jnp.bfloat16
HIGHEST = lax.Precision.HIGHEST

DEPTH = 2
S5_GROUP = 16
S5_STATE = 64
MOBA_HEAD_DIM = 64
MOBA_BLOCK = 256
MOBA_TOPK = 3
REL_BUCKETS = 32
REL_MAX_DIST = 2048
GLA_HEADS = 4
GLA_GATE_TAU = 16.0
GLA_CHUNK = 64
N_EXPERTS = 32
TOP_K = 4
SWIGLU_LIMIT = 7.0
SWIGLU_ALPHA = 1.702
MOE_BLOCK = 512
MOE_TILE = 256
MIX_TILE = 512
MOE_CHUNK = 32
SUBLANES = 8
MOE_REMAINDERS = (16, 8)
DN_ALPHA = (2 * DEPTH) ** 0.25
LN_EPS = 1e-5
RMS_EPS = 1e-5

V7X_VMEM_LIMIT_BYTES = 56 * 1024 * 1024
LANES = 128

S5_GROUPS_PER_CHUNK = LANES // S5_GROUP
S5_CHUNK_STATES = S5_GROUPS_PER_CHUNK * S5_STATE


def _params(sem):
    return pltpu.CompilerParams(dimension_semantics=sem, vmem_limit_bytes=V7X_VMEM_LIMIT_BYTES)


def _layer_norm(r, g, b):
    mu = jnp.mean(r, axis=-1, keepdims=True)
    c = r - mu
    var = jnp.mean(c * c, axis=-1, keepdims=True)
    return c * lax.rsqrt(var + LN_EPS) * g + b


def _inproj0_kernel(x_ref, wuk_ref, wqvt_ref, u_ref, k_ref, qt_ref, vt_ref, *, s5w):
    xb = x_ref[0].astype(BF16)
    h = jnp.dot(xb, wuk_ref[...], preferred_element_type=F32)
    u_ref[...] = h[:, :s5w]
    k_ref[0] = h[:, s5w:]
    ht = lax.dot_general(wqvt_ref[...], xb, (((1,), (1,)), ((), ())), preferred_element_type=F32)
    aw = ht.shape[0] // 2
    qt_ref[0] = ht[:aw]
    vt_ref[0] = ht[aw:].astype(BF16)


def _inproj0(x, w_in0, s5w, tl=512):
    B, L, D = x.shape
    aw = (w_in0.shape[1] - s5w) // 3
    wb = w_in0.astype(BF16)
    wuk = jnp.concatenate([wb[:, :s5w], wb[:, s5w + aw:s5w + 2 * aw]], axis=1)
    wqvt = jnp.concatenate([wb[:, s5w:s5w + aw], wb[:, s5w + 2 * aw:]], axis=1).T
    return pl.pallas_call(
        functools.partial(_inproj0_kernel, s5w=s5w),
        grid=(B, L // tl),
        in_specs=[pl.BlockSpec((1, tl, D), lambda b, l: (b, l, 0)),
                  pl.BlockSpec(wuk.shape, lambda b, l: (0, 0)),
                  pl.BlockSpec(wqvt.shape, lambda b, l: (0, 0))],
        out_specs=[pl.BlockSpec((tl, s5w), lambda b, l: (l, b)),
                   pl.BlockSpec((1, tl, aw), lambda b, l: (b, l, 0)),
                   pl.BlockSpec((1, aw, tl), lambda b, l: (b, 0, l)),
                   pl.BlockSpec((1, aw, tl), lambda b, l: (b, 0, l))],
        out_shape=[jax.ShapeDtypeStruct((L, B * s5w), F32),
                   jax.ShapeDtypeStruct((B, L, aw), F32),
                   jax.ShapeDtypeStruct((B, aw, L), F32),
                   jax.ShapeDtypeStruct((B, aw, L), BF16)],
        compiler_params=_params(("parallel", "parallel")),
        name="inproj0",
    )(x, wuk, wqvt)


def _s5_discretize(lam_re, lam_im, log_dt, b_re, b_im):
    dt = jnp.exp(log_dt.astype(F32))[:, None]
    lr, li = lam_re.astype(F32), lam_im.astype(F32)
    mag = jnp.exp(lr * dt)
    ab_re, ab_im = mag * jnp.cos(li * dt), mag * jnp.sin(li * dt)
    er, ei = ab_re - 1.0, ab_im
    den = lr * lr + li * li
    q_re = (er * lr + ei * li) / den
    q_im = (ei * lr - er * li) / den
    br_, bi_ = b_re.astype(F32), b_im.astype(F32)
    bb_re = q_re[..., None] * br_ - q_im[..., None] * bi_
    bb_im = q_re[..., None] * bi_ + q_im[..., None] * br_
    return ab_re, ab_im, bb_re, bb_im


def _s5_kernel(u_ref, bm_ref, cm_ref, are_ref, aim_ref, d_ref, wg_ref, bg_ref, y_ref,
               s_scr, st_scr, z_scr, *, tl, nb, nchunk):
    ns = S5_CHUNK_STATES

    @pl.when(pl.program_id(0) == 0)
    def _():
        st_scr[...] = jnp.zeros_like(st_scr)

    u = u_ref[...].reshape(tl * nb, nchunk * LANES)
    for j in range(nchunk):
        uj = u[:, j * LANES:(j + 1) * LANES]
        s_scr[...] = jnp.dot(uj.astype(BF16), bm_ref[j], preferred_element_type=F32)
        ar = jnp.broadcast_to(are_ref[j], (nb, ns))
        ai = jnp.broadcast_to(aim_ref[j], (nb, ns))

        def step(t, carry, ar=ar, ai=ai):
            sre, sim = carry
            r0 = pl.multiple_of(t * nb, nb)
            nre = ar * sre - ai * sim + s_scr[pl.ds(r0, nb), 0:ns]
            nim = ar * sim + ai * sre + s_scr[pl.ds(r0, nb), ns:2 * ns]
            s_scr[pl.ds(r0, nb), 0:ns] = nre
            s_scr[pl.ds(r0, nb), ns:2 * ns] = nim
            return nre, nim

        sre, sim = lax.fori_loop(0, tl, step, (st_scr[j, :, 0:ns], st_scr[j, :, ns:2 * ns]), unroll=2)
        st_scr[j, :, 0:ns] = sre
        st_scr[j, :, ns:2 * ns] = sim
        yj = jnp.dot(s_scr[...].astype(BF16), cm_ref[j], preferred_element_type=F32)
        z_scr[:, j * LANES:(j + 1) * LANES] = yj + uj * d_ref[:, j * LANES:(j + 1) * LANES]
    z = jax.nn.gelu(z_scr[...])
    gate = jax.nn.sigmoid(jnp.dot(z.astype(BF16), wg_ref[...], preferred_element_type=F32) + bg_ref[...])
    y_ref[...] = (z * gate).reshape(tl, nb, nchunk * LANES)


def _s5_mixer(u_tm, lam_re, lam_im, log_dt, b_re, b_im, c_re, c_im, d_skip, w_glu, b_glu, tl=32):
    L, B, W = u_tm.shape
    G, P, H = lam_re.shape[0], S5_STATE, S5_GROUP
    gc = S5_GROUPS_PER_CHUNK
    nchunk = G // gc
    ab_re, ab_im, bb_re, bb_im = _s5_discretize(lam_re, lam_im, log_dt, b_re, b_im)
    eye = jnp.eye(gc, dtype=F32)

    def b_blocks(bb):
        return jnp.einsum('jgph,gk->jghkp', bb.reshape(nchunk, gc, P, H), eye).reshape(nchunk, gc * H, gc * P)

    def c_blocks(cc):
        return jnp.einsum('jghp,gk->jgpkh', cc.reshape(nchunk, gc, H, P), eye).reshape(nchunk, gc * P, gc * H)

    bm = jnp.concatenate([b_blocks(bb_re), b_blocks(bb_im)], axis=2).astype(BF16)
    cm = jnp.concatenate([c_blocks(c_re.astype(F32)), -c_blocks(c_im.astype(F32))], axis=1).astype(BF16)
    are = ab_re.reshape(nchunk, 1, gc * P)
    aim = ab_im.reshape(nchunk, 1, gc * P)
    m = tl * B
    const3 = lambda l: (0, 0, 0)
    const2 = lambda l: (0, 0)
    return pl.pallas_call(
        functools.partial(_s5_kernel, tl=tl, nb=B, nchunk=nchunk),
        grid=(L // tl,),
        in_specs=[pl.BlockSpec((tl, B, W), lambda l: (l, 0, 0)),
                  pl.BlockSpec(bm.shape, const3), pl.BlockSpec(cm.shape, const3),
                  pl.BlockSpec(are.shape, const3), pl.BlockSpec(aim.shape, const3),
                  pl.BlockSpec((1, W), const2), pl.BlockSpec((W, W), const2), pl.BlockSpec((1, W), const2)],
        out_specs=pl.BlockSpec((tl, B, W), lambda l: (l, 0, 0)),
        out_shape=jax.ShapeDtypeStruct((L, B, W), F32),
        scratch_shapes=[pltpu.VMEM((m, 2 * S5_CHUNK_STATES), F32),
                        pltpu.VMEM((nchunk, B, 2 * S5_CHUNK_STATES), F32),
                        pltpu.VMEM((m, W), F32)],
        compiler_params=_params(("arbitrary",)),
        name="s5_mixer",
    )(u_tm, bm, cm, are, aim, d_skip.reshape(1, W).astype(F32), w_glu.astype(BF16),
      b_glu.reshape(1, W).astype(F32))


def _rel_bucket(n):
    n = jnp.maximum(n, 0)
    max_exact = REL_BUCKETS // 2
    nf = jnp.maximum(n, 1).astype(F32)
    large = max_exact + (jnp.log(nf / max_exact) / math.log(REL_MAX_DIST / max_exact)
                         * (REL_BUCKETS - max_exact)).astype(jnp.int32)
    large = jnp.minimum(large, REL_BUCKETS - 1)
    return jnp.where(n < max_exact, n, large)


MOBA_ONES_ROWS = 16
MOBA_NEG = -1e30
LOG2E = math.log2(math.e)


def _moba_kernel(qt_ref, k_ref, vt_ref, wv_ref, hot_ref, o_ref, bias_scr, kk_scr, va_scr, km_scr, qa_scr, s_scr,
                 *, nblk, bs, dh, topk):
    b = pl.program_id(1)
    tq = bs
    hpb = LANES // dh

    @pl.when(b == 0)
    def _():
        r_io = lax.broadcasted_iota(jnp.int32, (bs, 2 * bs), 0)
        c_io = lax.broadcasted_iota(jnp.int32, (bs, 2 * bs), 1)
        for hh in range(hpb):
            for dd in range(nblk):
                t = pltpu.roll(jnp.broadcast_to(wv_ref[hh, dd:dd + 1, :], (bs, 2 * bs)), bs + 1, 1,
                               stride=1, stride_axis=0)
                if dd == 0:
                    t = jnp.where(c_io >= r_io, t, MOBA_NEG)
                bias_scr[dd, :, hh * bs:(hh + 1) * bs] = t[:, 0:bs]

    def prepare():
        kf = k_ref[0]
        km_scr[...] = jnp.mean(kf.reshape(nblk, bs, LANES), axis=1)
        kk_scr[:, 0:LANES] = kf.astype(BF16)
        kk_scr[:, LANES:] = hot_ref[...]
        for n in range(nblk):
            va_scr[n, 0:LANES, :] = vt_ref[0, :, n * bs:(n + 1) * bs]
            va_scr[n, LANES:, :] = jnp.ones((MOBA_ONES_ROWS, bs), BF16)
        seq = nblk * bs
        q2 = qt_ref[0] * (dh ** -0.5 * LOG2E)
        f_io = lax.broadcasted_iota(jnp.int32, (LANES, seq), 0)
        blk_io = lax.broadcasted_iota(jnp.int32, (nblk, seq), 0)
        own_q = lax.broadcasted_iota(jnp.int32, (nblk, seq), 1) // bs
        qms, pens = [], []
        for hh in range(hpb):
            qm = jnp.where((f_io >= hh * dh) & (f_io < (hh + 1) * dh), q2, 0.0)
            gate = jnp.dot(km_scr[...], qm, precision=HIGHEST, preferred_element_type=F32)
            cnt = jnp.zeros((nblk, seq), jnp.int32)
            for m in range(nblk):
                gm = gate[m:m + 1, :]
                beats = (gm > gate) | ((gm == gate) & (m < blk_io))
                cnt = cnt + jnp.where(beats & (m < own_q), 1, 0)
            keep = (blk_io >= own_q) | (cnt < topk)
            pens.append(jnp.where(keep, 0.0, MOBA_NEG))
            qms.append(qm.astype(BF16))
        for n in range(nblk):
            cols = slice(n * bs, (n + 1) * bs)
            qa_scr[n, 0:LANES, :] = jnp.concatenate([qm[:, cols] for qm in qms], axis=1)
            pen = jnp.concatenate([p[:, cols] for p in pens], axis=1)
            qa_scr[n, LANES:, :] = jnp.concatenate(
                [pen, jnp.zeros((LANES - nblk, hpb * bs), F32)], axis=0).astype(BF16)

    prepare()

    for own in range(nblk):
        qa = qa_scr[own]
        m = None
        for j in range(own + 1):
            s = (jnp.dot(kk_scr[j * bs:(j + 1) * bs, :], qa, preferred_element_type=F32)
                 + bias_scr[own - j])
            s_scr[j] = s
            mj = jnp.max(s, axis=0, keepdims=True)
            m = mj if m is None else jnp.maximum(m, mj)
        acc = None
        for j in range(own + 1):
            p = jnp.exp2(s_scr[j] - m).astype(BF16)
            t = jnp.dot(va_scr[j], p, preferred_element_type=F32)
            acc = t if acc is None else acc + t
        on = acc[0:LANES, :] / acc[LANES:LANES + 1, :]
        ot = jnp.concatenate([on[hh * dh:(hh + 1) * dh, hh * tq:(hh + 1) * tq] for hh in range(hpb)], axis=0)
        o_ref[0, own * bs:(own + 1) * bs, :] = ot.T


def _moba(qt, k, vt, rel_bias):
    B, L, W = k.shape
    dh, bs = MOBA_HEAD_DIM, MOBA_BLOCK
    H = W // dh
    nblk = L // bs
    hpb = LANES // dh
    dist = jnp.arange(L, dtype=jnp.int32)
    by_dist = rel_bias.astype(F32).T[:, _rel_bucket(dist)] * LOG2E
    idx = np.clip(np.arange(nblk)[:, None] * bs - (bs - 1) + np.arange(2 * bs)[None, :], 0, L - 1)
    vecs = by_dist[:, idx]
    hot = np.zeros((L, LANES), np.float32)
    hot[np.arange(L), np.arange(L) // bs] = 1.0
    return pl.pallas_call(
        functools.partial(_moba_kernel, nblk=nblk, bs=bs, dh=dh, topk=min(MOBA_TOPK, nblk)),
        grid=(H // hpb, B),
        in_specs=[pl.BlockSpec((1, LANES, L), lambda h, b: (b, h, 0)),
                  pl.BlockSpec((1, L, LANES), lambda h, b: (b, 0, h)),
                  pl.BlockSpec((1, LANES, L), lambda h, b: (b, h, 0)),
                  pl.BlockSpec((hpb, nblk, 2 * bs), lambda h, b: (h, 0, 0)),
                  pl.BlockSpec((L, LANES), lambda h, b: (0, 0))],
        out_specs=pl.BlockSpec((1, L, LANES), lambda h, b: (b, 0, h)),
        out_shape=jax.ShapeDtypeStruct((B, L, W), F32),
        scratch_shapes=[pltpu.VMEM((nblk, bs, hpb * bs), F32),
                        pltpu.VMEM((L, 2 * LANES), BF16),
                        pltpu.VMEM((nblk, LANES + MOBA_ONES_ROWS, bs), BF16),
                        pltpu.VMEM((nblk, LANES), F32),
                        pltpu.VMEM((nblk, 2 * LANES, hpb * bs), BF16),
                        pltpu.VMEM((nblk, bs, hpb * bs), F32)],
        compiler_params=_params(("arbitrary", "arbitrary")),
        name="moba_attention",
    )(qt, k, vt, vecs, jnp.asarray(hot, BF16))


def _route_top_k(xn, wrt_ref, br_ref):
    lt = lax.dot_general(wrt_ref[...], xn, (((1,), (1,)), ((), ())), precision=HIGHEST,
                         preferred_element_type=F32) + br_ref[...]
    n_e = lt.shape[0]
    eid = lax.broadcasted_iota(jnp.int32, lt.shape, 0)
    cur = lt
    vals, idxs = [], []
    for _ in range(TOP_K):
        m = jnp.max(cur, axis=0, keepdims=True)
        idx = jnp.min(jnp.where(cur == m, eid, n_e), axis=0, keepdims=True)
        vals.append(m)
        idxs.append(idx)
        cur = jnp.where(eid == idx, -jnp.inf, cur)
    ex = [jnp.exp(v - vals[0]) for v in vals]
    den = ex[0]
    for t in ex[1:]:
        den = den + t
    hot = jnp.zeros(lt.shape, F32)
    for idx in idxs:
        hot = hot + jnp.where(eid == idx, 1.0, 0.0)
    return (jnp.concatenate(idxs, axis=0), jnp.concatenate([t / den for t in ex], axis=0),
            jnp.sum(hot, axis=1, keepdims=True))


def _mix_ln_kernel(*refs, n_in):
    x_ref = refs[0]
    y_refs = refs[1:1 + n_in]
    w_refs = refs[1 + n_in:1 + 2 * n_in]
    g_ref, b_ref, wrt_ref, br_ref, x1_ref, et_ref, wt_ref, cnt_ref = refs[1 + 2 * n_in:]
    mix = None
    for y_ref, w_ref in zip(y_refs, w_refs):
        y = y_ref[...]
        y = y.reshape(y.shape[-2], y.shape[-1]).astype(BF16)
        t = jnp.dot(y, w_ref[...], preferred_element_type=F32)
        mix = t if mix is None else mix + t
    xn = _layer_norm(DN_ALPHA * x_ref[0] + mix, g_ref[...], b_ref[...])
    x1_ref[0] = xn
    for s in range(et_ref.shape[0]):
        et, wt, cnt = _route_top_k(xn[s * MOE_TILE:(s + 1) * MOE_TILE], wrt_ref, br_ref)
        et_ref[s] = et
        wt_ref[s] = wt
        cnt_ref[s] = cnt


def _mix_ln(x, ys, ws, g, b, wr, br):
    B, L, D = x.shape
    E = wr.shape[1]
    tl = MIX_TILE
    nl = L // tl
    sub = tl // MOE_TILE
    nt = B * L // MOE_TILE
    const = lambda bb, l: (0, 0)
    tile = lambda bb, l: (bb * nl + l, 0, 0)
    in_specs = [pl.BlockSpec((1, tl, D), lambda bb, l: (bb, l, 0))]
    in_specs += [spec for _, spec in ys]
    in_specs += [pl.BlockSpec(w.shape, const) for w in ws]
    in_specs += [pl.BlockSpec((1, D), const), pl.BlockSpec((1, D), const),
                 pl.BlockSpec((E, D), const), pl.BlockSpec((E, 1), const)]
    return pl.pallas_call(
        functools.partial(_mix_ln_kernel, n_in=len(ys)),
        grid=(B, nl),
        in_specs=in_specs,
        out_specs=[pl.BlockSpec((1, tl, D), lambda bb, l: (bb, l, 0)),
                   pl.BlockSpec((sub, TOP_K, MOE_TILE), tile), pl.BlockSpec((sub, TOP_K, MOE_TILE), tile),
                   pl.BlockSpec((sub, E, 1), tile)],
        out_shape=[jax.ShapeDtypeStruct((B, L, D), F32),
                   jax.ShapeDtypeStruct((nt, TOP_K, MOE_TILE), jnp.int32),
                   jax.ShapeDtypeStruct((nt, TOP_K, MOE_TILE), F32),
                   jax.ShapeDtypeStruct((nt, E, 1), F32)],
        compiler_params=_params(("parallel", "parallel")),
        name="outproj_ln_router",
    )(x, *[a for a, _ in ys], *ws, g.reshape(1, D), b.reshape(1, D), wr.astype(F32).T, br.reshape(E, 1))


def _deinterleave_kernel(w_ref, p_ref, o_ref, *, ff):
    w = w_ref[0, 0].astype(BF16)
    for c in range(2 * ff // (2 * LANES)):
        t = jnp.dot(w[:, 2 * LANES * c:2 * LANES * (c + 1)], p_ref[...], preferred_element_type=F32)
        o_ref[0, :, LANES * c:LANES * (c + 1)] = t[:, :LANES].astype(BF16)
        o_ref[0, :, ff + LANES * c:ff + LANES * (c + 1)] = t[:, LANES:].astype(BF16)


def _deinterleave_gate_up(w_gate_up, layer, tk=512):
    _, E, D, F2 = w_gate_up.shape
    src = np.arange(2 * LANES)
    dst = np.where(src % 2 == 0, src // 2, LANES + src // 2)
    perm = np.zeros((2 * LANES, 2 * LANES), np.float32)
    perm[src, dst] = 1.0
    return pl.pallas_call(
        functools.partial(_deinterleave_kernel, ff=F2 // 2),
        grid=(E, D // tk),
        in_specs=[pl.BlockSpec((1, 1, tk, F2), lambda e, k: (layer, e, k, 0)),
                  pl.BlockSpec((2 * LANES, 2 * LANES), lambda e, k: (0, 0))],
        out_specs=pl.BlockSpec((1, tk, F2), lambda e, k: (e, k, 0)),
        out_shape=jax.ShapeDtypeStruct((E, D, F2), BF16),
        compiler_params=_params(("parallel", "parallel")),
        name="deinterleave_gate_up",
    )(w_gate_up, jnp.asarray(perm, BF16))


def _expert_kernel(be_ref, nu_ref, xs_ref, wgu_ref, wd_ref, bgu_ref, bd_ref, ys_ref):
    blk = pl.program_id(0)
    ff = wd_ref.shape[2]
    d = wd_ref.shape[3]

    @pl.when(blk < nu_ref[0])
    def _():
        x = xs_ref[:, :d].astype(BF16)
        pw = xs_ref[:, d:d + 1]
        h = jnp.dot(x, wgu_ref[0], preferred_element_type=F32) + bgu_ref[0]
        g = h[:, :ff]
        u = h[:, ff:]
        g = jnp.minimum(g, SWIGLU_LIMIT)
        u = jnp.clip(u, -SWIGLU_LIMIT, SWIGLU_LIMIT)
        act = g * jax.nn.sigmoid(SWIGLU_ALPHA * g) * (u + 1.0)
        y = jnp.dot(act.astype(BF16), wd_ref[0, 0].astype(BF16), preferred_element_type=F32) + bd_ref[0]
        ys_ref[...] = y * pw

    @pl.when(blk >= nu_ref[0])
    def _():
        ys_ref[...] = jnp.zeros_like(ys_ref)


def _expert_ffn(xs, plan, wgu, w_down, layer, bgu, bd):
    P, DW = xs.shape
    F, D = w_down.shape[2], w_down.shape[3]
    bm = MOE_BLOCK
    wmap = lambda i, be, nu: (be[i], 0, 0)
    rmap = lambda i, be, nu: (i, 0)
    grid_spec = pltpu.PrefetchScalarGridSpec(
        num_scalar_prefetch=2,
        grid=(P // bm,),
        in_specs=[pl.BlockSpec((bm, DW), rmap),
                  pl.BlockSpec((1, D, 2 * F), wmap),
                  pl.BlockSpec((1, 1, F, D), lambda i, be, nu: (layer, be[i], 0, 0)),
                  pl.BlockSpec((1, 1, 2 * F), wmap), pl.BlockSpec((1, 1, D), wmap)],
        out_specs=pl.BlockSpec((bm, D), rmap),
    )
    return pl.pallas_call(
        _expert_kernel,
        grid_spec=grid_spec,
        out_shape=jax.ShapeDtypeStruct((P, D), F32),
        compiler_params=_params(("arbitrary",)),
        name="moe_experts",
    )(plan["blk_expert"], plan["n_used"], xs, wgu, w_down, bgu, bd)


def _moe_rows(T, nt, E):
    bound = T * TOP_K + (SUBLANES - 1) * E * nt + E * MOE_BLOCK
    return -(-bound // MOE_BLOCK) * MOE_BLOCK


def _moe_plan(cnt, T):
    nt, E = cnt.shape[0], cnt.shape[1]
    c = cnt.reshape(nt, E).astype(jnp.int32)
    n8 = (c + SUBLANES - 1) // SUBLANES * SUBLANES
    tot = jnp.sum(n8, axis=0)
    seg = (tot + MOE_BLOCK - 1) // MOE_BLOCK * MOE_BLOCK
    seg_start = jnp.cumsum(seg) - seg
    strip = seg_start[None, :] + jnp.cumsum(n8, axis=0) - n8
    off = jnp.cumsum(n8, axis=1) - n8
    nblk = _moe_rows(T, nt, E) // MOE_BLOCK
    seg_blk = seg // MOE_BLOCK
    blk_end = jnp.cumsum(seg_blk)
    blk_ids = jnp.arange(nblk, dtype=jnp.int32)
    blk_expert = jnp.minimum(jnp.sum(blk_end[None, :] <= blk_ids[:, None], axis=1), E - 1).astype(jnp.int32)
    tail = jnp.stack([seg_start + tot, seg - tot], axis=1)
    exact = jnp.stack([jnp.sum(n8 // MOE_CHUNK, axis=1)]
                      + [jnp.sum((n8 // r) % 2, axis=1) for r in MOE_REMAINDERS], axis=1)
    return dict(strip=strip.reshape(-1).astype(jnp.int32), off=off.reshape(-1).astype(jnp.int32),
                n8=n8.reshape(-1).astype(jnp.int32), exact=exact.reshape(-1).astype(jnp.int32),
                off_col=off.reshape(nt, E, 1).astype(jnp.int32), blk_expert=blk_expert,
                tail=tail.reshape(-1).astype(jnp.int32),
                n_used=blk_end[-1:].astype(jnp.int32))


def _stage_rows(tm):
    return TOP_K * tm + N_EXPERTS * SUBLANES


MOE_COPY_SIZES = (MOE_CHUNK,) + MOE_REMAINDERS


def _for_each_piece(n, fn):
    nfull = n // MOE_CHUNK

    def per_chunk(c, carry):
        fn(c * MOE_CHUNK, MOE_CHUNK)
        return carry

    lax.fori_loop(0, nfull, per_chunk, 0)
    done = nfull * MOE_CHUNK
    for r in MOE_REMAINDERS:
        has = (n // r) % 2

        @pl.when(has == 1)
        def _(done=done, r=r):
            fn(done, r)

        done = done + has * r


def _dispatch_kernel(strip_ref, off_ref, n8_ref, exact_ref, tail_ref, nu_ref, x_ref, et_ref, wt_ref, offc_ref,
                     xs_ref, lp_ref, stage, sel_scr, zero, sem, *, tm):
    i = pl.program_id(0)
    last = pl.num_programs(0) - 1
    n_e = N_EXPERTS
    d = x_ref.shape[1]
    nrow = _stage_rows(tm)
    slot = i % 2

    def strip_copy(src0, dst0, rows, sl):
        return pltpu.make_async_copy(stage.at[sl, pl.ds(pl.multiple_of(src0, SUBLANES), rows)],
                                     xs_ref.at[pl.ds(pl.multiple_of(dst0, SUBLANES), rows)], sem.at[sl])

    def wait_tile(tile, sl):
        for s, rows in enumerate(MOE_COPY_SIZES):
            def one(c, carry, rows=rows):
                strip_copy(0, 0, rows, sl).wait()
                return carry

            lax.fori_loop(0, exact_ref[tile * len(MOE_COPY_SIZES) + s], one, 0)

    et = et_ref[0]
    wt = wt_ref[0]
    eid = lax.broadcasted_iota(jnp.int32, (n_e, tm), 0)
    hots = [eid == et[k:k + 1, :] for k in range(TOP_K)]
    m_t = jnp.zeros((n_e, tm), F32)
    for h in hots:
        m_t = m_t + jnp.where(h, 1.0, 0.0)
    before = jnp.where(lax.broadcasted_iota(jnp.int32, (tm, tm), 0) < lax.broadcasted_iota(jnp.int32, (tm, tm), 1),
                       1.0, 0.0).astype(BF16)
    rank = jnp.dot(m_t.astype(BF16), before, preferred_element_type=F32)
    base = offc_ref[0].astype(F32) + rank
    lps = [jnp.sum(jnp.where(h, base, 0.0), axis=0, keepdims=True) for h in hots]
    lpi = [lp.astype(jnp.int32) for lp in lps]
    lp_ref[0] = jnp.concatenate(lpi, axis=0)

    wcols = []
    for rc in range(nrow // LANES):
        rio = lax.broadcasted_iota(jnp.int32, (LANES, tm), 0) + rc * LANES
        sel = jnp.zeros((LANES, tm), F32)
        wsel = jnp.zeros((LANES, tm), F32)
        for k in reversed(range(TOP_K)):
            hit = rio == lpi[k]
            sel = jnp.where(hit, 1.0, sel)
            wsel = jnp.where(hit, wt[k:k + 1, :], wsel)
        sel_scr[rc * LANES:(rc + 1) * LANES, :] = sel.astype(BF16)
        wcols.append(jnp.sum(wsel, axis=1, keepdims=True))

    @pl.when(i >= 2)
    def _():
        wait_tile(i - 2, slot)

    stage[slot, :, 0:d] = jnp.dot(sel_scr[...], x_ref[...].astype(BF16), preferred_element_type=F32)
    for rc, wcol in enumerate(wcols):
        stage[slot, rc * LANES:(rc + 1) * LANES, d:d + LANES] = jnp.broadcast_to(wcol, (LANES, LANES))

    def per_expert(e, carry):
        idx = i * n_e + e
        src, dst = off_ref[idx], strip_ref[idx]
        _for_each_piece(n8_ref[idx], lambda o, rows: strip_copy(src + o, dst + o, rows, slot).start())
        return carry

    lax.fori_loop(0, n_e, per_expert, 0)

    def zero_fill(wait):
        def go(dst0, rows):
            cp = pltpu.make_async_copy(zero.at[pl.ds(0, rows)],
                                       xs_ref.at[pl.ds(pl.multiple_of(dst0, SUBLANES), rows)], sem.at[2])
            cp.wait() if wait else cp.start()

        def per_tail(e, carry):
            start = tail_ref[2 * e]
            _for_each_piece(tail_ref[2 * e + 1], lambda o, rows: go(start + o, rows))
            return carry

        lax.fori_loop(0, n_e, per_tail, 0)

        def per_block(blk, carry):
            go(blk * MOE_BLOCK, MOE_BLOCK)
            return carry

        lax.fori_loop(nu_ref[0], xs_ref.shape[0] // MOE_BLOCK, per_block, 0)

    @pl.when(i == last)
    def _():
        @pl.when(i >= 1)
        def _():
            wait_tile(i - 1, 1 - slot)

        wait_tile(i, slot)
        zero[...] = jnp.zeros_like(zero)
        zero_fill(wait=False)
        zero_fill(wait=True)


def _dispatch(x1, et, wt, plan):
    T, D = x1.shape
    nt = et.shape[0]
    tm = T // nt
    E = N_EXPERTS
    P = _moe_rows(T, nt, E)
    nrow = _stage_rows(tm)
    tile3 = lambda i, *_: (i, 0, 0)
    grid_spec = pltpu.PrefetchScalarGridSpec(
        num_scalar_prefetch=6,
        grid=(nt,),
        in_specs=[pl.BlockSpec((tm, D), lambda i, *_: (i, 0)),
                  pl.BlockSpec((1, TOP_K, tm), tile3), pl.BlockSpec((1, TOP_K, tm), tile3),
                  pl.BlockSpec((1, E, 1), tile3)],
        out_specs=[pl.BlockSpec(memory_space=pl.ANY), pl.BlockSpec((1, TOP_K, tm), tile3)],
        scratch_shapes=[pltpu.VMEM((2, nrow, D + LANES), F32),
                        pltpu.VMEM((nrow, tm), BF16),
                        pltpu.VMEM((MOE_BLOCK, D + LANES), F32),
                        pltpu.SemaphoreType.DMA((3,))],
    )
    return pl.pallas_call(
        functools.partial(_dispatch_kernel, tm=tm),
        grid_spec=grid_spec,
        out_shape=[jax.ShapeDtypeStruct((P, D + LANES), F32), jax.ShapeDtypeStruct((nt, TOP_K, tm), jnp.int32)],
        compiler_params=_params(("arbitrary",)),
        name="moe_dispatch",
    )(plan["strip"], plan["off"], plan["n8"], plan["exact"], plan["tail"], plan["n_used"], x1, et, wt,
      plan["off_col"])


def _combine_kernel(strip_ref, off_ref, n8_ref, exact_ref, ys_ref, lp_ref, x_ref, g_ref, b_ref, *rest, tm, kd):
    if kd:
        wqk_ref, wvg_ref, wgr_ref, w2_ref, b2_ref, xo_ref, qkl_ref, vg_ref, land, sel_scr, sem = rest
    else:
        xo_ref, land, sel_scr, sem = rest
    i = pl.program_id(0)
    n_e = N_EXPERTS
    nrow = _stage_rows(tm)
    slot = i % 2

    def strip_copy(src0, dst0, rows, sl):
        return pltpu.make_async_copy(ys_ref.at[pl.ds(pl.multiple_of(src0, SUBLANES), rows)],
                                     land.at[sl, pl.ds(pl.multiple_of(dst0, SUBLANES), rows)], sem.at[sl])

    def fetch(tile, sl):
        def per_expert(e, carry):
            idx = tile * n_e + e
            src, dst = strip_ref[idx], off_ref[idx]
            _for_each_piece(n8_ref[idx], lambda o, rows: strip_copy(src + o, dst + o, rows, sl).start())
            return carry

        lax.fori_loop(0, n_e, per_expert, 0)

    @pl.when(i == 0)
    def _():
        land[...] = jnp.zeros_like(land)
        fetch(0, 0)

    @pl.when(i + 1 < pl.num_programs(0))
    def _():
        fetch(i + 1, 1 - slot)

    lp = lp_ref[0]
    for rc in range(nrow // LANES):
        rio = lax.broadcasted_iota(jnp.int32, (LANES, tm), 0) + rc * LANES
        piece = jnp.zeros((LANES, tm), F32)
        for k in range(TOP_K):
            piece = jnp.where(rio == lp[k:k + 1, :], 1.0, piece)
        sel_scr[rc * LANES:(rc + 1) * LANES, :] = piece.astype(BF16)

    for s, rows in enumerate(MOE_COPY_SIZES):
        def one(c, carry, rows=rows):
            strip_copy(0, 0, rows, slot).wait()
            return carry

        lax.fori_loop(0, exact_ref[i * len(MOE_COPY_SIZES) + s], one, 0)

    ffn = lax.dot_general(sel_scr[...], land[slot, 0:nrow, :].astype(BF16), (((0,), (0,)), ((), ())),
                          preferred_element_type=F32)
    xn = _layer_norm(DN_ALPHA * x_ref[...] + ffn, g_ref[...], b_ref[...])
    xo_ref[...] = xn
    if kd:
        xb = xn.astype(BF16)
        qkl_ref[:, :2 * kd] = jnp.dot(xb, wqk_ref[...], preferred_element_type=F32)
        vg_ref[...] = jnp.dot(xb, wvg_ref[...], preferred_element_type=F32)
        gr = jnp.dot(xb, wgr_ref[...], preferred_element_type=F32)
        z = jnp.dot(gr, w2_ref[...], precision=HIGHEST, preferred_element_type=F32) + b2_ref[...]
        log_sig = jnp.minimum(z, 0.0) - jnp.log1p(jnp.exp(-jnp.abs(z)))
        qkl_ref[:, 2 * kd:] = log_sig / GLA_GATE_TAU


def _combine_ln(ys, lp, x, plan, g, b, gla=None):
    T, D = x.shape
    tm = MOE_TILE
    nt = T // tm
    const = lambda i, *_: (0, 0)
    row = lambda i, *_: (i, 0)
    in_specs = [pl.BlockSpec(memory_space=pl.ANY), pl.BlockSpec((1, TOP_K, tm), lambda i, *_: (i, 0, 0)),
                pl.BlockSpec((tm, D), row),
                pl.BlockSpec((1, D), const), pl.BlockSpec((1, D), const)]
    out_specs = [pl.BlockSpec((tm, D), row)]
    out_shape = [jax.ShapeDtypeStruct((T, D), F32)]
    args = [ys, lp, x, g.reshape(1, D), b.reshape(1, D)]
    kd = 0
    if gla is not None:
        w_in1, w_gate2, b_gate, kd, vd = gla
        ws = [w_in1[:, :2 * kd].astype(BF16), w_in1[:, 2 * kd:2 * kd + 2 * vd].astype(BF16),
              w_in1[:, 2 * kd + 2 * vd:].astype(BF16), w_gate2.astype(F32), b_gate.reshape(1, kd)]
        in_specs += [pl.BlockSpec(w.shape, const) for w in ws]
        args += ws
        out_specs += [pl.BlockSpec((tm, 3 * kd), row), pl.BlockSpec((tm, 2 * vd), row)]
        out_shape += [jax.ShapeDtypeStruct((T, 3 * kd), F32), jax.ShapeDtypeStruct((T, 2 * vd), F32)]
    grid_spec = pltpu.PrefetchScalarGridSpec(
        num_scalar_prefetch=4,
        grid=(nt,),
        in_specs=in_specs,
        out_specs=out_specs,
        scratch_shapes=[pltpu.VMEM((2, _stage_rows(tm), D), F32), pltpu.VMEM((_stage_rows(tm), tm), BF16),
                        pltpu.SemaphoreType.DMA((2,))],
    )
    return pl.pallas_call(
        functools.partial(_combine_kernel, tm=tm, kd=kd),
        grid_spec=grid_spec,
        out_shape=out_shape,
        compiler_params=_params(("arbitrary",)),
        name="moe_combine_ln",
    )(plan["strip"], plan["off"], plan["n8"], plan["exact"], *args)


def _moe_ln(x1, et, wt, cnt, layer, w_gate_up, b_gate_up, w_down, b_down, g, b, gla=None):
    T = x1.shape[0]
    plan = _moe_plan(cnt, T)
    xs, lp = _dispatch(x1, et, wt, plan)
    wgu = _deinterleave_gate_up(w_gate_up, layer)
    bgu = jnp.concatenate([b_gate_up[layer, :, 0::2], b_gate_up[layer, :, 1::2]], axis=-1)[:, None, :].astype(F32)
    ys = _expert_ffn(xs, plan, wgu, w_down, layer, bgu, b_down[layer, :, None, :].astype(F32))
    return _combine_ln(ys, lp, x1, plan, g, b, gla)


def _gla_kernel(q_ref, k_ref, la_ref, v_ref, g_ref, nw_ref, o_ref, s_scr, *, tl, nh, dk, dv):
    C = GLA_CHUNK

    @pl.when(pl.program_id(1) == 0)
    def _():
        s_scr[...] = jnp.zeros_like(s_scr)

    ri = lax.broadcasted_iota(jnp.int32, (C, C), 0)
    ci = lax.broadcasted_iota(jnp.int32, (C, C), 1)
    lower = ri >= ci
    tri = jnp.where(lower, 1.0, 0.0).astype(F32)
    states = [s_scr[h] for h in range(nh)]
    for c in range(tl // C):
        rows = slice(c * C, (c + 1) * C)
        bcum = jnp.dot(tri, la_ref[0, rows, :], precision=HIGHEST, preferred_element_type=F32)
        btot = bcum[C - 1:C, :]
        qb = (q_ref[0, rows, :] * (dk ** -0.5) * jnp.exp(bcum)).astype(BF16)
        kc = k_ref[0, rows, :]
        k_in = (kc * jnp.exp(-bcum)).astype(BF16)
        k_st = kc * jnp.exp(btot - bcum)
        dec = jnp.exp(btot)
        for h in range(nh):
            ks, vs = slice(h * dk, (h + 1) * dk), slice(h * dv, (h + 1) * dv)
            vb = v_ref[0, rows, vs].astype(BF16)
            att = lax.dot_general(qb[:, ks], k_in[:, ks], (((1,), (1,)), ((), ())), preferred_element_type=F32)
            att = jnp.where(lower, att, 0.0)
            o = (jnp.dot(att.astype(BF16), vb, preferred_element_type=F32)
                 + jnp.dot(qb[:, ks], states[h].astype(BF16), preferred_element_type=F32))
            kv = jnp.dot(k_st[:, ks].T.astype(BF16), vb, preferred_element_type=F32)
            decay = jnp.broadcast_to(dec[:, ks], (SUBLANES, dk)).T[:, 0:1]
            states[h] = decay * states[h] + kv
            o = o * lax.rsqrt(jnp.mean(o * o, axis=-1, keepdims=True) + RMS_EPS) * nw_ref[...]
            gg = g_ref[0, rows, vs]
            o_ref[0, rows, vs] = o * (gg * jax.nn.sigmoid(gg))
    for h in range(nh):
        s_scr[h] = states[h]


def _gla(qkl, vg, norm_w, B, L, tl=256):
    T, kd3 = qkl.shape
    kd = kd3 // 3
    vd = vg.shape[1] // 2
    H = GLA_HEADS
    dk, dv = kd // H, vd // H
    qkl3 = qkl.reshape(B, L, kd3)
    vg3 = vg.reshape(B, L, 2 * vd)
    return pl.pallas_call(
        functools.partial(_gla_kernel, tl=tl, nh=H, dk=dk, dv=dv),
        grid=(B, L // tl),
        in_specs=[pl.BlockSpec((1, tl, kd), lambda b, l: (b, l, 0)),
                  pl.BlockSpec((1, tl, kd), lambda b, l: (b, l, 1)),
                  pl.BlockSpec((1, tl, kd), lambda b, l: (b, l, 2)),
                  pl.BlockSpec((1, tl, vd), lambda b, l: (b, l, 0)),
                  pl.BlockSpec((1, tl, vd), lambda b, l: (b, l, 1)),
                  pl.BlockSpec((1, dv), lambda b, l: (0, 0))],
        out_specs=pl.BlockSpec((1, tl, vd), lambda b, l: (b, l, 0)),
        out_shape=jax.ShapeDtypeStruct((B, L, vd), F32),
        scratch_shapes=[pltpu.VMEM((H, dk, dv), F32)],
        compiler_params=_params(("parallel", "arbitrary")),
        name="gla_mixer",
    )(qkl3, qkl3, qkl3, vg3, vg3, norm_w.reshape(1, dv).astype(F32))


def kernel(x, w_in0, s5_lam_re, s5_lam_im, s5_log_dt, s5_b_re, s5_b_im, s5_c_re, s5_c_im, s5_d, s5_w_glu,
           s5_b_glu, rel_bias, w_out0, w_in1, gla_w_gate2, gla_b_gate, gla_norm_w, w_out1, ln_mix_g, ln_mix_b,
           ln_ffn_g, ln_ffn_b, router_w, router_b, exp_w_gate_up, exp_b_gate_up, exp_w_down, exp_b_down):
    B, L, D = x.shape
    T = B * L
    s5w = s5_w_glu.shape[-1]
    kd = gla_w_gate2.shape[-1]
    vd = gla_norm_w.shape[-1] * GLA_HEADS

    u_tm, k_att, qt_att, vt_att = _inproj0(x, w_in0[0], s5w)
    y_a = _s5_mixer(u_tm.reshape(L, B, s5w), s5_lam_re[0], s5_lam_im[0], s5_log_dt[0], s5_b_re[0], s5_b_im[0],
                    s5_c_re[0], s5_c_im[0], s5_d[0].reshape(-1), s5_w_glu[0], s5_b_glu[0])
    y_b = _moba(qt_att, k_att, vt_att, rel_bias)
    tl = MIX_TILE
    w0 = w_out0[0].astype(BF16)
    ys = [(y_a.reshape(L, B * s5w), pl.BlockSpec((tl, s5w), lambda bb, l: (l, bb))),
          (y_b, pl.BlockSpec((1, tl, D - s5w), lambda bb, l: (bb, l, 0)))]
    x1, et, wt, cnt = _mix_ln(x, ys, [w0[:s5w], w0[s5w:]], ln_mix_g[0], ln_mix_b[0], router_w[0], router_b[0])
    x2, qkl, vg = _moe_ln(x1.reshape(T, D), et, wt, cnt, 0, exp_w_gate_up, exp_b_gate_up, exp_w_down,
                          exp_b_down, ln_ffn_g[0], ln_ffn_b[0],
                          gla=(w_in1[0], gla_w_gate2[0], gla_b_gate[0], kd, vd))

    y_c = _gla(qkl, vg, gla_norm_w[0], B, L)
    ys = [(y_c, pl.BlockSpec((1, tl, vd), lambda bb, l: (bb, l, 0)))]
    x3, et, wt, cnt = _mix_ln(x2.reshape(B, L, D), ys, [w_out1[0].astype(BF16)], ln_mix_g[1], ln_mix_b[1],
                              router_w[1], router_b[1])
    (out,) = _moe_ln(x3.reshape(T, D), et, wt, cnt, 1, exp_w_gate_up, exp_b_gate_up, exp_w_down,
                     exp_b_down, ln_ffn_g[1], ln_ffn_b[1])
    return out.reshape(B, L, D)
```

```python
import functools
import math

import jax
import jax.numpy as jnp
import numpy as np
from jax import lax
from jax.experimental import pallas as pl
from jax.experimental.pallas import tpu as pltpu

F32 = jnp.float32
BF16 = jnp.bfloat16
HIGHEST = lax.Precision.HIGHEST

DEPTH = 2
S5_GROUP = 16
S5_STATE = 64
MOBA_HEAD_DIM = 64
MOBA_BLOCK = 256
MOBA_TOPK = 3
REL_BUCKETS = 32
REL_MAX_DIST = 2048
GLA_HEADS = 4
GLA_GATE_TAU = 16.0
GLA_CHUNK = 64
N_EXPERTS = 32
TOP_K = 4
SWIGLU_LIMIT = 7.0
SWIGLU_ALPHA = 1.702
MOE_BLOCK = 512
MOE_TILE = 256
MIX_TILE = 512
MOE_CHUNK = 32
SUBLANES = 8
MOE_REMAINDERS = (16, 8)
DN_ALPHA = (2 * DEPTH) ** 0.25
LN_EPS = 1e-5
RMS_EPS = 1e-5

V7X_VMEM_LIMIT_BYTES = 56 * 1024 * 1024
LANES = 128

S5_GROUPS_PER_CHUNK = LANES // S5_GROUP
S5_CHUNK_STATES = S5_GROUPS_PER_CHUNK * S5_STATE


def _params(sem):
    return pltpu.CompilerParams(dimension_semantics=sem, vmem_limit_bytes=V7X_VMEM_LIMIT_BYTES)


def _layer_norm(r, g, b):
    mu = jnp.mean(r, axis=-1, keepdims=True)
    c = r - mu
    var = jnp.mean(c * c, axis=-1, keepdims=True)
    return c * lax.rsqrt(var + LN_EPS) * g + b


def _inproj0_kernel(x_ref, wuk_ref, wqvt_ref, u_ref, k_ref, qt_ref, vt_ref, *, s5w):
    xb = x_ref[0].astype(BF16)
    h = jnp.dot(xb, wuk_ref[...], preferred_element_type=F32)
    u_ref[...] = h[:, :s5w]
    k_ref[0] = h[:, s5w:]
    ht = lax.dot_general(wqvt_ref[...], xb, (((1,), (1,)), ((), ())), preferred_element_type=F32)
    aw = ht.shape[0] // 2
    qt_ref[0] = ht[:aw]
    vt_ref[0] = ht[aw:].astype(BF16)


def _inproj0(x, w_in0, s5w, tl=512):
    B, L, D = x.shape
    aw = (w_in0.shape[1] - s5w) // 3
    wb = w_in0.astype(BF16)
    wuk = jnp.concatenate([wb[:, :s5w], wb[:, s5w + aw:s5w + 2 * aw]], axis=1)
    wqvt = jnp.concatenate([wb[:, s5w:s5w + aw], wb[:, s5w + 2 * aw:]], axis=1).T
    return pl.pallas_call(
        functools.partial(_inproj0_kernel, s5w=s5w),
        grid=(B, L // tl),
        in_specs=[pl.BlockSpec((1, tl, D), lambda b, l: (b, l, 0)),
                  pl.BlockSpec(wuk.shape, lambda b, l: (0, 0)),
                  pl.BlockSpec(wqvt.shape, lambda b, l: (0, 0))],
        out_specs=[pl.BlockSpec((tl, s5w), lambda b, l: (l, b)),
                   pl.BlockSpec((1, tl, aw), lambda b, l: (b, l, 0)),
                   pl.BlockSpec((1, aw, tl), lambda b, l: (b, 0, l)),
                   pl.BlockSpec((1, aw, tl), lambda b, l: (b, 0, l))],
        out_shape=[jax.ShapeDtypeStruct((L, B * s5w), F32),
                   jax.ShapeDtypeStruct((B, L, aw), F32),
                   jax.ShapeDtypeStruct((B, aw, L), F32),
                   jax.ShapeDtypeStruct((B, aw, L), BF16)],
        compiler_params=_params(("parallel", "parallel")),
        name="inproj0",
    )(x, wuk, wqvt)


def _s5_discretize(lam_re, lam_im, log_dt, b_re, b_im):
    dt = jnp.exp(log_dt.astype(F32))[:, None]
    lr, li = lam_re.astype(F32), lam_im.astype(F32)
    mag = jnp.exp(lr * dt)
    ab_re, ab_im = mag * jnp.cos(li * dt), mag * jnp.sin(li * dt)
    er, ei = ab_re - 1.0, ab_im
    den = lr * lr + li * li
    q_re = (er * lr + ei * li) / den
    q_im = (ei * lr - er * li) / den
    br_, bi_ = b_re.astype(F32), b_im.astype(F32)
    bb_re = q_re[..., None] * br_ - q_im[..., None] * bi_
    bb_im = q_re[..., None] * bi_ + q_im[..., None] * br_
    return ab_re, ab_im, bb_re, bb_im


def _s5_kernel(u_ref, bm_ref, cm_ref, are_ref, aim_ref, d_ref, wg_ref, bg_ref, y_ref,
               s_scr, st_scr, z_scr, *, tl, nb, nchunk):
    ns = S5_CHUNK_STATES

    @pl.when(pl.program_id(0) == 0)
    def _():
        st_scr[...] = jnp.zeros_like(st_scr)

    u = u_ref[...].reshape(tl * nb, nchunk * LANES)
    for j in range(nchunk):
        uj = u[:, j * LANES:(j + 1) * LANES]
        s_scr[...] = jnp.dot(uj.astype(BF16), bm_ref[j], preferred_element_type=F32)
        ar = jnp.broadcast_to(are_ref[j], (nb, ns))
        ai = jnp.broadcast_to(aim_ref[j], (nb, ns))

        def step(t, carry, ar=ar, ai=ai):
            sre, sim = carry
            r0 = pl.multiple_of(t * nb, nb)
            nre = ar * sre - ai * sim + s_scr[pl.ds(r0, nb), 0:ns]
            nim = ar * sim + ai * sre + s_scr[pl.ds(r0, nb), ns:2 * ns]
            s_scr[pl.ds(r0, nb), 0:ns] = nre
            s_scr[pl.ds(r0, nb), ns:2 * ns] = nim
            return nre, nim

        sre, sim = lax.fori_loop(0, tl, step, (st_scr[j, :, 0:ns], st_scr[j, :, ns:2 * ns]), unroll=2)
        st_scr[j, :, 0:ns] = sre
        st_scr[j, :, ns:2 * ns] = sim
        yj = jnp.dot(s_scr[...].astype(BF16), cm_ref[j], preferred_element_type=F32)
        z_scr[:, j * LANES:(j + 1) * LANES] = yj + uj * d_ref[:, j * LANES:(j + 1) * LANES]
    z = jax.nn.gelu(z_scr[...])
    gate = jax.nn.sigmoid(jnp.dot(z.astype(BF16), wg_ref[...], preferred_element_type=F32) + bg_ref[...])
    y_ref[...] = (z * gate).reshape(tl, nb, nchunk * LANES)


def _s5_mixer(u_tm, lam_re, lam_im, log_dt, b_re, b_im, c_re, c_im, d_skip, w_glu, b_glu, tl=32):
    L, B, W = u_tm.shape
    G, P, H = lam_re.shape[0], S5_STATE, S5_GROUP
    gc = S5_GROUPS_PER_CHUNK
    nchunk = G // gc
    ab_re, ab_im, bb_re, bb_im = _s5_discretize(lam_re, lam_im, log_dt, b_re, b_im)
    eye = jnp.eye(gc, dtype=F32)

    def b_blocks(bb):
        return jnp.einsum('jgph,gk->jghkp', bb.reshape(nchunk, gc, P, H), eye).reshape(nchunk, gc * H, gc * P)

    def c_blocks(cc):
        return jnp.einsum('jghp,gk->jgpkh', cc.reshape(nchunk, gc, H, P), eye).reshape(nchunk, gc * P, gc * H)

    bm = jnp.concatenate([b_blocks(bb_re), b_blocks(bb_im)], axis=2).astype(BF16)
    cm = jnp.concatenate([c_blocks(c_re.astype(F32)), -c_blocks(c_im.astype(F32))], axis=1).astype(BF16)
    are = ab_re.reshape(nchunk, 1, gc * P)
    aim = ab_im.reshape(nchunk, 1, gc * P)
    m = tl * B
    const3 = lambda l: (0, 0, 0)
    const2 = lambda l: (0, 0)
    return pl.pallas_call(
        functools.partial(_s5_kernel, tl=tl, nb=B, nchunk=nchunk),
        grid=(L // tl,),
        in_specs=[pl.BlockSpec((tl, B, W), lambda l: (l, 0, 0)),
                  pl.BlockSpec(bm.shape, const3), pl.BlockSpec(cm.shape, const3),
                  pl.BlockSpec(are.shape, const3), pl.BlockSpec(aim.shape, const3),
                  pl.BlockSpec((1, W), const2), pl.BlockSpec((W, W), const2), pl.BlockSpec((1, W), const2)],
        out_specs=pl.BlockSpec((tl, B, W), lambda l: (l, 0, 0)),
        out_shape=jax.ShapeDtypeStruct((L, B, W), F32),
        scratch_shapes=[pltpu.VMEM((m, 2 * S5_CHUNK_STATES), F32),
                        pltpu.VMEM((nchunk, B, 2 * S5_CHUNK_STATES), F32),
                        pltpu.VMEM((m, W), F32)],
        compiler_params=_params(("arbitrary",)),
        name="s5_mixer",
    )(u_tm, bm, cm, are, aim, d_skip.reshape(1, W).astype(F32), w_glu.astype(BF16),
      b_glu.reshape(1, W).astype(F32))


def _rel_bucket(n):
    n = jnp.maximum(n, 0)
    max_exact = REL_BUCKETS // 2
    nf = jnp.maximum(n, 1).astype(F32)
    large = max_exact + (jnp.log(nf / max_exact) / math.log(REL_MAX_DIST / max_exact)
                         * (REL_BUCKETS - max_exact)).astype(jnp.int32)
    large = jnp.minimum(large, REL_BUCKETS - 1)
    return jnp.where(n < max_exact, n, large)


MOBA_ONES_ROWS = 16
MOBA_NEG = -1e30
LOG2E = math.log2(math.e)


def _moba_kernel(qt_ref, k_ref, vt_ref, wv_ref, hot_ref, o_ref, bias_scr, kk_scr, va_scr, km_scr, qa_scr, s_scr,
                 *, nblk, bs, dh, topk):
    b = pl.program_id(1)
    tq = bs
    hpb = LANES // dh

    @pl.when(b == 0)
    def _():
        r_io = lax.broadcasted_iota(jnp.int32, (bs, 2 * bs), 0)
        c_io = lax.broadcasted_iota(jnp.int32, (bs, 2 * bs), 1)
        for hh in range(hpb):
            for dd in range(nblk):
                t = pltpu.roll(jnp.broadcast_to(wv_ref[hh, dd:dd + 1, :], (bs, 2 * bs)), bs + 1, 1,
                               stride=1, stride_axis=0)
                if dd == 0:
                    t = jnp.where(c_io >= r_io, t, MOBA_NEG)
                bias_scr[dd, :, hh * bs:(hh + 1) * bs] = t[:, 0:bs]

    def prepare():
        kf = k_ref[0]
        km_scr[...] = jnp.mean(kf.reshape(nblk, bs, LANES), axis=1)
        kk_scr[:, 0:LANES] = kf.astype(BF16)
        kk_scr[:, LANES:] = hot_ref[...]
        for n in range(nblk):
            va_scr[n, 0:LANES, :] = vt_ref[0, :, n * bs:(n + 1) * bs]
            va_scr[n, LANES:, :] = jnp.ones((MOBA_ONES_ROWS, bs), BF16)
        seq = nblk * bs
        q2 = qt_ref[0] * (dh ** -0.5 * LOG2E)
        f_io = lax.broadcasted_iota(jnp.int32, (LANES, seq), 0)
        blk_io = lax.broadcasted_iota(jnp.int32, (nblk, seq), 0)
        own_q = lax.broadcasted_iota(jnp.int32, (nblk, seq), 1) // bs
        qms, pens = [], []
        for hh in range(hpb):
            qm = jnp.where((f_io >= hh * dh) & (f_io < (hh + 1) * dh), q2, 0.0)
            gate = jnp.dot(km_scr[...], qm, precision=HIGHEST, preferred_element_type=F32)
            cnt = jnp.zeros((nblk, seq), jnp.int32)
            for m in range(nblk):
                gm = gate[m:m + 1, :]
                beats = (gm > gate) | ((gm == gate) & (m < blk_io))
                cnt = cnt + jnp.where(beats & (m < own_q), 1, 0)
            keep = (blk_io >= own_q) | (cnt < topk)
            pens.append(jnp.where(keep, 0.0, MOBA_NEG))
            qms.append(qm.astype(BF16))
        for n in range(nblk):
            cols = slice(n * bs, (n + 1) * bs)
            qa_scr[n, 0:LANES, :] = jnp.concatenate([qm[:, cols] for qm in qms], axis=1)
            pen = jnp.concatenate([p[:, cols] for p in pens], axis=1)
            qa_scr[n, LANES:, :] = jnp.concatenate(
                [pen, jnp.zeros((LANES - nblk, hpb * bs), F32)], axis=0).astype(BF16)

    prepare()

    for own in range(nblk):
        qa = qa_scr[own]
        m = None
        for j in range(own + 1):
            s = (jnp.dot(kk_scr[j * bs:(j + 1) * bs, :], qa, preferred_element_type=F32)
                 + bias_scr[own - j])
            s_scr[j] = s
            mj = jnp.max(s, axis=0, keepdims=True)
            m = mj if m is None else jnp.maximum(m, mj)
        acc = None
        for j in range(own + 1):
            p = jnp.exp2(s_scr[j] - m).astype(BF16)
            t = jnp.dot(va_scr[j], p, preferred_element_type=F32)
            acc = t if acc is None else acc + t
        on = acc[0:LANES, :] / acc[LANES:LANES + 1, :]
        ot = jnp.concatenate([on[hh * dh:(hh + 1) * dh, hh * tq:(hh + 1) * tq] for hh in range(hpb)], axis=0)
        o_ref[0, own * bs:(own + 1) * bs, :] = ot.T


def _moba(qt, k, vt, rel_bias):
    B, L, W = k.shape
    dh, bs = MOBA_HEAD_DIM, MOBA_BLOCK
    H = W // dh
    nblk = L // bs
    hpb = LANES // dh
    dist = jnp.arange(L, dtype=jnp.int32)
    by_dist = rel_bias.astype(F32).T[:, _rel_bucket(dist)] * LOG2E
    idx = np.clip(np.arange(nblk)[:, None] * bs - (bs - 1) + np.arange(2 * bs)[None, :], 0, L - 1)
    vecs = by_dist[:, idx]
    hot = np.zeros((L, LANES), np.float32)
    hot[np.arange(L), np.arange(L) // bs] = 1.0
    return pl.pallas_call(
        functools.partial(_moba_kernel, nblk=nblk, bs=bs, dh=dh, topk=min(MOBA_TOPK, nblk)),
        grid=(H // hpb, B),
        in_specs=[pl.BlockSpec((1, LANES, L), lambda h, b: (b, h, 0)),
                  pl.BlockSpec((1, L, LANES), lambda h, b: (b, 0, h)),
                  pl.BlockSpec((1, LANES, L), lambda h, b: (b, h, 0)),
                  pl.BlockSpec((hpb, nblk, 2 * bs), lambda h, b: (h, 0, 0)),
                  pl.BlockSpec((L, LANES), lambda h, b: (0, 0))],
        out_specs=pl.BlockSpec((1, L, LANES), lambda h, b: (b, 0, h)),
        out_shape=jax.ShapeDtypeStruct((B, L, W), F32),
        scratch_shapes=[pltpu.VMEM((nblk, bs, hpb * bs), F32),
                        pltpu.VMEM((L, 2 * LANES), BF16),
                        pltpu.VMEM((nblk, LANES + MOBA_ONES_ROWS, bs), BF16),
                        pltpu.VMEM((nblk, LANES), F32),
                        pltpu.VMEM((nblk, 2 * LANES, hpb * bs), BF16),
                        pltpu.VMEM((nblk, bs, hpb * bs), F32)],
        compiler_params=_params(("arbitrary", "arbitrary")),
        name="moba_attention",
    )(qt, k, vt, vecs, jnp.asarray(hot, BF16))


def _route_top_k(xn, wrt_ref, br_ref):
    lt = lax.dot_general(wrt_ref[...], xn, (((1,), (1,)), ((), ())), precision=HIGHEST,
                         preferred_element_type=F32) + br_ref[...]
    n_e = lt.shape[0]
    eid = lax.broadcasted_iota(jnp.int32, lt.shape, 0)
    cur = lt
    vals, idxs = [], []
    for _ in range(TOP_K):
        m = jnp.max(cur, axis=0, keepdims=True)
        idx = jnp.min(jnp.where(cur == m, eid, n_e), axis=0, keepdims=True)
        vals.append(m)
        idxs.append(idx)
        cur = jnp.where(eid == idx, -jnp.inf, cur)
    ex = [jnp.exp(v - vals[0]) for v in vals]
    den = ex[0]
    for t in ex[1:]:
        den = den + t
    hot = jnp.zeros(lt.shape, F32)
    for idx in idxs:
        hot = hot + jnp.where(eid == idx, 1.0, 0.0)
    return (jnp.concatenate(idxs, axis=0), jnp.concatenate([t / den for t in ex], axis=0),
            jnp.sum(hot, axis=1, keepdims=True))


def _mix_ln_kernel(*refs, n_in):
    x_ref = refs[0]
    y_refs = refs[1:1 + n_in]
    w_refs = refs[1 + n_in:1 + 2 * n_in]
    g_ref, b_ref, wrt_ref, br_ref, x1_ref, et_ref, wt_ref, cnt_ref = refs[1 + 2 * n_in:]
    mix = None
    for y_ref, w_ref in zip(y_refs, w_refs):
        y = y_ref[...]
        y = y.reshape(y.shape[-2], y.shape[-1]).astype(BF16)
        t = jnp.dot(y, w_ref[...], preferred_element_type=F32)
        mix = t if mix is None else mix + t
    xn = _layer_norm(DN_ALPHA * x_ref[0] + mix, g_ref[...], b_ref[...])
    x1_ref[0] = xn
    for s in range(et_ref.shape[0]):
        et, wt, cnt = _route_top_k(xn[s * MOE_TILE:(s + 1) * MOE_TILE], wrt_ref, br_ref)
        et_ref[s] = et
        wt_ref[s] = wt
        cnt_ref[s] = cnt


def _mix_ln(x, ys, ws, g, b, wr, br):
    B, L, D = x.shape
    E = wr.shape[1]
    tl = MIX_TILE
    nl = L // tl
    sub = tl // MOE_TILE
    nt = B * L // MOE_TILE
    const = lambda bb, l: (0, 0)
    tile = lambda bb, l: (bb * nl + l, 0, 0)
    in_specs = [pl.BlockSpec((1, tl, D), lambda bb, l: (bb, l, 0))]
    in_specs += [spec for _, spec in ys]
    in_specs += [pl.BlockSpec(w.shape, const) for w in ws]
    in_specs += [pl.BlockSpec((1, D), const), pl.BlockSpec((1, D), const),
                 pl.BlockSpec((E, D), const), pl.BlockSpec((E, 1), const)]
    return pl.pallas_call(
        functools.partial(_mix_ln_kernel, n_in=len(ys)),
        grid=(B, nl),
        in_specs=in_specs,
        out_specs=[pl.BlockSpec((1, tl, D), lambda bb, l: (bb, l, 0)),
                   pl.BlockSpec((sub, TOP_K, MOE_TILE), tile), pl.BlockSpec((sub, TOP_K, MOE_TILE), tile),
                   pl.BlockSpec((sub, E, 1), tile)],
        out_shape=[jax.ShapeDtypeStruct((B, L, D), F32),
                   jax.ShapeDtypeStruct((nt, TOP_K, MOE_TILE), jnp.int32),
                   jax.ShapeDtypeStruct((nt, TOP_K, MOE_TILE), F32),
                   jax.ShapeDtypeStruct((nt, E, 1), F32)],
        compiler_params=_params(("parallel", "parallel")),
        name="outproj_ln_router",
    )(x, *[a for a, _ in ys], *ws, g.reshape(1, D), b.reshape(1, D), wr.astype(F32).T, br.reshape(E, 1))


def _deinterleave_kernel(w_ref, p_ref, o_ref, *, ff):
    w = w_ref[0, 0].astype(BF16)
    for c in range(2 * ff // (2 * LANES)):
        t = jnp.dot(w[:, 2 * LANES * c:2 * LANES * (c + 1)], p_ref[...], preferred_element_type=F32)
        o_ref[0, :, LANES * c:LANES * (c + 1)] = t[:, :LANES].astype(BF16)
        o_ref[0, :, ff + LANES * c:ff + LANES * (c + 1)] = t[:, LANES:].astype(BF16)


def _deinterleave_gate_up(w_gate_up, layer, tk=512):
    _, E, D, F2 = w_gate_up.shape
    src = np.arange(2 * LANES)
    dst = np.where(src % 2 == 0, src // 2, LANES + src // 2)
    perm = np.zeros((2 * LANES, 2 * LANES), np.float32)
    perm[src, dst] = 1.0
    return pl.pallas_call(
        functools.partial(_deinterleave_kernel, ff=F2 // 2),
        grid=(E, D // tk),
        in_specs=[pl.BlockSpec((1, 1, tk, F2), lambda e, k: (layer, e, k, 0)),
                  pl.BlockSpec((2 * LANES, 2 * LANES), lambda e, k: (0, 0))],
        out_specs=pl.BlockSpec((1, tk, F2), lambda e, k: (e, k, 0)),
        out_shape=jax.ShapeDtypeStruct((E, D, F2), BF16),
        compiler_params=_params(("parallel", "parallel")),
        name="deinterleave_gate_up",
    )(w_gate_up, jnp.asarray(perm, BF16))


def _expert_kernel(be_ref, nu_ref, xs_ref, wgu_ref, wd_ref, bgu_ref, bd_ref, ys_ref):
    blk = pl.program_id(0)
    ff = wd_ref.shape[2]
    d = wd_ref.shape[3]

    @pl.when(blk < nu_ref[0])
    def _():
        x = xs_ref[:, :d].astype(BF16)
        pw = xs_ref[:, d:d + 1]
        h = jnp.dot(x, wgu_ref[0], preferred_element_type=F32) + bgu_ref[0]
        g = h[:, :ff]
        u = h[:, ff:]
        g = jnp.minimum(g, SWIGLU_LIMIT)
        u = jnp.clip(u, -SWIGLU_LIMIT, SWIGLU_LIMIT)
        act = g * jax.nn.sigmoid(SWIGLU_ALPHA * g) * (u + 1.0)
        y = jnp.dot(act.astype(BF16), wd_ref[0, 0].astype(BF16), preferred_element_type=F32) + bd_ref[0]
        ys_ref[...] = y * pw

    @pl.when(blk >= nu_ref[0])
    def _():
        ys_ref[...] = jnp.zeros_like(ys_ref)


def _expert_ffn(xs, plan, wgu, w_down, layer, bgu, bd):
    P, DW = xs.shape
    F, D = w_down.shape[2], w_down.shape[3]
    bm = MOE_BLOCK
    wmap = lambda i, be, nu: (be[i], 0, 0)
    rmap = lambda i, be, nu: (i, 0)
    grid_spec = pltpu.PrefetchScalarGridSpec(
        num_scalar_prefetch=2,
        grid=(P // bm,),
        in_specs=[pl.BlockSpec((bm, DW), rmap),
                  pl.BlockSpec((1, D, 2 * F), wmap),
                  pl.BlockSpec((1, 1, F, D), lambda i, be, nu: (layer, be[i], 0, 0)),
                  pl.BlockSpec((1, 1, 2 * F), wmap), pl.BlockSpec((1, 1, D), wmap)],
        out_specs=pl.BlockSpec((bm, D), rmap),
    )
    return pl.pallas_call(
        _expert_kernel,
        grid_spec=grid_spec,
        out_shape=jax.ShapeDtypeStruct((P, D), F32),
        compiler_params=_params(("arbitrary",)),
        name="moe_experts",
    )(plan["blk_expert"], plan["n_used"], xs, wgu, w_down, bgu, bd)


def _moe_rows(T, nt, E):
    bound = T * TOP_K + (SUBLANES - 1) * E * nt + E * MOE_BLOCK
    return -(-bound // MOE_BLOCK) * MOE_BLOCK


def _stage_rows(tm):
    return TOP_K * tm + N_EXPERTS * SUBLANES


MOE_COPY_SIZES = (MOE_CHUNK,) + MOE_REMAINDERS
MOE_PACK = 256


def _piece_slots(tm):
    assert _stage_rows(tm) // SUBLANES <= MOE_PACK
    return (_stage_rows(tm) // MOE_CHUNK,) + (N_EXPERTS,) * len(MOE_REMAINDERS)


def _moe_plan(cnt, T):
    nt, E = cnt.shape[0], cnt.shape[1]
    c = cnt.reshape(nt, E).astype(jnp.int32)
    n8 = (c + SUBLANES - 1) // SUBLANES * SUBLANES
    tot = jnp.sum(n8, axis=0)
    seg = (tot + MOE_BLOCK - 1) // MOE_BLOCK * MOE_BLOCK
    seg_start = jnp.cumsum(seg) - seg
    strip = seg_start[None, :] + jnp.cumsum(n8, axis=0) - n8
    off = jnp.cumsum(n8, axis=1) - n8
    nblk = _moe_rows(T, nt, E) // MOE_BLOCK
    seg_blk = seg // MOE_BLOCK
    blk_end = jnp.cumsum(seg_blk)
    blk_ids = jnp.arange(nblk, dtype=jnp.int32)
    blk_expert = jnp.minimum(jnp.sum(blk_end[None, :] <= blk_ids[:, None], axis=1), E - 1).astype(jnp.int32)
    tail = jnp.stack([seg_start + tot, seg - tot], axis=1)
    per_class = [n8 // MOE_CHUNK] + [(n8 // r) % 2 for r in MOE_REMAINDERS]
    exact = jnp.stack([jnp.sum(p, axis=1) for p in per_class], axis=1)
    pieces = []
    done = jnp.zeros_like(n8)
    for p, rows, slots in zip(per_class, MOE_COPY_SIZES, _piece_slots(T // nt)):
        last = jnp.cumsum(p, axis=1)
        first = last - p
        k = jnp.arange(slots, dtype=jnp.int32)[None, :, None]
        mine = (first[:, None, :] <= k) & (k < last[:, None, :])
        inner = done[:, None, :] + (k - first[:, None, :]) * rows
        packed = ((strip[:, None, :] + inner) // SUBLANES * MOE_PACK + (off[:, None, :] + inner) // SUBLANES)
        pieces.append(jnp.sum(jnp.where(mine, packed, 0), axis=2))
        done = done + p * rows
    return dict(strip=strip.reshape(-1).astype(jnp.int32), off=off.reshape(-1).astype(jnp.int32),
                n8=n8.reshape(-1).astype(jnp.int32), exact=exact.reshape(-1).astype(jnp.int32),
                pieces=jnp.concatenate([jnp.concatenate(pieces, axis=1).reshape(-1).astype(jnp.int32),
                                        jnp.zeros((1,), jnp.int32)]),
                off_col=off.reshape(nt, E, 1).astype(jnp.int32), blk_expert=blk_expert,
                tail=tail.reshape(-1).astype(jnp.int32),
                n_used=blk_end[-1:].astype(jnp.int32))


def _for_each_listed_copy(pieces_ref, exact_ref, tile, tm, fn):
    slots = _piece_slots(tm)
    base = tile * sum(slots)
    for s, rows in enumerate(MOE_COPY_SIZES):
        start = base + sum(slots[:s])

        def one(k, packed, start=start, rows=rows):
            ahead = pieces_ref[start + k + 1]
            fn(lax.shift_right_logical(packed, MOE_PACK.bit_length() - 1) * SUBLANES,
               (packed & (MOE_PACK - 1)) * SUBLANES, rows)
            return ahead

        lax.fori_loop(0, exact_ref[tile * len(MOE_COPY_SIZES) + s], one, pieces_ref[start])


def _for_each_piece(n, fn):
    nfull = n // MOE_CHUNK

    def per_chunk(c, carry):
        fn(c * MOE_CHUNK, MOE_CHUNK)
        return carry

    lax.fori_loop(0, nfull, per_chunk, 0)
    done = nfull * MOE_CHUNK
    for r in MOE_REMAINDERS:
        has = (n // r) % 2

        @pl.when(has == 1)
        def _(done=done, r=r):
            fn(done, r)

        done = done + has * r


def _dispatch_kernel(pieces_ref, exact_ref, tail_ref, nu_ref, x_ref, et_ref, wt_ref, offc_ref,
                     xs_ref, lp_ref, stage, sel_scr, zero, sem, *, tm):
    i = pl.program_id(0)
    last = pl.num_programs(0) - 1
    n_e = N_EXPERTS
    d = x_ref.shape[1]
    nrow = _stage_rows(tm)
    slot = i % 2

    def strip_copy(src0, dst0, rows, sl):
        return pltpu.make_async_copy(stage.at[sl, pl.ds(pl.multiple_of(src0, SUBLANES), rows)],
                                     xs_ref.at[pl.ds(pl.multiple_of(dst0, SUBLANES), rows)], sem.at[sl])

    def wait_tile(tile, sl):
        for s, rows in enumerate(MOE_COPY_SIZES):
            def one(c, carry, rows=rows):
                strip_copy(0, 0, rows, sl).wait()
                return carry

            lax.fori_loop(0, exact_ref[tile * len(MOE_COPY_SIZES) + s], one, 0)

    et = et_ref[0]
    wt = wt_ref[0]
    eid = lax.broadcasted_iota(jnp.int32, (n_e, tm), 0)
    hots = [eid == et[k:k + 1, :] for k in range(TOP_K)]
    m_t = jnp.zeros((n_e, tm), F32)
    for h in hots:
        m_t = m_t + jnp.where(h, 1.0, 0.0)
    before = jnp.where(lax.broadcasted_iota(jnp.int32, (tm, tm), 0) < lax.broadcasted_iota(jnp.int32, (tm, tm), 1),
                       1.0, 0.0).astype(BF16)
    rank = jnp.dot(m_t.astype(BF16), before, preferred_element_type=F32)
    base = offc_ref[0].astype(F32) + rank
    lps = [jnp.sum(jnp.where(h, base, 0.0), axis=0, keepdims=True) for h in hots]
    lpi = [lp.astype(jnp.int32) for lp in lps]
    lp_ref[0] = jnp.concatenate(lpi, axis=0)

    wcols = []
    for rc in range(nrow // LANES):
        rio = lax.broadcasted_iota(jnp.int32, (LANES, tm), 0) + rc * LANES
        sel = jnp.zeros((LANES, tm), F32)
        wsel = jnp.zeros((LANES, tm), F32)
        for k in reversed(range(TOP_K)):
            hit = rio == lpi[k]
            sel = jnp.where(hit, 1.0, sel)
            wsel = jnp.where(hit, wt[k:k + 1, :], wsel)
        sel_scr[rc * LANES:(rc + 1) * LANES, :] = sel.astype(BF16)
        wcols.append(jnp.sum(wsel, axis=1, keepdims=True))

    @pl.when(i >= 2)
    def _():
        wait_tile(i - 2, slot)

    stage[slot, :, 0:d] = jnp.dot(sel_scr[...], x_ref[...].astype(BF16), preferred_element_type=F32)
    for rc, wcol in enumerate(wcols):
        stage[slot, rc * LANES:(rc + 1) * LANES, d:d + LANES] = jnp.broadcast_to(wcol, (LANES, LANES))

    _for_each_listed_copy(pieces_ref, exact_ref, i, tm,
                          lambda far, near, rows: strip_copy(near, far, rows, slot).start())

    def zero_fill(wait):
        def go(dst0, rows):
            cp = pltpu.make_async_copy(zero.at[pl.ds(0, rows)],
                                       xs_ref.at[pl.ds(pl.multiple_of(dst0, SUBLANES), rows)], sem.at[2])
            cp.wait() if wait else cp.start()

        def per_tail(e, carry):
            start = tail_ref[2 * e]
            _for_each_piece(tail_ref[2 * e + 1], lambda o, rows: go(start + o, rows))
            return carry

        lax.fori_loop(0, n_e, per_tail, 0)

        def per_block(blk, carry):
            go(blk * MOE_BLOCK, MOE_BLOCK)
            return carry

        lax.fori_loop(nu_ref[0], xs_ref.shape[0] // MOE_BLOCK, per_block, 0)

    @pl.when(i == last)
    def _():
        @pl.when(i >= 1)
        def _():
            wait_tile(i - 1, 1 - slot)

        wait_tile(i, slot)
        zero[...] = jnp.zeros_like(zero)
        zero_fill(wait=False)
        zero_fill(wait=True)


def _dispatch(x1, et, wt, plan):
    T, D = x1.shape
    nt = et.shape[0]
    tm = T // nt
    E = N_EXPERTS
    P = _moe_rows(T, nt, E)
    nrow = _stage_rows(tm)
    tile3 = lambda i, *_: (i, 0, 0)
    grid_spec = pltpu.PrefetchScalarGridSpec(
        num_scalar_prefetch=4,
        grid=(nt,),
        in_specs=[pl.BlockSpec((tm, D), lambda i, *_: (i, 0)),
                  pl.BlockSpec((1, TOP_K, tm), tile3), pl.BlockSpec((1, TOP_K, tm), tile3),
                  pl.BlockSpec((1, E, 1), tile3)],
        out_specs=[pl.BlockSpec(memory_space=pl.ANY), pl.BlockSpec((1, TOP_K, tm), tile3)],
        scratch_shapes=[pltpu.VMEM((2, nrow, D + LANES), F32),
                        pltpu.VMEM((nrow, tm), BF16),
                        pltpu.VMEM((MOE_BLOCK, D + LANES), F32),
                        pltpu.SemaphoreType.DMA((3,))],
    )
    return pl.pallas_call(
        functools.partial(_dispatch_kernel, tm=tm),
        grid_spec=grid_spec,
        out_shape=[jax.ShapeDtypeStruct((P, D + LANES), F32), jax.ShapeDtypeStruct((nt, TOP_K, tm), jnp.int32)],
        compiler_params=_params(("arbitrary",)),
        name="moe_dispatch",
    )(plan["pieces"], plan["exact"], plan["tail"], plan["n_used"], x1, et, wt, plan["off_col"])


def _combine_kernel(pieces_ref, exact_ref, ys_ref, lp_ref, x_ref, g_ref, b_ref, *rest, tm, kd):
    if kd:
        wqk_ref, wvg_ref, wgr_ref, w2_ref, b2_ref, xo_ref, qkl_ref, vg_ref, land, sel_scr, sem = rest
    else:
        xo_ref, land, sel_scr, sem = rest
    i = pl.program_id(0)
    n_e = N_EXPERTS
    nrow = _stage_rows(tm)
    slot = i % 2

    def strip_copy(src0, dst0, rows, sl):
        return pltpu.make_async_copy(ys_ref.at[pl.ds(pl.multiple_of(src0, SUBLANES), rows)],
                                     land.at[sl, pl.ds(pl.multiple_of(dst0, SUBLANES), rows)], sem.at[sl])

    def fetch(tile, sl):
        _for_each_listed_copy(pieces_ref, exact_ref, tile, tm,
                              lambda far, near, rows: strip_copy(far, near, rows, sl).start())

    @pl.when(i == 0)
    def _():
        land[...] = jnp.zeros_like(land)
        fetch(0, 0)

    @pl.when(i + 1 < pl.num_programs(0))
    def _():
        fetch(i + 1, 1 - slot)

    lp = lp_ref[0]
    for rc in range(nrow // LANES):
        rio = lax.broadcasted_iota(jnp.int32, (LANES, tm), 0) + rc * LANES
        piece = jnp.zeros((LANES, tm), F32)
        for k in range(TOP_K):
            piece = jnp.where(rio == lp[k:k + 1, :], 1.0, piece)
        sel_scr[rc * LANES:(rc + 1) * LANES, :] = piece.astype(BF16)

    for s, rows in enumerate(MOE_COPY_SIZES):
        def one(c, carry, rows=rows):
            strip_copy(0, 0, rows, slot).wait()
            return carry

        lax.fori_loop(0, exact_ref[i * len(MOE_COPY_SIZES) + s], one, 0)

    ffn = lax.dot_general(sel_scr[...], land[slot, 0:nrow, :].astype(BF16), (((0,), (0,)), ((), ())),
                          preferred_element_type=F32)
    xn = _layer_norm(DN_ALPHA * x_ref[...] + ffn, g_ref[...], b_ref[...])
    xo_ref[...] = xn
    if kd:
        xb = xn.astype(BF16)
        qkl_ref[:, :2 * kd] = jnp.dot(xb, wqk_ref[...], preferred_element_type=F32)
        vg_ref[...] = jnp.dot(xb, wvg_ref[...], preferred_element_type=F32)
        gr = jnp.dot(xb, wgr_ref[...], preferred_element_type=F32)
        z = jnp.dot(gr, w2_ref[...], precision=HIGHEST, preferred_element_type=F32) + b2_ref[...]
        log_sig = jnp.minimum(z, 0.0) - jnp.log1p(jnp.exp(-jnp.abs(z)))
        qkl_ref[:, 2 * kd:] = log_sig / GLA_GATE_TAU


def _combine_ln(ys, lp, x, plan, g, b, gla=None):
    T, D = x.shape
    tm = MOE_TILE
    nt = T // tm
    const = lambda i, *_: (0, 0)
    row = lambda i, *_: (i, 0)
    in_specs = [pl.BlockSpec(memory_space=pl.ANY), pl.BlockSpec((1, TOP_K, tm), lambda i, *_: (i, 0, 0)),
                pl.BlockSpec((tm, D), row),
                pl.BlockSpec((1, D), const), pl.BlockSpec((1, D), const)]
    out_specs = [pl.BlockSpec((tm, D), row)]
    out_shape = [jax.ShapeDtypeStruct((T, D), F32)]
    args = [ys, lp, x, g.reshape(1, D), b.reshape(1, D)]
    kd = 0
    if gla is not None:
        w_in1, w_gate2, b_gate, kd, vd = gla
        ws = [w_in1[:, :2 * kd].astype(BF16), w_in1[:, 2 * kd:2 * kd + 2 * vd].astype(BF16),
              w_in1[:, 2 * kd + 2 * vd:].astype(BF16), w_gate2.astype(F32), b_gate.reshape(1, kd)]
        in_specs += [pl.BlockSpec(w.shape, const) for w in ws]
        args += ws
        out_specs += [pl.BlockSpec((tm, 3 * kd), row), pl.BlockSpec((tm, 2 * vd), row)]
        out_shape += [jax.ShapeDtypeStruct((T, 3 * kd), F32), jax.ShapeDtypeStruct((T, 2 * vd), F32)]
    grid_spec = pltpu.PrefetchScalarGridSpec(
        num_scalar_prefetch=2,
        grid=(nt,),
        in_specs=in_specs,
        out_specs=out_specs,
        scratch_shapes=[pltpu.VMEM((2, _stage_rows(tm), D), F32), pltpu.VMEM((_stage_rows(tm), tm), BF16),
                        pltpu.SemaphoreType.DMA((2,))],
    )
    return pl.pallas_call(
        functools.partial(_combine_kernel, tm=tm, kd=kd),
        grid_spec=grid_spec,
        out_shape=out_shape,
        compiler_params=_params(("arbitrary",)),
        name="moe_combine_ln",
    )(plan["pieces"], plan["exact"], *args)


def _moe_ln(x1, et, wt, cnt, layer, w_gate_up, b_gate_up, w_down, b_down, g, b, gla=None):
    T = x1.shape[0]
    plan = _moe_plan(cnt, T)
    xs, lp = _dispatch(x1, et, wt, plan)
    wgu = _deinterleave_gate_up(w_gate_up, layer)
    bgu = jnp.concatenate([b_gate_up[layer, :, 0::2], b_gate_up[layer, :, 1::2]], axis=-1)[:, None, :].astype(F32)
    ys = _expert_ffn(xs, plan, wgu, w_down, layer, bgu, b_down[layer, :, None, :].astype(F32))
    return _combine_ln(ys, lp, x1, plan, g, b, gla)


def _gla_kernel(q_ref, k_ref, la_ref, v_ref, g_ref, nw_ref, o_ref, s_scr, *, tl, nh, dk, dv):
    C = GLA_CHUNK

    @pl.when(pl.program_id(1) == 0)
    def _():
        s_scr[...] = jnp.zeros_like(s_scr)

    ri = lax.broadcasted_iota(jnp.int32, (C, C), 0)
    ci = lax.broadcasted_iota(jnp.int32, (C, C), 1)
    lower = ri >= ci
    tri = jnp.where(lower, 1.0, 0.0).astype(F32)
    states = [s_scr[h] for h in range(nh)]
    for c in range(tl // C):
        rows = slice(c * C, (c + 1) * C)
        bcum = jnp.dot(tri, la_ref[0, rows, :], precision=HIGHEST, preferred_element_type=F32)
        btot = bcum[C - 1:C, :]
        qb = (q_ref[0, rows, :] * (dk ** -0.5) * jnp.exp(bcum)).astype(BF16)
        kc = k_ref[0, rows, :]
        k_in = (kc * jnp.exp(-bcum)).astype(BF16)
        k_st = kc * jnp.exp(btot - bcum)
        dec = jnp.exp(btot)
        for h in range(nh):
            ks, vs = slice(h * dk, (h + 1) * dk), slice(h * dv, (h + 1) * dv)
            vb = v_ref[0, rows, vs].astype(BF16)
            att = lax.dot_general(qb[:, ks], k_in[:, ks], (((1,), (1,)), ((), ())), preferred_element_type=F32)
            att = jnp.where(lower, att, 0.0)
            o = (jnp.dot(att.astype(BF16), vb, preferred_element_type=F32)
                 + jnp.dot(qb[:, ks], states[h].astype(BF16), preferred_element_type=F32))
            kv = jnp.dot(k_st[:, ks].T.astype(BF16), vb, preferred_element_type=F32)
            decay = jnp.broadcast_to(dec[:, ks], (SUBLANES, dk)).T[:, 0:1]
            states[h] = decay * states[h] + kv
            o = o * lax.rsqrt(jnp.mean(o * o, axis=-1, keepdims=True) + RMS_EPS) * nw_ref[...]
            gg = g_ref[0, rows, vs]
            o_ref[0, rows, vs] = o * (gg * jax.nn.sigmoid(gg))
    for h in range(nh):
        s_scr[h] = states[h]


def _gla(qkl, vg, norm_w, B, L, tl=256):
    T, kd3 = qkl.shape
    kd = kd3 // 3
    vd = vg.shape[1] // 2
    H = GLA_HEADS
    dk, dv = kd // H, vd // H
    qkl3 = qkl.reshape(B, L, kd3)
    vg3 = vg.reshape(B, L, 2 * vd)
    return pl.pallas_call(
        functools.partial(_gla_kernel, tl=tl, nh=H, dk=dk, dv=dv),
        grid=(B, L // tl),
        in_specs=[pl.BlockSpec((1, tl, kd), lambda b, l: (b, l, 0)),
                  pl.BlockSpec((1, tl, kd), lambda b, l: (b, l, 1)),
                  pl.BlockSpec((1, tl, kd), lambda b, l: (b, l, 2)),
                  pl.BlockSpec((1, tl, vd), lambda b, l: (b, l, 0)),
                  pl.BlockSpec((1, tl, vd), lambda b, l: (b, l, 1)),
                  pl.BlockSpec((1, dv), lambda b, l: (0, 0))],
        out_specs=pl.BlockSpec((1, tl, vd), lambda b, l: (b, l, 0)),
        out_shape=jax.ShapeDtypeStruct((B, L, vd), F32),
        scratch_shapes=[pltpu.VMEM((H, dk, dv), F32)],
        compiler_params=_params(("parallel", "arbitrary")),
        name="gla_mixer",
    )(qkl3, qkl3, qkl3, vg3, vg3, norm_w.reshape(1, dv).astype(F32))


def kernel(x, w_in0, s5_lam_re, s5_lam_im, s5_log_dt, s5_b_re, s5_b_im, s5_c_re, s5_c_im, s5_d, s5_w_glu,
           s5_b_glu, rel_bias, w_out0, w_in1, gla_w_gate2, gla_b_gate, gla_norm_w, w_out1, ln_mix_g, ln_mix_b,
           ln_ffn_g, ln_ffn_b, router_w, router_b, exp_w_gate_up, exp_b_gate_up, exp_w_down, exp_b_down):
    B, L, D = x.shape
    T = B * L
    s5w = s5_w_glu.shape[-1]
    kd = gla_w_gate2.shape[-1]
    vd = gla_norm_w.shape[-1] * GLA_HEADS

    u_tm, k_att, qt_att, vt_att = _inproj0(x, w_in0[0], s5w)
    y_a = _s5_mixer(u_tm.reshape(L, B, s5w), s5_lam_re[0], s5_lam_im[0], s5_log_dt[0], s5_b_re[0], s5_b_im[0],
                    s5_c_re[0], s5_c_im[0], s5_d[0].reshape(-1), s5_w_glu[0], s5_b_glu[0])
    y_b = _moba(qt_att, k_att, vt_att, rel_bias)
    tl = MIX_TILE
    w0 = w_out0[0].astype(BF16)
    ys = [(y_a.reshape(L, B * s5w), pl.BlockSpec((tl, s5w), lambda bb, l: (l, bb))),
          (y_b, pl.BlockSpec((1, tl, D - s5w), lambda bb, l: (bb, l, 0)))]
    x1, et, wt, cnt = _mix_ln(x, ys, [w0[:s5w], w0[s5w:]], ln_mix_g[0], ln_mix_b[0], router_w[0], router_b[0])
    x2, qkl, vg = _moe_ln(x1.reshape(T, D), et, wt, cnt, 0, exp_w_gate_up, exp_b_gate_up, exp_w_down,
                          exp_b_down, ln_ffn_g[0], ln_ffn_b[0],
                          gla=(w_in1[0], gla_w_gate2[0], gla_b_gate[0], kd, vd))

    y_c = _gla(qkl, vg, gla_norm_w[0], B, L)
    ys = [(y_c, pl.BlockSpec((1, tl, vd), lambda bb, l: (bb, l, 0)))]
    x3, et, wt, cnt = _mix_ln(x2.reshape(B, L, D), ys, [w_out1[0].astype(BF16)], ln_mix_g[1], ln_mix_b[1],
                              router_w[1], router_b[1])
    (out,) = _moe_ln(x3.reshape(T, D), et, wt, cnt, 1, exp_w_gate_up, exp_b_gate_up, exp_w_down,
                     exp_b_down, ln_ffn_g[1], ln_ffn_b[1])
    return out.reshape(B, L, D)
```

```python
import functools
import math

import jax
import jax.numpy as jnp
import numpy as np
from jax import lax
from jax.experimental import pallas as pl
from jax.experimental.pallas import tpu as pltpu

F32 = jnp.float32
BF16 = jnp.bfloat16
HIGHEST = lax.Precision.HIGHEST

DEPTH = 2
S5_GROUP = 16
S5_STATE = 64
MOBA_HEAD_DIM = 64
MOBA_BLOCK = 256
MOBA_TOPK = 3
REL_BUCKETS = 32
REL_MAX_DIST = 2048
GLA_HEADS = 4
GLA_GATE_TAU = 16.0
GLA_CHUNK = 64
N_EXPERTS = 32
TOP_K = 4
SWIGLU_LIMIT = 7.0
SWIGLU_ALPHA = 1.702
MOE_BLOCK = 512
MOE_TILE = 256
MIX_TILE = 1024
MOE_CHUNK = 32
SUBLANES = 8
MOE_REMAINDERS = (16, 8)
DN_ALPHA = (2 * DEPTH) ** 0.25
LN_EPS = 1e-5
RMS_EPS = 1e-5

V7X_VMEM_LIMIT_BYTES = 56 * 1024 * 1024
LANES = 128

S5_GROUPS_PER_CHUNK = LANES // S5_GROUP
S5_CHUNK_STATES = S5_GROUPS_PER_CHUNK * S5_STATE


def _params(sem):
    return pltpu.CompilerParams(dimension_semantics=sem, vmem_limit_bytes=V7X_VMEM_LIMIT_BYTES)


def _layer_norm(r, g, b):
    mu = jnp.mean(r, axis=-1, keepdims=True)
    c = r - mu
    var = jnp.mean(c * c, axis=-1, keepdims=True)
    return c * lax.rsqrt(var + LN_EPS) * g + b


def _inproj0_kernel(x_ref, wuk_ref, wqvt_ref, u_ref, k_ref, qt_ref, vt_ref, *, s5w):
    xb = x_ref[0].astype(BF16)
    h = jnp.dot(xb, wuk_ref[...], preferred_element_type=F32)
    u_ref[...] = h[:, :s5w]
    k_ref[0] = h[:, s5w:]
    ht = lax.dot_general(wqvt_ref[...], xb, (((1,), (1,)), ((), ())), preferred_element_type=F32)
    aw = ht.shape[0] // 2
    qt_ref[0] = ht[:aw]
    vt_ref[0] = ht[aw:].astype(BF16)


def _inproj0(x, w_in0, s5w, tl=512):
    B, L, D = x.shape
    aw = (w_in0.shape[1] - s5w) // 3
    wb = w_in0.astype(BF16)
    wuk = jnp.concatenate([wb[:, :s5w], wb[:, s5w + aw:s5w + 2 * aw]], axis=1)
    wqvt = jnp.concatenate([wb[:, s5w:s5w + aw], wb[:, s5w + 2 * aw:]], axis=1).T
    return pl.pallas_call(
        functools.partial(_inproj0_kernel, s5w=s5w),
        grid=(B, L // tl),
        in_specs=[pl.BlockSpec((1, tl, D), lambda b, l: (b, l, 0)),
                  pl.BlockSpec(wuk.shape, lambda b, l: (0, 0)),
                  pl.BlockSpec(wqvt.shape, lambda b, l: (0, 0))],
        out_specs=[pl.BlockSpec((tl, s5w), lambda b, l: (l, b)),
                   pl.BlockSpec((1, tl, aw), lambda b, l: (b, l, 0)),
                   pl.BlockSpec((1, aw, tl), lambda b, l: (b, 0, l)),
                   pl.BlockSpec((1, aw, tl), lambda b, l: (b, 0, l))],
        out_shape=[jax.ShapeDtypeStruct((L, B * s5w), F32),
                   jax.ShapeDtypeStruct((B, L, aw), F32),
                   jax.ShapeDtypeStruct((B, aw, L), F32),
                   jax.ShapeDtypeStruct((B, aw, L), BF16)],
        compiler_params=_params(("parallel", "parallel")),
        name="inproj0",
    )(x, wuk, wqvt)


def _s5_discretize(lam_re, lam_im, log_dt, b_re, b_im):
    dt = jnp.exp(log_dt.astype(F32))[:, None]
    lr, li = lam_re.astype(F32), lam_im.astype(F32)
    mag = jnp.exp(lr * dt)
    ab_re, ab_im = mag * jnp.cos(li * dt), mag * jnp.sin(li * dt)
    er, ei = ab_re - 1.0, ab_im
    den = lr * lr + li * li
    q_re = (er * lr + ei * li) / den
    q_im = (ei * lr - er * li) / den
    br_, bi_ = b_re.astype(F32), b_im.astype(F32)
    bb_re = q_re[..., None] * br_ - q_im[..., None] * bi_
    bb_im = q_re[..., None] * bi_ + q_im[..., None] * br_
    return ab_re, ab_im, bb_re, bb_im


def _s5_kernel(u_ref, bm_ref, cm_ref, are_ref, aim_ref, d_ref, wg_ref, bg_ref, y_ref,
               s_scr, st_scr, z_scr, *, tl, nb, nchunk):
    ns = S5_CHUNK_STATES

    @pl.when(pl.program_id(0) == 0)
    def _():
        st_scr[...] = jnp.zeros_like(st_scr)

    u = u_ref[...].reshape(tl * nb, nchunk * LANES)
    for j in range(nchunk):
        uj = u[:, j * LANES:(j + 1) * LANES]
        s_scr[...] = jnp.dot(uj.astype(BF16), bm_ref[j], preferred_element_type=F32)
        ar = jnp.broadcast_to(are_ref[j], (nb, ns))
        ai = jnp.broadcast_to(aim_ref[j], (nb, ns))

        def step(t, carry, ar=ar, ai=ai):
            sre, sim = carry
            r0 = pl.multiple_of(t * nb, nb)
            nre = ar * sre - ai * sim + s_scr[pl.ds(r0, nb), 0:ns]
            nim = ar * sim + ai * sre + s_scr[pl.ds(r0, nb), ns:2 * ns]
            s_scr[pl.ds(r0, nb), 0:ns] = nre
            s_scr[pl.ds(r0, nb), ns:2 * ns] = nim
            return nre, nim

        sre, sim = lax.fori_loop(0, tl, step, (st_scr[j, :, 0:ns], st_scr[j, :, ns:2 * ns]), unroll=2)
        st_scr[j, :, 0:ns] = sre
        st_scr[j, :, ns:2 * ns] = sim
        yj = jnp.dot(s_scr[...].astype(BF16), cm_ref[j], preferred_element_type=F32)
        z_scr[:, j * LANES:(j + 1) * LANES] = yj + uj * d_ref[:, j * LANES:(j + 1) * LANES]
    z = jax.nn.gelu(z_scr[...])
    gate = jax.nn.sigmoid(jnp.dot(z.astype(BF16), wg_ref[...], preferred_element_type=F32) + bg_ref[...])
    y_ref[...] = (z * gate).reshape(tl, nb, nchunk * LANES)


def _s5_mixer(u_tm, lam_re, lam_im, log_dt, b_re, b_im, c_re, c_im, d_skip, w_glu, b_glu, tl=64):
    L, B, W = u_tm.shape
    G, P, H = lam_re.shape[0], S5_STATE, S5_GROUP
    gc = S5_GROUPS_PER_CHUNK
    nchunk = G // gc
    ab_re, ab_im, bb_re, bb_im = _s5_discretize(lam_re, lam_im, log_dt, b_re, b_im)
    eye = jnp.eye(gc, dtype=F32)

    def b_blocks(bb):
        return jnp.einsum('jgph,gk->jghkp', bb.reshape(nchunk, gc, P, H), eye).reshape(nchunk, gc * H, gc * P)

    def c_blocks(cc):
        return jnp.einsum('jghp,gk->jgpkh', cc.reshape(nchunk, gc, H, P), eye).reshape(nchunk, gc * P, gc * H)

    bm = jnp.concatenate([b_blocks(bb_re), b_blocks(bb_im)], axis=2).astype(BF16)
    cm = jnp.concatenate([c_blocks(c_re.astype(F32)), -c_blocks(c_im.astype(F32))], axis=1).astype(BF16)
    are = ab_re.reshape(nchunk, 1, gc * P)
    aim = ab_im.reshape(nchunk, 1, gc * P)
    m = tl * B
    const3 = lambda l: (0, 0, 0)
    const2 = lambda l: (0, 0)
    return pl.pallas_call(
        functools.partial(_s5_kernel, tl=tl, nb=B, nchunk=nchunk),
        grid=(L // tl,),
        in_specs=[pl.BlockSpec((tl, B, W), lambda l: (l, 0, 0)),
                  pl.BlockSpec(bm.shape, const3), pl.BlockSpec(cm.shape, const3),
                  pl.BlockSpec(are.shape, const3), pl.BlockSpec(aim.shape, const3),
                  pl.BlockSpec((1, W), const2), pl.BlockSpec((W, W), const2), pl.BlockSpec((1, W), const2)],
        out_specs=pl.BlockSpec((tl, B, W), lambda l: (l, 0, 0)),
        out_shape=jax.ShapeDtypeStruct((L, B, W), F32),
        scratch_shapes=[pltpu.VMEM((m, 2 * S5_CHUNK_STATES), F32),
                        pltpu.VMEM((nchunk, B, 2 * S5_CHUNK_STATES), F32),
                        pltpu.VMEM((m, W), F32)],
        compiler_params=_params(("arbitrary",)),
        name="s5_mixer",
    )(u_tm, bm, cm, are, aim, d_skip.reshape(1, W).astype(F32), w_glu.astype(BF16),
      b_glu.reshape(1, W).astype(F32))


def _rel_bucket(n):
    n = jnp.maximum(n, 0)
    max_exact = REL_BUCKETS // 2
    nf = jnp.maximum(n, 1).astype(F32)
    large = max_exact + (jnp.log(nf / max_exact) / math.log(REL_MAX_DIST / max_exact)
                         * (REL_BUCKETS - max_exact)).astype(jnp.int32)
    large = jnp.minimum(large, REL_BUCKETS - 1)
    return jnp.where(n < max_exact, n, large)


MOBA_ONES_ROWS = 16
MOBA_NEG = -1e30
LOG2E = math.log2(math.e)


def _moba_kernel(qt_ref, k_ref, vt_ref, wv_ref, hot_ref, o_ref, bias_scr, kk_scr, va_scr, km_scr, qa_scr, s_scr,
                 *, nblk, bs, dh, topk):
    b = pl.program_id(1)
    tq = bs
    hpb = LANES // dh

    @pl.when(b == 0)
    def _():
        r_io = lax.broadcasted_iota(jnp.int32, (bs, 2 * bs), 0)
        c_io = lax.broadcasted_iota(jnp.int32, (bs, 2 * bs), 1)
        for hh in range(hpb):
            for dd in range(nblk):
                t = pltpu.roll(jnp.broadcast_to(wv_ref[hh, dd:dd + 1, :], (bs, 2 * bs)), bs + 1, 1,
                               stride=1, stride_axis=0)
                if dd == 0:
                    t = jnp.where(c_io >= r_io, t, MOBA_NEG)
                bias_scr[dd, :, hh * bs:(hh + 1) * bs] = t[:, 0:bs]

    def prepare():
        kf = k_ref[0]
        km_scr[...] = jnp.mean(kf.reshape(nblk, bs, LANES), axis=1)
        kk_scr[:, 0:LANES] = kf.astype(BF16)
        kk_scr[:, LANES:] = hot_ref[...]
        for n in range(nblk):
            va_scr[n, 0:LANES, :] = vt_ref[0, :, n * bs:(n + 1) * bs]
            va_scr[n, LANES:, :] = jnp.ones((MOBA_ONES_ROWS, bs), BF16)
        seq = nblk * bs
        q2 = qt_ref[0] * (dh ** -0.5 * LOG2E)
        f_io = lax.broadcasted_iota(jnp.int32, (LANES, seq), 0)
        blk_io = lax.broadcasted_iota(jnp.int32, (nblk, seq), 0)
        own_q = lax.broadcasted_iota(jnp.int32, (nblk, seq), 1) // bs
        qms, pens = [], []
        for hh in range(hpb):
            qm = jnp.where((f_io >= hh * dh) & (f_io < (hh + 1) * dh), q2, 0.0)
            gate = jnp.dot(km_scr[...], qm, precision=HIGHEST, preferred_element_type=F32)
            cnt = jnp.zeros((nblk, seq), jnp.int32)
            for m in range(nblk):
                gm = gate[m:m + 1, :]
                beats = (gm > gate) | ((gm == gate) & (m < blk_io))
                cnt = cnt + jnp.where(beats & (m < own_q), 1, 0)
            keep = (blk_io >= own_q) | (cnt < topk)
            pens.append(jnp.where(keep, 0.0, MOBA_NEG))
            qms.append(qm.astype(BF16))
        for n in range(nblk):
            cols = slice(n * bs, (n + 1) * bs)
            qa_scr[n, 0:LANES, :] = jnp.concatenate([qm[:, cols] for qm in qms], axis=1)
            pen = jnp.concatenate([p[:, cols] for p in pens], axis=1)
            qa_scr[n, LANES:, :] = jnp.concatenate(
                [pen, jnp.zeros((LANES - nblk, hpb * bs), F32)], axis=0).astype(BF16)

    prepare()

    for own in range(nblk):
        qa = qa_scr[own]
        m = None
        for j in range(own + 1):
            s = (jnp.dot(kk_scr[j * bs:(j + 1) * bs, :], qa, preferred_element_type=F32)
                 + bias_scr[own - j])
            s_scr[j] = s
            mj = jnp.max(s, axis=0, keepdims=True)
            m = mj if m is None else jnp.maximum(m, mj)
        acc = None
        for j in range(own + 1):
            p = jnp.exp2(s_scr[j] - m).astype(BF16)
            t = jnp.dot(va_scr[j], p, preferred_element_type=F32)
            acc = t if acc is None else acc + t
        on = acc[0:LANES, :] / acc[LANES:LANES + 1, :]
        ot = jnp.concatenate([on[hh * dh:(hh + 1) * dh, hh * tq:(hh + 1) * tq] for hh in range(hpb)], axis=0)
        o_ref[0, own * bs:(own + 1) * bs, :] = ot.T


def _moba(qt, k, vt, rel_bias):
    B, L, W = k.shape
    dh, bs = MOBA_HEAD_DIM, MOBA_BLOCK
    H = W // dh
    nblk = L // bs
    hpb = LANES // dh
    dist = jnp.arange(L, dtype=jnp.int32)
    by_dist = rel_bias.astype(F32).T[:, _rel_bucket(dist)] * LOG2E
    idx = np.clip(np.arange(nblk)[:, None] * bs - (bs - 1) + np.arange(2 * bs)[None, :], 0, L - 1)
    vecs = by_dist[:, idx]
    hot = np.zeros((L, LANES), np.float32)
    hot[np.arange(L), np.arange(L) // bs] = 1.0
    return pl.pallas_call(
        functools.partial(_moba_kernel, nblk=nblk, bs=bs, dh=dh, topk=min(MOBA_TOPK, nblk)),
        grid=(H // hpb, B),
        in_specs=[pl.BlockSpec((1, LANES, L), lambda h, b: (b, h, 0)),
                  pl.BlockSpec((1, L, LANES), lambda h, b: (b, 0, h)),
                  pl.BlockSpec((1, LANES, L), lambda h, b: (b, h, 0)),
                  pl.BlockSpec((hpb, nblk, 2 * bs), lambda h, b: (h, 0, 0)),
                  pl.BlockSpec((L, LANES), lambda h, b: (0, 0))],
        out_specs=pl.BlockSpec((1, L, LANES), lambda h, b: (b, 0, h)),
        out_shape=jax.ShapeDtypeStruct((B, L, W), F32),
        scratch_shapes=[pltpu.VMEM((nblk, bs, hpb * bs), F32),
                        pltpu.VMEM((L, 2 * LANES), BF16),
                        pltpu.VMEM((nblk, LANES + MOBA_ONES_ROWS, bs), BF16),
                        pltpu.VMEM((nblk, LANES), F32),
                        pltpu.VMEM((nblk, 2 * LANES, hpb * bs), BF16),
                        pltpu.VMEM((nblk, bs, hpb * bs), F32)],
        compiler_params=_params(("arbitrary", "arbitrary")),
        name="moba_attention",
    )(qt, k, vt, vecs, jnp.asarray(hot, BF16))


def _route_top_k(xn, wrt_ref, br_ref):
    lt = lax.dot_general(wrt_ref[...], xn, (((1,), (1,)), ((), ())), precision=HIGHEST,
                         preferred_element_type=F32) + br_ref[...]
    n_e = lt.shape[0]
    eid = lax.broadcasted_iota(jnp.int32, lt.shape, 0)
    cur = lt
    vals, idxs = [], []
    for _ in range(TOP_K):
        m = jnp.max(cur, axis=0, keepdims=True)
        idx = jnp.min(jnp.where(cur == m, eid, n_e), axis=0, keepdims=True)
        vals.append(m)
        idxs.append(idx)
        cur = jnp.where(eid == idx, -jnp.inf, cur)
    ex = [jnp.exp(v - vals[0]) for v in vals]
    den = ex[0]
    for t in ex[1:]:
        den = den + t
    hot = jnp.zeros(lt.shape, F32)
    for idx in idxs:
        hot = hot + jnp.where(eid == idx, 1.0, 0.0)
    return (jnp.concatenate(idxs, axis=0), jnp.concatenate([t / den for t in ex], axis=0),
            jnp.sum(hot, axis=1, keepdims=True))


def _mix_ln_kernel(*refs, n_in):
    x_ref = refs[0]
    y_refs = refs[1:1 + n_in]
    w_refs = refs[1 + n_in:1 + 2 * n_in]
    g_ref, b_ref, wrt_ref, br_ref, x1_ref, et_ref, wt_ref, cnt_ref = refs[1 + 2 * n_in:]
    mix = None
    for y_ref, w_ref in zip(y_refs, w_refs):
        y = y_ref[...]
        y = y.reshape(y.shape[-2], y.shape[-1]).astype(BF16)
        t = jnp.dot(y, w_ref[...], preferred_element_type=F32)
        mix = t if mix is None else mix + t
    xn = _layer_norm(DN_ALPHA * x_ref[0] + mix, g_ref[...], b_ref[...])
    x1_ref[0] = xn
    for s in range(et_ref.shape[0]):
        et, wt, cnt = _route_top_k(xn[s * MOE_TILE:(s + 1) * MOE_TILE], wrt_ref, br_ref)
        et_ref[s] = et
        wt_ref[s] = wt
        cnt_ref[s] = cnt


def _mix_ln(x, ys, ws, g, b, wr, br):
    B, L, D = x.shape
    E = wr.shape[1]
    tl = MIX_TILE
    nl = L // tl
    sub = tl // MOE_TILE
    nt = B * L // MOE_TILE
    const = lambda bb, l: (0, 0)
    tile = lambda bb, l: (bb * nl + l, 0, 0)
    in_specs = [pl.BlockSpec((1, tl, D), lambda bb, l: (bb, l, 0))]
    in_specs += [spec for _, spec in ys]
    in_specs += [pl.BlockSpec(w.shape, const) for w in ws]
    in_specs += [pl.BlockSpec((1, D), const), pl.BlockSpec((1, D), const),
                 pl.BlockSpec((E, D), const), pl.BlockSpec((E, 1), const)]
    return pl.pallas_call(
        functools.partial(_mix_ln_kernel, n_in=len(ys)),
        grid=(B, nl),
        in_specs=in_specs,
        out_specs=[pl.BlockSpec((1, tl, D), lambda bb, l: (bb, l, 0)),
                   pl.BlockSpec((sub, TOP_K, MOE_TILE), tile), pl.BlockSpec((sub, TOP_K, MOE_TILE), tile),
                   pl.BlockSpec((sub, E, 1), tile)],
        out_shape=[jax.ShapeDtypeStruct((B, L, D), F32),
                   jax.ShapeDtypeStruct((nt, TOP_K, MOE_TILE), jnp.int32),
                   jax.ShapeDtypeStruct((nt, TOP_K, MOE_TILE), F32),
                   jax.ShapeDtypeStruct((nt, E, 1), F32)],
        compiler_params=_params(("parallel", "parallel")),
        name="outproj_ln_router",
    )(x, *[a for a, _ in ys], *ws, g.reshape(1, D), b.reshape(1, D), wr.astype(F32).T, br.reshape(E, 1))


def _deinterleave_kernel(w_ref, p_ref, o_ref, *, ff):
    w = w_ref[0, 0].astype(BF16)
    for c in range(2 * ff // (2 * LANES)):
        t = jnp.dot(w[:, 2 * LANES * c:2 * LANES * (c + 1)], p_ref[...], preferred_element_type=F32)
        o_ref[0, :, LANES * c:LANES * (c + 1)] = t[:, :LANES].astype(BF16)
        o_ref[0, :, ff + LANES * c:ff + LANES * (c + 1)] = t[:, LANES:].astype(BF16)


def _deinterleave_gate_up(w_gate_up, layer, tk=512):
    _, E, D, F2 = w_gate_up.shape
    src = np.arange(2 * LANES)
    dst = np.where(src % 2 == 0, src // 2, LANES + src // 2)
    perm = np.zeros((2 * LANES, 2 * LANES), np.float32)
    perm[src, dst] = 1.0
    return pl.pallas_call(
        functools.partial(_deinterleave_kernel, ff=F2 // 2),
        grid=(E, D // tk),
        in_specs=[pl.BlockSpec((1, 1, tk, F2), lambda e, k: (layer, e, k, 0)),
                  pl.BlockSpec((2 * LANES, 2 * LANES), lambda e, k: (0, 0))],
        out_specs=pl.BlockSpec((1, tk, F2), lambda e, k: (e, k, 0)),
        out_shape=jax.ShapeDtypeStruct((E, D, F2), BF16),
        compiler_params=_params(("parallel", "parallel")),
        name="deinterleave_gate_up",
    )(w_gate_up, jnp.asarray(perm, BF16))


def _expert_kernel(be_ref, nu_ref, xs_ref, wgu_ref, wd_ref, bgu_ref, bd_ref, ys_ref):
    blk = pl.program_id(0)
    ff = wd_ref.shape[2]
    d = wd_ref.shape[3]

    @pl.when(blk < nu_ref[0])
    def _():
        x = xs_ref[:, :d].astype(BF16)
        pw = xs_ref[:, d:d + 1]
        h = jnp.dot(x, wgu_ref[0], preferred_element_type=F32) + bgu_ref[0]
        g = h[:, :ff]
        u = h[:, ff:]
        g = jnp.minimum(g, SWIGLU_LIMIT)
        u = jnp.clip(u, -SWIGLU_LIMIT, SWIGLU_LIMIT)
        act = g * jax.nn.sigmoid(SWIGLU_ALPHA * g) * (u + 1.0)
        y = jnp.dot(act.astype(BF16), wd_ref[0, 0].astype(BF16), preferred_element_type=F32) + bd_ref[0]
        ys_ref[...] = y * pw

    @pl.when(blk >= nu_ref[0])
    def _():
        ys_ref[...] = jnp.zeros_like(ys_ref)


def _expert_ffn(xs, plan, wgu, w_down, layer, bgu, bd):
    P, DW = xs.shape
    F, D = w_down.shape[2], w_down.shape[3]
    bm = MOE_BLOCK
    wmap = lambda i, be, nu: (be[i], 0, 0)
    rmap = lambda i, be, nu: (i, 0)
    grid_spec = pltpu.PrefetchScalarGridSpec(
        num_scalar_prefetch=2,
        grid=(P // bm,),
        in_specs=[pl.BlockSpec((bm, DW), rmap),
                  pl.BlockSpec((1, D, 2 * F), wmap),
                  pl.BlockSpec((1, 1, F, D), lambda i, be, nu: (layer, be[i], 0, 0)),
                  pl.BlockSpec((1, 1, 2 * F), wmap), pl.BlockSpec((1, 1, D), wmap)],
        out_specs=pl.BlockSpec((bm, D), rmap),
    )
    return pl.pallas_call(
        _expert_kernel,
        grid_spec=grid_spec,
        out_shape=jax.ShapeDtypeStruct((P, D), F32),
        compiler_params=_params(("arbitrary",)),
        name="moe_experts",
    )(plan["blk_expert"], plan["n_used"], xs, wgu, w_down, bgu, bd)


def _moe_rows(T, nt, E):
    bound = T * TOP_K + (SUBLANES - 1) * E * nt + E * MOE_BLOCK
    return -(-bound // MOE_BLOCK) * MOE_BLOCK


def _stage_rows(tm):
    return TOP_K * tm + N_EXPERTS * SUBLANES


MOE_COPY_SIZES = (MOE_CHUNK,) + MOE_REMAINDERS
MOE_PACK = 256


def _piece_slots(tm):
    assert _stage_rows(tm) // SUBLANES <= MOE_PACK
    return (_stage_rows(tm) // MOE_CHUNK,) + (N_EXPERTS,) * len(MOE_REMAINDERS)


def _moe_plan(cnt, T):
    nt, E = cnt.shape[0], cnt.shape[1]
    c = cnt.reshape(nt, E).astype(jnp.int32)
    n8 = (c + SUBLANES - 1) // SUBLANES * SUBLANES
    tot = jnp.sum(n8, axis=0)
    seg = (tot + MOE_BLOCK - 1) // MOE_BLOCK * MOE_BLOCK
    seg_start = jnp.cumsum(seg) - seg
    strip = seg_start[None, :] + jnp.cumsum(n8, axis=0) - n8
    off = jnp.cumsum(n8, axis=1) - n8
    nblk = _moe_rows(T, nt, E) // MOE_BLOCK
    seg_blk = seg // MOE_BLOCK
    blk_end = jnp.cumsum(seg_blk)
    blk_ids = jnp.arange(nblk, dtype=jnp.int32)
    blk_expert = jnp.minimum(jnp.sum(blk_end[None, :] <= blk_ids[:, None], axis=1), E - 1).astype(jnp.int32)
    tail = jnp.stack([seg_start + tot, seg - tot], axis=1)
    per_class = [n8 // MOE_CHUNK] + [(n8 // r) % 2 for r in MOE_REMAINDERS]
    exact = jnp.stack([jnp.sum(p, axis=1) for p in per_class], axis=1)
    pieces = []
    done = jnp.zeros_like(n8)
    for p, rows, slots in zip(per_class, MOE_COPY_SIZES, _piece_slots(T // nt)):
        last = jnp.cumsum(p, axis=1)
        first = last - p
        k = jnp.arange(slots, dtype=jnp.int32)[None, :, None]
        mine = (first[:, None, :] <= k) & (k < last[:, None, :])
        inner = done[:, None, :] + (k - first[:, None, :]) * rows
        packed = ((strip[:, None, :] + inner) // SUBLANES * MOE_PACK + (off[:, None, :] + inner) // SUBLANES)
        pieces.append(jnp.sum(jnp.where(mine, packed, 0), axis=2))
        done = done + p * rows
    return dict(strip=strip.reshape(-1).astype(jnp.int32), off=off.reshape(-1).astype(jnp.int32),
                n8=n8.reshape(-1).astype(jnp.int32), exact=exact.reshape(-1).astype(jnp.int32),
                pieces=jnp.concatenate([jnp.concatenate(pieces, axis=1).reshape(-1).astype(jnp.int32),
                                        jnp.zeros((1,), jnp.int32)]),
                off_col=off.reshape(nt, E, 1).astype(jnp.int32), blk_expert=blk_expert,
                tail=tail.reshape(-1).astype(jnp.int32),
                n_used=blk_end[-1:].astype(jnp.int32))


def _for_each_listed_copy(pieces_ref, exact_ref, tile, tm, fn):
    slots = _piece_slots(tm)
    base = tile * sum(slots)
    for s, rows in enumerate(MOE_COPY_SIZES):
        start = base + sum(slots[:s])

        def one(k, packed, start=start, rows=rows):
            ahead = pieces_ref[start + k + 1]
            fn(lax.shift_right_logical(packed, MOE_PACK.bit_length() - 1) * SUBLANES,
               (packed & (MOE_PACK - 1)) * SUBLANES, rows)
            return ahead

        lax.fori_loop(0, exact_ref[tile * len(MOE_COPY_SIZES) + s], one, pieces_ref[start])


def _for_each_piece(n, fn):
    nfull = n // MOE_CHUNK

    def per_chunk(c, carry):
        fn(c * MOE_CHUNK, MOE_CHUNK)
        return carry

    lax.fori_loop(0, nfull, per_chunk, 0)
    done = nfull * MOE_CHUNK
    for r in MOE_REMAINDERS:
        has = (n // r) % 2

        @pl.when(has == 1)
        def _(done=done, r=r):
            fn(done, r)

        done = done + has * r


def _dispatch_kernel(pieces_ref, exact_ref, tail_ref, nu_ref, x_ref, et_ref, wt_ref, offc_ref,
                     xs_ref, lp_ref, stage, sel_scr, zero, sem, *, tm):
    i = pl.program_id(0)
    last = pl.num_programs(0) - 1
    n_e = N_EXPERTS
    d = x_ref.shape[1]
    nrow = _stage_rows(tm)
    slot = i % 2

    def strip_copy(src0, dst0, rows, sl):
        return pltpu.make_async_copy(stage.at[sl, pl.ds(pl.multiple_of(src0, SUBLANES), rows)],
                                     xs_ref.at[pl.ds(pl.multiple_of(dst0, SUBLANES), rows)], sem.at[sl])

    def wait_tile(tile, sl):
        for s, rows in enumerate(MOE_COPY_SIZES):
            def one(c, carry, rows=rows):
                strip_copy(0, 0, rows, sl).wait()
                return carry

            lax.fori_loop(0, exact_ref[tile * len(MOE_COPY_SIZES) + s], one, 0)

    et = et_ref[0]
    wt = wt_ref[0]
    eid = lax.broadcasted_iota(jnp.int32, (n_e, tm), 0)
    hots = [eid == et[k:k + 1, :] for k in range(TOP_K)]
    m_t = jnp.zeros((n_e, tm), F32)
    for h in hots:
        m_t = m_t + jnp.where(h, 1.0, 0.0)
    before = jnp.where(lax.broadcasted_iota(jnp.int32, (tm, tm), 0) < lax.broadcasted_iota(jnp.int32, (tm, tm), 1),
                       1.0, 0.0).astype(BF16)
    rank = jnp.dot(m_t.astype(BF16), before, preferred_element_type=F32)
    base = offc_ref[0].astype(F32) + rank
    lps = [jnp.sum(jnp.where(h, base, 0.0), axis=0, keepdims=True) for h in hots]
    lpi = [lp.astype(jnp.int32) for lp in lps]
    lp_ref[0] = jnp.concatenate(lpi, axis=0)

    wcols = []
    for rc in range(nrow // LANES):
        rio = lax.broadcasted_iota(jnp.int32, (LANES, tm), 0) + rc * LANES
        sel = jnp.zeros((LANES, tm), F32)
        wsel = jnp.zeros((LANES, tm), F32)
        for k in reversed(range(TOP_K)):
            hit = rio == lpi[k]
            sel = jnp.where(hit, 1.0, sel)
            wsel = jnp.where(hit, wt[k:k + 1, :], wsel)
        sel_scr[rc * LANES:(rc + 1) * LANES, :] = sel.astype(BF16)
        wcols.append(jnp.sum(wsel, axis=1, keepdims=True))

    @pl.when(i >= 2)
    def _():
        wait_tile(i - 2, slot)

    stage[slot, :, 0:d] = jnp.dot(sel_scr[...], x_ref[...].astype(BF16), preferred_element_type=F32)
    for rc, wcol in enumerate(wcols):
        stage[slot, rc * LANES:(rc + 1) * LANES, d:d + LANES] = jnp.broadcast_to(wcol, (LANES, LANES))

    _for_each_listed_copy(pieces_ref, exact_ref, i, tm,
                          lambda far, near, rows: strip_copy(near, far, rows, slot).start())

    def zero_fill(wait):
        def go(dst0, rows):
            cp = pltpu.make_async_copy(zero.at[pl.ds(0, rows)],
                                       xs_ref.at[pl.ds(pl.multiple_of(dst0, SUBLANES), rows)], sem.at[2])
            cp.wait() if wait else cp.start()

        def per_tail(e, carry):
            start = tail_ref[2 * e]
            _for_each_piece(tail_ref[2 * e + 1], lambda o, rows: go(start + o, rows))
            return carry

        lax.fori_loop(0, n_e, per_tail, 0)

        def per_block(blk, carry):
            go(blk * MOE_BLOCK, MOE_BLOCK)
            return carry

        lax.fori_loop(nu_ref[0], xs_ref.shape[0] // MOE_BLOCK, per_block, 0)

    @pl.when(i == last)
    def _():
        @pl.when(i >= 1)
        def _():
            wait_tile(i - 1, 1 - slot)

        wait_tile(i, slot)
        zero[...] = jnp.zeros_like(zero)
        zero_fill(wait=False)
        zero_fill(wait=True)


def _dispatch(x1, et, wt, plan):
    T, D = x1.shape
    nt = et.shape[0]
    tm = T // nt
    E = N_EXPERTS
    P = _moe_rows(T, nt, E)
    nrow = _stage_rows(tm)
    tile3 = lambda i, *_: (i, 0, 0)
    grid_spec = pltpu.PrefetchScalarGridSpec(
        num_scalar_prefetch=4,
        grid=(nt,),
        in_specs=[pl.BlockSpec((tm, D), lambda i, *_: (i, 0)),
                  pl.BlockSpec((1, TOP_K, tm), tile3), pl.BlockSpec((1, TOP_K, tm), tile3),
                  pl.BlockSpec((1, E, 1), tile3)],
        out_specs=[pl.BlockSpec(memory_space=pl.ANY), pl.BlockSpec((1, TOP_K, tm), tile3)],
        scratch_shapes=[pltpu.VMEM((2, nrow, D + LANES), F32),
                        pltpu.VMEM((nrow, tm), BF16),
                        pltpu.VMEM((MOE_BLOCK, D + LANES), F32),
                        pltpu.SemaphoreType.DMA((3,))],
    )
    return pl.pallas_call(
        functools.partial(_dispatch_kernel, tm=tm),
        grid_spec=grid_spec,
        out_shape=[jax.ShapeDtypeStruct((P, D + LANES), F32), jax.ShapeDtypeStruct((nt, TOP_K, tm), jnp.int32)],
        compiler_params=_params(("arbitrary",)),
        name="moe_dispatch",
    )(plan["pieces"], plan["exact"], plan["tail"], plan["n_used"], x1, et, wt, plan["off_col"])


def _combine_kernel(pieces_ref, exact_ref, ys_ref, lp_ref, x_ref, g_ref, b_ref, *rest, tm, kd):
    if kd:
        wqk_ref, wvg_ref, wgr_ref, w2_ref, b2_ref, xo_ref, qkl_ref, vg_ref, land, sel_scr, sem = rest
    else:
        xo_ref, land, sel_scr, sem = rest
    i = pl.program_id(0)
    n_e = N_EXPERTS
    nrow = _stage_rows(tm)
    slot = i % 2

    def strip_copy(src0, dst0, rows, sl):
        return pltpu.make_async_copy(ys_ref.at[pl.ds(pl.multiple_of(src0, SUBLANES), rows)],
                                     land.at[sl, pl.ds(pl.multiple_of(dst0, SUBLANES), rows)], sem.at[sl])

    def fetch(tile, sl):
        _for_each_listed_copy(pieces_ref, exact_ref, tile, tm,
                              lambda far, near, rows: strip_copy(far, near, rows, sl).start())

    @pl.when(i == 0)
    def _():
        land[...] = jnp.zeros_like(land)
        fetch(0, 0)

    @pl.when(i + 1 < pl.num_programs(0))
    def _():
        fetch(i + 1, 1 - slot)

    lp = lp_ref[0]
    for rc in range(nrow // LANES):
        rio = lax.broadcasted_iota(jnp.int32, (LANES, tm), 0) + rc * LANES
        piece = jnp.zeros((LANES, tm), F32)
        for k in range(TOP_K):
            piece = jnp.where(rio == lp[k:k + 1, :], 1.0, piece)
        sel_scr[rc * LANES:(rc + 1) * LANES, :] = piece.astype(BF16)

    for s, rows in enumerate(MOE_COPY_SIZES):
        def one(c, carry, rows=rows):
            strip_copy(0, 0, rows, slot).wait()
            return carry

        lax.fori_loop(0, exact_ref[i * len(MOE_COPY_SIZES) + s], one, 0)

    ffn = lax.dot_general(sel_scr[...], land[slot, 0:nrow, :].astype(BF16), (((0,), (0,)), ((), ())),
                          preferred_element_type=F32)
    xn = _layer_norm(DN_ALPHA * x_ref[...] + ffn, g_ref[...], b_ref[...])
    xo_ref[...] = xn
    if kd:
        xb = xn.astype(BF16)
        qkl_ref[:, :2 * kd] = jnp.dot(xb, wqk_ref[...], preferred_element_type=F32)
        vg_ref[...] = jnp.dot(xb, wvg_ref[...], preferred_element_type=F32)
        gr = jnp.dot(xb, wgr_ref[...], preferred_element_type=F32)
        z = jnp.dot(gr, w2_ref[...], precision=HIGHEST, preferred_element_type=F32) + b2_ref[...]
        log_sig = jnp.minimum(z, 0.0) - jnp.log1p(jnp.exp(-jnp.abs(z)))
        qkl_ref[:, 2 * kd:] = log_sig / GLA_GATE_TAU


def _combine_ln(ys, lp, x, plan, g, b, gla=None):
    T, D = x.shape
    tm = MOE_TILE
    nt = T // tm
    const = lambda i, *_: (0, 0)
    row = lambda i, *_: (i, 0)
    in_specs = [pl.BlockSpec(memory_space=pl.ANY), pl.BlockSpec((1, TOP_K, tm), lambda i, *_: (i, 0, 0)),
                pl.BlockSpec((tm, D), row),
                pl.BlockSpec((1, D), const), pl.BlockSpec((1, D), const)]
    out_specs = [pl.BlockSpec((tm, D), row)]
    out_shape = [jax.ShapeDtypeStruct((T, D), F32)]
    args = [ys, lp, x, g.reshape(1, D), b.reshape(1, D)]
    kd = 0
    if gla is not None:
        w_in1, w_gate2, b_gate, kd, vd = gla
        ws = [w_in1[:, :2 * kd].astype(BF16), w_in1[:, 2 * kd:2 * kd + 2 * vd].astype(BF16),
              w_in1[:, 2 * kd + 2 * vd:].astype(BF16), w_gate2.astype(F32), b_gate.reshape(1, kd)]
        in_specs += [pl.BlockSpec(w.shape, const) for w in ws]
        args += ws
        out_specs += [pl.BlockSpec((tm, 3 * kd), row), pl.BlockSpec((tm, 2 * vd), row)]
        out_shape += [jax.ShapeDtypeStruct((T, 3 * kd), F32), jax.ShapeDtypeStruct((T, 2 * vd), F32)]
    grid_spec = pltpu.PrefetchScalarGridSpec(
        num_scalar_prefetch=2,
        grid=(nt,),
        in_specs=in_specs,
        out_specs=out_specs,
        scratch_shapes=[pltpu.VMEM((2, _stage_rows(tm), D), F32), pltpu.VMEM((_stage_rows(tm), tm), BF16),
                        pltpu.SemaphoreType.DMA((2,))],
    )
    return pl.pallas_call(
        functools.partial(_combine_kernel, tm=tm, kd=kd),
        grid_spec=grid_spec,
        out_shape=out_shape,
        compiler_params=_params(("arbitrary",)),
        name="moe_combine_ln",
    )(plan["pieces"], plan["exact"], *args)


def _moe_ln(x1, et, wt, cnt, layer, w_gate_up, b_gate_up, w_down, b_down, g, b, gla=None):
    T = x1.shape[0]
    plan = _moe_plan(cnt, T)
    xs, lp = _dispatch(x1, et, wt, plan)
    wgu = _deinterleave_gate_up(w_gate_up, layer)
    bgu = jnp.concatenate([b_gate_up[layer, :, 0::2], b_gate_up[layer, :, 1::2]], axis=-1)[:, None, :].astype(F32)
    ys = _expert_ffn(xs, plan, wgu, w_down, layer, bgu, b_down[layer, :, None, :].astype(F32))
    return _combine_ln(ys, lp, x1, plan, g, b, gla)


def _gla_kernel(q_ref, k_ref, la_ref, v_ref, g_ref, nw_ref, o_ref, s_scr, *, tl, nh, dk, dv):
    C = GLA_CHUNK

    @pl.when(pl.program_id(1) == 0)
    def _():
        s_scr[...] = jnp.zeros_like(s_scr)

    ri = lax.broadcasted_iota(jnp.int32, (C, C), 0)
    ci = lax.broadcasted_iota(jnp.int32, (C, C), 1)
    lower = ri >= ci
    tri = jnp.where(lower, 1.0, 0.0).astype(BF16)
    states = [s_scr[h] for h in range(nh)]
    for c in range(tl // C):
        rows = slice(c * C, (c + 1) * C)
        rest = la_ref[0, rows, :]
        bcum = None
        for _ in range(3):
            piece = rest.astype(BF16)
            part = jnp.dot(tri, piece, preferred_element_type=F32)
            bcum = part if bcum is None else bcum + part
            rest = rest - piece.astype(F32)
        btot = bcum[C - 1:C, :]
        qb = (q_ref[0, rows, :] * (dk ** -0.5) * jnp.exp(bcum)).astype(BF16)
        kc = k_ref[0, rows, :]
        k_in = (kc * jnp.exp(-bcum)).astype(BF16)
        k_st = kc * jnp.exp(btot - bcum)
        dec = jnp.exp(btot)
        for h in range(nh):
            ks, vs = slice(h * dk, (h + 1) * dk), slice(h * dv, (h + 1) * dv)
            vb = v_ref[0, rows, vs].astype(BF16)
            att = lax.dot_general(qb[:, ks], k_in[:, ks], (((1,), (1,)), ((), ())), preferred_element_type=F32)
            att = jnp.where(lower, att, 0.0)
            o = (jnp.dot(att.astype(BF16), vb, preferred_element_type=F32)
                 + jnp.dot(qb[:, ks], states[h].astype(BF16), preferred_element_type=F32))
            kv = lax.dot_general(k_st[:, ks].astype(BF16), vb, (((0,), (0,)), ((), ())),
                                 preferred_element_type=F32)
            decay = jnp.broadcast_to(dec[:, ks], (SUBLANES, dk)).T[:, 0:1]
            states[h] = decay * states[h] + kv
            o = o * lax.rsqrt(jnp.mean(o * o, axis=-1, keepdims=True) + RMS_EPS) * nw_ref[...]
            gg = g_ref[0, rows, vs]
            o_ref[0, rows, vs] = o * (gg * jax.nn.sigmoid(gg))
    for h in range(nh):
        s_scr[h] = states[h]


def _gla(qkl, vg, norm_w, B, L, tl=256):
    T, kd3 = qkl.shape
    kd = kd3 // 3
    vd = vg.shape[1] // 2
    H = GLA_HEADS
    dk, dv = kd // H, vd // H
    qkl3 = qkl.reshape(B, L, kd3)
    vg3 = vg.reshape(B, L, 2 * vd)
    return pl.pallas_call(
        functools.partial(_gla_kernel, tl=tl, nh=H, dk=dk, dv=dv),
        grid=(B, L // tl),
        in_specs=[pl.BlockSpec((1, tl, kd), lambda b, l: (b, l, 0)),
                  pl.BlockSpec((1, tl, kd), lambda b, l: (b, l, 1)),
                  pl.BlockSpec((1, tl, kd), lambda b, l: (b, l, 2)),
                  pl.BlockSpec((1, tl, vd), lambda b, l: (b, l, 0)),
                  pl.BlockSpec((1, tl, vd), lambda b, l: (b, l, 1)),
                  pl.BlockSpec((1, dv), lambda b, l: (0, 0))],
        out_specs=pl.BlockSpec((1, tl, vd), lambda b, l: (b, l, 0)),
        out_shape=jax.ShapeDtypeStruct((B, L, vd), F32),
        scratch_shapes=[pltpu.VMEM((H, dk, dv), F32)],
        compiler_params=_params(("parallel", "arbitrary")),
        name="gla_mixer",
    )(qkl3, qkl3, qkl3, vg3, vg3, norm_w.reshape(1, dv).astype(F32))


def kernel(x, w_in0, s5_lam_re, s5_lam_im, s5_log_dt, s5_b_re, s5_b_im, s5_c_re, s5_c_im, s5_d, s5_w_glu,
           s5_b_glu, rel_bias, w_out0, w_in1, gla_w_gate2, gla_b_gate, gla_norm_w, w_out1, ln_mix_g, ln_mix_b,
           ln_ffn_g, ln_ffn_b, router_w, router_b, exp_w_gate_up, exp_b_gate_up, exp_w_down, exp_b_down):
    B, L, D = x.shape
    T = B * L
    s5w = s5_w_glu.shape[-1]
    kd = gla_w_gate2.shape[-1]
    vd = gla_norm_w.shape[-1] * GLA_HEADS

    u_tm, k_att, qt_att, vt_att = _inproj0(x, w_in0[0], s5w)
    y_a = _s5_mixer(u_tm.reshape(L, B, s5w), s5_lam_re[0], s5_lam_im[0], s5_log_dt[0], s5_b_re[0], s5_b_im[0],
                    s5_c_re[0], s5_c_im[0], s5_d[0].reshape(-1), s5_w_glu[0], s5_b_glu[0])
    y_b = _moba(qt_att, k_att, vt_att, rel_bias)
    tl = MIX_TILE
    w0 = w_out0[0].astype(BF16)
    ys = [(y_a.reshape(L, B * s5w), pl.BlockSpec((tl, s5w), lambda bb, l: (l, bb))),
          (y_b, pl.BlockSpec((1, tl, D - s5w), lambda bb, l: (bb, l, 0)))]
    x1, et, wt, cnt = _mix_ln(x, ys, [w0[:s5w], w0[s5w:]], ln_mix_g[0], ln_mix_b[0], router_w[0], router_b[0])
    x2, qkl, vg = _moe_ln(x1.reshape(T, D), et, wt, cnt, 0, exp_w_gate_up, exp_b_gate_up, exp_w_down,
                          exp_b_down, ln_ffn_g[0], ln_ffn_b[0],
                          gla=(w_in1[0], gla_w_gate2[0], gla_b_gate[0], kd, vd))

    y_c = _gla(qkl, vg, gla_norm_w[0], B, L)
    ys = [(y_c, pl.BlockSpec((1, tl, vd), lambda bb, l: (bb, l, 0)))]
    x3, et, wt, cnt = _mix_ln(x2.reshape(B, L, D), ys, [w_out1[0].astype(BF16)], ln_mix_g[1], ln_mix_b[1],
                              router_w[1], router_b[1])
    (out,) = _moe_ln(x3.reshape(T, D), et, wt, cnt, 1, exp_w_gate_up, exp_b_gate_up, exp_w_down,
                     exp_b_down, ln_ffn_g[1], ln_ffn_b[1])
    return out.reshape(B, L, D)
```

```python
import functools
import math

import jax
import jax.numpy as jnp
import numpy as np
from jax import lax
from jax.experimental import pallas as pl
from jax.experimental.pallas import tpu as pltpu

F32 = jnp.float32
BF16 = jnp.bfloat16
HIGHEST = lax.Precision.HIGHEST

DEPTH = 2
S5_GROUP = 16
S5_STATE = 64
MOBA_HEAD_DIM = 64
MOBA_BLOCK = 256
MOBA_TOPK = 3
REL_BUCKETS = 32
REL_MAX_DIST = 2048
GLA_HEADS = 4
GLA_GATE_TAU = 16.0
GLA_CHUNK = 64
N_EXPERTS = 32
TOP_K = 4
SWIGLU_LIMIT = 7.0
SWIGLU_ALPHA = 1.702
MOE_BLOCK = 512
MOE_TILE = 256
MIX_TILE = 1024
MOE_CHUNK = 32
SUBLANES = 8
MOE_REMAINDERS = (16, 8)
DN_ALPHA = (2 * DEPTH) ** 0.25
LN_EPS = 1e-5
RMS_EPS = 1e-5

V7X_VMEM_LIMIT_BYTES = 56 * 1024 * 1024
LANES = 128

S5_GROUPS_PER_CHUNK = LANES // S5_GROUP
S5_CHUNK_STATES = S5_GROUPS_PER_CHUNK * S5_STATE


def _params(sem):
    return pltpu.CompilerParams(dimension_semantics=sem, vmem_limit_bytes=V7X_VMEM_LIMIT_BYTES)


def _layer_norm(r, g, b):
    mu = jnp.mean(r, axis=-1, keepdims=True)
    c = r - mu
    var = jnp.mean(c * c, axis=-1, keepdims=True)
    return c * lax.rsqrt(var + LN_EPS) * g + b


def _inproj0_kernel(x_ref, wuk_ref, wqvt_ref, u_ref, k_ref, qt_ref, vt_ref, *, s5w):
    xb = x_ref[0].astype(BF16)
    h = jnp.dot(xb, wuk_ref[...], preferred_element_type=F32)
    u_ref[...] = h[:, :s5w]
    k_ref[0] = h[:, s5w:]
    ht = lax.dot_general(wqvt_ref[...], xb, (((1,), (1,)), ((), ())), preferred_element_type=F32)
    aw = ht.shape[0] // 2
    qt_ref[0] = ht[:aw]
    vt_ref[0] = ht[aw:].astype(BF16)


def _inproj0(x, w_in0, s5w, tl=512):
    B, L, D = x.shape
    aw = (w_in0.shape[1] - s5w) // 3
    wb = w_in0.astype(BF16)
    wuk = jnp.concatenate([wb[:, :s5w], wb[:, s5w + aw:s5w + 2 * aw]], axis=1)
    wqvt = jnp.concatenate([wb[:, s5w:s5w + aw], wb[:, s5w + 2 * aw:]], axis=1).T
    return pl.pallas_call(
        functools.partial(_inproj0_kernel, s5w=s5w),
        grid=(B, L // tl),
        in_specs=[pl.BlockSpec((1, tl, D), lambda b, l: (b, l, 0)),
                  pl.BlockSpec(wuk.shape, lambda b, l: (0, 0)),
                  pl.BlockSpec(wqvt.shape, lambda b, l: (0, 0))],
        out_specs=[pl.BlockSpec((tl, s5w), lambda b, l: (l, b)),
                   pl.BlockSpec((1, tl, aw), lambda b, l: (b, l, 0)),
                   pl.BlockSpec((1, aw, tl), lambda b, l: (b, 0, l)),
                   pl.BlockSpec((1, aw, tl), lambda b, l: (b, 0, l))],
        out_shape=[jax.ShapeDtypeStruct((L, B * s5w), F32),
                   jax.ShapeDtypeStruct((B, L, aw), F32),
                   jax.ShapeDtypeStruct((B, aw, L), F32),
                   jax.ShapeDtypeStruct((B, aw, L), BF16)],
        compiler_params=_params(("parallel", "parallel")),
        name="inproj0",
    )(x, wuk, wqvt)


def _s5_discretize(lam_re, lam_im, log_dt, b_re, b_im):
    dt = jnp.exp(log_dt.astype(F32))[:, None]
    lr, li = lam_re.astype(F32), lam_im.astype(F32)
    mag = jnp.exp(lr * dt)
    ab_re, ab_im = mag * jnp.cos(li * dt), mag * jnp.sin(li * dt)
    er, ei = ab_re - 1.0, ab_im
    den = lr * lr + li * li
    q_re = (er * lr + ei * li) / den
    q_im = (ei * lr - er * li) / den
    br_, bi_ = b_re.astype(F32), b_im.astype(F32)
    bb_re = q_re[..., None] * br_ - q_im[..., None] * bi_
    bb_im = q_re[..., None] * bi_ + q_im[..., None] * br_
    return ab_re, ab_im, bb_re, bb_im


def _s5_kernel(u_ref, bm_ref, cm_ref, are_ref, aim_ref, d_ref, wg_ref, bg_ref, y_ref,
               s_scr, st_scr, z_scr, *, tl, nb, nchunk):
    ns = S5_CHUNK_STATES

    @pl.when(pl.program_id(0) == 0)
    def _():
        st_scr[...] = jnp.zeros_like(st_scr)

    u = u_ref[...].reshape(tl * nb, nchunk * LANES)
    for j in range(nchunk):
        uj = u[:, j * LANES:(j + 1) * LANES]
        s_scr[...] = jnp.dot(uj.astype(BF16), bm_ref[j], preferred_element_type=F32)
        ar = jnp.broadcast_to(are_ref[j], (nb, ns))
        ai = jnp.broadcast_to(aim_ref[j], (nb, ns))

        def step(t, carry, ar=ar, ai=ai):
            sre, sim = carry
            r0 = pl.multiple_of(t * nb, nb)
            nre = ar * sre - ai * sim + s_scr[pl.ds(r0, nb), 0:ns]
            nim = ar * sim + ai * sre + s_scr[pl.ds(r0, nb), ns:2 * ns]
            s_scr[pl.ds(r0, nb), 0:ns] = nre
            s_scr[pl.ds(r0, nb), ns:2 * ns] = nim
            return nre, nim

        sre, sim = lax.fori_loop(0, tl, step, (st_scr[j, :, 0:ns], st_scr[j, :, ns:2 * ns]), unroll=2)
        st_scr[j, :, 0:ns] = sre
        st_scr[j, :, ns:2 * ns] = sim
        yj = jnp.dot(s_scr[...].astype(BF16), cm_ref[j], preferred_element_type=F32)
        z_scr[:, j * LANES:(j + 1) * LANES] = yj + uj * d_ref[:, j * LANES:(j + 1) * LANES]
    z = jax.nn.gelu(z_scr[...])
    gate = jax.nn.sigmoid(jnp.dot(z.astype(BF16), wg_ref[...], preferred_element_type=F32) + bg_ref[...])
    y_ref[...] = (z * gate).reshape(tl, nb, nchunk * LANES)


def _s5_mixer(u_tm, lam_re, lam_im, log_dt, b_re, b_im, c_re, c_im, d_skip, w_glu, b_glu, tl=64):
    L, B, W = u_tm.shape
    G, P, H = lam_re.shape[0], S5_STATE, S5_GROUP
    gc = S5_GROUPS_PER_CHUNK
    nchunk = G // gc
    ab_re, ab_im, bb_re, bb_im = _s5_discretize(lam_re, lam_im, log_dt, b_re, b_im)
    eye = jnp.eye(gc, dtype=F32)

    def b_blocks(bb):
        return jnp.einsum('jgph,gk->jghkp', bb.reshape(nchunk, gc, P, H), eye).reshape(nchunk, gc * H, gc * P)

    def c_blocks(cc):
        return jnp.einsum('jghp,gk->jgpkh', cc.reshape(nchunk, gc, H, P), eye).reshape(nchunk, gc * P, gc * H)

    bm = jnp.concatenate([b_blocks(bb_re), b_blocks(bb_im)], axis=2).astype(BF16)
    cm = jnp.concatenate([c_blocks(c_re.astype(F32)), -c_blocks(c_im.astype(F32))], axis=1).astype(BF16)
    are = ab_re.reshape(nchunk, 1, gc * P)
    aim = ab_im.reshape(nchunk, 1, gc * P)
    m = tl * B
    const3 = lambda l: (0, 0, 0)
    const2 = lambda l: (0, 0)
    return pl.pallas_call(
        functools.partial(_s5_kernel, tl=tl, nb=B, nchunk=nchunk),
        grid=(L // tl,),
        in_specs=[pl.BlockSpec((tl, B, W), lambda l: (l, 0, 0)),
                  pl.BlockSpec(bm.shape, const3), pl.BlockSpec(cm.shape, const3),
                  pl.BlockSpec(are.shape, const3), pl.BlockSpec(aim.shape, const3),
                  pl.BlockSpec((1, W), const2), pl.BlockSpec((W, W), const2), pl.BlockSpec((1, W), const2)],
        out_specs=pl.BlockSpec((tl, B, W), lambda l: (l, 0, 0)),
        out_shape=jax.ShapeDtypeStruct((L, B, W), F32),
        scratch_shapes=[pltpu.VMEM((m, 2 * S5_CHUNK_STATES), F32),
                        pltpu.VMEM((nchunk, B, 2 * S5_CHUNK_STATES), F32),
                        pltpu.VMEM((m, W), F32)],
        compiler_params=_params(("arbitrary",)),
        name="s5_mixer",
    )(u_tm, bm, cm, are, aim, d_skip.reshape(1, W).astype(F32), w_glu.astype(BF16),
      b_glu.reshape(1, W).astype(F32))


def _rel_bucket(n):
    n = jnp.maximum(n, 0)
    max_exact = REL_BUCKETS // 2
    nf = jnp.maximum(n, 1).astype(F32)
    large = max_exact + (jnp.log(nf / max_exact) / math.log(REL_MAX_DIST / max_exact)
                         * (REL_BUCKETS - max_exact)).astype(jnp.int32)
    large = jnp.minimum(large, REL_BUCKETS - 1)
    return jnp.where(n < max_exact, n, large)


MOBA_ONES_ROWS = 16
MOBA_NEG = -1e30
LOG2E = math.log2(math.e)


def _moba_kernel(qt_ref, k_ref, vt_ref, wv_ref, hot_ref, o_ref, bias_scr, kk_scr, va_scr, km_scr, qa_scr, s_scr,
                 *, nblk, bs, dh, topk):
    b = pl.program_id(1)
    tq = bs
    hpb = LANES // dh

    @pl.when(b == 0)
    def _():
        r_io = lax.broadcasted_iota(jnp.int32, (bs, 2 * bs), 0)
        c_io = lax.broadcasted_iota(jnp.int32, (bs, 2 * bs), 1)
        for hh in range(hpb):
            for dd in range(nblk):
                t = pltpu.roll(jnp.broadcast_to(wv_ref[hh, dd:dd + 1, :], (bs, 2 * bs)), bs + 1, 1,
                               stride=1, stride_axis=0)
                if dd == 0:
                    t = jnp.where(c_io >= r_io, t, MOBA_NEG)
                bias_scr[dd, :, hh * bs:(hh + 1) * bs] = t[:, 0:bs]

    def prepare():
        kf = k_ref[0]
        km_scr[...] = jnp.mean(kf.reshape(nblk, bs, LANES), axis=1)
        kk_scr[:, 0:LANES] = kf.astype(BF16)
        kk_scr[:, LANES:] = hot_ref[...]
        for n in range(nblk):
            va_scr[n, 0:LANES, :] = vt_ref[0, :, n * bs:(n + 1) * bs]
            va_scr[n, LANES:, :] = jnp.ones((MOBA_ONES_ROWS, bs), BF16)
        seq = nblk * bs
        q2 = qt_ref[0] * (dh ** -0.5 * LOG2E)
        f_io = lax.broadcasted_iota(jnp.int32, (LANES, seq), 0)
        blk_io = lax.broadcasted_iota(jnp.int32, (nblk, seq), 0)
        own_q = lax.broadcasted_iota(jnp.int32, (nblk, seq), 1) // bs
        qms, pens = [], []
        for hh in range(hpb):
            qm = jnp.where((f_io >= hh * dh) & (f_io < (hh + 1) * dh), q2, 0.0)
            gate = jnp.dot(km_scr[...], qm, precision=HIGHEST, preferred_element_type=F32)
            cnt = jnp.zeros((nblk, seq), jnp.int32)
            for m in range(nblk):
                gm = gate[m:m + 1, :]
                beats = (gm > gate) | ((gm == gate) & (m < blk_io))
                cnt = cnt + jnp.where(beats & (m < own_q), 1, 0)
            keep = (blk_io >= own_q) | (cnt < topk)
            pens.append(jnp.where(keep, 0.0, MOBA_NEG))
            qms.append(qm.astype(BF16))
        for n in range(nblk):
            cols = slice(n * bs, (n + 1) * bs)
            qa_scr[n, 0:LANES, :] = jnp.concatenate([qm[:, cols] for qm in qms], axis=1)
            pen = jnp.concatenate([p[:, cols] for p in pens], axis=1)
            qa_scr[n, LANES:, :] = jnp.concatenate(
                [pen, jnp.zeros((LANES - nblk, hpb * bs), F32)], axis=0).astype(BF16)

    prepare()

    for own in range(nblk):
        qa = qa_scr[own]
        m = None
        for j in range(own + 1):
            s = (jnp.dot(kk_scr[j * bs:(j + 1) * bs, :], qa, preferred_element_type=F32)
                 + bias_scr[own - j])
            s_scr[j] = s
            mj = jnp.max(s, axis=0, keepdims=True)
            m = mj if m is None else jnp.maximum(m, mj)
        acc = None
        for j in range(own + 1):
            p = jnp.exp2(s_scr[j] - m).astype(BF16)
            t = jnp.dot(va_scr[j], p, preferred_element_type=F32)
            acc = t if acc is None else acc + t
        on = acc[0:LANES, :] / acc[LANES:LANES + 1, :]
        ot = jnp.concatenate([on[hh * dh:(hh + 1) * dh, hh * tq:(hh + 1) * tq] for hh in range(hpb)], axis=0)
        o_ref[0, own * bs:(own + 1) * bs, :] = ot.T


def _moba(qt, k, vt, rel_bias):
    B, L, W = k.shape
    dh, bs = MOBA_HEAD_DIM, MOBA_BLOCK
    H = W // dh
    nblk = L // bs
    hpb = LANES // dh
    dist = jnp.arange(L, dtype=jnp.int32)
    by_dist = rel_bias.astype(F32).T[:, _rel_bucket(dist)] * LOG2E
    idx = np.clip(np.arange(nblk)[:, None] * bs - (bs - 1) + np.arange(2 * bs)[None, :], 0, L - 1)
    vecs = by_dist[:, idx]
    hot = np.zeros((L, LANES), np.float32)
    hot[np.arange(L), np.arange(L) // bs] = 1.0
    return pl.pallas_call(
        functools.partial(_moba_kernel, nblk=nblk, bs=bs, dh=dh, topk=min(MOBA_TOPK, nblk)),
        grid=(H // hpb, B),
        in_specs=[pl.BlockSpec((1, LANES, L), lambda h, b: (b, h, 0)),
                  pl.BlockSpec((1, L, LANES), lambda h, b: (b, 0, h)),
                  pl.BlockSpec((1, LANES, L), lambda h, b: (b, h, 0)),
                  pl.BlockSpec((hpb, nblk, 2 * bs), lambda h, b: (h, 0, 0)),
                  pl.BlockSpec((L, LANES), lambda h, b: (0, 0))],
        out_specs=pl.BlockSpec((1, L, LANES), lambda h, b: (b, 0, h)),
        out_shape=jax.ShapeDtypeStruct((B, L, W), F32),
        scratch_shapes=[pltpu.VMEM((nblk, bs, hpb * bs), F32),
                        pltpu.VMEM((L, 2 * LANES), BF16),
                        pltpu.VMEM((nblk, LANES + MOBA_ONES_ROWS, bs), BF16),
                        pltpu.VMEM((nblk, LANES), F32),
                        pltpu.VMEM((nblk, 2 * LANES, hpb * bs), BF16),
                        pltpu.VMEM((nblk, bs, hpb * bs), F32)],
        compiler_params=_params(("arbitrary", "arbitrary")),
        name="moba_attention",
    )(qt, k, vt, vecs, jnp.asarray(hot, BF16))


def _route_top_k(xn, wrt_ref, br_ref):
    lt = lax.dot_general(wrt_ref[...], xn, (((1,), (1,)), ((), ())), precision=HIGHEST,
                         preferred_element_type=F32) + br_ref[...]
    n_e = lt.shape[0]
    eid = lax.broadcasted_iota(jnp.int32, lt.shape, 0)
    cur = lt
    vals, idxs = [], []
    for _ in range(TOP_K):
        m = jnp.max(cur, axis=0, keepdims=True)
        idx = jnp.min(jnp.where(cur == m, eid, n_e), axis=0, keepdims=True)
        vals.append(m)
        idxs.append(idx)
        cur = jnp.where(eid == idx, -jnp.inf, cur)
    ex = [jnp.exp(v - vals[0]) for v in vals]
    den = ex[0]
    for t in ex[1:]:
        den = den + t
    hot = jnp.zeros(lt.shape, F32)
    for idx in idxs:
        hot = hot + jnp.where(eid == idx, 1.0, 0.0)
    return (jnp.concatenate(idxs, axis=0), jnp.concatenate([t / den for t in ex], axis=0),
            jnp.sum(hot, axis=1, keepdims=True))


def _mix_ln_kernel(*refs, n_in):
    x_ref = refs[0]
    y_refs = refs[1:1 + n_in]
    w_refs = refs[1 + n_in:1 + 2 * n_in]
    g_ref, b_ref, wrt_ref, br_ref, x1_ref, et_ref, wt_ref, cnt_ref = refs[1 + 2 * n_in:]
    mix = None
    for y_ref, w_ref in zip(y_refs, w_refs):
        y = y_ref[...]
        y = y.reshape(y.shape[-2], y.shape[-1]).astype(BF16)
        t = jnp.dot(y, w_ref[...], preferred_element_type=F32)
        mix = t if mix is None else mix + t
    xn = _layer_norm(DN_ALPHA * x_ref[0] + mix, g_ref[...], b_ref[...])
    x1_ref[0] = xn
    for s in range(et_ref.shape[0]):
        et, wt, cnt = _route_top_k(xn[s * MOE_TILE:(s + 1) * MOE_TILE], wrt_ref, br_ref)
        et_ref[s] = et
        wt_ref[s] = wt
        cnt_ref[s] = cnt


def _mix_ln(x, ys, ws, g, b, wr, br):
    B, L, D = x.shape
    E = wr.shape[1]
    tl = MIX_TILE
    nl = L // tl
    sub = tl // MOE_TILE
    nt = B * L // MOE_TILE
    const = lambda bb, l: (0, 0)
    tile = lambda bb, l: (bb * nl + l, 0, 0)
    in_specs = [pl.BlockSpec((1, tl, D), lambda bb, l: (bb, l, 0))]
    in_specs += [spec for _, spec in ys]
    in_specs += [pl.BlockSpec(w.shape, const) for w in ws]
    in_specs += [pl.BlockSpec((1, D), const), pl.BlockSpec((1, D), const),
                 pl.BlockSpec((E, D), const), pl.BlockSpec((E, 1), const)]
    return pl.pallas_call(
        functools.partial(_mix_ln_kernel, n_in=len(ys)),
        grid=(B, nl),
        in_specs=in_specs,
        out_specs=[pl.BlockSpec((1, tl, D), lambda bb, l: (bb, l, 0)),
                   pl.BlockSpec((sub, TOP_K, MOE_TILE), tile), pl.BlockSpec((sub, TOP_K, MOE_TILE), tile),
                   pl.BlockSpec((sub, E, 1), tile)],
        out_shape=[jax.ShapeDtypeStruct((B, L, D), F32),
                   jax.ShapeDtypeStruct((nt, TOP_K, MOE_TILE), jnp.int32),
                   jax.ShapeDtypeStruct((nt, TOP_K, MOE_TILE), F32),
                   jax.ShapeDtypeStruct((nt, E, 1), F32)],
        compiler_params=_params(("parallel", "parallel")),
        name="outproj_ln_router",
    )(x, *[a for a, _ in ys], *ws, g.reshape(1, D), b.reshape(1, D), wr.astype(F32).T, br.reshape(E, 1))


def _deinterleave_kernel(w_ref, p_ref, o_ref, *, ff):
    w = w_ref[0, 0].astype(BF16)
    for c in range(2 * ff // (2 * LANES)):
        t = jnp.dot(w[:, 2 * LANES * c:2 * LANES * (c + 1)], p_ref[...], preferred_element_type=F32)
        o_ref[0, :, LANES * c:LANES * (c + 1)] = t[:, :LANES].astype(BF16)
        o_ref[0, :, ff + LANES * c:ff + LANES * (c + 1)] = t[:, LANES:].astype(BF16)


def _deinterleave_gate_up(w_gate_up, layer, tk=512):
    _, E, D, F2 = w_gate_up.shape
    src = np.arange(2 * LANES)
    dst = np.where(src % 2 == 0, src // 2, LANES + src // 2)
    perm = np.zeros((2 * LANES, 2 * LANES), np.float32)
    perm[src, dst] = 1.0
    return pl.pallas_call(
        functools.partial(_deinterleave_kernel, ff=F2 // 2),
        grid=(E, D // tk),
        in_specs=[pl.BlockSpec((1, 1, tk, F2), lambda e, k: (layer, e, k, 0)),
                  pl.BlockSpec((2 * LANES, 2 * LANES), lambda e, k: (0, 0))],
        out_specs=pl.BlockSpec((1, tk, F2), lambda e, k: (e, k, 0)),
        out_shape=jax.ShapeDtypeStruct((E, D, F2), BF16),
        compiler_params=_params(("parallel", "parallel")),
        name="deinterleave_gate_up",
    )(w_gate_up, jnp.asarray(perm, BF16))


def _expert_kernel(be_ref, nu_ref, xs_ref, wgu_ref, wd_ref, bgu_ref, bd_ref, ys_ref):
    blk = pl.program_id(0)
    ff = wd_ref.shape[2]
    d = wd_ref.shape[3]

    @pl.when(blk < nu_ref[0])
    def _():
        x = xs_ref[:, :d].astype(BF16)
        pw = xs_ref[:, d:d + 1]
        h = jnp.dot(x, wgu_ref[0], preferred_element_type=F32) + bgu_ref[0]
        g = h[:, :ff]
        u = h[:, ff:]
        g = jnp.minimum(g, SWIGLU_LIMIT)
        u = jnp.clip(u, -SWIGLU_LIMIT, SWIGLU_LIMIT)
        act = g * jax.nn.sigmoid(SWIGLU_ALPHA * g) * (u + 1.0)
        y = jnp.dot(act.astype(BF16), wd_ref[0, 0].astype(BF16), preferred_element_type=F32) + bd_ref[0]
        ys_ref[...] = y * pw

    @pl.when(blk >= nu_ref[0])
    def _():
        ys_ref[...] = jnp.zeros_like(ys_ref)


def _expert_ffn(xs, plan, wgu, w_down, layer, bgu, bd):
    P, DW = xs.shape
    F, D = w_down.shape[2], w_down.shape[3]
    bm = MOE_BLOCK
    wmap = lambda i, be, nu: (be[i], 0, 0)
    rmap = lambda i, be, nu: (i, 0)
    xmap = lambda i, be, nu: (jnp.minimum(i, nu[0] - 1), 0)
    grid_spec = pltpu.PrefetchScalarGridSpec(
        num_scalar_prefetch=2,
        grid=(P // bm,),
        in_specs=[pl.BlockSpec((bm, DW), xmap),
                  pl.BlockSpec((1, D, 2 * F), wmap),
                  pl.BlockSpec((1, 1, F, D), lambda i, be, nu: (layer, be[i], 0, 0)),
                  pl.BlockSpec((1, 1, 2 * F), wmap), pl.BlockSpec((1, 1, D), wmap)],
        out_specs=pl.BlockSpec((bm, D), rmap),
    )
    return pl.pallas_call(
        _expert_kernel,
        grid_spec=grid_spec,
        out_shape=jax.ShapeDtypeStruct((P, D), F32),
        compiler_params=_params(("arbitrary",)),
        name="moe_experts",
    )(plan["blk_expert"], plan["n_used"], xs, wgu, w_down, bgu, bd)


def _moe_rows(T, nt, E):
    bound = T * TOP_K + (SUBLANES - 1) * E * nt + E * MOE_BLOCK
    return -(-bound // MOE_BLOCK) * MOE_BLOCK


def _stage_rows(tm):
    return TOP_K * tm + N_EXPERTS * SUBLANES


MOE_COPY_SIZES = (MOE_CHUNK,) + MOE_REMAINDERS
MOE_PACK = 256


def _piece_slots(tm):
    assert _stage_rows(tm) // SUBLANES <= MOE_PACK
    return (_stage_rows(tm) // MOE_CHUNK,) + (N_EXPERTS,) * len(MOE_REMAINDERS)


def _moe_plan(cnt, T):
    nt, E = cnt.shape[0], cnt.shape[1]
    c = cnt.reshape(nt, E).astype(jnp.int32)
    n8 = (c + SUBLANES - 1) // SUBLANES * SUBLANES
    tot = jnp.sum(n8, axis=0)
    seg = (tot + MOE_BLOCK - 1) // MOE_BLOCK * MOE_BLOCK
    seg_start = jnp.cumsum(seg) - seg
    strip = seg_start[None, :] + jnp.cumsum(n8, axis=0) - n8
    off = jnp.cumsum(n8, axis=1) - n8
    nblk = _moe_rows(T, nt, E) // MOE_BLOCK
    seg_blk = seg // MOE_BLOCK
    blk_end = jnp.cumsum(seg_blk)
    blk_ids = jnp.arange(nblk, dtype=jnp.int32)
    blk_expert = jnp.minimum(jnp.sum(blk_end[None, :] <= blk_ids[:, None], axis=1), E - 1).astype(jnp.int32)
    tail = jnp.stack([seg_start + tot, seg - tot], axis=1)
    per_class = [n8 // MOE_CHUNK] + [(n8 // r) % 2 for r in MOE_REMAINDERS]
    exact = jnp.stack([jnp.sum(p, axis=1) for p in per_class], axis=1)
    pieces = []
    done = jnp.zeros_like(n8)
    for p, rows, slots in zip(per_class, MOE_COPY_SIZES, _piece_slots(T // nt)):
        last = jnp.cumsum(p, axis=1)
        first = last - p
        k = jnp.arange(slots, dtype=jnp.int32)[None, :, None]
        mine = (first[:, None, :] <= k) & (k < last[:, None, :])
        inner = done[:, None, :] + (k - first[:, None, :]) * rows
        packed = ((strip[:, None, :] + inner) // SUBLANES * MOE_PACK + (off[:, None, :] + inner) // SUBLANES)
        pieces.append(jnp.sum(jnp.where(mine, packed, 0), axis=2))
        done = done + p * rows
    return dict(strip=strip.reshape(-1).astype(jnp.int32), off=off.reshape(-1).astype(jnp.int32),
                n8=n8.reshape(-1).astype(jnp.int32), exact=exact.reshape(-1).astype(jnp.int32),
                pieces=jnp.concatenate([jnp.concatenate(pieces, axis=1).reshape(-1).astype(jnp.int32),
                                        jnp.zeros((1,), jnp.int32)]),
                off_col=off.reshape(nt, E, 1).astype(jnp.int32), blk_expert=blk_expert,
                tail=tail.reshape(-1).astype(jnp.int32),
                n_used=blk_end[-1:].astype(jnp.int32))


def _for_each_listed_copy(pieces_ref, exact_ref, tile, tm, fn):
    slots = _piece_slots(tm)
    base = tile * sum(slots)
    for s, rows in enumerate(MOE_COPY_SIZES):
        start = base + sum(slots[:s])

        def one(k, packed, start=start, rows=rows):
            ahead = pieces_ref[start + k + 1]
            fn(lax.shift_right_logical(packed, MOE_PACK.bit_length() - 1) * SUBLANES,
               (packed & (MOE_PACK - 1)) * SUBLANES, rows)
            return ahead

        lax.fori_loop(0, exact_ref[tile * len(MOE_COPY_SIZES) + s], one, pieces_ref[start])


def _for_each_piece(n, fn):
    nfull = n // MOE_CHUNK

    def per_chunk(c, carry):
        fn(c * MOE_CHUNK, MOE_CHUNK)
        return carry

    lax.fori_loop(0, nfull, per_chunk, 0)
    done = nfull * MOE_CHUNK
    for r in MOE_REMAINDERS:
        has = (n // r) % 2

        @pl.when(has == 1)
        def _(done=done, r=r):
            fn(done, r)

        done = done + has * r


def _dispatch_kernel(pieces_ref, exact_ref, tail_ref, nu_ref, x_ref, et_ref, wt_ref, offc_ref,
                     xs_ref, lp_ref, stage, sel_scr, zero, sem, *, tm):
    i = pl.program_id(0)
    last = pl.num_programs(0) - 1
    n_e = N_EXPERTS
    d = x_ref.shape[1]
    nrow = _stage_rows(tm)
    slot = i % 2

    def strip_copy(src0, dst0, rows, sl):
        return pltpu.make_async_copy(stage.at[sl, pl.ds(pl.multiple_of(src0, SUBLANES), rows)],
                                     xs_ref.at[pl.ds(pl.multiple_of(dst0, SUBLANES), rows)], sem.at[sl])

    def wait_tile(tile, sl):
        for s, rows in enumerate(MOE_COPY_SIZES):
            def one(c, carry, rows=rows):
                strip_copy(0, 0, rows, sl).wait()
                return carry

            lax.fori_loop(0, exact_ref[tile * len(MOE_COPY_SIZES) + s], one, 0)

    et = et_ref[0]
    wt = wt_ref[0]
    eid = lax.broadcasted_iota(jnp.int32, (n_e, tm), 0)
    hots = [eid == et[k:k + 1, :] for k in range(TOP_K)]
    m_t = jnp.zeros((n_e, tm), F32)
    for h in hots:
        m_t = m_t + jnp.where(h, 1.0, 0.0)
    before = jnp.where(lax.broadcasted_iota(jnp.int32, (tm, tm), 0) < lax.broadcasted_iota(jnp.int32, (tm, tm), 1),
                       1.0, 0.0).astype(BF16)
    rank = jnp.dot(m_t.astype(BF16), before, preferred_element_type=F32)
    base = offc_ref[0].astype(F32) + rank
    lps = [jnp.sum(jnp.where(h, base, 0.0), axis=0, keepdims=True) for h in hots]
    lpi = [lp.astype(jnp.int32) for lp in lps]
    lp_ref[0] = jnp.concatenate(lpi, axis=0)

    wcols = []
    for rc in range(nrow // LANES):
        rio = lax.broadcasted_iota(jnp.int32, (LANES, tm), 0) + rc * LANES
        sel = jnp.zeros((LANES, tm), F32)
        wsel = jnp.zeros((LANES, tm), F32)
        for k in reversed(range(TOP_K)):
            hit = rio == lpi[k]
            sel = jnp.where(hit, 1.0, sel)
            wsel = jnp.where(hit, wt[k:k + 1, :], wsel)
        sel_scr[rc * LANES:(rc + 1) * LANES, :] = sel.astype(BF16)
        wcols.append(jnp.sum(wsel, axis=1, keepdims=True))

    @pl.when(i >= 2)
    def _():
        wait_tile(i - 2, slot)

    stage[slot, :, 0:d] = jnp.dot(sel_scr[...], x_ref[...].astype(BF16), preferred_element_type=F32)
    for rc, wcol in enumerate(wcols):
        stage[slot, rc * LANES:(rc + 1) * LANES, d:d + LANES] = jnp.broadcast_to(wcol, (LANES, LANES))

    _for_each_listed_copy(pieces_ref, exact_ref, i, tm,
                          lambda far, near, rows: strip_copy(near, far, rows, slot).start())

    def zero_fill(wait):
        def go(dst0, rows):
            cp = pltpu.make_async_copy(zero.at[pl.ds(0, rows)],
                                       xs_ref.at[pl.ds(pl.multiple_of(dst0, SUBLANES), rows)], sem.at[2])
            cp.wait() if wait else cp.start()

        def per_tail(e, carry):
            start = tail_ref[2 * e]
            _for_each_piece(tail_ref[2 * e + 1], lambda o, rows: go(start + o, rows))
            return carry

        lax.fori_loop(0, n_e, per_tail, 0)

        def per_block(blk, carry):
            go(blk * MOE_BLOCK, MOE_BLOCK)
            return carry

        lax.fori_loop(nu_ref[0], xs_ref.shape[0] // MOE_BLOCK, per_block, 0)

    @pl.when(i == last)
    def _():
        @pl.when(i >= 1)
        def _():
            wait_tile(i - 1, 1 - slot)

        wait_tile(i, slot)
        zero[...] = jnp.zeros_like(zero)
        zero_fill(wait=False)
        zero_fill(wait=True)


def _dispatch(x1, et, wt, plan):
    T, D = x1.shape
    nt = et.shape[0]
    tm = T // nt
    E = N_EXPERTS
    P = _moe_rows(T, nt, E)
    nrow = _stage_rows(tm)
    tile3 = lambda i, *_: (i, 0, 0)
    grid_spec = pltpu.PrefetchScalarGridSpec(
        num_scalar_prefetch=4,
        grid=(nt,),
        in_specs=[pl.BlockSpec((tm, D), lambda i, *_: (i, 0)),
                  pl.BlockSpec((1, TOP_K, tm), tile3), pl.BlockSpec((1, TOP_K, tm), tile3),
                  pl.BlockSpec((1, E, 1), tile3)],
        out_specs=[pl.BlockSpec(memory_space=pl.ANY), pl.BlockSpec((1, TOP_K, tm), tile3)],
        scratch_shapes=[pltpu.VMEM((2, nrow, D + LANES), F32),
                        pltpu.VMEM((nrow, tm), BF16),
                        pltpu.VMEM((MOE_BLOCK, D + LANES), F32),
                        pltpu.SemaphoreType.DMA((3,))],
    )
    return pl.pallas_call(
        functools.partial(_dispatch_kernel, tm=tm),
        grid_spec=grid_spec,
        out_shape=[jax.ShapeDtypeStruct((P, D + LANES), F32), jax.ShapeDtypeStruct((nt, TOP_K, tm), jnp.int32)],
        compiler_params=_params(("arbitrary",)),
        name="moe_dispatch",
    )(plan["pieces"], plan["exact"], plan["tail"], plan["n_used"], x1, et, wt, plan["off_col"])


def _combine_kernel(pieces_ref, exact_ref, ys_ref, lp_ref, x_ref, g_ref, b_ref, *rest, tm, kd):
    if kd:
        wqk_ref, wvg_ref, wgr_ref, w2_ref, b2_ref, xo_ref, qkl_ref, vg_ref, land, sel_scr, sem = rest
    else:
        xo_ref, land, sel_scr, sem = rest
    i = pl.program_id(0)
    n_e = N_EXPERTS
    nrow = _stage_rows(tm)
    slot = i % 2

    def strip_copy(src0, dst0, rows, sl):
        return pltpu.make_async_copy(ys_ref.at[pl.ds(pl.multiple_of(src0, SUBLANES), rows)],
                                     land.at[sl, pl.ds(pl.multiple_of(dst0, SUBLANES), rows)], sem.at[sl])

    def fetch(tile, sl):
        _for_each_listed_copy(pieces_ref, exact_ref, tile, tm,
                              lambda far, near, rows: strip_copy(far, near, rows, sl).start())

    @pl.when(i == 0)
    def _():
        land[...] = jnp.zeros_like(land)
        fetch(0, 0)

    @pl.when(i + 1 < pl.num_programs(0))
    def _():
        fetch(i + 1, 1 - slot)

    lp = lp_ref[0]
    for rc in range(nrow // LANES):
        rio = lax.broadcasted_iota(jnp.int32, (LANES, tm), 0) + rc * LANES
        piece = jnp.zeros((LANES, tm), F32)
        for k in range(TOP_K):
            piece = jnp.where(rio == lp[k:k + 1, :], 1.0, piece)
        sel_scr[rc * LANES:(rc + 1) * LANES, :] = piece.astype(BF16)

    for s, rows in enumerate(MOE_COPY_SIZES):
        def one(c, carry, rows=rows):
            strip_copy(0, 0, rows, slot).wait()
            return carry

        lax.fori_loop(0, exact_ref[i * len(MOE_COPY_SIZES) + s], one, 0)

    ffn = lax.dot_general(sel_scr[...], land[slot, 0:nrow, :].astype(BF16), (((0,), (0,)), ((), ())),
                          preferred_element_type=F32)
    xn = _layer_norm(DN_ALPHA * x_ref[...] + ffn, g_ref[...], b_ref[...])
    xo_ref[...] = xn
    if kd:
        xb = xn.astype(BF16)
        qkl_ref[:, :2 * kd] = jnp.dot(xb, wqk_ref[...], preferred_element_type=F32)
        vg_ref[...] = jnp.dot(xb, wvg_ref[...], preferred_element_type=F32)
        gr = jnp.dot(xb, wgr_ref[...], preferred_element_type=F32)
        z = jnp.dot(gr, w2_ref[...], precision=HIGHEST, preferred_element_type=F32) + b2_ref[...]
        log_sig = jnp.minimum(z, 0.0) - jnp.log1p(jnp.exp(-jnp.abs(z)))
        qkl_ref[:, 2 * kd:] = log_sig / GLA_GATE_TAU


def _combine_ln(ys, lp, x, plan, g, b, gla=None):
    T, D = x.shape
    tm = MOE_TILE
    nt = T // tm
    const = lambda i, *_: (0, 0)
    row = lambda i, *_: (i, 0)
    in_specs = [pl.BlockSpec(memory_space=pl.ANY), pl.BlockSpec((1, TOP_K, tm), lambda i, *_: (i, 0, 0)),
                pl.BlockSpec((tm, D), row),
                pl.BlockSpec((1, D), const), pl.BlockSpec((1, D), const)]
    out_specs = [pl.BlockSpec((tm, D), row)]
    out_shape = [jax.ShapeDtypeStruct((T, D), F32)]
    args = [ys, lp, x, g.reshape(1, D), b.reshape(1, D)]
    kd = 0
    if gla is not None:
        w_in1, w_gate2, b_gate, kd, vd = gla
        ws = [w_in1[:, :2 * kd].astype(BF16), w_in1[:, 2 * kd:2 * kd + 2 * vd].astype(BF16),
              w_in1[:, 2 * kd + 2 * vd:].astype(BF16), w_gate2.astype(F32), b_gate.reshape(1, kd)]
        in_specs += [pl.BlockSpec(w.shape, const) for w in ws]
        args += ws
        out_specs += [pl.BlockSpec((tm, 3 * kd), row), pl.BlockSpec((tm, 2 * vd), row)]
        out_shape += [jax.ShapeDtypeStruct((T, 3 * kd), F32), jax.ShapeDtypeStruct((T, 2 * vd), F32)]
    grid_spec = pltpu.PrefetchScalarGridSpec(
        num_scalar_prefetch=2,
        grid=(nt,),
        in_specs=in_specs,
        out_specs=out_specs,
        scratch_shapes=[pltpu.VMEM((2, _stage_rows(tm), D), F32), pltpu.VMEM((_stage_rows(tm), tm), BF16),
                        pltpu.SemaphoreType.DMA((2,))],
    )
    return pl.pallas_call(
        functools.partial(_combine_kernel, tm=tm, kd=kd),
        grid_spec=grid_spec,
        out_shape=out_shape,
        compiler_params=_params(("arbitrary",)),
        name="moe_combine_ln",
    )(plan["pieces"], plan["exact"], *args)


def _moe_ln(x1, et, wt, cnt, layer, w_gate_up, b_gate_up, w_down, b_down, g, b, gla=None):
    T = x1.shape[0]
    plan = _moe_plan(cnt, T)
    xs, lp = _dispatch(x1, et, wt, plan)
    wgu = _deinterleave_gate_up(w_gate_up, layer)
    bgu = jnp.concatenate([b_gate_up[layer, :, 0::2], b_gate_up[layer, :, 1::2]], axis=-1)[:, None, :].astype(F32)
    ys = _expert_ffn(xs, plan, wgu, w_down, layer, bgu, b_down[layer, :, None, :].astype(F32))
    return _combine_ln(ys, lp, x1, plan, g, b, gla)


def _gla_kernel(q_ref, k_ref, la_ref, v_ref, g_ref, nw_ref, o_ref, s_scr, *, tl, nh, dk, dv):
    C = GLA_CHUNK

    @pl.when(pl.program_id(1) == 0)
    def _():
        s_scr[...] = jnp.zeros_like(s_scr)

    ri = lax.broadcasted_iota(jnp.int32, (C, C), 0)
    ci = lax.broadcasted_iota(jnp.int32, (C, C), 1)
    lower = ri >= ci
    tri = jnp.where(lower, 1.0, 0.0).astype(BF16)
    states = [s_scr[h] for h in range(nh)]
    for c in range(tl // C):
        rows = slice(c * C, (c + 1) * C)
        rest = la_ref[0, rows, :]
        bcum = None
        for _ in range(3):
            piece = rest.astype(BF16)
            part = jnp.dot(tri, piece, preferred_element_type=F32)
            bcum = part if bcum is None else bcum + part
            rest = rest - piece.astype(F32)
        btot = bcum[C - 1:C, :]
        qb = (q_ref[0, rows, :] * (dk ** -0.5) * jnp.exp(bcum)).astype(BF16)
        kc = k_ref[0, rows, :]
        k_in = (kc * jnp.exp(-bcum)).astype(BF16)
        k_st = kc * jnp.exp(btot - bcum)
        dec = jnp.exp(btot)
        for h in range(nh):
            ks, vs = slice(h * dk, (h + 1) * dk), slice(h * dv, (h + 1) * dv)
            vb = v_ref[0, rows, vs].astype(BF16)
            att = lax.dot_general(qb[:, ks], k_in[:, ks], (((1,), (1,)), ((), ())), preferred_element_type=F32)
            att = jnp.where(lower, att, 0.0)
            o = (jnp.dot(att.astype(BF16), vb, preferred_element_type=F32)
                 + jnp.dot(qb[:, ks], states[h].astype(BF16), preferred_element_type=F32))
            kv = lax.dot_general(k_st[:, ks].astype(BF16), vb, (((0,), (0,)), ((), ())),
                                 preferred_element_type=F32)
            decay = jnp.broadcast_to(dec[:, ks], (SUBLANES, dk)).T[:, 0:1]
            states[h] = decay * states[h] + kv
            o = o * lax.rsqrt(jnp.mean(o * o, axis=-1, keepdims=True) + RMS_EPS) * nw_ref[...]
            gg = g_ref[0, rows, vs]
            o_ref[0, rows, vs] = o * (gg * jax.nn.sigmoid(gg))
    for h in range(nh):
        s_scr[h] = states[h]


def _gla(qkl, vg, norm_w, B, L, tl=256):
    T, kd3 = qkl.shape
    kd = kd3 // 3
    vd = vg.shape[1] // 2
    H = GLA_HEADS
    dk, dv = kd // H, vd // H
    qkl3 = qkl.reshape(B, L, kd3)
    vg3 = vg.reshape(B, L, 2 * vd)
    return pl.pallas_call(
        functools.partial(_gla_kernel, tl=tl, nh=H, dk=dk, dv=dv),
        grid=(B, L // tl),
        in_specs=[pl.BlockSpec((1, tl, kd), lambda b, l: (b, l, 0)),
                  pl.BlockSpec((1, tl, kd), lambda b, l: (b, l, 1)),
                  pl.BlockSpec((1, tl, kd), lambda b, l: (b, l, 2)),
                  pl.BlockSpec((1, tl, vd), lambda b, l: (b, l, 0)),
                  pl.BlockSpec((1, tl, vd), lambda b, l: (b, l, 1)),
                  pl.BlockSpec((1, dv), lambda b, l: (0, 0))],
        out_specs=pl.BlockSpec((1, tl, vd), lambda b, l: (b, l, 0)),
        out_shape=jax.ShapeDtypeStruct((B, L, vd), F32),
        scratch_shapes=[pltpu.VMEM((H, dk, dv), F32)],
        compiler_params=_params(("parallel", "arbitrary")),
        name="gla_mixer",
    )(qkl3, qkl3, qkl3, vg3, vg3, norm_w.reshape(1, dv).astype(F32))


def kernel(x, w_in0, s5_lam_re, s5_lam_im, s5_log_dt, s5_b_re, s5_b_im, s5_c_re, s5_c_im, s5_d, s5_w_glu,
           s5_b_glu, rel_bias, w_out0, w_in1, gla_w_gate2, gla_b_gate, gla_norm_w, w_out1, ln_mix_g, ln_mix_b,
           ln_ffn_g, ln_ffn_b, router_w, router_b, exp_w_gate_up, exp_b_gate_up, exp_w_down, exp_b_down):
    B, L, D = x.shape
    T = B * L
    s5w = s5_w_glu.shape[-1]
    kd = gla_w_gate2.shape[-1]
    vd = gla_norm_w.shape[-1] * GLA_HEADS

    u_tm, k_att, qt_att, vt_att = _inproj0(x, w_in0[0], s5w)
    y_a = _s5_mixer(u_tm.reshape(L, B, s5w), s5_lam_re[0], s5_lam_im[0], s5_log_dt[0], s5_b_re[0], s5_b_im[0],
                    s5_c_re[0], s5_c_im[0], s5_d[0].reshape(-1), s5_w_glu[0], s5_b_glu[0])
    y_b = _moba(qt_att, k_att, vt_att, rel_bias)
    tl = MIX_TILE
    w0 = w_out0[0].astype(BF16)
    ys = [(y_a.reshape(L, B * s5w), pl.BlockSpec((tl, s5w), lambda bb, l: (l, bb))),
          (y_b, pl.BlockSpec((1, tl, D - s5w), lambda bb, l: (bb, l, 0)))]
    x1, et, wt, cnt = _mix_ln(x, ys, [w0[:s5w], w0[s5w:]], ln_mix_g[0], ln_mix_b[0], router_w[0], router_b[0])
    x2, qkl, vg = _moe_ln(x1.reshape(T, D), et, wt, cnt, 0, exp_w_gate_up, exp_b_gate_up, exp_w_down,
                          exp_b_down, ln_ffn_g[0], ln_ffn_b[0],
                          gla=(w_in1[0], gla_w_gate2[0], gla_b_gate[0], kd, vd))

    y_c = _gla(qkl, vg, gla_norm_w[0], B, L)
    ys = [(y_c, pl.BlockSpec((1, tl, vd), lambda bb, l: (bb, l, 0)))]
    x3, et, wt, cnt = _mix_ln(x2.reshape(B, L, D), ys, [w_out1[0].astype(BF16)], ln_mix_g[1], ln_mix_b[1],
                              router_w[1], router_b[1])
    (out,) = _moe_ln(x3.reshape(T, D), et, wt, cnt, 1, exp_w_gate_up, exp_b_gate_up, exp_w_down,
                     exp_b_down, ln_ffn_g[1], ln_ffn_b[1])
    return out.reshape(B, L, D)
```

```python
import functools
import math

import jax
import jax.numpy as jnp
import numpy as np
from jax import lax
from jax.experimental import pallas as pl
from jax.experimental.pallas import tpu as pltpu

F32 = jnp.float32
BF16 = jnp.bfloat16
HIGHEST = lax.Precision.HIGHEST

DEPTH = 2
S5_GROUP = 16
S5_STATE = 64
MOBA_HEAD_DIM = 64
MOBA_BLOCK = 256
MOBA_TOPK = 3
REL_BUCKETS = 32
REL_MAX_DIST = 2048
GLA_HEADS = 4
GLA_GATE_TAU = 16.0
GLA_CHUNK = 64
N_EXPERTS = 32
TOP_K = 4
SWIGLU_LIMIT = 7.0
SWIGLU_ALPHA = 1.702
MOE_BLOCK = 512
MOE_TILE = 256
MIX_TILE = 1024
MOE_CHUNK = 32
SUBLANES = 8
MOE_REMAINDERS = (16, 8)
DN_ALPHA = (2 * DEPTH) ** 0.25
LN_EPS = 1e-5
RMS_EPS = 1e-5

V7X_VMEM_LIMIT_BYTES = 56 * 1024 * 1024
LANES = 128

S5_GROUPS_PER_CHUNK = LANES // S5_GROUP
S5_CHUNK_STATES = S5_GROUPS_PER_CHUNK * S5_STATE


def _params(sem):
    return pltpu.CompilerParams(dimension_semantics=sem, vmem_limit_bytes=V7X_VMEM_LIMIT_BYTES)


def _layer_norm(r, g, b):
    mu = jnp.mean(r, axis=-1, keepdims=True)
    c = r - mu
    var = jnp.mean(c * c, axis=-1, keepdims=True)
    return c * lax.rsqrt(var + LN_EPS) * g + b


def _inproj0_kernel(x_ref, wuk_ref, wqvt_ref, u_ref, k_ref, qt_ref, vt_ref, *, s5w):
    xb = x_ref[0].astype(BF16)
    h = jnp.dot(xb, wuk_ref[...], preferred_element_type=F32)
    u_ref[...] = h[:, :s5w]
    k_ref[0] = h[:, s5w:]
    ht = lax.dot_general(wqvt_ref[...], xb, (((1,), (1,)), ((), ())), preferred_element_type=F32)
    aw = ht.shape[0] // 2
    qt_ref[0] = ht[:aw]
    vt_ref[0] = ht[aw:].astype(BF16)


def _inproj0(x, w_in0, s5w, tl=512):
    B, L, D = x.shape
    aw = (w_in0.shape[1] - s5w) // 3
    wb = w_in0.astype(BF16)
    wuk = jnp.concatenate([wb[:, :s5w], wb[:, s5w + aw:s5w + 2 * aw]], axis=1)
    wqvt = jnp.concatenate([wb[:, s5w:s5w + aw], wb[:, s5w + 2 * aw:]], axis=1).T
    return pl.pallas_call(
        functools.partial(_inproj0_kernel, s5w=s5w),
        grid=(B, L // tl),
        in_specs=[pl.BlockSpec((1, tl, D), lambda b, l: (b, l, 0)),
                  pl.BlockSpec(wuk.shape, lambda b, l: (0, 0)),
                  pl.BlockSpec(wqvt.shape, lambda b, l: (0, 0))],
        out_specs=[pl.BlockSpec((tl, s5w), lambda b, l: (l, b)),
                   pl.BlockSpec((1, tl, aw), lambda b, l: (b, l, 0)),
                   pl.BlockSpec((1, aw, tl), lambda b, l: (b, 0, l)),
                   pl.BlockSpec((1, aw, tl), lambda b, l: (b, 0, l))],
        out_shape=[jax.ShapeDtypeStruct((L, B * s5w), F32),
                   jax.ShapeDtypeStruct((B, L, aw), F32),
                   jax.ShapeDtypeStruct((B, aw, L), F32),
                   jax.ShapeDtypeStruct((B, aw, L), BF16)],
        compiler_params=_params(("parallel", "parallel")),
        name="inproj0",
    )(x, wuk, wqvt)


def _s5_discretize(lam_re, lam_im, log_dt, b_re, b_im):
    dt = jnp.exp(log_dt.astype(F32))[:, None]
    lr, li = lam_re.astype(F32), lam_im.astype(F32)
    mag = jnp.exp(lr * dt)
    ab_re, ab_im = mag * jnp.cos(li * dt), mag * jnp.sin(li * dt)
    er, ei = ab_re - 1.0, ab_im
    den = lr * lr + li * li
    q_re = (er * lr + ei * li) / den
    q_im = (ei * lr - er * li) / den
    br_, bi_ = b_re.astype(F32), b_im.astype(F32)
    bb_re = q_re[..., None] * br_ - q_im[..., None] * bi_
    bb_im = q_re[..., None] * bi_ + q_im[..., None] * br_
    return ab_re, ab_im, bb_re, bb_im


def _s5_kernel(u_ref, bm_ref, cm_ref, are_ref, aim_ref, d_ref, wg_ref, bg_ref, y_ref,
               s_scr, st_scr, z_scr, *, tl, nb, nchunk):
    ns = S5_CHUNK_STATES

    @pl.when(pl.program_id(0) == 0)
    def _():
        st_scr[...] = jnp.zeros_like(st_scr)

    u = u_ref[...].reshape(tl * nb, nchunk * LANES)
    for j in range(nchunk):
        uj = u[:, j * LANES:(j + 1) * LANES]
        s_scr[...] = jnp.dot(uj.astype(BF16), bm_ref[j], preferred_element_type=F32)
        ar = jnp.broadcast_to(are_ref[j], (nb, ns))
        ai = jnp.broadcast_to(aim_ref[j], (nb, ns))

        def step(t, carry, ar=ar, ai=ai):
            sre, sim = carry
            r0 = pl.multiple_of(t * nb, nb)
            nre = ar * sre - ai * sim + s_scr[pl.ds(r0, nb), 0:ns]
            nim = ar * sim + ai * sre + s_scr[pl.ds(r0, nb), ns:2 * ns]
            s_scr[pl.ds(r0, nb), 0:ns] = nre
            s_scr[pl.ds(r0, nb), ns:2 * ns] = nim
            return nre, nim

        sre, sim = lax.fori_loop(0, tl, step, (st_scr[j, :, 0:ns], st_scr[j, :, ns:2 * ns]), unroll=2)
        st_scr[j, :, 0:ns] = sre
        st_scr[j, :, ns:2 * ns] = sim
        yj = jnp.dot(s_scr[...].astype(BF16), cm_ref[j], preferred_element_type=F32)
        z_scr[:, j * LANES:(j + 1) * LANES] = yj + uj * d_ref[:, j * LANES:(j + 1) * LANES]
    z = jax.nn.gelu(z_scr[...])
    gate = jax.nn.sigmoid(jnp.dot(z.astype(BF16), wg_ref[...], preferred_element_type=F32) + bg_ref[...])
    y_ref[...] = (z * gate).reshape(tl, nb, nchunk * LANES)


def _s5_mixer(u_tm, lam_re, lam_im, log_dt, b_re, b_im, c_re, c_im, d_skip, w_glu, b_glu, tl=64):
    L, B, W = u_tm.shape
    G, P, H = lam_re.shape[0], S5_STATE, S5_GROUP
    gc = S5_GROUPS_PER_CHUNK
    nchunk = G // gc
    ab_re, ab_im, bb_re, bb_im = _s5_discretize(lam_re, lam_im, log_dt, b_re, b_im)
    eye = jnp.eye(gc, dtype=F32)

    def b_blocks(bb):
        return jnp.einsum('jgph,gk->jghkp', bb.reshape(nchunk, gc, P, H), eye).reshape(nchunk, gc * H, gc * P)

    def c_blocks(cc):
        return jnp.einsum('jghp,gk->jgpkh', cc.reshape(nchunk, gc, H, P), eye).reshape(nchunk, gc * P, gc * H)

    bm = jnp.concatenate([b_blocks(bb_re), b_blocks(bb_im)], axis=2).astype(BF16)
    cm = jnp.concatenate([c_blocks(c_re.astype(F32)), -c_blocks(c_im.astype(F32))], axis=1).astype(BF16)
    are = ab_re.reshape(nchunk, 1, gc * P)
    aim = ab_im.reshape(nchunk, 1, gc * P)
    m = tl * B
    const3 = lambda l: (0, 0, 0)
    const2 = lambda l: (0, 0)
    return pl.pallas_call(
        functools.partial(_s5_kernel, tl=tl, nb=B, nchunk=nchunk),
        grid=(L // tl,),
        in_specs=[pl.BlockSpec((tl, B, W), lambda l: (l, 0, 0)),
                  pl.BlockSpec(bm.shape, const3), pl.BlockSpec(cm.shape, const3),
                  pl.BlockSpec(are.shape, const3), pl.BlockSpec(aim.shape, const3),
                  pl.BlockSpec((1, W), const2), pl.BlockSpec((W, W), const2), pl.BlockSpec((1, W), const2)],
        out_specs=pl.BlockSpec((tl, B, W), lambda l: (l, 0, 0)),
        out_shape=jax.ShapeDtypeStruct((L, B, W), F32),
        scratch_shapes=[pltpu.VMEM((m, 2 * S5_CHUNK_STATES), F32),
                        pltpu.VMEM((nchunk, B, 2 * S5_CHUNK_STATES), F32),
                        pltpu.VMEM((m, W), F32)],
        compiler_params=_params(("arbitrary",)),
        name="s5_mixer",
    )(u_tm, bm, cm, are, aim, d_skip.reshape(1, W).astype(F32), w_glu.astype(BF16),
      b_glu.reshape(1, W).astype(F32))


def _rel_bucket(n):
    n = jnp.maximum(n, 0)
    max_exact = REL_BUCKETS // 2
    nf = jnp.maximum(n, 1).astype(F32)
    large = max_exact + (jnp.log(nf / max_exact) / math.log(REL_MAX_DIST / max_exact)
                         * (REL_BUCKETS - max_exact)).astype(jnp.int32)
    large = jnp.minimum(large, REL_BUCKETS - 1)
    return jnp.where(n < max_exact, n, large)


MOBA_ONES_ROWS = 16
MOBA_NEG = -1e30
LOG2E = math.log2(math.e)


def _moba_kernel(qt_ref, k_ref, vt_ref, wv_ref, hot_ref, o_ref, bias_scr, kk_scr, va_scr, km_scr, qa_scr, s_scr,
                 *, nblk, bs, dh, topk):
    b = pl.program_id(1)
    tq = bs
    hpb = LANES // dh

    @pl.when(b == 0)
    def _():
        r_io = lax.broadcasted_iota(jnp.int32, (bs, 2 * bs), 0)
        c_io = lax.broadcasted_iota(jnp.int32, (bs, 2 * bs), 1)
        for hh in range(hpb):
            for dd in range(nblk):
                t = pltpu.roll(jnp.broadcast_to(wv_ref[hh, dd:dd + 1, :], (bs, 2 * bs)), bs + 1, 1,
                               stride=1, stride_axis=0)
                if dd == 0:
                    t = jnp.where(c_io >= r_io, t, MOBA_NEG)
                bias_scr[dd, :, hh * bs:(hh + 1) * bs] = t[:, 0:bs]

    def prepare():
        kf = k_ref[0]
        km_scr[...] = jnp.mean(kf.reshape(nblk, bs, LANES), axis=1)
        kk_scr[:, 0:LANES] = kf.astype(BF16)
        kk_scr[:, LANES:] = hot_ref[...]
        for n in range(nblk):
            va_scr[n, 0:LANES, :] = vt_ref[0, :, n * bs:(n + 1) * bs]
            va_scr[n, LANES:, :] = jnp.ones((MOBA_ONES_ROWS, bs), BF16)
        seq = nblk * bs
        q2 = qt_ref[0] * (dh ** -0.5 * LOG2E)
        f_io = lax.broadcasted_iota(jnp.int32, (LANES, seq), 0)
        blk_io = lax.broadcasted_iota(jnp.int32, (nblk, seq), 0)
        own_q = lax.broadcasted_iota(jnp.int32, (nblk, seq), 1) // bs
        qms, pens = [], []
        for hh in range(hpb):
            qm = jnp.where((f_io >= hh * dh) & (f_io < (hh + 1) * dh), q2, 0.0)
            gate = jnp.dot(km_scr[...], qm, precision=HIGHEST, preferred_element_type=F32)
            cnt = jnp.zeros((nblk, seq), jnp.int32)
            for m in range(nblk):
                gm = gate[m:m + 1, :]
                beats = (gm > gate) | ((gm == gate) & (m < blk_io))
                cnt = cnt + jnp.where(beats & (m < own_q), 1, 0)
            keep = (blk_io >= own_q) | (cnt < topk)
            pens.append(jnp.where(keep, 0.0, MOBA_NEG))
            qms.append(qm.astype(BF16))
        for n in range(nblk):
            cols = slice(n * bs, (n + 1) * bs)
            qa_scr[n, 0:LANES, :] = jnp.concatenate([qm[:, cols] for qm in qms], axis=1)
            pen = jnp.concatenate([p[:, cols] for p in pens], axis=1)
            qa_scr[n, LANES:, :] = jnp.concatenate(
                [pen, jnp.zeros((LANES - nblk, hpb * bs), F32)], axis=0).astype(BF16)

    prepare()

    for own in range(nblk):
        qa = qa_scr[own]
        m = None
        for j in range(own + 1):
            s = (jnp.dot(kk_scr[j * bs:(j + 1) * bs, :], qa, preferred_element_type=F32)
                 + bias_scr[own - j])
            s_scr[j] = s
            mj = jnp.max(s, axis=0, keepdims=True)
            m = mj if m is None else jnp.maximum(m, mj)
        acc = None
        for j in range(own + 1):
            p = jnp.exp2(s_scr[j] - m).astype(BF16)
            t = jnp.dot(va_scr[j], p, preferred_element_type=F32)
            acc = t if acc is None else acc + t
        on = acc[0:LANES, :] / acc[LANES:LANES + 1, :]
        ot = jnp.concatenate([on[hh * dh:(hh + 1) * dh, hh * tq:(hh + 1) * tq] for hh in range(hpb)], axis=0)
        o_ref[0, own * bs:(own + 1) * bs, :] = ot.T


def _moba(qt, k, vt, rel_bias):
    B, L, W = k.shape
    dh, bs = MOBA_HEAD_DIM, MOBA_BLOCK
    H = W // dh
    nblk = L // bs
    hpb = LANES // dh
    dist = jnp.arange(L, dtype=jnp.int32)
    by_dist = rel_bias.astype(F32).T[:, _rel_bucket(dist)] * LOG2E
    idx = np.clip(np.arange(nblk)[:, None] * bs - (bs - 1) + np.arange(2 * bs)[None, :], 0, L - 1)
    vecs = by_dist[:, idx]
    hot = np.zeros((L, LANES), np.float32)
    hot[np.arange(L), np.arange(L) // bs] = 1.0
    return pl.pallas_call(
        functools.partial(_moba_kernel, nblk=nblk, bs=bs, dh=dh, topk=min(MOBA_TOPK, nblk)),
        grid=(H // hpb, B),
        in_specs=[pl.BlockSpec((1, LANES, L), lambda h, b: (b, h, 0)),
                  pl.BlockSpec((1, L, LANES), lambda h, b: (b, 0, h)),
                  pl.BlockSpec((1, LANES, L), lambda h, b: (b, h, 0)),
                  pl.BlockSpec((hpb, nblk, 2 * bs), lambda h, b: (h, 0, 0)),
                  pl.BlockSpec((L, LANES), lambda h, b: (0, 0))],
        out_specs=pl.BlockSpec((1, L, LANES), lambda h, b: (b, 0, h)),
        out_shape=jax.ShapeDtypeStruct((B, L, W), F32),
        scratch_shapes=[pltpu.VMEM((nblk, bs, hpb * bs), F32),
                        pltpu.VMEM((L, 2 * LANES), BF16),
                        pltpu.VMEM((nblk, LANES + MOBA_ONES_ROWS, bs), BF16),
                        pltpu.VMEM((nblk, LANES), F32),
                        pltpu.VMEM((nblk, 2 * LANES, hpb * bs), BF16),
                        pltpu.VMEM((nblk, bs, hpb * bs), F32)],
        compiler_params=_params(("arbitrary", "arbitrary")),
        name="moba_attention",
    )(qt, k, vt, vecs, jnp.asarray(hot, BF16))


def _route_top_k(xn, wrt_ref, br_ref):
    lt = lax.dot_general(wrt_ref[...], xn, (((1,), (1,)), ((), ())), precision=HIGHEST,
                         preferred_element_type=F32) + br_ref[...]
    n_e = lt.shape[0]
    eid = lax.broadcasted_iota(jnp.int32, lt.shape, 0)
    cur = lt
    vals, idxs = [], []
    for _ in range(TOP_K):
        m = jnp.max(cur, axis=0, keepdims=True)
        idx = jnp.min(jnp.where(cur == m, eid, n_e), axis=0, keepdims=True)
        vals.append(m)
        idxs.append(idx)
        cur = jnp.where(eid == idx, -jnp.inf, cur)
    ex = [jnp.exp(v - vals[0]) for v in vals]
    den = ex[0]
    for t in ex[1:]:
        den = den + t
    hot = jnp.zeros(lt.shape, F32)
    for idx in idxs:
        hot = hot + jnp.where(eid == idx, 1.0, 0.0)
    return (jnp.concatenate(idxs, axis=0), jnp.concatenate([t / den for t in ex], axis=0),
            jnp.sum(hot, axis=1, keepdims=True))


def _mix_ln_kernel(*refs, n_in):
    x_ref = refs[0]
    y_refs = refs[1:1 + n_in]
    w_refs = refs[1 + n_in:1 + 2 * n_in]
    g_ref, b_ref, wrt_ref, br_ref, x1_ref, et_ref, wt_ref, cnt_ref = refs[1 + 2 * n_in:]
    mix = None
    for y_ref, w_ref in zip(y_refs, w_refs):
        y = y_ref[...]
        y = y.reshape(y.shape[-2], y.shape[-1]).astype(BF16)
        t = jnp.dot(y, w_ref[...], preferred_element_type=F32)
        mix = t if mix is None else mix + t
    xn = _layer_norm(DN_ALPHA * x_ref[0] + mix, g_ref[...], b_ref[...])
    x1_ref[0] = xn
    for s in range(et_ref.shape[0]):
        et, wt, cnt = _route_top_k(xn[s * MOE_TILE:(s + 1) * MOE_TILE], wrt_ref, br_ref)
        et_ref[s] = et
        wt_ref[s] = wt
        cnt_ref[s] = cnt


def _mix_ln(x, ys, ws, g, b, wr, br):
    B, L, D = x.shape
    E = wr.shape[1]
    tl = MIX_TILE
    nl = L // tl
    sub = tl // MOE_TILE
    nt = B * L // MOE_TILE
    const = lambda bb, l: (0, 0)
    tile = lambda bb, l: (bb * nl + l, 0, 0)
    in_specs = [pl.BlockSpec((1, tl, D), lambda bb, l: (bb, l, 0))]
    in_specs += [spec for _, spec in ys]
    in_specs += [pl.BlockSpec(w.shape, const) for w in ws]
    in_specs += [pl.BlockSpec((1, D), const), pl.BlockSpec((1, D), const),
                 pl.BlockSpec((E, D), const), pl.BlockSpec((E, 1), const)]
    return pl.pallas_call(
        functools.partial(_mix_ln_kernel, n_in=len(ys)),
        grid=(B, nl),
        in_specs=in_specs,
        out_specs=[pl.BlockSpec((1, tl, D), lambda bb, l: (bb, l, 0)),
                   pl.BlockSpec((sub, TOP_K, MOE_TILE), tile), pl.BlockSpec((sub, TOP_K, MOE_TILE), tile),
                   pl.BlockSpec((sub, E, 1), tile)],
        out_shape=[jax.ShapeDtypeStruct((B, L, D), F32),
                   jax.ShapeDtypeStruct((nt, TOP_K, MOE_TILE), jnp.int32),
                   jax.ShapeDtypeStruct((nt, TOP_K, MOE_TILE), F32),
                   jax.ShapeDtypeStruct((nt, E, 1), F32)],
        compiler_params=_params(("parallel", "parallel")),
        name="outproj_ln_router",
    )(x, *[a for a, _ in ys], *ws, g.reshape(1, D), b.reshape(1, D), wr.astype(F32).T, br.reshape(E, 1))


def _deinterleave_kernel(w_ref, p_ref, o_ref, *, ff):
    w = w_ref[0, 0].astype(BF16)
    for c in range(2 * ff // (2 * LANES)):
        t = jnp.dot(w[:, 2 * LANES * c:2 * LANES * (c + 1)], p_ref[...], preferred_element_type=F32)
        o_ref[0, :, LANES * c:LANES * (c + 1)] = t[:, :LANES].astype(BF16)
        o_ref[0, :, ff + LANES * c:ff + LANES * (c + 1)] = t[:, LANES:].astype(BF16)


def _deinterleave_gate_up(w_gate_up, layer, tk=512):
    _, E, D, F2 = w_gate_up.shape
    src = np.arange(2 * LANES)
    dst = np.where(src % 2 == 0, src // 2, LANES + src // 2)
    perm = np.zeros((2 * LANES, 2 * LANES), np.float32)
    perm[src, dst] = 1.0
    return pl.pallas_call(
        functools.partial(_deinterleave_kernel, ff=F2 // 2),
        grid=(E, D // tk),
        in_specs=[pl.BlockSpec((1, 1, tk, F2), lambda e, k: (layer, e, k, 0)),
                  pl.BlockSpec((2 * LANES, 2 * LANES), lambda e, k: (0, 0))],
        out_specs=pl.BlockSpec((1, tk, F2), lambda e, k: (e, k, 0)),
        out_shape=jax.ShapeDtypeStruct((E, D, F2), BF16),
        compiler_params=_params(("parallel", "parallel")),
        name="deinterleave_gate_up",
    )(w_gate_up, jnp.asarray(perm, BF16))


def _expert_kernel(be_ref, nu_ref, xs_ref, wgu_ref, wd_ref, bgu_ref, bd_ref, ys_ref):
    blk = pl.program_id(0)
    ff = wd_ref.shape[2]
    d = wd_ref.shape[3]

    @pl.when(blk < nu_ref[0])
    def _():
        x = xs_ref[:, :d].astype(BF16)
        pw = xs_ref[:, d:d + 1]
        h = jnp.dot(x, wgu_ref[0], preferred_element_type=F32) + bgu_ref[0]
        g = h[:, :ff]
        u = h[:, ff:]
        g = jnp.minimum(g, SWIGLU_LIMIT)
        u = jnp.clip(u, -SWIGLU_LIMIT, SWIGLU_LIMIT)
        act = g * jax.nn.sigmoid(SWIGLU_ALPHA * g) * (u + 1.0)
        y = jnp.dot(act.astype(BF16), wd_ref[0, 0].astype(BF16), preferred_element_type=F32) + bd_ref[0]
        ys_ref[...] = y * pw

    @pl.when(blk >= nu_ref[0])
    def _():
        ys_ref[...] = jnp.zeros_like(ys_ref)


def _expert_ffn(xs, plan, wgu, w_down, layer, bgu, bd):
    P, DW = xs.shape
    F, D = w_down.shape[2], w_down.shape[3]
    bm = MOE_BLOCK
    wmap = lambda i, be, nu: (be[i], 0, 0)
    rmap = lambda i, be, nu: (i, 0)
    xmap = lambda i, be, nu: (jnp.minimum(i, nu[0] - 1), 0)
    grid_spec = pltpu.PrefetchScalarGridSpec(
        num_scalar_prefetch=2,
        grid=(P // bm,),
        in_specs=[pl.BlockSpec((bm, DW), xmap),
                  pl.BlockSpec((1, D, 2 * F), wmap),
                  pl.BlockSpec((1, 1, F, D), lambda i, be, nu: (layer, be[i], 0, 0)),
                  pl.BlockSpec((1, 1, 2 * F), wmap), pl.BlockSpec((1, 1, D), wmap)],
        out_specs=pl.BlockSpec((bm, D), rmap),
    )
    return pl.pallas_call(
        _expert_kernel,
        grid_spec=grid_spec,
        out_shape=jax.ShapeDtypeStruct((P, D), F32),
        compiler_params=_params(("arbitrary",)),
        name="moe_experts",
    )(plan["blk_expert"], plan["n_used"], xs, wgu, w_down, bgu, bd)


def _moe_rows(T, nt, E):
    bound = T * TOP_K + (SUBLANES - 1) * E * nt + E * MOE_BLOCK
    return -(-bound // MOE_BLOCK) * MOE_BLOCK


def _stage_rows(tm):
    return TOP_K * tm + N_EXPERTS * SUBLANES


MOE_COPY_SIZES = (MOE_CHUNK,) + MOE_REMAINDERS
MOE_PACK = 256


def _piece_slots(tm):
    assert _stage_rows(tm) // SUBLANES <= MOE_PACK
    return (_stage_rows(tm) // MOE_CHUNK,) + (N_EXPERTS,) * len(MOE_REMAINDERS)


def _moe_plan(cnt, T):
    nt, E = cnt.shape[0], cnt.shape[1]
    c = cnt.reshape(nt, E).astype(jnp.int32)
    n8 = (c + SUBLANES - 1) // SUBLANES * SUBLANES
    tot = jnp.sum(n8, axis=0)
    seg = (tot + MOE_BLOCK - 1) // MOE_BLOCK * MOE_BLOCK
    seg_start = jnp.cumsum(seg) - seg
    strip = seg_start[None, :] + jnp.cumsum(n8, axis=0) - n8
    off = jnp.cumsum(n8, axis=1) - n8
    nblk = _moe_rows(T, nt, E) // MOE_BLOCK
    seg_blk = seg // MOE_BLOCK
    blk_end = jnp.cumsum(seg_blk)
    blk_ids = jnp.arange(nblk, dtype=jnp.int32)
    blk_expert = jnp.minimum(jnp.sum(blk_end[None, :] <= blk_ids[:, None], axis=1), E - 1).astype(jnp.int32)
    tail = jnp.stack([seg_start + tot, seg - tot], axis=1)
    per_class = [n8 // MOE_CHUNK] + [(n8 // r) % 2 for r in MOE_REMAINDERS]
    exact = jnp.stack([jnp.sum(p, axis=1) for p in per_class], axis=1)
    pieces = []
    done = jnp.zeros_like(n8)
    for p, rows, slots in zip(per_class, MOE_COPY_SIZES, _piece_slots(T // nt)):
        last = jnp.cumsum(p, axis=1)
        first = last - p
        k = jnp.arange(slots, dtype=jnp.int32)[None, :, None]
        mine = (first[:, None, :] <= k) & (k < last[:, None, :])
        inner = done[:, None, :] + (k - first[:, None, :]) * rows
        packed = ((strip[:, None, :] + inner) // SUBLANES * MOE_PACK + (off[:, None, :] + inner) // SUBLANES)
        pieces.append(jnp.sum(jnp.where(mine, packed, 0), axis=2))
        done = done + p * rows
    return dict(exact=exact.reshape(-1).astype(jnp.int32),
                pieces=jnp.concatenate([jnp.concatenate(pieces, axis=1).reshape(-1).astype(jnp.int32),
                                        jnp.zeros((1,), jnp.int32)]),
                off_col=off.reshape(nt, E, 1).astype(jnp.int32), blk_expert=blk_expert,
                tail=tail.reshape(-1).astype(jnp.int32),
                n_used=blk_end[-1:].astype(jnp.int32))


def _for_each_listed_copy(pieces_ref, exact_ref, tile, tm, fn):
    slots = _piece_slots(tm)
    base = tile * sum(slots)
    for s, rows in enumerate(MOE_COPY_SIZES):
        start = base + sum(slots[:s])

        def one(k, packed, start=start, rows=rows):
            ahead = pieces_ref[start + k + 1]
            fn(lax.shift_right_logical(packed, MOE_PACK.bit_length() - 1) * SUBLANES,
               (packed & (MOE_PACK - 1)) * SUBLANES, rows)
            return ahead

        lax.fori_loop(0, exact_ref[tile * len(MOE_COPY_SIZES) + s], one, pieces_ref[start])


def _for_each_piece(n, fn):
    nfull = n // MOE_CHUNK

    def per_chunk(c, carry):
        fn(c * MOE_CHUNK, MOE_CHUNK)
        return carry

    lax.fori_loop(0, nfull, per_chunk, 0)
    done = nfull * MOE_CHUNK
    for r in MOE_REMAINDERS:
        has = (n // r) % 2

        @pl.when(has == 1)
        def _(done=done, r=r):
            fn(done, r)

        done = done + has * r


def _dispatch_kernel(pieces_ref, exact_ref, tail_ref, nu_ref, x_ref, et_ref, wt_ref, offc_ref,
                     xs_ref, lp_ref, stage, sel_scr, zero, sem, *, tm):
    i = pl.program_id(0)
    last = pl.num_programs(0) - 1
    n_e = N_EXPERTS
    d = x_ref.shape[1]
    nrow = _stage_rows(tm)
    slot = i % 2

    def strip_copy(src0, dst0, rows, sl):
        return pltpu.make_async_copy(stage.at[sl, pl.ds(pl.multiple_of(src0, SUBLANES), rows)],
                                     xs_ref.at[pl.ds(pl.multiple_of(dst0, SUBLANES), rows)], sem.at[sl])

    def wait_tile(tile, sl):
        for s, rows in enumerate(MOE_COPY_SIZES):
            def one(c, carry, rows=rows):
                strip_copy(0, 0, rows, sl).wait()
                return carry

            lax.fori_loop(0, exact_ref[tile * len(MOE_COPY_SIZES) + s], one, 0)

    et = et_ref[0]
    wt = wt_ref[0]
    eid = lax.broadcasted_iota(jnp.int32, (n_e, tm), 0)
    hots = [eid == et[k:k + 1, :] for k in range(TOP_K)]
    m_t = jnp.zeros((n_e, tm), F32)
    for h in hots:
        m_t = m_t + jnp.where(h, 1.0, 0.0)
    before = jnp.where(lax.broadcasted_iota(jnp.int32, (tm, tm), 0) < lax.broadcasted_iota(jnp.int32, (tm, tm), 1),
                       1.0, 0.0).astype(BF16)
    rank = jnp.dot(m_t.astype(BF16), before, preferred_element_type=F32)
    base = offc_ref[0].astype(F32) + rank
    lps = [jnp.sum(jnp.where(h, base, 0.0), axis=0, keepdims=True) for h in hots]
    lpi = [lp.astype(jnp.int32) for lp in lps]
    lp_ref[0] = jnp.concatenate(lpi, axis=0)

    wcols = []
    for rc in range(nrow // LANES):
        rio = lax.broadcasted_iota(jnp.int32, (LANES, tm), 0) + rc * LANES
        sel = jnp.zeros((LANES, tm), F32)
        wsel = jnp.zeros((LANES, tm), F32)
        for k in reversed(range(TOP_K)):
            hit = rio == lpi[k]
            sel = jnp.where(hit, 1.0, sel)
            wsel = jnp.where(hit, wt[k:k + 1, :], wsel)
        sel_scr[rc * LANES:(rc + 1) * LANES, :] = sel.astype(BF16)
        wcols.append(jnp.sum(wsel, axis=1, keepdims=True))

    @pl.when(i >= 2)
    def _():
        wait_tile(i - 2, slot)

    stage[slot, :, 0:d] = jnp.dot(sel_scr[...], x_ref[...].astype(BF16), preferred_element_type=F32)
    for rc, wcol in enumerate(wcols):
        stage[slot, rc * LANES:(rc + 1) * LANES, d:d + LANES] = jnp.broadcast_to(wcol, (LANES, LANES))

    _for_each_listed_copy(pieces_ref, exact_ref, i, tm,
                          lambda far, near, rows: strip_copy(near, far, rows, slot).start())

    def zero_fill(wait):
        def go(dst0, rows):
            cp = pltpu.make_async_copy(zero.at[pl.ds(0, rows)],
                                       xs_ref.at[pl.ds(pl.multiple_of(dst0, SUBLANES), rows)], sem.at[2])
            cp.wait() if wait else cp.start()

        def per_tail(e, carry):
            start = tail_ref[2 * e]
            _for_each_piece(tail_ref[2 * e + 1], lambda o, rows: go(start + o, rows))
            return carry

        lax.fori_loop(0, n_e, per_tail, 0)

        def per_block(blk, carry):
            go(blk * MOE_BLOCK, MOE_BLOCK)
            return carry

        lax.fori_loop(nu_ref[0], xs_ref.shape[0] // MOE_BLOCK, per_block, 0)

    @pl.when(i == last)
    def _():
        @pl.when(i >= 1)
        def _():
            wait_tile(i - 1, 1 - slot)

        wait_tile(i, slot)
        zero[...] = jnp.zeros_like(zero)
        zero_fill(wait=False)
        zero_fill(wait=True)


def _dispatch(x1, et, wt, plan):
    T, D = x1.shape
    nt = et.shape[0]
    tm = T // nt
    E = N_EXPERTS
    P = _moe_rows(T, nt, E)
    nrow = _stage_rows(tm)
    tile3 = lambda i, *_: (i, 0, 0)
    grid_spec = pltpu.PrefetchScalarGridSpec(
        num_scalar_prefetch=4,
        grid=(nt,),
        in_specs=[pl.BlockSpec((tm, D), lambda i, *_: (i, 0)),
                  pl.BlockSpec((1, TOP_K, tm), tile3), pl.BlockSpec((1, TOP_K, tm), tile3),
                  pl.BlockSpec((1, E, 1), tile3)],
        out_specs=[pl.BlockSpec(memory_space=pl.ANY), pl.BlockSpec((1, TOP_K, tm), tile3)],
        scratch_shapes=[pltpu.VMEM((2, nrow, D + LANES), F32),
                        pltpu.VMEM((nrow, tm), BF16),
                        pltpu.VMEM((MOE_BLOCK, D + LANES), F32),
                        pltpu.SemaphoreType.DMA((3,))],
    )
    return pl.pallas_call(
        functools.partial(_dispatch_kernel, tm=tm),
        grid_spec=grid_spec,
        out_shape=[jax.ShapeDtypeStruct((P, D + LANES), F32), jax.ShapeDtypeStruct((nt, TOP_K, tm), jnp.int32)],
        compiler_params=_params(("arbitrary",)),
        name="moe_dispatch",
    )(plan["pieces"], plan["exact"], plan["tail"], plan["n_used"], x1, et, wt, plan["off_col"])


def _combine_kernel(pieces_ref, exact_ref, ys_ref, lp_ref, x_ref, g_ref, b_ref, *rest, tm, kd):
    if kd:
        wqk_ref, wvg_ref, wgr_ref, w2_ref, b2_ref, xo_ref, qkl_ref, vg_ref, land, sel_scr, sem = rest
    else:
        xo_ref, land, sel_scr, sem = rest
    i = pl.program_id(0)
    nrow = _stage_rows(tm)
    slot = i % 2

    def strip_copy(src0, dst0, rows, sl):
        return pltpu.make_async_copy(ys_ref.at[pl.ds(pl.multiple_of(src0, SUBLANES), rows)],
                                     land.at[sl, pl.ds(pl.multiple_of(dst0, SUBLANES), rows)], sem.at[sl])

    def fetch(tile, sl):
        _for_each_listed_copy(pieces_ref, exact_ref, tile, tm,
                              lambda far, near, rows: strip_copy(far, near, rows, sl).start())

    @pl.when(i == 0)
    def _():
        land[...] = jnp.zeros_like(land)
        fetch(0, 0)

    @pl.when(i + 1 < pl.num_programs(0))
    def _():
        fetch(i + 1, 1 - slot)

    lp = lp_ref[0]
    for rc in range(nrow // LANES):
        rio = lax.broadcasted_iota(jnp.int32, (LANES, tm), 0) + rc * LANES
        piece = jnp.zeros((LANES, tm), F32)
        for k in range(TOP_K):
            piece = jnp.where(rio == lp[k:k + 1, :], 1.0, piece)
        sel_scr[rc * LANES:(rc + 1) * LANES, :] = piece.astype(BF16)

    for s, rows in enumerate(MOE_COPY_SIZES):
        def one(c, carry, rows=rows):
            strip_copy(0, 0, rows, slot).wait()
            return carry

        lax.fori_loop(0, exact_ref[i * len(MOE_COPY_SIZES) + s], one, 0)

    ffn = lax.dot_general(sel_scr[...], land[slot, 0:nrow, :].astype(BF16), (((0,), (0,)), ((), ())),
                          preferred_element_type=F32)
    xn = _layer_norm(DN_ALPHA * x_ref[...] + ffn, g_ref[...], b_ref[...])
    xo_ref[...] = xn
    if kd:
        xb = xn.astype(BF16)
        qkl_ref[:, :2 * kd] = jnp.dot(xb, wqk_ref[...], preferred_element_type=F32)
        vg_ref[...] = jnp.dot(xb, wvg_ref[...], preferred_element_type=F32)
        gr = jnp.dot(xb, wgr_ref[...], preferred_element_type=F32)
        z = jnp.dot(gr, w2_ref[...], precision=HIGHEST, preferred_element_type=F32) + b2_ref[...]
        log_sig = jnp.minimum(z, 0.0) - jnp.log1p(jnp.exp(-jnp.abs(z)))
        qkl_ref[:, 2 * kd:] = log_sig / GLA_GATE_TAU


def _combine_ln(ys, lp, x, plan, g, b, gla=None):
    T, D = x.shape
    tm = MOE_TILE
    nt = T // tm
    const = lambda i, *_: (0, 0)
    row = lambda i, *_: (i, 0)
    in_specs = [pl.BlockSpec(memory_space=pl.ANY), pl.BlockSpec((1, TOP_K, tm), lambda i, *_: (i, 0, 0)),
                pl.BlockSpec((tm, D), row),
                pl.BlockSpec((1, D), const), pl.BlockSpec((1, D), const)]
    out_specs = [pl.BlockSpec((tm, D), row)]
    out_shape = [jax.ShapeDtypeStruct((T, D), F32)]
    args = [ys, lp, x, g.reshape(1, D), b.reshape(1, D)]
    kd = 0
    if gla is not None:
        w_in1, w_gate2, b_gate, kd, vd = gla
        ws = [w_in1[:, :2 * kd].astype(BF16), w_in1[:, 2 * kd:2 * kd + 2 * vd].astype(BF16),
              w_in1[:, 2 * kd + 2 * vd:].astype(BF16), w_gate2.astype(F32), b_gate.reshape(1, kd)]
        in_specs += [pl.BlockSpec(w.shape, const) for w in ws]
        args += ws
        out_specs += [pl.BlockSpec((tm, 3 * kd), row), pl.BlockSpec((tm, 2 * vd), row)]
        out_shape += [jax.ShapeDtypeStruct((T, 3 * kd), F32), jax.ShapeDtypeStruct((T, 2 * vd), F32)]
    grid_spec = pltpu.PrefetchScalarGridSpec(
        num_scalar_prefetch=2,
        grid=(nt,),
        in_specs=in_specs,
        out_specs=out_specs,
        scratch_shapes=[pltpu.VMEM((2, _stage_rows(tm), D), F32), pltpu.VMEM((_stage_rows(tm), tm), BF16),
                        pltpu.SemaphoreType.DMA((2,))],
    )
    return pl.pallas_call(
        functools.partial(_combine_kernel, tm=tm, kd=kd),
        grid_spec=grid_spec,
        out_shape=out_shape,
        compiler_params=_params(("arbitrary",)),
        name="moe_combine_ln",
    )(plan["pieces"], plan["exact"], *args)


def _moe_ln(x1, et, wt, cnt, layer, w_gate_up, b_gate_up, w_down, b_down, g, b, gla=None):
    T = x1.shape[0]
    plan = _moe_plan(cnt, T)
    xs, lp = _dispatch(x1, et, wt, plan)
    wgu = _deinterleave_gate_up(w_gate_up, layer)
    bgu = jnp.concatenate([b_gate_up[layer, :, 0::2], b_gate_up[layer, :, 1::2]], axis=-1)[:, None, :].astype(F32)
    ys = _expert_ffn(xs, plan, wgu, w_down, layer, bgu, b_down[layer, :, None, :].astype(F32))
    return _combine_ln(ys, lp, x1, plan, g, b, gla)


def _gla_kernel(q_ref, k_ref, la_ref, v_ref, g_ref, nw_ref, o_ref, s_scr, *, tl, nh, dk, dv):
    C = GLA_CHUNK

    @pl.when(pl.program_id(1) == 0)
    def _():
        s_scr[...] = jnp.zeros_like(s_scr)

    ri = lax.broadcasted_iota(jnp.int32, (C, C), 0)
    ci = lax.broadcasted_iota(jnp.int32, (C, C), 1)
    lower = ri >= ci
    tri = jnp.where(lower, 1.0, 0.0).astype(BF16)
    states = [s_scr[h] for h in range(nh)]
    for c in range(tl // C):
        rows = slice(c * C, (c + 1) * C)
        rest = la_ref[0, rows, :]
        bcum = None
        for _ in range(3):
            piece = rest.astype(BF16)
            part = jnp.dot(tri, piece, preferred_element_type=F32)
            bcum = part if bcum is None else bcum + part
            rest = rest - piece.astype(F32)
        btot = bcum[C - 1:C, :]
        qb = (q_ref[0, rows, :] * (dk ** -0.5) * jnp.exp(bcum)).astype(BF16)
        kc = k_ref[0, rows, :]
        k_in = (kc * jnp.exp(-bcum)).astype(BF16)
        k_st = kc * jnp.exp(btot - bcum)
        dec = jnp.exp(btot)
        for h in range(nh):
            ks, vs = slice(h * dk, (h + 1) * dk), slice(h * dv, (h + 1) * dv)
            vb = v_ref[0, rows, vs].astype(BF16)
            att = lax.dot_general(qb[:, ks], k_in[:, ks], (((1,), (1,)), ((), ())), preferred_element_type=F32)
            att = jnp.where(lower, att, 0.0)
            o = (jnp.dot(att.astype(BF16), vb, preferred_element_type=F32)
                 + jnp.dot(qb[:, ks], states[h].astype(BF16), preferred_element_type=F32))
            kv = lax.dot_general(k_st[:, ks].astype(BF16), vb, (((0,), (0,)), ((), ())),
                                 preferred_element_type=F32)
            decay = jnp.broadcast_to(dec[:, ks], (SUBLANES, dk)).T[:, 0:1]
            states[h] = decay * states[h] + kv
            o = o * lax.rsqrt(jnp.mean(o * o, axis=-1, keepdims=True) + RMS_EPS) * nw_ref[...]
            gg = g_ref[0, rows, vs]
            o_ref[0, rows, vs] = o * (gg * jax.nn.sigmoid(gg))
    for h in range(nh):
        s_scr[h] = states[h]


def _gla(qkl, vg, norm_w, B, L, tl=256):
    T, kd3 = qkl.shape
    kd = kd3 // 3
    vd = vg.shape[1] // 2
    H = GLA_HEADS
    dk, dv = kd // H, vd // H
    qkl3 = qkl.reshape(B, L, kd3)
    vg3 = vg.reshape(B, L, 2 * vd)
    return pl.pallas_call(
        functools.partial(_gla_kernel, tl=tl, nh=H, dk=dk, dv=dv),
        grid=(B, L // tl),
        in_specs=[pl.BlockSpec((1, tl, kd), lambda b, l: (b, l, 0)),
                  pl.BlockSpec((1, tl, kd), lambda b, l: (b, l, 1)),
                  pl.BlockSpec((1, tl, kd), lambda b, l: (b, l, 2)),
                  pl.BlockSpec((1, tl, vd), lambda b, l: (b, l, 0)),
                  pl.BlockSpec((1, tl, vd), lambda b, l: (b, l, 1)),
                  pl.BlockSpec((1, dv), lambda b, l: (0, 0))],
        out_specs=pl.BlockSpec((1, tl, vd), lambda b, l: (b, l, 0)),
        out_shape=jax.ShapeDtypeStruct((B, L, vd), F32),
        scratch_shapes=[pltpu.VMEM((H, dk, dv), F32)],
        compiler_params=_params(("parallel", "arbitrary")),
        name="gla_mixer",
    )(qkl3, qkl3, qkl3, vg3, vg3, norm_w.reshape(1, dv).astype(F32))


def kernel(x, w_in0, s5_lam_re, s5_lam_im, s5_log_dt, s5_b_re, s5_b_im, s5_c_re, s5_c_im, s5_d, s5_w_glu,
           s5_b_glu, rel_bias, w_out0, w_in1, gla_w_gate2, gla_b_gate, gla_norm_w, w_out1, ln_mix_g, ln_mix_b,
           ln_ffn_g, ln_ffn_b, router_w, router_b, exp_w_gate_up, exp_b_gate_up, exp_w_down, exp_b_down):
    B, L, D = x.shape
    T = B * L
    s5w = s5_w_glu.shape[-1]
    kd = gla_w_gate2.shape[-1]
    vd = gla_norm_w.shape[-1] * GLA_HEADS

    u_tm, k_att, qt_att, vt_att = _inproj0(x, w_in0[0], s5w)
    y_a = _s5_mixer(u_tm.reshape(L, B, s5w), s5_lam_re[0], s5_lam_im[0], s5_log_dt[0], s5_b_re[0], s5_b_im[0],
                    s5_c_re[0], s5_c_im[0], s5_d[0].reshape(-1), s5_w_glu[0], s5_b_glu[0])
    y_b = _moba(qt_att, k_att, vt_att, rel_bias)
    tl = MIX_TILE
    w0 = w_out0[0].astype(BF16)
    ys = [(y_a.reshape(L, B * s5w), pl.BlockSpec((tl, s5w), lambda bb, l: (l, bb))),
          (y_b, pl.BlockSpec((1, tl, D - s5w), lambda bb, l: (bb, l, 0)))]
    x1, et, wt, cnt = _mix_ln(x, ys, [w0[:s5w], w0[s5w:]], ln_mix_g[0], ln_mix_b[0], router_w[0], router_b[0])
    x2, qkl, vg = _moe_ln(x1.reshape(T, D), et, wt, cnt, 0, exp_w_gate_up, exp_b_gate_up, exp_w_down,
                          exp_b_down, ln_ffn_g[0], ln_ffn_b[0],
                          gla=(w_in1[0], gla_w_gate2[0], gla_b_gate[0], kd, vd))

    y_c = _gla(qkl, vg, gla_norm_w[0], B, L)
    ys = [(y_c, pl.BlockSpec((1, tl, vd), lambda bb, l: (bb, l, 0)))]
    x3, et, wt, cnt = _mix_ln(x2.reshape(B, L, D), ys, [w_out1[0].astype(BF16)], ln_mix_g[1], ln_mix_b[1],
                              router_w[1], router_b[1])
    (out,) = _moe_ln(x3.reshape(T, D), et, wt, cnt, 1, exp_w_gate_up, exp_b_gate_up, exp_w_down,
                     exp_b_down, ln_ffn_g[1], ln_ffn_b[1])
    return out.reshape(B, L, D)
```

```python
import functools
import math

import jax
import jax.numpy as jnp
import numpy as np
from jax import lax
from jax.experimental import pallas as pl
from jax.experimental.pallas import tpu as pltpu

F32 = jnp.float32
BF16 = jnp.bfloat16
HIGHEST = lax.Precision.HIGHEST

DEPTH = 2
S5_GROUP = 16
S5_STATE = 64
MOBA_HEAD_DIM = 64
MOBA_BLOCK = 256
MOBA_TOPK = 3
REL_BUCKETS = 32
REL_MAX_DIST = 2048
GLA_HEADS = 4
GLA_GATE_TAU = 16.0
GLA_CHUNK = 64
N_EXPERTS = 32
TOP_K = 4
SWIGLU_LIMIT = 7.0
SWIGLU_ALPHA = 1.702
MOE_BLOCK = 512
MOE_TILE = 256
MIX_TILE = 1024
MOE_CHUNK = 32
SUBLANES = 8
MOE_REMAINDERS = (16, 8)
DN_ALPHA = (2 * DEPTH) ** 0.25
LN_EPS = 1e-5
RMS_EPS = 1e-5

V7X_VMEM_LIMIT_BYTES = 56 * 1024 * 1024
LANES = 128

S5_GROUPS_PER_CHUNK = LANES // S5_GROUP
S5_CHUNK_STATES = S5_GROUPS_PER_CHUNK * S5_STATE


def _params(sem):
    return pltpu.CompilerParams(dimension_semantics=sem, vmem_limit_bytes=V7X_VMEM_LIMIT_BYTES)


def _layer_norm(r, g, b):
    mu = jnp.mean(r, axis=-1, keepdims=True)
    c = r - mu
    var = jnp.mean(c * c, axis=-1, keepdims=True)
    return c * lax.rsqrt(var + LN_EPS) * g + b


def _inproj0_kernel(x_ref, wuk_ref, wqvt_ref, u_ref, k_ref, qt_ref, vt_ref, *, s5w):
    xb = x_ref[0].astype(BF16)
    h = jnp.dot(xb, wuk_ref[...], preferred_element_type=F32)
    u_ref[...] = h[:, :s5w]
    k_ref[0] = h[:, s5w:]
    ht = lax.dot_general(wqvt_ref[...], xb, (((1,), (1,)), ((), ())), preferred_element_type=F32)
    aw = ht.shape[0] // 2
    qt_ref[0] = ht[:aw]
    vt_ref[0] = ht[aw:].astype(BF16)


def _inproj0(x, w_in0, s5w, tl=512):
    B, L, D = x.shape
    aw = (w_in0.shape[1] - s5w) // 3
    wb = w_in0.astype(BF16)
    wuk = jnp.concatenate([wb[:, :s5w], wb[:, s5w + aw:s5w + 2 * aw]], axis=1)
    wqvt = jnp.concatenate([wb[:, s5w:s5w + aw], wb[:, s5w + 2 * aw:]], axis=1).T
    return pl.pallas_call(
        functools.partial(_inproj0_kernel, s5w=s5w),
        grid=(B, L // tl),
        in_specs=[pl.BlockSpec((1, tl, D), lambda b, l: (b, l, 0)),
                  pl.BlockSpec(wuk.shape, lambda b, l: (0, 0)),
                  pl.BlockSpec(wqvt.shape, lambda b, l: (0, 0))],
        out_specs=[pl.BlockSpec((tl, s5w), lambda b, l: (l, b)),
                   pl.BlockSpec((1, tl, aw), lambda b, l: (b, l, 0)),
                   pl.BlockSpec((1, aw, tl), lambda b, l: (b, 0, l)),
                   pl.BlockSpec((1, aw, tl), lambda b, l: (b, 0, l))],
        out_shape=[jax.ShapeDtypeStruct((L, B * s5w), F32),
                   jax.ShapeDtypeStruct((B, L, aw), F32),
                   jax.ShapeDtypeStruct((B, aw, L), F32),
                   jax.ShapeDtypeStruct((B, aw, L), BF16)],
        compiler_params=_params(("parallel", "parallel")),
        name="inproj0",
    )(x, wuk, wqvt)


def _s5_discretize(lam_re, lam_im, log_dt, b_re, b_im):
    dt = jnp.exp(log_dt.astype(F32))[:, None]
    lr, li = lam_re.astype(F32), lam_im.astype(F32)
    mag = jnp.exp(lr * dt)
    ab_re, ab_im = mag * jnp.cos(li * dt), mag * jnp.sin(li * dt)
    er, ei = ab_re - 1.0, ab_im
    den = lr * lr + li * li
    q_re = (er * lr + ei * li) / den
    q_im = (ei * lr - er * li) / den
    br_, bi_ = b_re.astype(F32), b_im.astype(F32)
    bb_re = q_re[..., None] * br_ - q_im[..., None] * bi_
    bb_im = q_re[..., None] * bi_ + q_im[..., None] * br_
    return ab_re, ab_im, bb_re, bb_im


def _s5_kernel(u_ref, bm_ref, cm_ref, are_ref, aim_ref, d_ref, wg_ref, bg_ref, y_ref,
               s_scr, st_scr, z_scr, *, tl, nb, nchunk):
    ns = S5_CHUNK_STATES

    @pl.when(pl.program_id(0) == 0)
    def _():
        st_scr[...] = jnp.zeros_like(st_scr)

    u = u_ref[...].reshape(tl * nb, nchunk * LANES)
    for j in range(nchunk):
        uj = u[:, j * LANES:(j + 1) * LANES]
        s_scr[...] = jnp.dot(uj.astype(BF16), bm_ref[j], preferred_element_type=F32)
        ar = jnp.broadcast_to(are_ref[j], (nb, ns))
        ai = jnp.broadcast_to(aim_ref[j], (nb, ns))

        def step(t, carry, ar=ar, ai=ai):
            sre, sim = carry
            r0 = pl.multiple_of(t * nb, nb)
            nre = ar * sre - ai * sim + s_scr[pl.ds(r0, nb), 0:ns]
            nim = ar * sim + ai * sre + s_scr[pl.ds(r0, nb), ns:2 * ns]
            s_scr[pl.ds(r0, nb), 0:ns] = nre
            s_scr[pl.ds(r0, nb), ns:2 * ns] = nim
            return nre, nim

        sre, sim = lax.fori_loop(0, tl, step, (st_scr[j, :, 0:ns], st_scr[j, :, ns:2 * ns]), unroll=2)
        st_scr[j, :, 0:ns] = sre
        st_scr[j, :, ns:2 * ns] = sim
        yj = jnp.dot(s_scr[...].astype(BF16), cm_ref[j], preferred_element_type=F32)
        z_scr[:, j * LANES:(j + 1) * LANES] = yj + uj * d_ref[:, j * LANES:(j + 1) * LANES]
    z = jax.nn.gelu(z_scr[...])
    gate = jax.nn.sigmoid(jnp.dot(z.astype(BF16), wg_ref[...], preferred_element_type=F32) + bg_ref[...])
    y_ref[...] = (z * gate).reshape(tl, nb, nchunk * LANES)


def _s5_mixer(u_tm, lam_re, lam_im, log_dt, b_re, b_im, c_re, c_im, d_skip, w_glu, b_glu, tl=64):
    L, B, W = u_tm.shape
    G, P, H = lam_re.shape[0], S5_STATE, S5_GROUP
    gc = S5_GROUPS_PER_CHUNK
    nchunk = G // gc
    ab_re, ab_im, bb_re, bb_im = _s5_discretize(lam_re, lam_im, log_dt, b_re, b_im)
    eye = jnp.eye(gc, dtype=F32)

    def b_blocks(bb):
        return jnp.einsum('jgph,gk->jghkp', bb.reshape(nchunk, gc, P, H), eye).reshape(nchunk, gc * H, gc * P)

    def c_blocks(cc):
        return jnp.einsum('jghp,gk->jgpkh', cc.reshape(nchunk, gc, H, P), eye).reshape(nchunk, gc * P, gc * H)

    bm = jnp.concatenate([b_blocks(bb_re), b_blocks(bb_im)], axis=2).astype(BF16)
    cm = jnp.concatenate([c_blocks(c_re.astype(F32)), -c_blocks(c_im.astype(F32))], axis=1).astype(BF16)
    are = ab_re.reshape(nchunk, 1, gc * P)
    aim = ab_im.reshape(nchunk, 1, gc * P)
    m = tl * B
    const3 = lambda l: (0, 0, 0)
    const2 = lambda l: (0, 0)
    return pl.pallas_call(
        functools.partial(_s5_kernel, tl=tl, nb=B, nchunk=nchunk),
        grid=(L // tl,),
        in_specs=[pl.BlockSpec((tl, B, W), lambda l: (l, 0, 0)),
                  pl.BlockSpec(bm.shape, const3), pl.BlockSpec(cm.shape, const3),
                  pl.BlockSpec(are.shape, const3), pl.BlockSpec(aim.shape, const3),
                  pl.BlockSpec((1, W), const2), pl.BlockSpec((W, W), const2), pl.BlockSpec((1, W), const2)],
        out_specs=pl.BlockSpec((tl, B, W), lambda l: (l, 0, 0)),
        out_shape=jax.ShapeDtypeStruct((L, B, W), F32),
        scratch_shapes=[pltpu.VMEM((m, 2 * S5_CHUNK_STATES), F32),
                        pltpu.VMEM((nchunk, B, 2 * S5_CHUNK_STATES), F32),
                        pltpu.VMEM((m, W), F32)],
        compiler_params=_params(("arbitrary",)),
        name="s5_mixer",
    )(u_tm, bm, cm, are, aim, d_skip.reshape(1, W).astype(F32), w_glu.astype(BF16),
      b_glu.reshape(1, W).astype(F32))


def _rel_bucket(n):
    n = jnp.maximum(n, 0)
    max_exact = REL_BUCKETS // 2
    nf = jnp.maximum(n, 1).astype(F32)
    large = max_exact + (jnp.log(nf / max_exact) / math.log(REL_MAX_DIST / max_exact)
                         * (REL_BUCKETS - max_exact)).astype(jnp.int32)
    large = jnp.minimum(large, REL_BUCKETS - 1)
    return jnp.where(n < max_exact, n, large)


MOBA_ONES_ROWS = 16
MOBA_NEG = -1e30
LOG2E = math.log2(math.e)


def _moba_kernel(qt_ref, k_ref, vt_ref, wv_ref, hot_ref, o_ref, bias_scr, kk_scr, va_scr, km_scr, qa_scr, s_scr,
                 *, nblk, bs, dh, topk):
    b = pl.program_id(1)
    tq = bs
    hpb = LANES // dh

    @pl.when(b == 0)
    def _():
        r_io = lax.broadcasted_iota(jnp.int32, (bs, 2 * bs), 0)
        c_io = lax.broadcasted_iota(jnp.int32, (bs, 2 * bs), 1)
        for hh in range(hpb):
            for dd in range(nblk):
                t = pltpu.roll(jnp.broadcast_to(wv_ref[hh, dd:dd + 1, :], (bs, 2 * bs)), bs + 1, 1,
                               stride=1, stride_axis=0)
                if dd == 0:
                    t = jnp.where(c_io >= r_io, t, MOBA_NEG)
                bias_scr[dd, :, hh * bs:(hh + 1) * bs] = t[:, 0:bs]

    def prepare():
        kf = k_ref[0]
        km_scr[...] = jnp.mean(kf.reshape(nblk, bs, LANES), axis=1)
        kk_scr[:, 0:LANES] = kf.astype(BF16)
        kk_scr[:, LANES:] = hot_ref[...]
        for n in range(nblk):
            va_scr[n, 0:LANES, :] = vt_ref[0, :, n * bs:(n + 1) * bs]
            va_scr[n, LANES:, :] = jnp.ones((MOBA_ONES_ROWS, bs), BF16)
        seq = nblk * bs
        q2 = qt_ref[0] * (dh ** -0.5 * LOG2E)
        f_io = lax.broadcasted_iota(jnp.int32, (LANES, seq), 0)
        blk_io = lax.broadcasted_iota(jnp.int32, (nblk, seq), 0)
        own_q = lax.broadcasted_iota(jnp.int32, (nblk, seq), 1) // bs
        qms, pens = [], []
        for hh in range(hpb):
            qm = jnp.where((f_io >= hh * dh) & (f_io < (hh + 1) * dh), q2, 0.0)
            gate = jnp.dot(km_scr[...], qm, precision=HIGHEST, preferred_element_type=F32)
            cnt = jnp.zeros((nblk, seq), jnp.int32)
            for m in range(nblk):
                gm = gate[m:m + 1, :]
                beats = (gm > gate) | ((gm == gate) & (m < blk_io))
                cnt = cnt + jnp.where(beats & (m < own_q), 1, 0)
            keep = (blk_io >= own_q) | (cnt < topk)
            pens.append(jnp.where(keep, 0.0, MOBA_NEG))
            qms.append(qm.astype(BF16))
        for n in range(nblk):
            cols = slice(n * bs, (n + 1) * bs)
            qa_scr[n, 0:LANES, :] = jnp.concatenate([qm[:, cols] for qm in qms], axis=1)
            pen = jnp.concatenate([p[:, cols] for p in pens], axis=1)
            qa_scr[n, LANES:, :] = jnp.concatenate(
                [pen, jnp.zeros((LANES - nblk, hpb * bs), F32)], axis=0).astype(BF16)

    prepare()

    for own in range(nblk):
        qa = qa_scr[own]
        m = None
        for j in range(own + 1):
            s = (jnp.dot(kk_scr[j * bs:(j + 1) * bs, :], qa, preferred_element_type=F32)
                 + bias_scr[own - j])
            s_scr[j] = s
            mj = jnp.max(s, axis=0, keepdims=True)
            m = mj if m is None else jnp.maximum(m, mj)
        acc = None
        for j in range(own + 1):
            p = jnp.exp2(s_scr[j] - m).astype(BF16)
            t = jnp.dot(va_scr[j], p, preferred_element_type=F32)
            acc = t if acc is None else acc + t
        on = acc[0:LANES, :] / acc[LANES:LANES + 1, :]
        ot = jnp.concatenate([on[hh * dh:(hh + 1) * dh, hh * tq:(hh + 1) * tq] for hh in range(hpb)], axis=0)
        o_ref[0, own * bs:(own + 1) * bs, :] = ot.T


def _moba(qt, k, vt, rel_bias):
    B, L, W = k.shape
    dh, bs = MOBA_HEAD_DIM, MOBA_BLOCK
    H = W // dh
    nblk = L // bs
    hpb = LANES // dh
    dist = jnp.arange(L, dtype=jnp.int32)
    by_dist = rel_bias.astype(F32).T[:, _rel_bucket(dist)] * LOG2E
    idx = np.clip(np.arange(nblk)[:, None] * bs - (bs - 1) + np.arange(2 * bs)[None, :], 0, L - 1)
    vecs = by_dist[:, idx]
    hot = np.zeros((L, LANES), np.float32)
    hot[np.arange(L), np.arange(L) // bs] = 1.0
    return pl.pallas_call(
        functools.partial(_moba_kernel, nblk=nblk, bs=bs, dh=dh, topk=min(MOBA_TOPK, nblk)),
        grid=(H // hpb, B),
        in_specs=[pl.BlockSpec((1, LANES, L), lambda h, b: (b, h, 0)),
                  pl.BlockSpec((1, L, LANES), lambda h, b: (b, 0, h)),
                  pl.BlockSpec((1, LANES, L), lambda h, b: (b, h, 0)),
                  pl.BlockSpec((hpb, nblk, 2 * bs), lambda h, b: (h, 0, 0)),
                  pl.BlockSpec((L, LANES), lambda h, b: (0, 0))],
        out_specs=pl.BlockSpec((1, L, LANES), lambda h, b: (b, 0, h)),
        out_shape=jax.ShapeDtypeStruct((B, L, W), F32),
        scratch_shapes=[pltpu.VMEM((nblk, bs, hpb * bs), F32),
                        pltpu.VMEM((L, 2 * LANES), BF16),
                        pltpu.VMEM((nblk, LANES + MOBA_ONES_ROWS, bs), BF16),
                        pltpu.VMEM((nblk, LANES), F32),
                        pltpu.VMEM((nblk, 2 * LANES, hpb * bs), BF16),
                        pltpu.VMEM((nblk, bs, hpb * bs), F32)],
        compiler_params=_params(("arbitrary", "arbitrary")),
        name="moba_attention",
    )(qt, k, vt, vecs, jnp.asarray(hot, BF16))


def _route_top_k(xn, wrt_ref, br_ref):
    lt = lax.dot_general(wrt_ref[...], xn, (((1,), (1,)), ((), ())), precision=HIGHEST,
                         preferred_element_type=F32) + br_ref[...]
    n_e = lt.shape[0]
    eid = lax.broadcasted_iota(jnp.int32, lt.shape, 0)
    cur = lt
    vals, idxs = [], []
    for _ in range(TOP_K):
        m = jnp.max(cur, axis=0, keepdims=True)
        idx = jnp.min(jnp.where(cur == m, eid, n_e), axis=0, keepdims=True)
        vals.append(m)
        idxs.append(idx)
        cur = jnp.where(eid == idx, -jnp.inf, cur)
    ex = [jnp.exp(v - vals[0]) for v in vals]
    den = ex[0]
    for t in ex[1:]:
        den = den + t
    hot = jnp.zeros(lt.shape, F32)
    for idx in idxs:
        hot = hot + jnp.where(eid == idx, 1.0, 0.0)
    return (jnp.concatenate(idxs, axis=0), jnp.concatenate([t / den for t in ex], axis=0),
            jnp.sum(hot, axis=1, keepdims=True))


def _mix_ln_kernel(*refs, n_in):
    x_ref = refs[0]
    y_refs = refs[1:1 + n_in]
    w_refs = refs[1 + n_in:1 + 2 * n_in]
    g_ref, b_ref, wrt_ref, br_ref, x1_ref, et_ref, wt_ref, cnt_ref = refs[1 + 2 * n_in:]
    mix = None
    for y_ref, w_ref in zip(y_refs, w_refs):
        y = y_ref[...]
        y = y.reshape(y.shape[-2], y.shape[-1]).astype(BF16)
        t = jnp.dot(y, w_ref[...], preferred_element_type=F32)
        mix = t if mix is None else mix + t
    xn = _layer_norm(DN_ALPHA * x_ref[0] + mix, g_ref[...], b_ref[...])
    x1_ref[0] = xn
    for s in range(et_ref.shape[0]):
        et, wt, cnt = _route_top_k(xn[s * MOE_TILE:(s + 1) * MOE_TILE], wrt_ref, br_ref)
        et_ref[s] = et
        wt_ref[s] = wt
        cnt_ref[s] = cnt


def _mix_ln(x, ys, ws, g, b, wr, br):
    B, L, D = x.shape
    E = wr.shape[1]
    tl = MIX_TILE
    nl = L // tl
    sub = tl // MOE_TILE
    nt = B * L // MOE_TILE
    const = lambda bb, l: (0, 0)
    tile = lambda bb, l: (bb * nl + l, 0, 0)
    in_specs = [pl.BlockSpec((1, tl, D), lambda bb, l: (bb, l, 0))]
    in_specs += [spec for _, spec in ys]
    in_specs += [pl.BlockSpec(w.shape, const) for w in ws]
    in_specs += [pl.BlockSpec((1, D), const), pl.BlockSpec((1, D), const),
                 pl.BlockSpec((E, D), const), pl.BlockSpec((E, 1), const)]
    return pl.pallas_call(
        functools.partial(_mix_ln_kernel, n_in=len(ys)),
        grid=(B, nl),
        in_specs=in_specs,
        out_specs=[pl.BlockSpec((1, tl, D), lambda bb, l: (bb, l, 0)),
                   pl.BlockSpec((sub, TOP_K, MOE_TILE), tile), pl.BlockSpec((sub, TOP_K, MOE_TILE), tile),
                   pl.BlockSpec((sub, E, 1), tile)],
        out_shape=[jax.ShapeDtypeStruct((B, L, D), F32),
                   jax.ShapeDtypeStruct((nt, TOP_K, MOE_TILE), jnp.int32),
                   jax.ShapeDtypeStruct((nt, TOP_K, MOE_TILE), F32),
                   jax.ShapeDtypeStruct((nt, E, 1), F32)],
        compiler_params=_params(("parallel", "parallel")),
        name="outproj_ln_router",
    )(x, *[a for a, _ in ys], *ws, g.reshape(1, D), b.reshape(1, D), wr.astype(F32).T, br.reshape(E, 1))


def _deinterleave_kernel(w_ref, p_ref, o_ref, *, ff):
    w = w_ref[0, 0].astype(BF16)
    for c in range(2 * ff // (2 * LANES)):
        t = jnp.dot(w[:, 2 * LANES * c:2 * LANES * (c + 1)], p_ref[...], preferred_element_type=F32)
        o_ref[0, :, LANES * c:LANES * (c + 1)] = t[:, :LANES].astype(BF16)
        o_ref[0, :, ff + LANES * c:ff + LANES * (c + 1)] = t[:, LANES:].astype(BF16)


def _deinterleave_gate_up(w_gate_up, layer, tk=512):
    _, E, D, F2 = w_gate_up.shape
    src = np.arange(2 * LANES)
    dst = np.where(src % 2 == 0, src // 2, LANES + src // 2)
    perm = np.zeros((2 * LANES, 2 * LANES), np.float32)
    perm[src, dst] = 1.0
    return pl.pallas_call(
        functools.partial(_deinterleave_kernel, ff=F2 // 2),
        grid=(E, D // tk),
        in_specs=[pl.BlockSpec((1, 1, tk, F2), lambda e, k: (layer, e, k, 0)),
                  pl.BlockSpec((2 * LANES, 2 * LANES), lambda e, k: (0, 0))],
        out_specs=pl.BlockSpec((1, tk, F2), lambda e, k: (e, k, 0)),
        out_shape=jax.ShapeDtypeStruct((E, D, F2), BF16),
        compiler_params=_params(("parallel", "parallel")),
        name="deinterleave_gate_up",
    )(w_gate_up, jnp.asarray(perm, BF16))


def _expert_kernel(be_ref, nu_ref, xs_ref, wgu_ref, wd_ref, bgu_ref, bd_ref, ys_ref):
    blk = pl.program_id(0)
    ff = wd_ref.shape[2]

    @pl.when(blk < nu_ref[0])
    def _():
        x = xs_ref[...].astype(BF16)
        h = jnp.dot(x, wgu_ref[0], preferred_element_type=F32) + bgu_ref[0]
        g = h[:, :ff]
        u = h[:, ff:]
        g = jnp.minimum(g, SWIGLU_LIMIT)
        u = jnp.clip(u, -SWIGLU_LIMIT, SWIGLU_LIMIT)
        act = g * jax.nn.sigmoid(SWIGLU_ALPHA * g) * (u + 1.0)
        ys_ref[...] = (jnp.dot(act.astype(BF16), wd_ref[0, 0].astype(BF16), preferred_element_type=F32)
                       + bd_ref[0])

    @pl.when(blk >= nu_ref[0])
    def _():
        ys_ref[...] = jnp.zeros_like(ys_ref)


def _expert_ffn(xs, plan, wgu, w_down, layer, bgu, bd):
    P, DW = xs.shape
    F, D = w_down.shape[2], w_down.shape[3]
    bm = MOE_BLOCK
    wmap = lambda i, be, nu: (be[i], 0, 0)
    rmap = lambda i, be, nu: (i, 0)
    xmap = lambda i, be, nu: (jnp.minimum(i, nu[0] - 1), 0)
    grid_spec = pltpu.PrefetchScalarGridSpec(
        num_scalar_prefetch=2,
        grid=(P // bm,),
        in_specs=[pl.BlockSpec((bm, DW), xmap),
                  pl.BlockSpec((1, D, 2 * F), wmap),
                  pl.BlockSpec((1, 1, F, D), lambda i, be, nu: (layer, be[i], 0, 0)),
                  pl.BlockSpec((1, 1, 2 * F), wmap), pl.BlockSpec((1, 1, D), wmap)],
        out_specs=pl.BlockSpec((bm, D), rmap),
    )
    return pl.pallas_call(
        _expert_kernel,
        grid_spec=grid_spec,
        out_shape=jax.ShapeDtypeStruct((P, D), F32),
        compiler_params=_params(("arbitrary",)),
        name="moe_experts",
    )(plan["blk_expert"], plan["n_used"], xs, wgu, w_down, bgu, bd)


def _moe_rows(T, nt, E):
    bound = T * TOP_K + (SUBLANES - 1) * E * nt + E * MOE_BLOCK
    return -(-bound // MOE_BLOCK) * MOE_BLOCK


def _stage_rows(tm):
    return TOP_K * tm + N_EXPERTS * SUBLANES


MOE_COPY_SIZES = (MOE_CHUNK,) + MOE_REMAINDERS
MOE_PACK = 256


def _piece_slots(tm):
    assert _stage_rows(tm) // SUBLANES <= MOE_PACK
    return (_stage_rows(tm) // MOE_CHUNK,) + (N_EXPERTS,) * len(MOE_REMAINDERS)


def _moe_plan(cnt, T):
    nt, E = cnt.shape[0], cnt.shape[1]
    c = cnt.reshape(nt, E).astype(jnp.int32)
    n8 = (c + SUBLANES - 1) // SUBLANES * SUBLANES
    tot = jnp.sum(n8, axis=0)
    seg = (tot + MOE_BLOCK - 1) // MOE_BLOCK * MOE_BLOCK
    seg_start = jnp.cumsum(seg) - seg
    strip = seg_start[None, :] + jnp.cumsum(n8, axis=0) - n8
    off = jnp.cumsum(n8, axis=1) - n8
    nblk = _moe_rows(T, nt, E) // MOE_BLOCK
    seg_blk = seg // MOE_BLOCK
    blk_end = jnp.cumsum(seg_blk)
    blk_ids = jnp.arange(nblk, dtype=jnp.int32)
    blk_expert = jnp.minimum(jnp.sum(blk_end[None, :] <= blk_ids[:, None], axis=1), E - 1).astype(jnp.int32)
    tail = jnp.stack([seg_start + tot, seg - tot], axis=1)
    per_class = [n8 // MOE_CHUNK] + [(n8 // r) % 2 for r in MOE_REMAINDERS]
    exact = jnp.stack([jnp.sum(p, axis=1) for p in per_class], axis=1)
    pieces = []
    done = jnp.zeros_like(n8)
    for p, rows, slots in zip(per_class, MOE_COPY_SIZES, _piece_slots(T // nt)):
        last = jnp.cumsum(p, axis=1)
        first = last - p
        k = jnp.arange(slots, dtype=jnp.int32)[None, :, None]
        mine = (first[:, None, :] <= k) & (k < last[:, None, :])
        inner = done[:, None, :] + (k - first[:, None, :]) * rows
        packed = ((strip[:, None, :] + inner) // SUBLANES * MOE_PACK + (off[:, None, :] + inner) // SUBLANES)
        pieces.append(jnp.sum(jnp.where(mine, packed, 0), axis=2))
        done = done + p * rows
    return dict(exact=exact.reshape(-1).astype(jnp.int32),
                pieces=jnp.concatenate([jnp.concatenate(pieces, axis=1).reshape(-1).astype(jnp.int32),
                                        jnp.zeros((1,), jnp.int32)]),
                off_col=off.reshape(nt, E, 1).astype(jnp.int32), blk_expert=blk_expert,
                tail=tail.reshape(-1).astype(jnp.int32),
                n_used=blk_end[-1:].astype(jnp.int32))


def _for_each_listed_copy(pieces_ref, exact_ref, tile, tm, fn):
    slots = _piece_slots(tm)
    base = tile * sum(slots)
    for s, rows in enumerate(MOE_COPY_SIZES):
        start = base + sum(slots[:s])

        def one(k, packed, start=start, rows=rows):
            ahead = pieces_ref[start + k + 1]
            fn(lax.shift_right_logical(packed, MOE_PACK.bit_length() - 1) * SUBLANES,
               (packed & (MOE_PACK - 1)) * SUBLANES, rows)
            return ahead

        lax.fori_loop(0, exact_ref[tile * len(MOE_COPY_SIZES) + s], one, pieces_ref[start])


def _for_each_piece(n, fn):
    nfull = n // MOE_CHUNK

    def per_chunk(c, carry):
        fn(c * MOE_CHUNK, MOE_CHUNK)
        return carry

    lax.fori_loop(0, nfull, per_chunk, 0)
    done = nfull * MOE_CHUNK
    for r in MOE_REMAINDERS:
        has = (n // r) % 2

        @pl.when(has == 1)
        def _(done=done, r=r):
            fn(done, r)

        done = done + has * r


def _dispatch_kernel(pieces_ref, exact_ref, tail_ref, nu_ref, x_ref, et_ref, offc_ref,
                     xs_ref, lp_ref, stage, sel_scr, zero, sem, *, tm):
    i = pl.program_id(0)
    last = pl.num_programs(0) - 1
    n_e = N_EXPERTS
    d = x_ref.shape[1]
    nrow = _stage_rows(tm)
    slot = i % 2

    def strip_copy(src0, dst0, rows, sl):
        return pltpu.make_async_copy(stage.at[sl, pl.ds(pl.multiple_of(src0, SUBLANES), rows)],
                                     xs_ref.at[pl.ds(pl.multiple_of(dst0, SUBLANES), rows)], sem.at[sl])

    def wait_tile(tile, sl):
        for s, rows in enumerate(MOE_COPY_SIZES):
            def one(c, carry, rows=rows):
                strip_copy(0, 0, rows, sl).wait()
                return carry

            lax.fori_loop(0, exact_ref[tile * len(MOE_COPY_SIZES) + s], one, 0)

    et = et_ref[0]
    eid = lax.broadcasted_iota(jnp.int32, (n_e, tm), 0)
    hots = [eid == et[k:k + 1, :] for k in range(TOP_K)]
    m_t = jnp.zeros((n_e, tm), F32)
    for h in hots:
        m_t = m_t + jnp.where(h, 1.0, 0.0)
    before = jnp.where(lax.broadcasted_iota(jnp.int32, (tm, tm), 0) < lax.broadcasted_iota(jnp.int32, (tm, tm), 1),
                       1.0, 0.0).astype(BF16)
    rank = jnp.dot(m_t.astype(BF16), before, preferred_element_type=F32)
    base = offc_ref[0].astype(F32) + rank
    lps = [jnp.sum(jnp.where(h, base, 0.0), axis=0, keepdims=True) for h in hots]
    lpi = [lp.astype(jnp.int32) for lp in lps]
    lp_ref[0] = jnp.concatenate(lpi, axis=0)

    for rc in range(nrow // LANES):
        rio = lax.broadcasted_iota(jnp.int32, (LANES, tm), 0) + rc * LANES
        sel = jnp.zeros((LANES, tm), F32)
        for k in range(TOP_K):
            sel = jnp.where(rio == lpi[k], 1.0, sel)
        sel_scr[rc * LANES:(rc + 1) * LANES, :] = sel.astype(BF16)

    @pl.when(i >= 2)
    def _():
        wait_tile(i - 2, slot)

    stage[slot] = jnp.dot(sel_scr[...], x_ref[...].astype(BF16), preferred_element_type=F32)

    _for_each_listed_copy(pieces_ref, exact_ref, i, tm,
                          lambda far, near, rows: strip_copy(near, far, rows, slot).start())

    def zero_fill(wait):
        def go(dst0, rows):
            cp = pltpu.make_async_copy(zero.at[pl.ds(0, rows)],
                                       xs_ref.at[pl.ds(pl.multiple_of(dst0, SUBLANES), rows)], sem.at[2])
            cp.wait() if wait else cp.start()

        def per_tail(e, carry):
            start = tail_ref[2 * e]
            _for_each_piece(tail_ref[2 * e + 1], lambda o, rows: go(start + o, rows))
            return carry

        lax.fori_loop(0, n_e, per_tail, 0)

        def per_block(blk, carry):
            go(blk * MOE_BLOCK, MOE_BLOCK)
            return carry

        lax.fori_loop(nu_ref[0], xs_ref.shape[0] // MOE_BLOCK, per_block, 0)

    @pl.when(i == last)
    def _():
        @pl.when(i >= 1)
        def _():
            wait_tile(i - 1, 1 - slot)

        wait_tile(i, slot)
        zero[...] = jnp.zeros_like(zero)
        zero_fill(wait=False)
        zero_fill(wait=True)


def _dispatch(x1, et, plan):
    T, D = x1.shape
    nt = et.shape[0]
    tm = T // nt
    E = N_EXPERTS
    P = _moe_rows(T, nt, E)
    nrow = _stage_rows(tm)
    tile3 = lambda i, *_: (i, 0, 0)
    grid_spec = pltpu.PrefetchScalarGridSpec(
        num_scalar_prefetch=4,
        grid=(nt,),
        in_specs=[pl.BlockSpec((tm, D), lambda i, *_: (i, 0)),
                  pl.BlockSpec((1, TOP_K, tm), tile3),
                  pl.BlockSpec((1, E, 1), tile3)],
        out_specs=[pl.BlockSpec(memory_space=pl.ANY), pl.BlockSpec((1, TOP_K, tm), tile3)],
        scratch_shapes=[pltpu.VMEM((2, nrow, D), F32),
                        pltpu.VMEM((nrow, tm), BF16),
                        pltpu.VMEM((MOE_BLOCK, D), F32),
                        pltpu.SemaphoreType.DMA((3,))],
    )
    return pl.pallas_call(
        functools.partial(_dispatch_kernel, tm=tm),
        grid_spec=grid_spec,
        out_shape=[jax.ShapeDtypeStruct((P, D), F32), jax.ShapeDtypeStruct((nt, TOP_K, tm), jnp.int32)],
        compiler_params=_params(("arbitrary",)),
        name="moe_dispatch",
    )(plan["pieces"], plan["exact"], plan["tail"], plan["n_used"], x1, et, plan["off_col"])


def _combine_kernel(pieces_ref, exact_ref, ys_ref, lp_ref, wt_ref, x_ref, g_ref, b_ref, *rest, tm, kd):
    if kd:
        wqk_ref, wvg_ref, wgr_ref, w2_ref, b2_ref, xo_ref, qkl_ref, vg_ref, land, sel_scr, sem = rest
    else:
        xo_ref, land, sel_scr, sem = rest
    i = pl.program_id(0)
    nrow = _stage_rows(tm)
    slot = i % 2

    def strip_copy(src0, dst0, rows, sl):
        return pltpu.make_async_copy(ys_ref.at[pl.ds(pl.multiple_of(src0, SUBLANES), rows)],
                                     land.at[sl, pl.ds(pl.multiple_of(dst0, SUBLANES), rows)], sem.at[sl])

    def fetch(tile, sl):
        _for_each_listed_copy(pieces_ref, exact_ref, tile, tm,
                              lambda far, near, rows: strip_copy(far, near, rows, sl).start())

    @pl.when(i == 0)
    def _():
        land[...] = jnp.zeros_like(land)
        fetch(0, 0)

    @pl.when(i + 1 < pl.num_programs(0))
    def _():
        fetch(i + 1, 1 - slot)

    lp = lp_ref[0]
    wt = wt_ref[0]
    for rc in range(nrow // LANES):
        rio = lax.broadcasted_iota(jnp.int32, (LANES, tm), 0) + rc * LANES
        piece = jnp.zeros((LANES, tm), F32)
        for k in range(TOP_K):
            piece = jnp.where(rio == lp[k:k + 1, :], wt[k:k + 1, :], piece)
        sel_scr[rc * LANES:(rc + 1) * LANES, :] = piece.astype(BF16)

    for s, rows in enumerate(MOE_COPY_SIZES):
        def one(c, carry, rows=rows):
            strip_copy(0, 0, rows, slot).wait()
            return carry

        lax.fori_loop(0, exact_ref[i * len(MOE_COPY_SIZES) + s], one, 0)

    ffn = lax.dot_general(sel_scr[...], land[slot, 0:nrow, :].astype(BF16), (((0,), (0,)), ((), ())),
                          preferred_element_type=F32)
    xn = _layer_norm(DN_ALPHA * x_ref[...] + ffn, g_ref[...], b_ref[...])
    xo_ref[...] = xn
    if kd:
        xb = xn.astype(BF16)
        qkl_ref[:, :2 * kd] = jnp.dot(xb, wqk_ref[...], preferred_element_type=F32)
        vg_ref[...] = jnp.dot(xb, wvg_ref[...], preferred_element_type=F32)
        gr = jnp.dot(xb, wgr_ref[...], preferred_element_type=F32)
        z = jnp.dot(gr, w2_ref[...], precision=HIGHEST, preferred_element_type=F32) + b2_ref[...]
        log_sig = jnp.minimum(z, 0.0) - jnp.log1p(jnp.exp(-jnp.abs(z)))
        qkl_ref[:, 2 * kd:] = log_sig / GLA_GATE_TAU


def _combine_ln(ys, lp, wt, x, plan, g, b, gla=None):
    T, D = x.shape
    tm = MOE_TILE
    nt = T // tm
    const = lambda i, *_: (0, 0)
    row = lambda i, *_: (i, 0)
    tile3 = lambda i, *_: (i, 0, 0)
    in_specs = [pl.BlockSpec(memory_space=pl.ANY), pl.BlockSpec((1, TOP_K, tm), tile3),
                pl.BlockSpec((1, TOP_K, tm), tile3), pl.BlockSpec((tm, D), row),
                pl.BlockSpec((1, D), const), pl.BlockSpec((1, D), const)]
    out_specs = [pl.BlockSpec((tm, D), row)]
    out_shape = [jax.ShapeDtypeStruct((T, D), F32)]
    args = [ys, lp, wt, x, g.reshape(1, D), b.reshape(1, D)]
    kd = 0
    if gla is not None:
        w_in1, w_gate2, b_gate, kd, vd = gla
        ws = [w_in1[:, :2 * kd].astype(BF16), w_in1[:, 2 * kd:2 * kd + 2 * vd].astype(BF16),
              w_in1[:, 2 * kd + 2 * vd:].astype(BF16), w_gate2.astype(F32), b_gate.reshape(1, kd)]
        in_specs += [pl.BlockSpec(w.shape, const) for w in ws]
        args += ws
        out_specs += [pl.BlockSpec((tm, 3 * kd), row), pl.BlockSpec((tm, 2 * vd), row)]
        out_shape += [jax.ShapeDtypeStruct((T, 3 * kd), F32), jax.ShapeDtypeStruct((T, 2 * vd), F32)]
    grid_spec = pltpu.PrefetchScalarGridSpec(
        num_scalar_prefetch=2,
        grid=(nt,),
        in_specs=in_specs,
        out_specs=out_specs,
        scratch_shapes=[pltpu.VMEM((2, _stage_rows(tm), D), F32), pltpu.VMEM((_stage_rows(tm), tm), BF16),
                        pltpu.SemaphoreType.DMA((2,))],
    )
    return pl.pallas_call(
        functools.partial(_combine_kernel, tm=tm, kd=kd),
        grid_spec=grid_spec,
        out_shape=out_shape,
        compiler_params=_params(("arbitrary",)),
        name="moe_combine_ln",
    )(plan["pieces"], plan["exact"], *args)


def _moe_ln(x1, et, wt, cnt, layer, w_gate_up, b_gate_up, w_down, b_down, g, b, gla=None):
    T = x1.shape[0]
    plan = _moe_plan(cnt, T)
    xs, lp = _dispatch(x1, et, plan)
    wgu = _deinterleave_gate_up(w_gate_up, layer)
    bgu = jnp.concatenate([b_gate_up[layer, :, 0::2], b_gate_up[layer, :, 1::2]], axis=-1)[:, None, :].astype(F32)
    ys = _expert_ffn(xs, plan, wgu, w_down, layer, bgu, b_down[layer, :, None, :].astype(F32))
    return _combine_ln(ys, lp, wt, x1, plan, g, b, gla)


def _gla_kernel(q_ref, k_ref, la_ref, v_ref, g_ref, nw_ref, o_ref, s_scr, *, tl, nh, dk, dv):
    C = GLA_CHUNK

    @pl.when(pl.program_id(1) == 0)
    def _():
        s_scr[...] = jnp.zeros_like(s_scr)

    ri = lax.broadcasted_iota(jnp.int32, (C, C), 0)
    ci = lax.broadcasted_iota(jnp.int32, (C, C), 1)
    lower = ri >= ci
    tri = jnp.where(lower, 1.0, 0.0).astype(BF16)
    states = [s_scr[h] for h in range(nh)]
    for c in range(tl // C):
        rows = slice(c * C, (c + 1) * C)
        rest = la_ref[0, rows, :]
        bcum = None
        for _ in range(3):
            piece = rest.astype(BF16)
            part = jnp.dot(tri, piece, preferred_element_type=F32)
            bcum = part if bcum is None else bcum + part
            rest = rest - piece.astype(F32)
        btot = bcum[C - 1:C, :]
        qb = (q_ref[0, rows, :] * (dk ** -0.5) * jnp.exp(bcum)).astype(BF16)
        kc = k_ref[0, rows, :]
        k_in = (kc * jnp.exp(-bcum)).astype(BF16)
        k_st = kc * jnp.exp(btot - bcum)
        dec = jnp.exp(btot)
        for h in range(nh):
            ks, vs = slice(h * dk, (h + 1) * dk), slice(h * dv, (h + 1) * dv)
            vb = v_ref[0, rows, vs].astype(BF16)
            att = lax.dot_general(qb[:, ks], k_in[:, ks], (((1,), (1,)), ((), ())), preferred_element_type=F32)
            att = jnp.where(lower, att, 0.0)
            o = (jnp.dot(att.astype(BF16), vb, preferred_element_type=F32)
                 + jnp.dot(qb[:, ks], states[h].astype(BF16), preferred_element_type=F32))
            kv = lax.dot_general(k_st[:, ks].astype(BF16), vb, (((0,), (0,)), ((), ())),
                                 preferred_element_type=F32)
            decay = jnp.broadcast_to(dec[:, ks], (SUBLANES, dk)).T[:, 0:1]
            states[h] = decay * states[h] + kv
            o = o * lax.rsqrt(jnp.mean(o * o, axis=-1, keepdims=True) + RMS_EPS) * nw_ref[...]
            gg = g_ref[0, rows, vs]
            o_ref[0, rows, vs] = o * (gg * jax.nn.sigmoid(gg))
    for h in range(nh):
        s_scr[h] = states[h]


def _gla(qkl, vg, norm_w, B, L, tl=256):
    T, kd3 = qkl.shape
    kd = kd3 // 3
    vd = vg.shape[1] // 2
    H = GLA_HEADS
    dk, dv = kd // H, vd // H
    qkl3 = qkl.reshape(B, L, kd3)
    vg3 = vg.reshape(B, L, 2 * vd)
    return pl.pallas_call(
        functools.partial(_gla_kernel, tl=tl, nh=H, dk=dk, dv=dv),
        grid=(B, L // tl),
        in_specs=[pl.BlockSpec((1, tl, kd), lambda b, l: (b, l, 0)),
                  pl.BlockSpec((1, tl, kd), lambda b, l: (b, l, 1)),
                  pl.BlockSpec((1, tl, kd), lambda b, l: (b, l, 2)),
                  pl.BlockSpec((1, tl, vd), lambda b, l: (b, l, 0)),
                  pl.BlockSpec((1, tl, vd), lambda b, l: (b, l, 1)),
                  pl.BlockSpec((1, dv), lambda b, l: (0, 0))],
        out_specs=pl.BlockSpec((1, tl, vd), lambda b, l: (b, l, 0)),
        out_shape=jax.ShapeDtypeStruct((B, L, vd), F32),
        scratch_shapes=[pltpu.VMEM((H, dk, dv), F32)],
        compiler_params=_params(("parallel", "arbitrary")),
        name="gla_mixer",
    )(qkl3, qkl3, qkl3, vg3, vg3, norm_w.reshape(1, dv).astype(F32))


def kernel(x, w_in0, s5_lam_re, s5_lam_im, s5_log_dt, s5_b_re, s5_b_im, s5_c_re, s5_c_im, s5_d, s5_w_glu,
           s5_b_glu, rel_bias, w_out0, w_in1, gla_w_gate2, gla_b_gate, gla_norm_w, w_out1, ln_mix_g, ln_mix_b,
           ln_ffn_g, ln_ffn_b, router_w, router_b, exp_w_gate_up, exp_b_gate_up, exp_w_down, exp_b_down):
    B, L, D = x.shape
    T = B * L
    s5w = s5_w_glu.shape[-1]
    kd = gla_w_gate2.shape[-1]
    vd = gla_norm_w.shape[-1] * GLA_HEADS

    u_tm, k_att, qt_att, vt_att = _inproj0(x, w_in0[0], s5w)
    y_a = _s5_mixer(u_tm.reshape(L, B, s5w), s5_lam_re[0], s5_lam_im[0], s5_log_dt[0], s5_b_re[0], s5_b_im[0],
                    s5_c_re[0], s5_c_im[0], s5_d[0].reshape(-1), s5_w_glu[0], s5_b_glu[0])
    y_b = _moba(qt_att, k_att, vt_att, rel_bias)
    tl = MIX_TILE
    w0 = w_out0[0].astype(BF16)
    ys = [(y_a.reshape(L, B * s5w), pl.BlockSpec((tl, s5w), lambda bb, l: (l, bb))),
          (y_b, pl.BlockSpec((1, tl, D - s5w), lambda bb, l: (bb, l, 0)))]
    x1, et, wt, cnt = _mix_ln(x, ys, [w0[:s5w], w0[s5w:]], ln_mix_g[0], ln_mix_b[0], router_w[0], router_b[0])
    x2, qkl, vg = _moe_ln(x1.reshape(T, D), et, wt, cnt, 0, exp_w_gate_up, exp_b_gate_up, exp_w_down,
                          exp_b_down, ln_ffn_g[0], ln_ffn_b[0],
                          gla=(w_in1[0], gla_w_gate2[0], gla_b_gate[0], kd, vd))

    y_c = _gla(qkl, vg, gla_norm_w[0], B, L)
    ys = [(y_c, pl.BlockSpec((1, tl, vd), lambda bb, l: (bb, l, 0)))]
    x3, et, wt, cnt = _mix_ln(x2.reshape(B, L, D), ys, [w_out1[0].astype(BF16)], ln_mix_g[1], ln_mix_b[1],
                              router_w[1], router_b[1])
    (out,) = _moe_ln(x3.reshape(T, D), et, wt, cnt, 1, exp_w_gate_up, exp_b_gate_up, exp_w_down,
                     exp_b_down, ln_ffn_g[1], ln_ffn_b[1])
    return out.reshape(B, L, D)
```

```python
import functools
import math

import jax
import jax.numpy as jnp
import numpy as np
from jax import lax
from jax.experimental import pallas as pl
from jax.experimental.pallas import tpu as pltpu

F32 = jnp.float32
BF16 = jnp.bfloat16
HIGHEST = lax.Precision.HIGHEST

DEPTH = 2
S5_GROUP = 16
S5_STATE = 64
MOBA_HEAD_DIM = 64
MOBA_BLOCK = 256
MOBA_TOPK = 3
REL_BUCKETS = 32
REL_MAX_DIST = 2048
GLA_HEADS = 4
GLA_GATE_TAU = 16.0
GLA_CHUNK = 64
N_EXPERTS = 32
TOP_K = 4
SWIGLU_LIMIT = 7.0
SWIGLU_ALPHA = 1.702
MOE_BLOCK = 512
MOE_TILE = 256
MIX_TILE = 1024
MOE_CHUNK = 32
SUBLANES = 8
MOE_REMAINDERS = (16, 8)
DN_ALPHA = (2 * DEPTH) ** 0.25
LN_EPS = 1e-5
RMS_EPS = 1e-5

V7X_VMEM_LIMIT_BYTES = 56 * 1024 * 1024
LANES = 128

S5_GROUPS_PER_CHUNK = LANES // S5_GROUP
S5_CHUNK_STATES = S5_GROUPS_PER_CHUNK * S5_STATE


def _params(sem):
    return pltpu.CompilerParams(dimension_semantics=sem, vmem_limit_bytes=V7X_VMEM_LIMIT_BYTES)


def _layer_norm(r, g, b):
    mu = jnp.mean(r, axis=-1, keepdims=True)
    c = r - mu
    var = jnp.mean(c * c, axis=-1, keepdims=True)
    return c * lax.rsqrt(var + LN_EPS) * g + b


def _inproj0_kernel(x_ref, wuk_ref, wqvt_ref, u_ref, k_ref, qt_ref, vt_ref, *, s5w):
    xb = x_ref[0].astype(BF16)
    h = jnp.dot(xb, wuk_ref[...], preferred_element_type=F32)
    u_ref[...] = h[:, :s5w]
    k_ref[0] = h[:, s5w:]
    ht = lax.dot_general(wqvt_ref[...], xb, (((1,), (1,)), ((), ())), preferred_element_type=F32)
    aw = ht.shape[0] // 2
    qt_ref[0] = ht[:aw]
    vt_ref[0] = ht[aw:].astype(BF16)


def _inproj0(x, w_in0, s5w, tl=512):
    B, L, D = x.shape
    aw = (w_in0.shape[1] - s5w) // 3
    wb = w_in0.astype(BF16)
    wuk = jnp.concatenate([wb[:, :s5w], wb[:, s5w + aw:s5w + 2 * aw]], axis=1)
    wqvt = jnp.concatenate([wb[:, s5w:s5w + aw], wb[:, s5w + 2 * aw:]], axis=1).T
    return pl.pallas_call(
        functools.partial(_inproj0_kernel, s5w=s5w),
        grid=(B, L // tl),
        in_specs=[pl.BlockSpec((1, tl, D), lambda b, l: (b, l, 0)),
                  pl.BlockSpec(wuk.shape, lambda b, l: (0, 0)),
                  pl.BlockSpec(wqvt.shape, lambda b, l: (0, 0))],
        out_specs=[pl.BlockSpec((tl, s5w), lambda b, l: (l, b)),
                   pl.BlockSpec((1, tl, aw), lambda b, l: (b, l, 0)),
                   pl.BlockSpec((1, aw, tl), lambda b, l: (b, 0, l)),
                   pl.BlockSpec((1, aw, tl), lambda b, l: (b, 0, l))],
        out_shape=[jax.ShapeDtypeStruct((L, B * s5w), F32),
                   jax.ShapeDtypeStruct((B, L, aw), F32),
                   jax.ShapeDtypeStruct((B, aw, L), F32),
                   jax.ShapeDtypeStruct((B, aw, L), BF16)],
        compiler_params=_params(("parallel", "parallel")),
        name="inproj0",
    )(x, wuk, wqvt)


def _s5_discretize(lam_re, lam_im, log_dt, b_re, b_im):
    dt = jnp.exp(log_dt.astype(F32))[:, None]
    lr, li = lam_re.astype(F32), lam_im.astype(F32)
    mag = jnp.exp(lr * dt)
    ab_re, ab_im = mag * jnp.cos(li * dt), mag * jnp.sin(li * dt)
    er, ei = ab_re - 1.0, ab_im
    den = lr * lr + li * li
    q_re = (er * lr + ei * li) / den
    q_im = (ei * lr - er * li) / den
    br_, bi_ = b_re.astype(F32), b_im.astype(F32)
    bb_re = q_re[..., None] * br_ - q_im[..., None] * bi_
    bb_im = q_re[..., None] * bi_ + q_im[..., None] * br_
    return ab_re, ab_im, bb_re, bb_im


def _s5_kernel(u_ref, bm_ref, cm_ref, are_ref, aim_ref, d_ref, wg_ref, bg_ref, y_ref,
               s_scr, st_scr, z_scr, *, tl, nb, nchunk):
    ns = S5_CHUNK_STATES

    @pl.when(pl.program_id(0) == 0)
    def _():
        st_scr[...] = jnp.zeros_like(st_scr)

    u = u_ref[...].reshape(tl * nb, nchunk * LANES)
    for j in range(nchunk):
        uj = u[:, j * LANES:(j + 1) * LANES]
        s_scr[...] = jnp.dot(uj.astype(BF16), bm_ref[j], preferred_element_type=F32)
        ar = jnp.broadcast_to(are_ref[j], (nb, ns))
        ai = jnp.broadcast_to(aim_ref[j], (nb, ns))

        def step(t, carry, ar=ar, ai=ai):
            sre, sim = carry
            r0 = pl.multiple_of(t * nb, nb)
            nre = ar * sre - ai * sim + s_scr[pl.ds(r0, nb), 0:ns]
            nim = ar * sim + ai * sre + s_scr[pl.ds(r0, nb), ns:2 * ns]
            s_scr[pl.ds(r0, nb), 0:ns] = nre
            s_scr[pl.ds(r0, nb), ns:2 * ns] = nim
            return nre, nim

        sre, sim = lax.fori_loop(0, tl, step, (st_scr[j, :, 0:ns], st_scr[j, :, ns:2 * ns]), unroll=2)
        st_scr[j, :, 0:ns] = sre
        st_scr[j, :, ns:2 * ns] = sim
        yj = jnp.dot(s_scr[...].astype(BF16), cm_ref[j], preferred_element_type=F32)
        z_scr[:, j * LANES:(j + 1) * LANES] = yj + uj * d_ref[:, j * LANES:(j + 1) * LANES]
    z = jax.nn.gelu(z_scr[...])
    gate = jax.nn.sigmoid(jnp.dot(z.astype(BF16), wg_ref[...], preferred_element_type=F32) + bg_ref[...])
    y_ref[...] = (z * gate).reshape(tl, nb, nchunk * LANES)


def _s5_mixer(u_tm, lam_re, lam_im, log_dt, b_re, b_im, c_re, c_im, d_skip, w_glu, b_glu, tl=64):
    L, B, W = u_tm.shape
    G, P, H = lam_re.shape[0], S5_STATE, S5_GROUP
    gc = S5_GROUPS_PER_CHUNK
    nchunk = G // gc
    ab_re, ab_im, bb_re, bb_im = _s5_discretize(lam_re, lam_im, log_dt, b_re, b_im)
    eye = jnp.eye(gc, dtype=F32)

    def b_blocks(bb):
        return jnp.einsum('jgph,gk->jghkp', bb.reshape(nchunk, gc, P, H), eye).reshape(nchunk, gc * H, gc * P)

    def c_blocks(cc):
        return jnp.einsum('jghp,gk->jgpkh', cc.reshape(nchunk, gc, H, P), eye).reshape(nchunk, gc * P, gc * H)

    bm = jnp.concatenate([b_blocks(bb_re), b_blocks(bb_im)], axis=2).astype(BF16)
    cm = jnp.concatenate([c_blocks(c_re.astype(F32)), -c_blocks(c_im.astype(F32))], axis=1).astype(BF16)
    are = ab_re.reshape(nchunk, 1, gc * P)
    aim = ab_im.reshape(nchunk, 1, gc * P)
    m = tl * B
    const3 = lambda l: (0, 0, 0)
    const2 = lambda l: (0, 0)
    return pl.pallas_call(
        functools.partial(_s5_kernel, tl=tl, nb=B, nchunk=nchunk),
        grid=(L // tl,),
        in_specs=[pl.BlockSpec((tl, B, W), lambda l: (l, 0, 0)),
                  pl.BlockSpec(bm.shape, const3), pl.BlockSpec(cm.shape, const3),
                  pl.BlockSpec(are.shape, const3), pl.BlockSpec(aim.shape, const3),
                  pl.BlockSpec((1, W), const2), pl.BlockSpec((W, W), const2), pl.BlockSpec((1, W), const2)],
        out_specs=pl.BlockSpec((tl, B, W), lambda l: (l, 0, 0)),
        out_shape=jax.ShapeDtypeStruct((L, B, W), F32),
        scratch_shapes=[pltpu.VMEM((m, 2 * S5_CHUNK_STATES), F32),
                        pltpu.VMEM((nchunk, B, 2 * S5_CHUNK_STATES), F32),
                        pltpu.VMEM((m, W), F32)],
        compiler_params=_params(("arbitrary",)),
        name="s5_mixer",
    )(u_tm, bm, cm, are, aim, d_skip.reshape(1, W).astype(F32), w_glu.astype(BF16),
      b_glu.reshape(1, W).astype(F32))


def _rel_bucket(n):
    n = jnp.maximum(n, 0)
    max_exact = REL_BUCKETS // 2
    nf = jnp.maximum(n, 1).astype(F32)
    large = max_exact + (jnp.log(nf / max_exact) / math.log(REL_MAX_DIST / max_exact)
                         * (REL_BUCKETS - max_exact)).astype(jnp.int32)
    large = jnp.minimum(large, REL_BUCKETS - 1)
    return jnp.where(n < max_exact, n, large)


MOBA_ONES_ROWS = 16
MOBA_NEG = -1e30
LOG2E = math.log2(math.e)


def _moba_kernel(qt_ref, k_ref, vt_ref, wv_ref, hot_ref, o_ref, bias_scr, kk_scr, va_scr, km_scr, qa_scr, s_scr,
                 *, nblk, bs, dh, topk):
    b = pl.program_id(1)
    tq = bs
    hpb = LANES // dh

    @pl.when(b == 0)
    def _():
        r_io = lax.broadcasted_iota(jnp.int32, (bs, 2 * bs), 0)
        c_io = lax.broadcasted_iota(jnp.int32, (bs, 2 * bs), 1)
        for hh in range(hpb):
            for dd in range(nblk):
                t = pltpu.roll(jnp.broadcast_to(wv_ref[hh, dd:dd + 1, :], (bs, 2 * bs)), bs + 1, 1,
                               stride=1, stride_axis=0)
                if dd == 0:
                    t = jnp.where(c_io >= r_io, t, MOBA_NEG)
                bias_scr[dd, :, hh * bs:(hh + 1) * bs] = t[:, 0:bs]

    def prepare():
        kf = k_ref[0]
        km_scr[...] = jnp.mean(kf.reshape(nblk, bs, LANES), axis=1)
        kk_scr[:, 0:LANES] = kf.astype(BF16)
        kk_scr[:, LANES:] = hot_ref[...]
        for n in range(nblk):
            va_scr[n, 0:LANES, :] = vt_ref[0, :, n * bs:(n + 1) * bs]
            va_scr[n, LANES:, :] = jnp.ones((MOBA_ONES_ROWS, bs), BF16)
        seq = nblk * bs
        q2 = qt_ref[0] * (dh ** -0.5 * LOG2E)
        f_io = lax.broadcasted_iota(jnp.int32, (LANES, seq), 0)
        blk_io = lax.broadcasted_iota(jnp.int32, (nblk, seq), 0)
        own_q = lax.broadcasted_iota(jnp.int32, (nblk, seq), 1) // bs
        qms, pens = [], []
        for hh in range(hpb):
            qm = jnp.where((f_io >= hh * dh) & (f_io < (hh + 1) * dh), q2, 0.0)
            gate = jnp.dot(km_scr[...], qm, precision=HIGHEST, preferred_element_type=F32)
            cnt = jnp.zeros((nblk, seq), jnp.int32)
            for m in range(nblk):
                gm = gate[m:m + 1, :]
                beats = (gm > gate) | ((gm == gate) & (m < blk_io))
                cnt = cnt + jnp.where(beats & (m < own_q), 1, 0)
            keep = (blk_io >= own_q) | (cnt < topk)
            pens.append(jnp.where(keep, 0.0, MOBA_NEG))
            qms.append(qm.astype(BF16))
        for n in range(nblk):
            cols = slice(n * bs, (n + 1) * bs)
            qa_scr[n, 0:LANES, :] = jnp.concatenate([qm[:, cols] for qm in qms], axis=1)
            pen = jnp.concatenate([p[:, cols] for p in pens], axis=1)
            qa_scr[n, LANES:, :] = jnp.concatenate(
                [pen, jnp.zeros((LANES - nblk, hpb * bs), F32)], axis=0).astype(BF16)

    prepare()

    for own in range(nblk):
        qa = qa_scr[own]
        m = None
        for j in range(own + 1):
            s = (jnp.dot(kk_scr[j * bs:(j + 1) * bs, :], qa, preferred_element_type=F32)
                 + bias_scr[own - j])
            s_scr[j] = s
            mj = jnp.max(s, axis=0, keepdims=True)
            m = mj if m is None else jnp.maximum(m, mj)
        acc = None
        for j in range(own + 1):
            p = jnp.exp2(s_scr[j] - m).astype(BF16)
            t = jnp.dot(va_scr[j], p, preferred_element_type=F32)
            acc = t if acc is None else acc + t
        on = acc[0:LANES, :] / acc[LANES:LANES + 1, :]
        ot = jnp.concatenate([on[hh * dh:(hh + 1) * dh, hh * tq:(hh + 1) * tq] for hh in range(hpb)], axis=0)
        o_ref[0, own * bs:(own + 1) * bs, :] = ot.T


def _moba(qt, k, vt, rel_bias):
    B, L, W = k.shape
    dh, bs = MOBA_HEAD_DIM, MOBA_BLOCK
    H = W // dh
    nblk = L // bs
    hpb = LANES // dh
    dist = jnp.arange(L, dtype=jnp.int32)
    by_dist = rel_bias.astype(F32).T[:, _rel_bucket(dist)] * LOG2E
    idx = np.clip(np.arange(nblk)[:, None] * bs - (bs - 1) + np.arange(2 * bs)[None, :], 0, L - 1)
    vecs = by_dist[:, idx]
    hot = np.zeros((L, LANES), np.float32)
    hot[np.arange(L), np.arange(L) // bs] = 1.0
    return pl.pallas_call(
        functools.partial(_moba_kernel, nblk=nblk, bs=bs, dh=dh, topk=min(MOBA_TOPK, nblk)),
        grid=(H // hpb, B),
        in_specs=[pl.BlockSpec((1, LANES, L), lambda h, b: (b, h, 0)),
                  pl.BlockSpec((1, L, LANES), lambda h, b: (b, 0, h)),
                  pl.BlockSpec((1, LANES, L), lambda h, b: (b, h, 0)),
                  pl.BlockSpec((hpb, nblk, 2 * bs), lambda h, b: (h, 0, 0)),
                  pl.BlockSpec((L, LANES), lambda h, b: (0, 0))],
        out_specs=pl.BlockSpec((1, L, LANES), lambda h, b: (b, 0, h)),
        out_shape=jax.ShapeDtypeStruct((B, L, W), F32),
        scratch_shapes=[pltpu.VMEM((nblk, bs, hpb * bs), F32),
                        pltpu.VMEM((L, 2 * LANES), BF16),
                        pltpu.VMEM((nblk, LANES + MOBA_ONES_ROWS, bs), BF16),
                        pltpu.VMEM((nblk, LANES), F32),
                        pltpu.VMEM((nblk, 2 * LANES, hpb * bs), BF16),
                        pltpu.VMEM((nblk, bs, hpb * bs), F32)],
        compiler_params=_params(("arbitrary", "arbitrary")),
        name="moba_attention",
    )(qt, k, vt, vecs, jnp.asarray(hot, BF16))


def _route_top_k(xn, wrt_ref, br_ref):
    lt = lax.dot_general(wrt_ref[...], xn, (((1,), (1,)), ((), ())), precision=HIGHEST,
                         preferred_element_type=F32) + br_ref[...]
    n_e = lt.shape[0]
    eid = lax.broadcasted_iota(jnp.int32, lt.shape, 0)
    cur = lt
    vals, idxs = [], []
    for _ in range(TOP_K):
        m = jnp.max(cur, axis=0, keepdims=True)
        idx = jnp.min(jnp.where(cur == m, eid, n_e), axis=0, keepdims=True)
        vals.append(m)
        idxs.append(idx)
        cur = jnp.where(eid == idx, -jnp.inf, cur)
    ex = [jnp.exp(v - vals[0]) for v in vals]
    den = ex[0]
    for t in ex[1:]:
        den = den + t
    hot = jnp.zeros(lt.shape, F32)
    for idx in idxs:
        hot = hot + jnp.where(eid == idx, 1.0, 0.0)
    return (jnp.concatenate(idxs, axis=0), jnp.concatenate([t / den for t in ex], axis=0),
            jnp.sum(hot, axis=1, keepdims=True))


def _mix_ln_kernel(*refs, n_in):
    x_ref = refs[0]
    y_refs = refs[1:1 + n_in]
    w_refs = refs[1 + n_in:1 + 2 * n_in]
    g_ref, b_ref, wrt_ref, br_ref, x1_ref, et_ref, wt_ref, cnt_ref = refs[1 + 2 * n_in:]
    mix = None
    for y_ref, w_ref in zip(y_refs, w_refs):
        y = y_ref[...]
        y = y.reshape(y.shape[-2], y.shape[-1]).astype(BF16)
        t = jnp.dot(y, w_ref[...], preferred_element_type=F32)
        mix = t if mix is None else mix + t
    xn = _layer_norm(DN_ALPHA * x_ref[0] + mix, g_ref[...], b_ref[...])
    x1_ref[0] = xn
    for s in range(et_ref.shape[0]):
        et, wt, cnt = _route_top_k(xn[s * MOE_TILE:(s + 1) * MOE_TILE], wrt_ref, br_ref)
        et_ref[s] = et
        wt_ref[s] = wt
        cnt_ref[s] = cnt


def _mix_ln(x, ys, ws, g, b, wr, br):
    B, L, D = x.shape
    E = wr.shape[1]
    tl = MIX_TILE
    nl = L // tl
    sub = tl // MOE_TILE
    nt = B * L // MOE_TILE
    const = lambda bb, l: (0, 0)
    tile = lambda bb, l: (bb * nl + l, 0, 0)
    in_specs = [pl.BlockSpec((1, tl, D), lambda bb, l: (bb, l, 0))]
    in_specs += [spec for _, spec in ys]
    in_specs += [pl.BlockSpec(w.shape, const) for w in ws]
    in_specs += [pl.BlockSpec((1, D), const), pl.BlockSpec((1, D), const),
                 pl.BlockSpec((E, D), const), pl.BlockSpec((E, 1), const)]
    return pl.pallas_call(
        functools.partial(_mix_ln_kernel, n_in=len(ys)),
        grid=(B, nl),
        in_specs=in_specs,
        out_specs=[pl.BlockSpec((1, tl, D), lambda bb, l: (bb, l, 0)),
                   pl.BlockSpec((sub, TOP_K, MOE_TILE), tile), pl.BlockSpec((sub, TOP_K, MOE_TILE), tile),
                   pl.BlockSpec((sub, E, 1), tile)],
        out_shape=[jax.ShapeDtypeStruct((B, L, D), F32),
                   jax.ShapeDtypeStruct((nt, TOP_K, MOE_TILE), jnp.int32),
                   jax.ShapeDtypeStruct((nt, TOP_K, MOE_TILE), F32),
                   jax.ShapeDtypeStruct((nt, E, 1), F32)],
        compiler_params=_params(("parallel", "parallel")),
        name="outproj_ln_router",
    )(x, *[a for a, _ in ys], *ws, g.reshape(1, D), b.reshape(1, D), wr.astype(F32).T, br.reshape(E, 1))


def _deinterleave_kernel(w_ref, p_ref, o_ref, *, ff):
    w = w_ref[0, 0].astype(BF16)
    for c in range(2 * ff // (2 * LANES)):
        t = jnp.dot(w[:, 2 * LANES * c:2 * LANES * (c + 1)], p_ref[...], preferred_element_type=F32)
        o_ref[0, :, LANES * c:LANES * (c + 1)] = t[:, :LANES].astype(BF16)
        o_ref[0, :, ff + LANES * c:ff + LANES * (c + 1)] = t[:, LANES:].astype(BF16)


def _deinterleave_gate_up(w_gate_up, layer, tk=512):
    _, E, D, F2 = w_gate_up.shape
    src = np.arange(2 * LANES)
    dst = np.where(src % 2 == 0, src // 2, LANES + src // 2)
    perm = np.zeros((2 * LANES, 2 * LANES), np.float32)
    perm[src, dst] = 1.0
    return pl.pallas_call(
        functools.partial(_deinterleave_kernel, ff=F2 // 2),
        grid=(E, D // tk),
        in_specs=[pl.BlockSpec((1, 1, tk, F2), lambda e, k: (layer, e, k, 0)),
                  pl.BlockSpec((2 * LANES, 2 * LANES), lambda e, k: (0, 0))],
        out_specs=pl.BlockSpec((1, tk, F2), lambda e, k: (e, k, 0)),
        out_shape=jax.ShapeDtypeStruct((E, D, F2), BF16),
        compiler_params=_params(("parallel", "parallel")),
        name="deinterleave_gate_up",
    )(w_gate_up, jnp.asarray(perm, BF16))


def _expert_kernel(be_ref, nu_ref, xs_ref, wgu_ref, wd_ref, bgu_ref, bd_ref, ys_ref):
    blk = pl.program_id(0)
    ff = wd_ref.shape[2]

    @pl.when(blk < nu_ref[0])
    def _():
        x = xs_ref[...].astype(BF16)
        h = jnp.dot(x, wgu_ref[0], preferred_element_type=F32) + bgu_ref[0]
        g = h[:, :ff]
        u = h[:, ff:]
        g = jnp.minimum(g, SWIGLU_LIMIT)
        u = jnp.clip(u, -SWIGLU_LIMIT, SWIGLU_LIMIT)
        act = g * jax.nn.sigmoid(SWIGLU_ALPHA * g) * (u + 1.0)
        ys_ref[...] = (jnp.dot(act.astype(BF16), wd_ref[0, 0].astype(BF16), preferred_element_type=F32)
                       + bd_ref[0])

    @pl.when(blk >= nu_ref[0])
    def _():
        ys_ref[...] = jnp.zeros_like(ys_ref)


def _expert_ffn(xs, plan, wgu, w_down, layer, bgu, bd):
    P, DW = xs.shape
    F, D = w_down.shape[2], w_down.shape[3]
    bm = MOE_BLOCK
    wmap = lambda i, be, nu: (be[i], 0, 0)
    rmap = lambda i, be, nu: (i, 0)
    xmap = lambda i, be, nu: (jnp.minimum(i, nu[0] - 1), 0)
    grid_spec = pltpu.PrefetchScalarGridSpec(
        num_scalar_prefetch=2,
        grid=(P // bm,),
        in_specs=[pl.BlockSpec((bm, DW), xmap),
                  pl.BlockSpec((1, D, 2 * F), wmap),
                  pl.BlockSpec((1, 1, F, D), lambda i, be, nu: (layer, be[i], 0, 0)),
                  pl.BlockSpec((1, 1, 2 * F), wmap), pl.BlockSpec((1, 1, D), wmap)],
        out_specs=pl.BlockSpec((bm, D), rmap),
    )
    return pl.pallas_call(
        _expert_kernel,
        grid_spec=grid_spec,
        out_shape=jax.ShapeDtypeStruct((P, D), F32),
        compiler_params=_params(("arbitrary",)),
        name="moe_experts",
    )(plan["blk_expert"], plan["n_used"], xs, wgu, w_down, bgu, bd)


def _moe_rows(T, nt, E):
    bound = T * TOP_K + (SUBLANES - 1) * E * nt + E * MOE_BLOCK
    return -(-bound // MOE_BLOCK) * MOE_BLOCK


def _stage_rows(tm):
    return TOP_K * tm + N_EXPERTS * SUBLANES


MOE_COPY_SIZES = (MOE_CHUNK,) + MOE_REMAINDERS
MOE_PACK = 256


def _piece_slots(tm):
    assert _stage_rows(tm) // SUBLANES <= MOE_PACK
    return (_stage_rows(tm) // MOE_CHUNK,) + (N_EXPERTS,) * len(MOE_REMAINDERS)


def _moe_plan(cnt, T):
    nt, E = cnt.shape[0], cnt.shape[1]
    c = cnt.reshape(nt, E).astype(jnp.int32)
    n8 = (c + SUBLANES - 1) // SUBLANES * SUBLANES
    tot = jnp.sum(n8, axis=0)
    seg = (tot + MOE_BLOCK - 1) // MOE_BLOCK * MOE_BLOCK
    seg_start = jnp.cumsum(seg) - seg
    strip = seg_start[None, :] + jnp.cumsum(n8, axis=0) - n8
    off = jnp.cumsum(n8, axis=1) - n8
    nblk = _moe_rows(T, nt, E) // MOE_BLOCK
    seg_blk = seg // MOE_BLOCK
    blk_end = jnp.cumsum(seg_blk)
    blk_ids = jnp.arange(nblk, dtype=jnp.int32)
    blk_expert = jnp.minimum(jnp.sum(blk_end[None, :] <= blk_ids[:, None], axis=1), E - 1).astype(jnp.int32)
    tail = jnp.stack([seg_start + tot, seg - tot], axis=1)
    per_class = [n8 // MOE_CHUNK] + [(n8 // r) % 2 for r in MOE_REMAINDERS]
    exact = jnp.stack([jnp.sum(p, axis=1) for p in per_class], axis=1)
    pieces = []
    done = jnp.zeros_like(n8)
    for p, rows, slots in zip(per_class, MOE_COPY_SIZES, _piece_slots(T // nt)):
        last = jnp.cumsum(p, axis=1)
        first = last - p
        k = jnp.arange(slots, dtype=jnp.int32)[None, :, None]
        mine = (first[:, None, :] <= k) & (k < last[:, None, :])
        inner = done[:, None, :] + (k - first[:, None, :]) * rows
        packed = ((strip[:, None, :] + inner) // SUBLANES * MOE_PACK + (off[:, None, :] + inner) // SUBLANES)
        pieces.append(jnp.sum(jnp.where(mine, packed, 0), axis=2))
        done = done + p * rows
    return dict(exact=exact.reshape(-1).astype(jnp.int32),
                pieces=jnp.concatenate([jnp.concatenate(pieces, axis=1).reshape(-1).astype(jnp.int32),
                                        jnp.zeros((1,), jnp.int32)]),
                off_col=off.reshape(nt, E, 1).astype(jnp.int32), blk_expert=blk_expert,
                tail=tail.reshape(-1).astype(jnp.int32),
                n_used=blk_end[-1:].astype(jnp.int32))


def _for_each_listed_copy(pieces_ref, exact_ref, tile, tm, fn):
    slots = _piece_slots(tm)
    base = tile * sum(slots)
    for s, rows in enumerate(MOE_COPY_SIZES):
        start = base + sum(slots[:s])

        def one(k, packed, start=start, rows=rows, prio=min(s, 1)):
            ahead = pieces_ref[start + k + 1]
            fn(lax.shift_right_logical(packed, MOE_PACK.bit_length() - 1) * SUBLANES,
               (packed & (MOE_PACK - 1)) * SUBLANES, rows, prio)
            return ahead

        lax.fori_loop(0, exact_ref[tile * len(MOE_COPY_SIZES) + s], one, pieces_ref[start])


def _for_each_piece(n, fn):
    nfull = n // MOE_CHUNK

    def per_chunk(c, carry):
        fn(c * MOE_CHUNK, MOE_CHUNK)
        return carry

    lax.fori_loop(0, nfull, per_chunk, 0)
    done = nfull * MOE_CHUNK
    for r in MOE_REMAINDERS:
        has = (n // r) % 2

        @pl.when(has == 1)
        def _(done=done, r=r):
            fn(done, r)

        done = done + has * r


def _dispatch_kernel(pieces_ref, exact_ref, tail_ref, nu_ref, x_ref, et_ref, offc_ref,
                     xs_ref, lp_ref, stage, sel_scr, zero, sem, *, tm):
    i = pl.program_id(0)
    last = pl.num_programs(0) - 1
    n_e = N_EXPERTS
    d = x_ref.shape[1]
    nrow = _stage_rows(tm)
    slot = i % 2

    def strip_copy(src0, dst0, rows, sl):
        return pltpu.make_async_copy(stage.at[sl, pl.ds(pl.multiple_of(src0, SUBLANES), rows)],
                                     xs_ref.at[pl.ds(pl.multiple_of(dst0, SUBLANES), rows)], sem.at[sl])

    def wait_tile(tile, sl):
        for s, rows in enumerate(MOE_COPY_SIZES):
            def one(c, carry, rows=rows):
                strip_copy(0, 0, rows, sl).wait()
                return carry

            lax.fori_loop(0, exact_ref[tile * len(MOE_COPY_SIZES) + s], one, 0)

    et = et_ref[0]
    eid = lax.broadcasted_iota(jnp.int32, (n_e, tm), 0)
    hots = [eid == et[k:k + 1, :] for k in range(TOP_K)]
    m_t = jnp.zeros((n_e, tm), F32)
    for h in hots:
        m_t = m_t + jnp.where(h, 1.0, 0.0)
    before = jnp.where(lax.broadcasted_iota(jnp.int32, (tm, tm), 0) < lax.broadcasted_iota(jnp.int32, (tm, tm), 1),
                       1.0, 0.0).astype(BF16)
    rank = jnp.dot(m_t.astype(BF16), before, preferred_element_type=F32)
    base = offc_ref[0].astype(F32) + rank
    lps = [jnp.sum(jnp.where(h, base, 0.0), axis=0, keepdims=True) for h in hots]
    lpi = [lp.astype(jnp.int32) for lp in lps]
    lp_ref[0] = jnp.concatenate(lpi, axis=0)

    for rc in range(nrow // LANES):
        rio = lax.broadcasted_iota(jnp.int32, (LANES, tm), 0) + rc * LANES
        sel = jnp.zeros((LANES, tm), F32)
        for k in range(TOP_K):
            sel = jnp.where(rio == lpi[k], 1.0, sel)
        sel_scr[rc * LANES:(rc + 1) * LANES, :] = sel.astype(BF16)

    @pl.when(i >= 2)
    def _():
        wait_tile(i - 2, slot)

    stage[slot] = jnp.dot(sel_scr[...], x_ref[...].astype(BF16), preferred_element_type=F32)

    _for_each_listed_copy(pieces_ref, exact_ref, i, tm,
                          lambda far, near, rows, prio: strip_copy(near, far, rows, slot).start(priority=prio))

    def zero_fill(wait):
        def go(dst0, rows):
            cp = pltpu.make_async_copy(zero.at[pl.ds(0, rows)],
                                       xs_ref.at[pl.ds(pl.multiple_of(dst0, SUBLANES), rows)], sem.at[2])
            cp.wait() if wait else cp.start()

        def per_tail(e, carry):
            start = tail_ref[2 * e]
            _for_each_piece(tail_ref[2 * e + 1], lambda o, rows: go(start + o, rows))
            return carry

        lax.fori_loop(0, n_e, per_tail, 0)

        def per_block(blk, carry):
            go(blk * MOE_BLOCK, MOE_BLOCK)
            return carry

        lax.fori_loop(nu_ref[0], xs_ref.shape[0] // MOE_BLOCK, per_block, 0)

    @pl.when(i == last)
    def _():
        @pl.when(i >= 1)
        def _():
            wait_tile(i - 1, 1 - slot)

        wait_tile(i, slot)
        zero[...] = jnp.zeros_like(zero)
        zero_fill(wait=False)
        zero_fill(wait=True)


def _dispatch(x1, et, plan):
    T, D = x1.shape
    nt = et.shape[0]
    tm = T // nt
    E = N_EXPERTS
    P = _moe_rows(T, nt, E)
    nrow = _stage_rows(tm)
    tile3 = lambda i, *_: (i, 0, 0)
    grid_spec = pltpu.PrefetchScalarGridSpec(
        num_scalar_prefetch=4,
        grid=(nt,),
        in_specs=[pl.BlockSpec((tm, D), lambda i, *_: (i, 0)),
                  pl.BlockSpec((1, TOP_K, tm), tile3),
                  pl.BlockSpec((1, E, 1), tile3)],
        out_specs=[pl.BlockSpec(memory_space=pl.ANY), pl.BlockSpec((1, TOP_K, tm), tile3)],
        scratch_shapes=[pltpu.VMEM((2, nrow, D), F32),
                        pltpu.VMEM((nrow, tm), BF16),
                        pltpu.VMEM((MOE_BLOCK, D), F32),
                        pltpu.SemaphoreType.DMA((3,))],
    )
    return pl.pallas_call(
        functools.partial(_dispatch_kernel, tm=tm),
        grid_spec=grid_spec,
        out_shape=[jax.ShapeDtypeStruct((P, D), F32), jax.ShapeDtypeStruct((nt, TOP_K, tm), jnp.int32)],
        compiler_params=_params(("arbitrary",)),
        name="moe_dispatch",
    )(plan["pieces"], plan["exact"], plan["tail"], plan["n_used"], x1, et, plan["off_col"])


def _combine_kernel(pieces_ref, exact_ref, ys_ref, lp_ref, wt_ref, x_ref, g_ref, b_ref, *rest, tm, kd):
    if kd:
        wqk_ref, wvg_ref, wgr_ref, w2_ref, b2_ref, xo_ref, qkl_ref, vg_ref, land, sel_scr, sem = rest
    else:
        xo_ref, land, sel_scr, sem = rest
    i = pl.program_id(0)
    nrow = _stage_rows(tm)
    slot = i % 2

    def strip_copy(src0, dst0, rows, sl):
        return pltpu.make_async_copy(ys_ref.at[pl.ds(pl.multiple_of(src0, SUBLANES), rows)],
                                     land.at[sl, pl.ds(pl.multiple_of(dst0, SUBLANES), rows)], sem.at[sl])

    def fetch(tile, sl):
        _for_each_listed_copy(pieces_ref, exact_ref, tile, tm,
                              lambda far, near, rows, prio: strip_copy(far, near, rows, sl).start(priority=prio))

    @pl.when(i == 0)
    def _():
        land[...] = jnp.zeros_like(land)
        fetch(0, 0)

    @pl.when(i + 1 < pl.num_programs(0))
    def _():
        fetch(i + 1, 1 - slot)

    lp = lp_ref[0]
    wt = wt_ref[0]
    for rc in range(nrow // LANES):
        rio = lax.broadcasted_iota(jnp.int32, (LANES, tm), 0) + rc * LANES
        piece = jnp.zeros((LANES, tm), F32)
        for k in range(TOP_K):
            piece = jnp.where(rio == lp[k:k + 1, :], wt[k:k + 1, :], piece)
        sel_scr[rc * LANES:(rc + 1) * LANES, :] = piece.astype(BF16)

    for s, rows in enumerate(MOE_COPY_SIZES):
        def one(c, carry, rows=rows):
            strip_copy(0, 0, rows, slot).wait()
            return carry

        lax.fori_loop(0, exact_ref[i * len(MOE_COPY_SIZES) + s], one, 0)

    ffn = lax.dot_general(sel_scr[...], land[slot, 0:nrow, :].astype(BF16), (((0,), (0,)), ((), ())),
                          preferred_element_type=F32)
    xn = _layer_norm(DN_ALPHA * x_ref[...] + ffn, g_ref[...], b_ref[...])
    xo_ref[...] = xn
    if kd:
        xb = xn.astype(BF16)
        qkl_ref[:, :2 * kd] = jnp.dot(xb, wqk_ref[...], preferred_element_type=F32)
        vg_ref[...] = jnp.dot(xb, wvg_ref[...], preferred_element_type=F32)
        gr = jnp.dot(xb, wgr_ref[...], preferred_element_type=F32)
        z = jnp.dot(gr, w2_ref[...], precision=HIGHEST, preferred_element_type=F32) + b2_ref[...]
        log_sig = jnp.minimum(z, 0.0) - jnp.log1p(jnp.exp(-jnp.abs(z)))
        qkl_ref[:, 2 * kd:] = log_sig / GLA_GATE_TAU


def _combine_ln(ys, lp, wt, x, plan, g, b, gla=None):
    T, D = x.shape
    tm = MOE_TILE
    nt = T // tm
    const = lambda i, *_: (0, 0)
    row = lambda i, *_: (i, 0)
    tile3 = lambda i, *_: (i, 0, 0)
    in_specs = [pl.BlockSpec(memory_space=pl.ANY), pl.BlockSpec((1, TOP_K, tm), tile3),
                pl.BlockSpec((1, TOP_K, tm), tile3), pl.BlockSpec((tm, D), row),
                pl.BlockSpec((1, D), const), pl.BlockSpec((1, D), const)]
    out_specs = [pl.BlockSpec((tm, D), row)]
    out_shape = [jax.ShapeDtypeStruct((T, D), F32)]
    args = [ys, lp, wt, x, g.reshape(1, D), b.reshape(1, D)]
    kd = 0
    if gla is not None:
        w_in1, w_gate2, b_gate, kd, vd = gla
        ws = [w_in1[:, :2 * kd].astype(BF16), w_in1[:, 2 * kd:2 * kd + 2 * vd].astype(BF16),
              w_in1[:, 2 * kd + 2 * vd:].astype(BF16), w_gate2.astype(F32), b_gate.reshape(1, kd)]
        in_specs += [pl.BlockSpec(w.shape, const) for w in ws]
        args += ws
        out_specs += [pl.BlockSpec((tm, 3 * kd), row), pl.BlockSpec((tm, 2 * vd), row)]
        out_shape += [jax.ShapeDtypeStruct((T, 3 * kd), F32), jax.ShapeDtypeStruct((T, 2 * vd), F32)]
    grid_spec = pltpu.PrefetchScalarGridSpec(
        num_scalar_prefetch=2,
        grid=(nt,),
        in_specs=in_specs,
        out_specs=out_specs,
        scratch_shapes=[pltpu.VMEM((2, _stage_rows(tm), D), F32), pltpu.VMEM((_stage_rows(tm), tm), BF16),
                        pltpu.SemaphoreType.DMA((2,))],
    )
    return pl.pallas_call(
        functools.partial(_combine_kernel, tm=tm, kd=kd),
        grid_spec=grid_spec,
        out_shape=out_shape,
        compiler_params=_params(("arbitrary",)),
        name="moe_combine_ln",
    )(plan["pieces"], plan["exact"], *args)


def _moe_ln(x1, et, wt, cnt, layer, w_gate_up, b_gate_up, w_down, b_down, g, b, gla=None):
    T = x1.shape[0]
    plan = _moe_plan(cnt, T)
    xs, lp = _dispatch(x1, et, plan)
    wgu = _deinterleave_gate_up(w_gate_up, layer)
    bgu = jnp.concatenate([b_gate_up[layer, :, 0::2], b_gate_up[layer, :, 1::2]], axis=-1)[:, None, :].astype(F32)
    ys = _expert_ffn(xs, plan, wgu, w_down, layer, bgu, b_down[layer, :, None, :].astype(F32))
    return _combine_ln(ys, lp, wt, x1, plan, g, b, gla)


def _gla_kernel(q_ref, k_ref, la_ref, v_ref, g_ref, nw_ref, o_ref, s_scr, *, tl, nh, dk, dv):
    C = GLA_CHUNK

    @pl.when(pl.program_id(1) == 0)
    def _():
        s_scr[...] = jnp.zeros_like(s_scr)

    ri = lax.broadcasted_iota(jnp.int32, (C, C), 0)
    ci = lax.broadcasted_iota(jnp.int32, (C, C), 1)
    lower = ri >= ci
    tri = jnp.where(lower, 1.0, 0.0).astype(BF16)
    states = [s_scr[h] for h in range(nh)]
    for c in range(tl // C):
        rows = slice(c * C, (c + 1) * C)
        rest = la_ref[0, rows, :]
        bcum = None
        for _ in range(3):
            piece = rest.astype(BF16)
            part = jnp.dot(tri, piece, preferred_element_type=F32)
            bcum = part if bcum is None else bcum + part
            rest = rest - piece.astype(F32)
        btot = bcum[C - 1:C, :]
        qb = (q_ref[0, rows, :] * (dk ** -0.5) * jnp.exp(bcum)).astype(BF16)
        kc = k_ref[0, rows, :]
        k_in = (kc * jnp.exp(-bcum)).astype(BF16)
        k_st = kc * jnp.exp(btot - bcum)
        dec = jnp.exp(btot)
        for h in range(nh):
            ks, vs = slice(h * dk, (h + 1) * dk), slice(h * dv, (h + 1) * dv)
            vb = v_ref[0, rows, vs].astype(BF16)
            att = lax.dot_general(qb[:, ks], k_in[:, ks], (((1,), (1,)), ((), ())), preferred_element_type=F32)
            att = jnp.where(lower, att, 0.0)
            o = (jnp.dot(att.astype(BF16), vb, preferred_element_type=F32)
                 + jnp.dot(qb[:, ks], states[h].astype(BF16), preferred_element_type=F32))
            kv = lax.dot_general(k_st[:, ks].astype(BF16), vb, (((0,), (0,)), ((), ())),
                                 preferred_element_type=F32)
            decay = jnp.broadcast_to(dec[:, ks], (SUBLANES, dk)).T[:, 0:1]
            states[h] = decay * states[h] + kv
            o = o * lax.rsqrt(jnp.mean(o * o, axis=-1, keepdims=True) + RMS_EPS) * nw_ref[...]
            gg = g_ref[0, rows, vs]
            o_ref[0, rows, vs] = o * (gg * jax.nn.sigmoid(gg))
    for h in range(nh):
        s_scr[h] = states[h]


def _gla(qkl, vg, norm_w, B, L, tl=256):
    T, kd3 = qkl.shape
    kd = kd3 // 3
    vd = vg.shape[1] // 2
    H = GLA_HEADS
    dk, dv = kd // H, vd // H
    qkl3 = qkl.reshape(B, L, kd3)
    vg3 = vg.reshape(B, L, 2 * vd)
    return pl.pallas_call(
        functools.partial(_gla_kernel, tl=tl, nh=H, dk=dk, dv=dv),
        grid=(B, L // tl),
        in_specs=[pl.BlockSpec((1, tl, kd), lambda b, l: (b, l, 0)),
                  pl.BlockSpec((1, tl, kd), lambda b, l: (b, l, 1)),
                  pl.BlockSpec((1, tl, kd), lambda b, l: (b, l, 2)),
                  pl.BlockSpec((1, tl, vd), lambda b, l: (b, l, 0)),
                  pl.BlockSpec((1, tl, vd), lambda b, l: (b, l, 1)),
                  pl.BlockSpec((1, dv), lambda b, l: (0, 0))],
        out_specs=pl.BlockSpec((1, tl, vd), lambda b, l: (b, l, 0)),
        out_shape=jax.ShapeDtypeStruct((B, L, vd), F32),
        scratch_shapes=[pltpu.VMEM((H, dk, dv), F32)],
        compiler_params=_params(("parallel", "arbitrary")),
        name="gla_mixer",
    )(qkl3, qkl3, qkl3, vg3, vg3, norm_w.reshape(1, dv).astype(F32))


def kernel(x, w_in0, s5_lam_re, s5_lam_im, s5_log_dt, s5_b_re, s5_b_im, s5_c_re, s5_c_im, s5_d, s5_w_glu,
           s5_b_glu, rel_bias, w_out0, w_in1, gla_w_gate2, gla_b_gate, gla_norm_w, w_out1, ln_mix_g, ln_mix_b,
           ln_ffn_g, ln_ffn_b, router_w, router_b, exp_w_gate_up, exp_b_gate_up, exp_w_down, exp_b_down):
    B, L, D = x.shape
    T = B * L
    s5w = s5_w_glu.shape[-1]
    kd = gla_w_gate2.shape[-1]
    vd = gla_norm_w.shape[-1] * GLA_HEADS

    u_tm, k_att, qt_att, vt_att = _inproj0(x, w_in0[0], s5w)
    y_a = _s5_mixer(u_tm.reshape(L, B, s5w), s5_lam_re[0], s5_lam_im[0], s5_log_dt[0], s5_b_re[0], s5_b_im[0],
                    s5_c_re[0], s5_c_im[0], s5_d[0].reshape(-1), s5_w_glu[0], s5_b_glu[0])
    y_b = _moba(qt_att, k_att, vt_att, rel_bias)
    tl = MIX_TILE
    w0 = w_out0[0].astype(BF16)
    ys = [(y_a.reshape(L, B * s5w), pl.BlockSpec((tl, s5w), lambda bb, l: (l, bb))),
          (y_b, pl.BlockSpec((1, tl, D - s5w), lambda bb, l: (bb, l, 0)))]
    x1, et, wt, cnt = _mix_ln(x, ys, [w0[:s5w], w0[s5w:]], ln_mix_g[0], ln_mix_b[0], router_w[0], router_b[0])
    x2, qkl, vg = _moe_ln(x1.reshape(T, D), et, wt, cnt, 0, exp_w_gate_up, exp_b_gate_up, exp_w_down,
                          exp_b_down, ln_ffn_g[0], ln_ffn_b[0],
                          gla=(w_in1[0], gla_w_gate2[0], gla_b_gate[0], kd, vd))

    y_c = _gla(qkl, vg, gla_norm_w[0], B, L)
    ys = [(y_c, pl.BlockSpec((1, tl, vd), lambda bb, l: (bb, l, 0)))]
    x3, et, wt, cnt = _mix_ln(x2.reshape(B, L, D), ys, [w_out1[0].astype(BF16)], ln_mix_g[1], ln_mix_b[1],
                              router_w[1], router_b[1])
    (out,) = _moe_ln(x3.reshape(T, D), et, wt, cnt, 1, exp_w_gate_up, exp_b_gate_up, exp_w_down,
                     exp_b_down, ln_ffn_g[1], ln_ffn_b[1])
    return out.reshape(B, L, D)
```
